```python
import math
import jax, jax.numpy as jnp
from jax import lax
import numpy as np

D_MODEL = 1024
BATCH = 8
SEQ = 2048
DEPTH = 1
DEC_BATCH = 128
DEC_SEQ = 8
PAST_LEN = 16384
PAGE_SIZE = 128

D_CONV = D_MODEL
CONV_W = 3
GLA_HEADS = 4
GLA_DK = D_MODEL // 2
GLA_DV = D_MODEL
DK_HEAD = GLA_DK // GLA_HEADS
DV_HEAD = GLA_DV // GLA_HEADS
GATE_RANK = 16
GATE_NORMALIZER = 16.0
GLA_CHUNK = 64
D_IN = 3 * D_CONV + 2 * GLA_DK + 2 * GLA_DV + GATE_RANK + 2 * D_MODEL
N_EXPERTS = 64
TOP_K = 8
N_GROUPS = 8
TOPK_GROUPS = 4
D_EXPERT = D_MODEL // 4
D_SHARED = D_MODEL // 4
ROUTED_SCALE = 2.5
MOE_BLOCK = 128
EPS = 1e-6

kernel_name = "hybrid_conv_gla_moe_adaln_step"


def _rmsnorm(x, w):
    xf = x.astype(jnp.float32)
    y = xf * lax.rsqrt(jnp.mean(xf * xf, axis=-1, keepdims=True) + EPS)
    return (y * w.astype(jnp.float32)).astype(x.dtype)


def _split_cols(p):
    sizes = [D_CONV, D_CONV, D_CONV, GLA_DK, GLA_DK, GLA_DV, GLA_DV, GATE_RANK, D_MODEL, D_MODEL]
    outs = []
    start = 0
    for s in sizes:
        outs.append(p[..., start:start + s])
        start += s
    return outs


def _gla_chunked(q, k, v, log_a, s0):
    bsz, nh, L, _ = q.shape
    dv = v.shape[-1]
    C = math.gcd(L, GLA_CHUNK)
    n = L // C

    def to_chunks(t):
        return jnp.moveaxis(t.reshape(bsz, nh, n, C, t.shape[-1]), 2, 0)

    causal = jnp.tril(jnp.ones((C, C), dtype=bool))

    def step(S, inp):
        qc, kc, vc, gc = inp
        b = jnp.cumsum(gc, axis=2)
        b_end = b[:, :, -1:, :]
        q_dec = qc * jnp.exp(b)
        o_inter = jnp.einsum('bhcd,bhde->bhce', q_dec, S)
        scores = jnp.einsum('bhcd,bhsd->bhcs', q_dec, kc * jnp.exp(-b))
        scores = jnp.where(causal, scores, 0.0)
        o = o_inter + jnp.einsum('bhcs,bhse->bhce', scores, vc)
        S_new = (jnp.exp(b_end[:, :, 0, :])[..., None] * S
                 + jnp.einsum('bhsd,bhse->bhde', kc * jnp.exp(b_end - b), vc))
        return S_new, o

    S, o = lax.scan(step, s0, (to_chunks(q), to_chunks(k), to_chunks(v), to_chunks(log_a)))
    o = jnp.moveaxis(o, 0, 2).reshape(bsz, nh, L, dv)
    return o, S


def _mixer(h, conv_state, gla_state, w_in, conv_w, w_gk, b_gk, gla_norm_w,
           w_out_conv, w_out_gla, w_o):
    bsz, L, _ = h.shape
    b_gate, c_gate, h_conv, q, k, v, g, z, u_a, u_b = _split_cols(h @ w_in)
    pre = c_gate * h_conv
    padded = jnp.concatenate([conv_state.astype(pre.dtype), pre], axis=1)
    conv = conv_w[0] * padded[:, 0:L]
    for j in range(1, CONV_W):
        conv = conv + conv_w[j] * padded[:, j:j + L]
    y_a = (b_gate * conv) @ w_out_conv
    new_conv = padded[:, L:]
    log_a = jax.nn.log_sigmoid((z @ w_gk + b_gk).astype(jnp.float32)) / GATE_NORMALIZER

    def heads(t, dh):
        return t.reshape(bsz, L, GLA_HEADS, dh).transpose(0, 2, 1, 3).astype(jnp.float32)

    o, S = _gla_chunked(heads(q, DK_HEAD) * (DK_HEAD ** -0.5), heads(k, DK_HEAD),
                        heads(v, DV_HEAD), heads(log_a, DK_HEAD),
                        gla_state.astype(jnp.float32))
    o = _rmsnorm(o, gla_norm_w).transpose(0, 2, 1, 3).reshape(bsz, L, GLA_DV).astype(h.dtype)
    y_b = (o * jax.nn.silu(g)) @ w_out_gla
    merged = jax.nn.sigmoid(u_a) * y_a + jax.nn.sigmoid(u_b) * y_b
    return merged @ w_o, new_conv, S.astype(gla_state.dtype)


def _moe(h, router_w, router_bias, we_gate, we_up, we_down, ws_gate, ws_up, ws_down):
    T, D = h.shape
    E = N_EXPERTS
    scores = jax.nn.sigmoid(h.astype(jnp.float32) @ router_w.astype(jnp.float32))
    choice = scores + router_bias.astype(jnp.float32)
    grp = choice.reshape(T, N_GROUPS, E // N_GROUPS)
    grp_score = lax.top_k(grp, 2)[0].sum(-1)
    _, gidx = lax.top_k(grp_score, TOPK_GROUPS)
    gmask = jnp.any(gidx[..., None] == jnp.arange(N_GROUPS), axis=1)
    emask = jnp.repeat(gmask, E // N_GROUPS, axis=1)
    _, eidx = lax.top_k(jnp.where(emask, choice, -jnp.inf), TOP_K)
    w = jnp.take_along_axis(scores, eidx, axis=1)
    w = w / jnp.sum(w, axis=-1, keepdims=True) * ROUTED_SCALE

    A = T * TOP_K
    flat_e = eidx.reshape(-1)
    flat_tok = jnp.repeat(jnp.arange(T, dtype=jnp.int32), TOP_K)
    order = jnp.argsort(flat_e)
    se, stok, sw = flat_e[order], flat_tok[order], w.reshape(-1)[order]
    counts = jnp.bincount(flat_e, length=E)
    starts = jnp.cumsum(counts) - counts
    padded = (counts + MOE_BLOCK - 1) // MOE_BLOCK * MOE_BLOCK
    pends = jnp.cumsum(padded)
    pstarts = pends - padded
    dest = pstarts[se] + (jnp.arange(A) - starts[se])
    n_blocks = (A + E * (MOE_BLOCK - 1) + MOE_BLOCK - 1) // MOE_BLOCK
    P = n_blocks * MOE_BLOCK
    tok_buf = jnp.full((P,), T, dtype=jnp.int32).at[dest].set(stok)
    w_buf = jnp.zeros((P,), jnp.float32).at[dest].set(sw)
    blk_e = jnp.clip(jnp.searchsorted(pends, jnp.arange(n_blocks) * MOE_BLOCK, side='right'), 0, E - 1)
    hpad = jnp.concatenate([h, jnp.zeros((1, D), h.dtype)], axis=0)

    def run(args):
        tok, e = args
        xb = hpad[tok]
        act = jax.nn.silu(xb @ we_gate[e]) * (xb @ we_up[e])
        return act @ we_down[e]

    out = lax.map(run, (tok_buf.reshape(n_blocks, MOE_BLOCK), blk_e)).reshape(P, D)
    routed = jnp.zeros((T + 1, D), out.dtype).at[tok_buf].add(out * w_buf[:, None].astype(out.dtype))[:T]
    shared = (jax.nn.silu(h @ ws_gate) * (h @ ws_up)) @ ws_down
    return routed + shared


def _forward(x, c, conv_state, gla_state, w_ada, b_ada, norm1_w, w_in, conv_w, w_gk, b_gk,
             gla_norm_w, w_out_conv, w_out_gla, w_o, norm2_w, router_w, router_bias,
             we_gate, we_up, we_down, ws_gate, ws_up, ws_down, final_norm_w):
    bsz, L, D = x.shape
    new_convs, new_glas = [], []
    for l in range(DEPTH):
        mod = jax.nn.silu(c) @ w_ada[l] + b_ada[l]
        sh1, sc1, g1, sh2, sc2, g2 = jnp.split(mod, 6, axis=-1)
        h = _rmsnorm(x, norm1_w[l]) * (1.0 + sc1[:, None]) + sh1[:, None]
        mix, nc, ns = _mixer(h, conv_state[l], gla_state[l], w_in[l], conv_w[l], w_gk[l], b_gk[l],
                             gla_norm_w[l], w_out_conv[l], w_out_gla[l], w_o[l])
        x = x + g1[:, None] * mix
        h = _rmsnorm(x, norm2_w[l]) * (1.0 + sc2[:, None]) + sh2[:, None]
        ffn = _moe(h.reshape(bsz * L, D), router_w[l], router_bias[l], we_gate[l], we_up[l],
                   we_down[l], ws_gate[l], ws_up[l], ws_down[l]).reshape(bsz, L, D)
        x = x + g2[:, None] * ffn
        new_convs.append(nc)
        new_glas.append(ns)
    return _rmsnorm(x, final_norm_w), jnp.stack(new_convs), jnp.stack(new_glas)


def setup_inputs(seed: int = 0) -> dict:
    key = jax.random.key(seed)
    ks = jax.random.split(key, 32)

    def nrm(k, shape, s):
        return jax.random.normal(k, shape, jnp.float32) * s

    D = D_MODEL
    return {
        "x_prompt": nrm(ks[0], (BATCH, SEQ, D), 1.0),
        "x_sample": nrm(ks[1], (DEC_BATCH, DEC_SEQ, D), 1.0),
        "state_conv": nrm(ks[2], (DEPTH, DEC_BATCH, CONV_W - 1, D_CONV), 1.0),
        "state_gla": nrm(ks[3], (DEPTH, DEC_BATCH, GLA_HEADS, DK_HEAD, DV_HEAD), 0.5),
        "c_prompt": nrm(ks[4], (BATCH, D), 1.0),
        "c_sample": nrm(ks[5], (DEC_BATCH, D), 1.0),
        "w_ada": nrm(ks[6], (DEPTH, D, 6 * D), 0.5 * D ** -0.5),
        "b_ada": nrm(ks[7], (DEPTH, 6 * D), 0.02),
        "norm1_w": 1.0 + nrm(ks[8], (DEPTH, D), 0.02),
        "w_in": nrm(ks[9], (DEPTH, D, D_IN), D ** -0.5),
        "conv_w": nrm(ks[10], (DEPTH, CONV_W, D_CONV), CONV_W ** -0.5),
        "w_gk": nrm(ks[11], (DEPTH, GATE_RANK, GLA_DK), GATE_RANK ** -0.5),
        "b_gk": nrm(ks[12], (DEPTH, GLA_DK), 0.02),
        "gla_norm_w": 1.0 + nrm(ks[13], (DEPTH, DV_HEAD), 0.02),
        "w_out_conv": nrm(ks[14], (DEPTH, D_CONV, D), D_CONV ** -0.5),
        "w_out_gla": nrm(ks[15], (DEPTH, GLA_DV, D), GLA_DV ** -0.5),
        "w_o": nrm(ks[16], (DEPTH, D, D), D ** -0.5),
        "norm2_w": 1.0 + nrm(ks[17], (DEPTH, D), 0.02),
        "router_w": nrm(ks[18], (DEPTH, D, N_EXPERTS), D ** -0.5),
        "router_bias": nrm(ks[19], (DEPTH, N_EXPERTS), 0.01),
        "we_gate": nrm(ks[20], (DEPTH, N_EXPERTS, D, D_EXPERT), D ** -0.5),
        "we_up": nrm(ks[21], (DEPTH, N_EXPERTS, D, D_EXPERT), D ** -0.5),
        "we_down": nrm(ks[22], (DEPTH, N_EXPERTS, D_EXPERT, D), D_EXPERT ** -0.5),
        "ws_gate": nrm(ks[23], (DEPTH, D, D_SHARED), D ** -0.5),
        "ws_up": nrm(ks[24], (DEPTH, D, D_SHARED), D ** -0.5),
        "ws_down": nrm(ks[25], (DEPTH, D_SHARED, D), D_SHARED ** -0.5),
        "final_norm_w": 1.0 + nrm(ks[26], (D,), 0.02),
    }


def reference(x_prompt, x_sample, state_conv, state_gla, c_prompt, c_sample, w_ada, b_ada,
              norm1_w, w_in, conv_w, w_gk, b_gk, gla_norm_w, w_out_conv, w_out_gla, w_o,
              norm2_w, router_w, router_bias, we_gate, we_up, we_down, ws_gate, ws_up,
              ws_down, final_norm_w):
    params = (w_ada, b_ada, norm1_w, w_in, conv_w, w_gk, b_gk, gla_norm_w, w_out_conv,
              w_out_gla, w_o, norm2_w, router_w, router_bias, we_gate, we_up, we_down,
              ws_gate, ws_up, ws_down, final_norm_w)
    bp = x_prompt.shape[0]
    conv0 = jnp.zeros((DEPTH, bp, CONV_W - 1, D_CONV), x_prompt.dtype)
    gla0 = jnp.zeros((DEPTH, bp, GLA_HEADS, DK_HEAD, DV_HEAD), x_prompt.dtype)
    y_prompt, conv_p, gla_p = _forward(x_prompt, c_prompt, conv0, gla0, *params)
    y_sample, conv_s, gla_s = _forward(x_sample, c_sample, state_conv, state_gla, *params)
    return (y_prompt, y_sample, conv_p, gla_p, conv_s, gla_s)
```

```python
import functools
import math

import jax
import jax.numpy as jnp
from jax import lax
from jax.experimental import pallas as pl
from jax.experimental.pallas import tpu as pltpu

f32 = jnp.float32
bf16 = jnp.bfloat16
i32 = jnp.int32

D = 1024
H = 4
DKH = 128
DVH = 256
DK = H * DKH
DV = H * DVH
RANK_PAD = 128
GATE_NORMALIZER = 16.0
GLA_CHUNK = 64
E = 64
TOPK = 8
NGROUPS = 8
TOPK_GROUPS = 4
DE = 256
ROUTED_SCALE = 2.5
EPS = 1e-6
NEG = float("-inf")

VMEM_LIMIT = 58 * 1024 * 1024
MOE_BM = 256
ROUTE_TL = 1024


def _dot(a, b):
    return jnp.dot(a, b, preferred_element_type=f32)


def _split3(x):
    hi = x.astype(bf16)
    r1 = x - hi.astype(f32)
    mid = r1.astype(bf16)
    lo = (r1 - mid.astype(f32)).astype(bf16)
    return hi, mid, lo


def _dot01(m01, x):
    hi, mid, lo = _split3(x)
    return _dot(m01, hi) + _dot(m01, mid) + _dot(m01, lo)


def _rms(v, w):
    ms = jnp.mean(v * v, axis=-1, keepdims=True)
    return v * lax.rsqrt(ms + EPS) * w


def _const_spec(shape):
    n = len(shape)
    return pl.BlockSpec(shape, lambda *_: (0,) * n, pipeline_mode=pl.Buffered(1))


def _adaln_body(c_ref, w_ref, b_ref, o_ref):
    c = c_ref[...]
    a = (c * jax.nn.sigmoid(c)).astype(bf16)
    o_ref[...] = _dot(a, w_ref[...].astype(bf16)) + b_ref[...]


def _adaln(c_all, w_ada, b_ada):
    nrow = c_all.shape[0]
    tn = 1536
    return pl.pallas_call(
        _adaln_body,
        grid=(6 * D // tn,),
        in_specs=[pl.BlockSpec((nrow, D), lambda j: (0, 0)),
                  pl.BlockSpec((D, tn), lambda j: (0, j)),
                  pl.BlockSpec((1, tn), lambda j: (0, j))],
        out_specs=pl.BlockSpec((nrow, tn), lambda j: (0, j)),
        out_shape=jax.ShapeDtypeStruct((nrow, 6 * D), f32),
        compiler_params=pltpu.CompilerParams(dimension_semantics=("arbitrary",), vmem_limit_bytes=VMEM_LIMIT),
        name="adaln",
    )(c_all, w_ada, b_ada)


def _mixer_body(nb, lt, chunk, has_state, n_alias, *refs):
    if has_state:
        (x_ref, mod_ref, cs_ref, gs_ref, *rest) = refs
    else:
        (x_ref, mod_ref, *rest) = refs
        cs_ref = gs_ref = None
    (n1_ref, wmain_ref, wz_ref, wu_ref, wgk_ref, bgk_ref, cw_ref, gnw_ref, woc_ref, wog_ref, wo_ref,
     n2_ref, rwt_ref, wsg_ref, wsu_ref, wsd_ref, *rest) = rest
    (x2_ref, h2_ref, lg_ref, nc_ref, ng_ref, carry_ref, st_ref, og_ref) = rest[n_alias:]
    m = nb * lt
    i = pl.program_id(1)
    last = pl.num_programs(1) - 1

    def rows(v):
        w = v.shape[-1]
        if nb == 1:
            return v.reshape(1, w)
        return jnp.broadcast_to(v, (nb, lt, w)).reshape(m, w)

    @pl.when(i == 0)
    def _():
        if has_state:
            cs = cs_ref[...]
            carry_ref[0] = cs[:, 0:1, :]
            carry_ref[1] = cs[:, 1:2, :]
            for hh in range(H):
                st_ref[hh] = jnp.swapaxes(gs_ref[:, hh], 1, 2)
        else:
            carry_ref[...] = jnp.zeros_like(carry_ref)
            st_ref[...] = jnp.zeros_like(st_ref)

    x = x_ref[...].reshape(m, D)
    sh1, sc1, g1, sh2, sc2, g2 = [rows(mod_ref[j]) for j in range(6)]
    hb = (_rms(x, n1_ref[...]) * (1.0 + sc1) + sh1).astype(bf16)

    def proj(lo, hi):
        return _dot(hb, wmain_ref[:, lo:hi])

    pre = proj(D, 2 * D) * proj(2 * D, 3 * D)
    c0 = rows(carry_ref[0])
    c1 = rows(carry_ref[1])
    l_idx = lax.broadcasted_iota(i32, (m, D), 0) & (lt - 1)
    r1 = pltpu.roll(pre, 1, 0)
    r2 = pltpu.roll(pre, 2, 0)
    prev1 = jnp.where(l_idx == 0, c1, r1)
    prev2 = jnp.where(l_idx == 0, c0, jnp.where(l_idx == 1, c1, r2))
    cw = cw_ref[...]
    conv = cw[0:1] * prev2 + cw[1:2] * prev1 + cw[2:3] * pre
    y_a = _dot((proj(0, D) * conv).astype(bf16), woc_ref[...])
    pre3 = pre.reshape(nb, lt, D)
    tail = pre3[:, lt - 2:lt, :]
    carry_ref[0] = tail[:, 0:1, :]
    carry_ref[1] = tail[:, 1:2, :]
    nc_ref[0] = tail

    q = proj(3 * D, 3 * D + DK)
    k = proj(3 * D + DK, 3 * D + 2 * DK)
    v = proj(4 * D, 5 * D)
    g = proj(5 * D, 6 * D)
    z = _dot(hb, wz_ref[...]).astype(bf16)
    pa = _dot(z, wgk_ref[...]) + bgk_ref[...]
    la = (jnp.minimum(pa, 0.0) - jnp.log1p(jnp.exp(-jnp.abs(pa)))) / GATE_NORMALIZER
    rr = lax.broadcasted_iota(i32, (m, m), 0)
    cc = lax.broadcasted_iota(i32, (m, m), 1)
    same_chunk = (rr & ~(chunk - 1)) == (cc & ~(chunk - 1))
    tri = (same_chunk & (cc <= rr)).astype(bf16)
    bcum = _dot01(tri, la)
    ngrp = m // chunk
    bend = bcum.reshape(ngrp, chunk, DK)[:, chunk - 1:chunk, :]
    bend_rows = jnp.broadcast_to(bend, (ngrp, chunk, DK)).reshape(m, DK)
    q_dec = (q * (DKH ** -0.5) * jnp.exp(bcum)).astype(bf16).reshape(nb, lt, DK)
    k_dec = (k * jnp.exp(-bcum)).astype(bf16).reshape(nb, lt, DK)
    k_end = (k * jnp.exp(bend_rows - bcum)).astype(bf16).reshape(nb, lt, DK)
    a_end = jnp.exp(bend_rows).reshape(nb, lt, DK)
    v3 = v.astype(bf16).reshape(nb, lt, DV)
    g3 = g.reshape(nb, lt, DV)
    causal = lax.broadcasted_iota(i32, (chunk, chunk), 1) <= lax.broadcasted_iota(i32, (chunk, chunk), 0)
    gnw = gnw_ref[...]
    for j in range(lt // chunk):
        js = slice(j * chunk, (j + 1) * chunk)
        for hh in range(H):
            ks = slice(hh * DKH, (hh + 1) * DKH)
            vs = slice(hh * DVH, (hh + 1) * DVH)
            qd, kd, ke, vv = q_dec[:, js, ks], k_dec[:, js, ks], k_end[:, js, ks], v3[:, js, vs]
            st = st_ref[hh]
            o = jnp.einsum('bcd,bed->bce', qd, st.astype(bf16), preferred_element_type=f32)
            sc = jnp.einsum('bcd,bsd->bcs', qd, kd, preferred_element_type=f32)
            sc = jnp.where(causal, sc, 0.0).astype(bf16)
            o = o + jnp.einsum('bcs,bse->bce', sc, vv, preferred_element_type=f32)
            a_row = a_end[:, j * chunk:j * chunk + 1, ks]
            st_ref[hh] = st * a_row + jnp.einsum('bse,bsd->bed', vv, ke, preferred_element_type=f32)
            on = _rms(o, gnw)
            gh = g3[:, js, vs]
            og_ref[:, js, vs] = (on * (gh * jax.nn.sigmoid(gh))).astype(bf16)

    @pl.when(i == last)
    def _():
        for hh in range(H):
            ng_ref[0, :, hh] = jnp.swapaxes(st_ref[hh], 1, 2)

    y_b = _dot(og_ref[...].reshape(m, DV), wog_ref[...])
    u_a = _dot(hb, wu_ref[:, 0:D])
    u_b = _dot(hb, wu_ref[:, D:2 * D])
    merged = (jax.nn.sigmoid(u_a) * y_a + jax.nn.sigmoid(u_b) * y_b).astype(bf16)
    x1 = x + g1 * _dot(merged, wo_ref[...])

    h2 = (_rms(x1, n2_ref[...]) * (1.0 + sc2) + sh2).astype(bf16)
    sg = _dot(h2, wsg_ref[...])
    act = (sg * jax.nn.sigmoid(sg) * _dot(h2, wsu_ref[...])).astype(bf16)
    x2_ref[...] = x1 + g2 * _dot(act, wsd_ref[...])
    h2_ref[...] = h2
    lg_ref[0] = lax.dot_general(rwt_ref[...], h2, (((1,), (1,)), ((), ())), preferred_element_type=f32)


def _mixer(x, mod4, states, weights, bufs, t_all, row0, nb, lt, chunk):
    bsz, seqlen, _ = x.shape
    has_state = states is not None
    m = nb * lt
    n_lt = seqlen // lt
    blk0 = row0 // m
    nsteps = (bsz // nb) * n_lt

    def tok_blk(b, i):
        return blk0 + b * n_lt + i

    in_specs = [pl.BlockSpec((nb, lt, D), lambda b, i: (b, i, 0)),
                pl.BlockSpec((6, nb, 1, D), lambda b, i: (0, b, 0, 0))]
    args = [x, mod4]
    if has_state:
        in_specs += [pl.BlockSpec((nb, 2, D), lambda b, i: (b, 0, 0)),
                     pl.BlockSpec((nb, H, DKH, DVH), lambda b, i: (b, 0, 0, 0))]
        args += list(states)
    in_specs += [_const_spec(w.shape) for w in weights]
    args += list(weights)
    aliases = {}
    if bufs is not None:
        aliases = {len(args): 0, len(args) + 1: 1}
        in_specs += [pl.BlockSpec(memory_space=pl.ANY)] * 2
        args += list(bufs)
    out_specs = [pl.BlockSpec((m, D), lambda b, i: (tok_blk(b, i), 0)),
                 pl.BlockSpec((m, D), lambda b, i: (tok_blk(b, i), 0)),
                 pl.BlockSpec((1, E, m), lambda b, i: (b * n_lt + i, 0, 0)),
                 pl.BlockSpec((1, nb, 2, D), lambda b, i: (0, b, 0, 0)),
                 pl.BlockSpec((1, nb, H, DKH, DVH), lambda b, i: (0, b, 0, 0, 0))]
    out_shape = [jax.ShapeDtypeStruct((t_all, D), f32),
                 jax.ShapeDtypeStruct((t_all, D), bf16),
                 jax.ShapeDtypeStruct((nsteps, E, m), f32),
                 jax.ShapeDtypeStruct((1, bsz, 2, D), f32),
                 jax.ShapeDtypeStruct((1, bsz, H, DKH, DVH), f32)]
    return pl.pallas_call(
        functools.partial(_mixer_body, nb, lt, chunk, has_state, len(aliases)),
        grid=(bsz // nb, n_lt),
        in_specs=in_specs,
        out_specs=out_specs,
        out_shape=out_shape,
        scratch_shapes=[pltpu.VMEM((2, nb, 1, D), f32),
                        pltpu.VMEM((H, nb, DVH, DKH), f32),
                        pltpu.VMEM((nb, lt, DV), bf16)],
        input_output_aliases=aliases,
        compiler_params=pltpu.CompilerParams(dimension_semantics=("arbitrary", "arbitrary"),
                                             vmem_limit_bytes=VMEM_LIMIT),
        name="mixer_state" if has_state else "mixer_prompt",
    )(*args)


def _route_body(lg_ref, bias_ref, eidx_ref, w_ref, rank_ref, cnt_ref, run_ref):
    step = pl.program_id(0)
    tl = lg_ref.shape[1]

    @pl.when(step == 0)
    def _():
        run_ref[...] = jnp.zeros_like(run_ref)

    scores = jax.nn.sigmoid(lg_ref[...])
    choice = scores + bias_ref[...]
    gsz = E // NGROUPS
    grp = choice.reshape(NGROUPS, gsz, tl)
    mi = lax.broadcasted_iota(i32, (NGROUPS, gsz, tl), 1)
    m1 = jnp.max(grp, axis=1, keepdims=True)
    first = jnp.min(jnp.where(grp == m1, mi, gsz), axis=1, keepdims=True)
    m2 = jnp.max(jnp.where(mi == first, NEG, grp), axis=1, keepdims=True)
    gscore = (m1 + m2).reshape(NGROUPS, tl)
    gi = lax.broadcasted_iota(i32, (NGROUPS, tl), 0)
    gsel = jnp.zeros((NGROUPS, tl), f32)
    work = gscore
    for _ in range(TOPK_GROUPS):
        mx = jnp.max(work, axis=0, keepdims=True)
        f = jnp.min(jnp.where(work == mx, gi, NGROUPS), axis=0, keepdims=True)
        hit = gi == f
        gsel = jnp.where(hit, 1.0, gsel)
        work = jnp.where(hit, NEG, work)
    emask = jnp.broadcast_to(gsel.reshape(NGROUPS, 1, tl), (NGROUPS, gsz, tl)).reshape(E, tl)
    masked = jnp.where(emask > 0.0, choice, NEG)
    ei = lax.broadcasted_iota(i32, (E, tl), 0)
    sel = jnp.zeros((E, tl), f32)
    hits, ws = [], []
    for kk in range(TOPK):
        mx = jnp.max(masked, axis=0, keepdims=True)
        f = jnp.min(jnp.where(masked == mx, ei, E), axis=0, keepdims=True)
        hit = ei == f
        ws.append(jnp.sum(jnp.where(hit, scores, 0.0), axis=0, keepdims=True))
        masked = jnp.where(hit, NEG, masked)
        sel = jnp.where(hit, 1.0, sel)
        hits.append(hit)
        eidx_ref[kk:kk + 1, :] = f
    wsum = ws[0]
    for t in ws[1:]:
        wsum = wsum + t
    for kk in range(TOPK):
        w_ref[kk:kk + 1, :] = ws[kk] / wsum * ROUTED_SCALE

    ui = lax.broadcasted_iota(i32, (tl, tl), 0)
    uj = lax.broadcasted_iota(i32, (tl, tl), 1)
    upper = (ui <= uj).astype(bf16)
    incl = _dot(sel.astype(bf16), upper)
    rank = run_ref[...] + incl - sel
    for kk in range(TOPK):
        rank_ref[kk:kk + 1, :] = jnp.sum(jnp.where(hits[kk], rank, 0.0), axis=0, keepdims=True).astype(i32)
    run_ref[...] = run_ref[...] + jnp.sum(sel, axis=1, keepdims=True)
    cnt_ref[...] = jnp.broadcast_to(run_ref[...], cnt_ref.shape)


def _route(logits_t, bias_col):
    t_all = logits_t.shape[1]
    tl = ROUTE_TL
    tok = lambda s: (0, s)
    return pl.pallas_call(
        _route_body,
        grid=(t_all // tl,),
        in_specs=[pl.BlockSpec((E, tl), tok), pl.BlockSpec((E, 1), lambda s: (0, 0))],
        out_specs=[pl.BlockSpec((TOPK, tl), tok), pl.BlockSpec((TOPK, tl), tok), pl.BlockSpec((TOPK, tl), tok),
                   pl.BlockSpec((E, 128), lambda s: (0, 0))],
        out_shape=[jax.ShapeDtypeStruct((TOPK, t_all), i32), jax.ShapeDtypeStruct((TOPK, t_all), f32),
                   jax.ShapeDtypeStruct((TOPK, t_all), i32), jax.ShapeDtypeStruct((E, 128), f32)],
        scratch_shapes=[pltpu.VMEM((E, 1), f32)],
        compiler_params=pltpu.CompilerParams(dimension_semantics=("arbitrary",), vmem_limit_bytes=VMEM_LIMIT),
        name="route",
    )(logits_t, bias_col)


def _plan_body(nblk, eidx_ref, rank_ref, cnt_ref, pos_ref, blk_ref):
    tl = eidx_ref.shape[1]
    cnt = cnt_ref[...]
    nblocks = jnp.floor((cnt + (MOE_BM - 1)) * (1.0 / MOE_BM))
    ri = lax.broadcasted_iota(i32, (E, E), 0)
    ci = lax.broadcasted_iota(i32, (E, E), 1)
    below = (ci < ri).astype(bf16)
    bstart = _dot01(below, nblocks)
    pstart = bstart[:, 0:1] * MOE_BM
    ei = lax.broadcasted_iota(i32, (E, tl), 0)
    for kk in range(TOPK):
        hit = ei == eidx_ref[kk:kk + 1, :]
        start = jnp.sum(jnp.where(hit, pstart, 0.0), axis=0, keepdims=True)
        pos_ref[kk:kk + 1, :] = start.astype(i32) + rank_ref[kk:kk + 1, :]

    @pl.when(pl.program_id(0) == 0)
    def _():
        bend = bstart[:, 0:1] + nblocks[:, 0:1]
        bi = lax.broadcasted_iota(i32, (E, nblk), 1).astype(f32)
        owner = jnp.sum((bend <= bi).astype(f32), axis=0, keepdims=True)
        used = jnp.max(bend, axis=0, keepdims=True)
        owner = jnp.where(bi[0:1] < used, jnp.minimum(owner, E - 1.0), -1.0)
        blk_ref[...] = owner.astype(i32)


def _plan(eidx, rank, cnt, nblk):
    t_all = eidx.shape[1]
    tl = ROUTE_TL
    tok = lambda s: (0, s)
    return pl.pallas_call(
        functools.partial(_plan_body, nblk),
        grid=(t_all // tl,),
        in_specs=[pl.BlockSpec((TOPK, tl), tok), pl.BlockSpec((TOPK, tl), tok),
                  pl.BlockSpec((E, 128), lambda s: (0, 0))],
        out_specs=[pl.BlockSpec((TOPK, tl), tok), pl.BlockSpec((1, nblk), lambda s: (0, 0))],
        out_shape=[jax.ShapeDtypeStruct((TOPK, t_all), i32), jax.ShapeDtypeStruct((1, nblk), i32)],
        compiler_params=pltpu.CompilerParams(dimension_semantics=("arbitrary",), vmem_limit_bytes=VMEM_LIMIT),
        name="plan",
    )(eidx, rank, cnt)


def _expert_body(blk_ref, xs_ref, wg_ref, wu_ref, wd_ref, o_ref):
    i = pl.program_id(0)

    @pl.when(blk_ref[i] >= 0)
    def _():
        xb = xs_ref[...]
        hg = _dot(xb, wg_ref[0].astype(bf16))
        act = (hg * jax.nn.sigmoid(hg) * _dot(xb, wu_ref[0].astype(bf16))).astype(bf16)
        o_ref[...] = _dot(act, wd_ref[0].astype(bf16))

    @pl.when(blk_ref[i] < 0)
    def _():
        o_ref[...] = jnp.zeros_like(o_ref)


def _experts(blk_e, xs, we_gate, we_up, we_down):
    nblk = blk_e.shape[0]
    ex = lambda i, blk: (jnp.maximum(blk[i], 0), 0, 0)
    return pl.pallas_call(
        _expert_body,
        grid_spec=pltpu.PrefetchScalarGridSpec(
            num_scalar_prefetch=1,
            grid=(nblk,),
            in_specs=[pl.BlockSpec((MOE_BM, D), lambda i, blk: (i, 0)),
                      pl.BlockSpec((1, D, DE), ex), pl.BlockSpec((1, D, DE), ex), pl.BlockSpec((1, DE, D), ex)],
            out_specs=pl.BlockSpec((MOE_BM, D), lambda i, blk: (i, 0))),
        out_shape=jax.ShapeDtypeStruct((nblk * MOE_BM, D), f32),
        compiler_params=pltpu.CompilerParams(dimension_semantics=("arbitrary",), vmem_limit_bytes=VMEM_LIMIT),
        name="experts",
    )(blk_e, xs, we_gate, we_up, we_down)


def _final_body(nb, lt, x2_ref, r_ref, g2_ref, fw_ref, o_ref):
    m = nb * lt
    g2 = g2_ref[0]
    if nb == 1:
        g2 = g2.reshape(1, D)
    else:
        g2 = jnp.broadcast_to(g2, (nb, lt, D)).reshape(m, D)
    y = x2_ref[...] + g2 * r_ref[...]
    o_ref[...] = _rms(y, fw_ref[...]).reshape(nb, lt, D)


def _final(x2_all, routed_all, mod4, fw, bsz, seqlen, row0, nb, lt):
    m = nb * lt
    n_lt = seqlen // lt
    blk0 = row0 // m
    tok = lambda b, i: (blk0 + b * n_lt + i, 0)
    return pl.pallas_call(
        functools.partial(_final_body, nb, lt),
        grid=(bsz // nb, n_lt),
        in_specs=[pl.BlockSpec((m, D), tok), pl.BlockSpec((m, D), tok),
                  pl.BlockSpec((1, nb, 1, D), lambda b, i: (5, b, 0, 0)),
                  pl.BlockSpec((1, D), lambda b, i: (0, 0))],
        out_specs=pl.BlockSpec((nb, lt, D), lambda b, i: (b, i, 0)),
        out_shape=jax.ShapeDtypeStruct((bsz, seqlen, D), f32),
        compiler_params=pltpu.CompilerParams(dimension_semantics=("arbitrary", "arbitrary"),
                                             vmem_limit_bytes=VMEM_LIMIT),
        name="final",
    )(x2_all, routed_all, mod4, fw)


def kernel(x_prompt, x_sample, state_conv, state_gla, c_prompt, c_sample, w_ada, b_ada, norm1_w, w_in, conv_w,
           w_gk, b_gk, gla_norm_w, w_out_conv, w_out_gla, w_o, norm2_w, router_w, router_bias, we_gate, we_up,
           we_down, ws_gate, ws_up, ws_down, final_norm_w):
    assert w_ada.shape[0] == 1, "single-layer step"
    bp, lp, _ = x_prompt.shape
    bs, ls, _ = x_sample.shape
    tp, ts = bp * lp, bs * ls
    t_all = tp + ts

    w_in0 = w_in[0]
    n_main = 3 * D + 2 * DK + 2 * DV
    rank = w_gk.shape[1]
    weights = (
        norm1_w[0].reshape(1, D),
        w_in0[:, :n_main].astype(bf16),
        jnp.pad(w_in0[:, n_main:n_main + rank], ((0, 0), (0, RANK_PAD - rank))).astype(bf16),
        w_in0[:, n_main + rank:].astype(bf16),
        jnp.pad(w_gk[0], ((0, RANK_PAD - rank), (0, 0))).astype(bf16),
        b_gk[0].reshape(1, DK),
        conv_w[0],
        gla_norm_w[0].reshape(1, DVH),
        w_out_conv[0].astype(bf16),
        w_out_gla[0].astype(bf16),
        w_o[0].astype(bf16),
        norm2_w[0].reshape(1, D),
        router_w[0].T.astype(bf16),
        ws_gate[0].astype(bf16),
        ws_up[0].astype(bf16),
        ws_down[0].astype(bf16),
    )

    mod = _adaln(jnp.concatenate([c_prompt, c_sample], axis=0), w_ada[0], b_ada[0].reshape(1, 6 * D))
    mod4 = mod.reshape(bp + bs, 6, 1, D).transpose(1, 0, 2, 3)
    mod_p, mod_s = mod4[:, :bp], mod4[:, bp:]

    lt_p = 256
    x2, h2, lg_p, conv_p, gla_p = _mixer(x_prompt, mod_p, None, weights, None, t_all, 0, 1, lt_p, GLA_CHUNK)
    nb_s = 8
    x2, h2, lg_s, conv_s, gla_s = _mixer(x_sample, mod_s, (state_conv[0], state_gla[0]), weights, (x2, h2),
                                         t_all, tp, nb_s, ls, math.gcd(ls, GLA_CHUNK))
    lg = jnp.concatenate([lg_p.transpose(1, 0, 2).reshape(E, tp), lg_s.transpose(1, 0, 2).reshape(E, ts)], axis=1)

    eidx, wts, rank_in_e, cnt = _route(lg, router_bias[0].reshape(E, 1))
    n_assign = t_all * TOPK
    nblk = (n_assign + E * (MOE_BM - 1) + MOE_BM - 1) // MOE_BM
    pos, blk_e = _plan(eidx, rank_in_e, cnt, -(-nblk // 128) * 128)

    pos_flat = pos.T.reshape(-1)
    tok_flat = jnp.repeat(jnp.arange(t_all, dtype=i32), TOPK)
    xs = jnp.zeros((nblk * MOE_BM, D), bf16).at[pos_flat].set(h2[tok_flat])
    ys = _experts(blk_e[0, :nblk], xs, we_gate[0], we_up[0], we_down[0])
    routed = jnp.sum(ys[pos.T] * wts.T[:, :, None], axis=1)

    fw = final_norm_w.reshape(1, D)
    y_prompt = _final(x2, routed, mod_p, fw, bp, lp, 0, 1, lt_p)
    y_sample = _final(x2, routed, mod_s, fw, bs, ls, tp, 16, ls)
    return (y_prompt, y_sample, conv_p, gla_p, conv_s, gla_s)
```

```python
import functools
import math

import jax
import jax.numpy as jnp
from jax import lax
from jax.experimental import pallas as pl
from jax.experimental.pallas import tpu as pltpu

f32 = jnp.float32
bf16 = jnp.bfloat16
i32 = jnp.int32
u32 = jnp.uint32

D = 1024
H = 4
DKH = 128
DVH = 256
DK = H * DKH
DV = H * DVH
RANK_PAD = 128
GATE_NORMALIZER = 16.0
GLA_CHUNK = 64
E = 64
TOPK = 8
NGROUPS = 8
TOPK_GROUPS = 4
DE = 256
ROUTED_SCALE = 2.5
EPS = 1e-6
NEG = float("-inf")

VMEM_LIMIT = 58 * 1024 * 1024
MOE_BM = 256
ROUTE_TL = 1024
DISPATCH_TT = 256
FINAL_TT = 256


def _dot(a, b):
    return jnp.dot(a, b, preferred_element_type=f32)


def _split3(x):
    hi = x.astype(bf16)
    r1 = x - hi.astype(f32)
    mid = r1.astype(bf16)
    lo = (r1 - mid.astype(f32)).astype(bf16)
    return hi, mid, lo


def _dot01(m01, x):
    hi, mid, lo = _split3(x)
    return _dot(m01, hi) + _dot(m01, mid) + _dot(m01, lo)


def _rms(v, w):
    ms = jnp.mean(v * v, axis=-1, keepdims=True)
    return v * lax.rsqrt(ms + EPS) * w


def _pack_halves(xb):
    half = xb.shape[1] // 2
    xf = xb.astype(f32)
    lo = lax.shift_right_logical(lax.bitcast_convert_type(xf[:, :half], u32), jnp.uint32(16))
    hi = lax.bitcast_convert_type(xf[:, half:], u32) & jnp.uint32(0xFFFF0000)
    return lo | hi


def _unpack_halves(u):
    lo = lax.bitcast_convert_type(u << jnp.uint32(16), f32).astype(bf16)
    hi = lax.bitcast_convert_type(u & jnp.uint32(0xFFFF0000), f32).astype(bf16)
    return lo, hi


def _const_spec(shape):
    n = len(shape)
    return pl.BlockSpec(shape, lambda *_: (0,) * n, pipeline_mode=pl.Buffered(1))


def _adaln_body(c_ref, w_ref, b_ref, o_ref):
    c = c_ref[...]
    a = (c * jax.nn.sigmoid(c)).astype(bf16)
    o_ref[...] = _dot(a, w_ref[...].astype(bf16)) + b_ref[...]


def _adaln(c_all, w_ada, b_ada):
    nrow = c_all.shape[0]
    tn = 1536
    return pl.pallas_call(
        _adaln_body,
        grid=(6 * D // tn,),
        in_specs=[pl.BlockSpec((nrow, D), lambda j: (0, 0)),
                  pl.BlockSpec((D, tn), lambda j: (0, j)),
                  pl.BlockSpec((1, tn), lambda j: (0, j))],
        out_specs=pl.BlockSpec((nrow, tn), lambda j: (0, j)),
        out_shape=jax.ShapeDtypeStruct((nrow, 6 * D), f32),
        compiler_params=pltpu.CompilerParams(dimension_semantics=("arbitrary",), vmem_limit_bytes=VMEM_LIMIT),
        name="adaln",
    )(c_all, w_ada, b_ada)


def _mixer_body(nb, lt, chunk, has_state, *refs):
    if has_state:
        (x_ref, mod_ref, cs_ref, gs_ref, *rest) = refs
    else:
        (x_ref, mod_ref, *rest) = refs
        cs_ref = gs_ref = None
    (n1_ref, wmain_ref, wz_ref, wu_ref, wgk_ref, bgk_ref, cw_ref, gnw_ref, woc_ref, wog_ref, wo_ref,
     n2_ref, rwt_ref, wsg_ref, wsu_ref, wsd_ref,
     x2_ref, h2_ref, lg_ref, nc_ref, ng_ref, carry_ref, st_ref, og_ref) = rest
    m = nb * lt
    i = pl.program_id(1)
    last = pl.num_programs(1) - 1

    def rows(v):
        w = v.shape[-1]
        if nb == 1:
            return v.reshape(1, w)
        return jnp.broadcast_to(v, (nb, lt, w)).reshape(m, w)

    @pl.when(i == 0)
    def _():
        if has_state:
            cs = cs_ref[...]
            carry_ref[0] = cs[:, 0:1, :]
            carry_ref[1] = cs[:, 1:2, :]
            for hh in range(H):
                st_ref[hh] = jnp.swapaxes(gs_ref[:, hh], 1, 2)
        else:
            carry_ref[...] = jnp.zeros_like(carry_ref)
            st_ref[...] = jnp.zeros_like(st_ref)

    x = x_ref[...].reshape(m, D)
    sh1, sc1, g1, sh2, sc2, g2 = [rows(mod_ref[j]) for j in range(6)]
    hb = (_rms(x, n1_ref[...]) * (1.0 + sc1) + sh1).astype(bf16)

    def proj(lo, hi):
        return _dot(hb, wmain_ref[:, lo:hi])

    pre = proj(D, 2 * D) * proj(2 * D, 3 * D)
    c0 = rows(carry_ref[0])
    c1 = rows(carry_ref[1])
    l_idx = lax.broadcasted_iota(i32, (m, D), 0) & (lt - 1)
    r1 = pltpu.roll(pre, 1, 0)
    r2 = pltpu.roll(pre, 2, 0)
    prev1 = jnp.where(l_idx == 0, c1, r1)
    prev2 = jnp.where(l_idx == 0, c0, jnp.where(l_idx == 1, c1, r2))
    cw = cw_ref[...]
    conv = cw[0:1] * prev2 + cw[1:2] * prev1 + cw[2:3] * pre
    y_a = _dot((proj(0, D) * conv).astype(bf16), woc_ref[...])
    pre3 = pre.reshape(nb, lt, D)
    tail = pre3[:, lt - 2:lt, :]
    carry_ref[0] = tail[:, 0:1, :]
    carry_ref[1] = tail[:, 1:2, :]
    nc_ref[0] = tail

    q = proj(3 * D, 3 * D + DK)
    k = proj(3 * D + DK, 3 * D + 2 * DK)
    v = proj(4 * D, 5 * D)
    g = proj(5 * D, 6 * D)
    z = _dot(hb, wz_ref[...]).astype(bf16)
    pa = _dot(z, wgk_ref[...]) + bgk_ref[...]
    la = (jnp.minimum(pa, 0.0) - jnp.log1p(jnp.exp(-jnp.abs(pa)))) / GATE_NORMALIZER
    rr = lax.broadcasted_iota(i32, (m, m), 0)
    cc = lax.broadcasted_iota(i32, (m, m), 1)
    same_chunk = (rr & ~(chunk - 1)) == (cc & ~(chunk - 1))
    tri = (same_chunk & (cc <= rr)).astype(bf16)
    bcum = _dot01(tri, la)
    ngrp = m // chunk
    bend = bcum.reshape(ngrp, chunk, DK)[:, chunk - 1:chunk, :]
    bend_rows = jnp.broadcast_to(bend, (ngrp, chunk, DK)).reshape(m, DK)
    q_dec = (q * (DKH ** -0.5) * jnp.exp(bcum)).astype(bf16).reshape(nb, lt, DK)
    k_dec = (k * jnp.exp(-bcum)).astype(bf16).reshape(nb, lt, DK)
    k_end = (k * jnp.exp(bend_rows - bcum)).astype(bf16).reshape(nb, lt, DK)
    a_end = jnp.exp(bend_rows).reshape(nb, lt, DK)
    v3 = v.astype(bf16).reshape(nb, lt, DV)
    g3 = g.reshape(nb, lt, DV)
    causal = lax.broadcasted_iota(i32, (chunk, chunk), 1) <= lax.broadcasted_iota(i32, (chunk, chunk), 0)
    gnw = gnw_ref[...]
    for j in range(lt // chunk):
        js = slice(j * chunk, (j + 1) * chunk)
        for hh in range(H):
            ks = slice(hh * DKH, (hh + 1) * DKH)
            vs = slice(hh * DVH, (hh + 1) * DVH)
            qd, kd, ke, vv = q_dec[:, js, ks], k_dec[:, js, ks], k_end[:, js, ks], v3[:, js, vs]
            st = st_ref[hh]
            o = jnp.einsum('bcd,bed->bce', qd, st.astype(bf16), preferred_element_type=f32)
            sc = jnp.einsum('bcd,bsd->bcs', qd, kd, preferred_element_type=f32)
            sc = jnp.where(causal, sc, 0.0).astype(bf16)
            o = o + jnp.einsum('bcs,bse->bce', sc, vv, preferred_element_type=f32)
            a_row = a_end[:, j * chunk:j * chunk + 1, ks]
            st_ref[hh] = st * a_row + jnp.einsum('bse,bsd->bed', vv, ke, preferred_element_type=f32)
            on = _rms(o, gnw)
            gh = g3[:, js, vs]
            og_ref[:, js, vs] = (on * (gh * jax.nn.sigmoid(gh))).astype(bf16)

    @pl.when(i == last)
    def _():
        for hh in range(H):
            ng_ref[0, :, hh] = jnp.swapaxes(st_ref[hh], 1, 2)

    y_b = _dot(og_ref[...].reshape(m, DV), wog_ref[...])
    u_a = _dot(hb, wu_ref[:, 0:D])
    u_b = _dot(hb, wu_ref[:, D:2 * D])
    merged = (jax.nn.sigmoid(u_a) * y_a + jax.nn.sigmoid(u_b) * y_b).astype(bf16)
    x1 = x + g1 * _dot(merged, wo_ref[...])

    h2 = (_rms(x1, n2_ref[...]) * (1.0 + sc2) + sh2).astype(bf16)
    sg = _dot(h2, wsg_ref[...])
    act = (sg * jax.nn.sigmoid(sg) * _dot(h2, wsu_ref[...])).astype(bf16)
    x2_ref[...] = x1 + g2 * _dot(act, wsd_ref[...])
    h2_ref[...] = _pack_halves(h2)
    lg_ref[0] = lax.dot_general(rwt_ref[...], h2, (((1,), (1,)), ((), ())), preferred_element_type=f32)


def _mixer(x, mod4, states, weights, nb, lt, chunk):
    bsz, seqlen, _ = x.shape
    has_state = states is not None
    m = nb * lt
    n_lt = seqlen // lt
    nsteps = (bsz // nb) * n_lt
    t_all = bsz * seqlen

    def tok_blk(b, i):
        return b * n_lt + i

    in_specs = [pl.BlockSpec((nb, lt, D), lambda b, i: (b, i, 0)),
                pl.BlockSpec((6, nb, 1, D), lambda b, i: (0, b, 0, 0))]
    args = [x, mod4]
    if has_state:
        in_specs += [pl.BlockSpec((nb, 2, D), lambda b, i: (b, 0, 0)),
                     pl.BlockSpec((nb, H, DKH, DVH), lambda b, i: (b, 0, 0, 0))]
        args += list(states)
    in_specs += [_const_spec(w.shape) for w in weights]
    args += list(weights)
    out_specs = [pl.BlockSpec((m, D), lambda b, i: (tok_blk(b, i), 0)),
                 pl.BlockSpec((m, D // 2), lambda b, i: (tok_blk(b, i), 0)),
                 pl.BlockSpec((1, E, m), lambda b, i: (b * n_lt + i, 0, 0)),
                 pl.BlockSpec((1, nb, 2, D), lambda b, i: (0, b, 0, 0)),
                 pl.BlockSpec((1, nb, H, DKH, DVH), lambda b, i: (0, b, 0, 0, 0))]
    out_shape = [jax.ShapeDtypeStruct((t_all, D), f32),
                 jax.ShapeDtypeStruct((t_all, D // 2), u32),
                 jax.ShapeDtypeStruct((nsteps, E, m), f32),
                 jax.ShapeDtypeStruct((1, bsz, 2, D), f32),
                 jax.ShapeDtypeStruct((1, bsz, H, DKH, DVH), f32)]
    return pl.pallas_call(
        functools.partial(_mixer_body, nb, lt, chunk, has_state),
        grid=(bsz // nb, n_lt),
        in_specs=in_specs,
        out_specs=out_specs,
        out_shape=out_shape,
        scratch_shapes=[pltpu.VMEM((2, nb, 1, D), f32),
                        pltpu.VMEM((H, nb, DVH, DKH), f32),
                        pltpu.VMEM((nb, lt, DV), bf16)],
        compiler_params=pltpu.CompilerParams(dimension_semantics=("arbitrary", "arbitrary"),
                                             vmem_limit_bytes=VMEM_LIMIT),
        name="mixer_state" if has_state else "mixer_prompt",
    )(*args)


def _route_body(lg_ref, bias_ref, eidx_ref, w_ref, rank_ref, cnt_ref, run_ref):
    step = pl.program_id(0)
    tl = lg_ref.shape[1]

    @pl.when(step == 0)
    def _():
        run_ref[...] = jnp.zeros_like(run_ref)

    scores = jax.nn.sigmoid(lg_ref[...])
    choice = scores + bias_ref[...]
    gsz = E // NGROUPS
    grp = choice.reshape(NGROUPS, gsz, tl)
    mi = lax.broadcasted_iota(i32, (NGROUPS, gsz, tl), 1)
    m1 = jnp.max(grp, axis=1, keepdims=True)
    first = jnp.min(jnp.where(grp == m1, mi, gsz), axis=1, keepdims=True)
    m2 = jnp.max(jnp.where(mi == first, NEG, grp), axis=1, keepdims=True)
    gscore = (m1 + m2).reshape(NGROUPS, tl)
    gi = lax.broadcasted_iota(i32, (NGROUPS, tl), 0)
    gsel = jnp.zeros((NGROUPS, tl), f32)
    work = gscore
    for _ in range(TOPK_GROUPS):
        mx = jnp.max(work, axis=0, keepdims=True)
        f = jnp.min(jnp.where(work == mx, gi, NGROUPS), axis=0, keepdims=True)
        hit = gi == f
        gsel = jnp.where(hit, 1.0, gsel)
        work = jnp.where(hit, NEG, work)
    emask = jnp.broadcast_to(gsel.reshape(NGROUPS, 1, tl), (NGROUPS, gsz, tl)).reshape(E, tl)
    masked = jnp.where(emask > 0.0, choice, NEG)
    ei = lax.broadcasted_iota(i32, (E, tl), 0)
    sel = jnp.zeros((E, tl), f32)
    hits, ws = [], []
    for kk in range(TOPK):
        mx = jnp.max(masked, axis=0, keepdims=True)
        f = jnp.min(jnp.where(masked == mx, ei, E), axis=0, keepdims=True)
        hit = ei == f
        ws.append(jnp.sum(jnp.where(hit, scores, 0.0), axis=0, keepdims=True))
        masked = jnp.where(hit, NEG, masked)
        sel = jnp.where(hit, 1.0, sel)
        hits.append(hit)
        eidx_ref[kk:kk + 1, :] = f
    wsum = ws[0]
    for t in ws[1:]:
        wsum = wsum + t
    for kk in range(TOPK):
        w_ref[kk:kk + 1, :] = ws[kk] / wsum * ROUTED_SCALE

    ui = lax.broadcasted_iota(i32, (tl, tl), 0)
    uj = lax.broadcasted_iota(i32, (tl, tl), 1)
    upper = (ui <= uj).astype(bf16)
    incl = _dot(sel.astype(bf16), upper)
    rank = run_ref[...] + incl - sel
    for kk in range(TOPK):
        rank_ref[kk:kk + 1, :] = jnp.sum(jnp.where(hits[kk], rank, 0.0), axis=0, keepdims=True).astype(i32)
    run_ref[...] = run_ref[...] + jnp.sum(sel, axis=1, keepdims=True)
    cnt_ref[...] = jnp.broadcast_to(run_ref[...], cnt_ref.shape)


def _route(logits_t, bias_col):
    t_all = logits_t.shape[1]
    tl = ROUTE_TL
    tok = lambda s: (0, s)
    return pl.pallas_call(
        _route_body,
        grid=(t_all // tl,),
        in_specs=[pl.BlockSpec((E, tl), tok), pl.BlockSpec((E, 1), lambda s: (0, 0))],
        out_specs=[pl.BlockSpec((TOPK, tl), tok), pl.BlockSpec((TOPK, tl), tok), pl.BlockSpec((TOPK, tl), tok),
                   pl.BlockSpec((E, 128), lambda s: (0, 0))],
        out_shape=[jax.ShapeDtypeStruct((TOPK, t_all), i32), jax.ShapeDtypeStruct((TOPK, t_all), f32),
                   jax.ShapeDtypeStruct((TOPK, t_all), i32), jax.ShapeDtypeStruct((E, 128), f32)],
        scratch_shapes=[pltpu.VMEM((E, 1), f32)],
        compiler_params=pltpu.CompilerParams(dimension_semantics=("arbitrary",), vmem_limit_bytes=VMEM_LIMIT),
        name="route",
    )(logits_t, bias_col)


def _plan_body(nblk, eidx_ref, rank_ref, cnt_ref, pos_ref, blk_ref):
    tl = eidx_ref.shape[1]
    cnt = cnt_ref[...]
    nblocks = jnp.floor((cnt + (MOE_BM - 1)) * (1.0 / MOE_BM))
    ri = lax.broadcasted_iota(i32, (E, E), 0)
    ci = lax.broadcasted_iota(i32, (E, E), 1)
    below = (ci < ri).astype(bf16)
    bstart = _dot01(below, nblocks)
    pstart = bstart[:, 0:1] * MOE_BM
    ei = lax.broadcasted_iota(i32, (E, tl), 0)
    for kk in range(TOPK):
        hit = ei == eidx_ref[kk:kk + 1, :]
        start = jnp.sum(jnp.where(hit, pstart, 0.0), axis=0, keepdims=True)
        pos_ref[kk:kk + 1, :] = start.astype(i32) + rank_ref[kk:kk + 1, :]

    @pl.when(pl.program_id(0) == 0)
    def _():
        bend = bstart[:, 0:1] + nblocks[:, 0:1]
        bi = lax.broadcasted_iota(i32, (E, nblk), 1).astype(f32)
        owner = jnp.sum((bend <= bi).astype(f32), axis=0, keepdims=True)
        used = jnp.max(bend, axis=0, keepdims=True)
        owner = jnp.where(bi[0:1] < used, jnp.minimum(owner, E - 1.0), -1.0)
        blk_ref[...] = owner.astype(i32)


def _plan(eidx, rank, cnt, nblk):
    t_all = eidx.shape[1]
    tl = ROUTE_TL
    tok = lambda s: (0, s)
    return pl.pallas_call(
        functools.partial(_plan_body, nblk),
        grid=(t_all // tl,),
        in_specs=[pl.BlockSpec((TOPK, tl), tok), pl.BlockSpec((TOPK, tl), tok),
                  pl.BlockSpec((E, 128), lambda s: (0, 0))],
        out_specs=[pl.BlockSpec((TOPK, tl), tok), pl.BlockSpec((1, nblk), lambda s: (0, 0))],
        out_shape=[jax.ShapeDtypeStruct((TOPK, t_all), i32), jax.ShapeDtypeStruct((1, nblk), i32)],
        compiler_params=pltpu.CompilerParams(dimension_semantics=("arbitrary",), vmem_limit_bytes=VMEM_LIMIT),
        name="plan",
    )(eidx, rank, cnt)


def _row_copy(src_ref, src_row, dst_ref, dst_row, sem):
    return pltpu.make_async_copy(src_ref.at[pl.ds(src_row, 1)], dst_ref.at[pl.ds(dst_row, 1)], sem)


def _dispatch_body(pos_ref, h_ref, _xs_in, xs_ref, sem):
    tt = h_ref.shape[0]

    def issue(t, c):
        for kk in range(TOPK):
            _row_copy(h_ref, t, xs_ref, pos_ref[kk, t], sem).start()
        return c
    lax.fori_loop(0, tt, issue, 0)

    def drain(t, c):
        for kk in range(TOPK):
            _row_copy(h_ref, 0, xs_ref, 0, sem).wait()
        return c
    lax.fori_loop(0, tt, drain, 0)


def _dispatch(pos, h2p, xs_init, row0):
    t_seg, width = h2p.shape
    tt = DISPATCH_TT
    tile0 = row0 // tt
    return pl.pallas_call(
        _dispatch_body,
        grid=(t_seg // tt,),
        in_specs=[pl.BlockSpec((TOPK, tt), lambda i: (0, tile0 + i), memory_space=pltpu.SMEM),
                  pl.BlockSpec((tt, width), lambda i: (i, 0)),
                  pl.BlockSpec(memory_space=pl.ANY)],
        out_specs=pl.BlockSpec(memory_space=pl.ANY),
        out_shape=jax.ShapeDtypeStruct(xs_init.shape, xs_init.dtype),
        scratch_shapes=[pltpu.SemaphoreType.DMA],
        input_output_aliases={2: 0},
        compiler_params=pltpu.CompilerParams(dimension_semantics=("arbitrary",), vmem_limit_bytes=VMEM_LIMIT),
        name="dispatch",
    )(pos, h2p, xs_init)


def _expert_body(blk_ref, xs_ref, wg_ref, wu_ref, wd_ref, o_ref):
    i = pl.program_id(0)

    @pl.when(blk_ref[i] >= 0)
    def _():
        xlo, xhi = _unpack_halves(xs_ref[...])
        half = D // 2

        def up(w_ref):
            return _dot(xlo, w_ref[0, :half, :].astype(bf16)) + _dot(xhi, w_ref[0, half:, :].astype(bf16))

        hg = up(wg_ref)
        act = (hg * jax.nn.sigmoid(hg) * up(wu_ref)).astype(bf16)
        o_ref[...] = _dot(act, wd_ref[0].astype(bf16))

    @pl.when(blk_ref[i] < 0)
    def _():
        o_ref[...] = jnp.zeros_like(o_ref)


def _experts(blk_e, xs, we_gate, we_up, we_down):
    nblk = blk_e.shape[0]
    ex = lambda i, blk: (jnp.maximum(blk[i], 0), 0, 0)
    return pl.pallas_call(
        _expert_body,
        grid_spec=pltpu.PrefetchScalarGridSpec(
            num_scalar_prefetch=1,
            grid=(nblk,),
            in_specs=[pl.BlockSpec((MOE_BM, D // 2), lambda i, blk: (i, 0)),
                      pl.BlockSpec((1, D, DE), ex), pl.BlockSpec((1, D, DE), ex), pl.BlockSpec((1, DE, D), ex)],
            out_specs=pl.BlockSpec((MOE_BM, D), lambda i, blk: (i, 0))),
        out_shape=jax.ShapeDtypeStruct((nblk * MOE_BM, D), f32),
        compiler_params=pltpu.CompilerParams(dimension_semantics=("arbitrary",), vmem_limit_bytes=VMEM_LIMIT),
        name="experts",
    )(blk_e, xs, we_gate, we_up, we_down)


def _final_body(nb, lt, pos_ref, posn_ref, x2_ref, wt_ref, g2_ref, fw_ref, ys_ref, o_ref, buf_ref, sem):
    m = nb * lt
    n_inner = pl.num_programs(1)
    step = pl.program_id(0) * n_inner + pl.program_id(1)
    nsteps = pl.num_programs(0) * n_inner
    cur = step % 2

    def fetch(p_ref, sl):
        def body(t, c):
            for kk in range(TOPK):
                _row_copy(ys_ref, p_ref[kk, t], buf_ref.at[sl, kk], t, sem.at[sl]).start()
            return c
        lax.fori_loop(0, m, body, 0)

    @pl.when(step == 0)
    def _():
        fetch(pos_ref, 0)

    @pl.when(step + 1 < nsteps)
    def _():
        fetch(posn_ref, 1 - cur)

    def drain(t, c):
        for kk in range(TOPK):
            _row_copy(ys_ref, 0, buf_ref.at[cur, kk], 0, sem.at[cur]).wait()
        return c
    lax.fori_loop(0, m, drain, 0)

    wt = wt_ref[...]
    routed = wt[:, 0:1] * buf_ref[cur, 0]
    for kk in range(1, TOPK):
        routed = routed + wt[:, kk:kk + 1] * buf_ref[cur, kk]
    g2 = g2_ref[0]
    if nb == 1:
        g2 = g2.reshape(1, D)
    else:
        g2 = jnp.broadcast_to(g2, (nb, lt, D)).reshape(m, D)
    y = x2_ref[...] + g2 * routed
    o_ref[...] = _rms(y, fw_ref[...]).reshape(nb, lt, D)


def _final(pos, wts_t, x2, ys, mod4, fw, bsz, seqlen, row0, nb, lt):
    m = nb * lt
    n_lt = seqlen // lt
    blk0 = row0 // m
    last = blk0 + (bsz // nb) * n_lt - 1
    tile = lambda b, i: blk0 + b * n_lt + i
    return pl.pallas_call(
        functools.partial(_final_body, nb, lt),
        grid=(bsz // nb, n_lt),
        in_specs=[pl.BlockSpec((TOPK, m), lambda b, i: (0, tile(b, i)), memory_space=pltpu.SMEM),
                  pl.BlockSpec((TOPK, m), lambda b, i: (0, jnp.minimum(tile(b, i) + 1, last)),
                               memory_space=pltpu.SMEM),
                  pl.BlockSpec((m, D), lambda b, i: (b * n_lt + i, 0)),
                  pl.BlockSpec((m, TOPK), lambda b, i: (tile(b, i), 0)),
                  pl.BlockSpec((1, nb, 1, D), lambda b, i: (5, b, 0, 0)),
                  pl.BlockSpec((1, D), lambda b, i: (0, 0)),
                  pl.BlockSpec(memory_space=pl.ANY)],
        out_specs=pl.BlockSpec((nb, lt, D), lambda b, i: (b, i, 0)),
        out_shape=jax.ShapeDtypeStruct((bsz, seqlen, D), f32),
        scratch_shapes=[pltpu.VMEM((2, TOPK, m, D), f32), pltpu.SemaphoreType.DMA((2,))],
        compiler_params=pltpu.CompilerParams(dimension_semantics=("arbitrary", "arbitrary"),
                                             vmem_limit_bytes=VMEM_LIMIT),
        name="final",
    )(pos, pos, x2, wts_t, mod4, fw, ys)


def kernel(x_prompt, x_sample, state_conv, state_gla, c_prompt, c_sample, w_ada, b_ada, norm1_w, w_in, conv_w,
           w_gk, b_gk, gla_norm_w, w_out_conv, w_out_gla, w_o, norm2_w, router_w, router_bias, we_gate, we_up,
           we_down, ws_gate, ws_up, ws_down, final_norm_w):
    assert w_ada.shape[0] == 1, "single-layer step"
    bp, lp, _ = x_prompt.shape
    bs, ls, _ = x_sample.shape
    tp, ts = bp * lp, bs * ls
    t_all = tp + ts

    w_in0 = w_in[0]
    n_main = 3 * D + 2 * DK + 2 * DV
    rank = w_gk.shape[1]
    weights = (
        norm1_w[0].reshape(1, D),
        w_in0[:, :n_main].astype(bf16),
        jnp.pad(w_in0[:, n_main:n_main + rank], ((0, 0), (0, RANK_PAD - rank))).astype(bf16),
        w_in0[:, n_main + rank:].astype(bf16),
        jnp.pad(w_gk[0], ((0, RANK_PAD - rank), (0, 0))).astype(bf16),
        b_gk[0].reshape(1, DK),
        conv_w[0],
        gla_norm_w[0].reshape(1, DVH),
        w_out_conv[0].astype(bf16),
        w_out_gla[0].astype(bf16),
        w_o[0].astype(bf16),
        norm2_w[0].reshape(1, D),
        router_w[0].T.astype(bf16),
        ws_gate[0].astype(bf16),
        ws_up[0].astype(bf16),
        ws_down[0].astype(bf16),
    )

    mod = _adaln(jnp.concatenate([c_prompt, c_sample], axis=0), w_ada[0], b_ada[0].reshape(1, 6 * D))
    mod4 = mod.reshape(bp + bs, 6, 1, D).transpose(1, 0, 2, 3)
    mod_p, mod_s = mod4[:, :bp], mod4[:, bp:]

    lt_p = 256
    x2_p, h2_p, lg_p, conv_p, gla_p = _mixer(x_prompt, mod_p, None, weights, 1, lt_p, GLA_CHUNK)
    nb_s = 8
    x2_s, h2_s, lg_s, conv_s, gla_s = _mixer(x_sample, mod_s, (state_conv[0], state_gla[0]), weights,
                                             nb_s, ls, math.gcd(ls, GLA_CHUNK))
    lg = jnp.concatenate([lg_p.transpose(1, 0, 2).reshape(E, tp), lg_s.transpose(1, 0, 2).reshape(E, ts)], axis=1)

    eidx, wts, rank_in_e, cnt = _route(lg, router_bias[0].reshape(E, 1))
    n_assign = t_all * TOPK
    nblk = (n_assign + E * (MOE_BM - 1) + MOE_BM - 1) // MOE_BM
    pos, blk_e = _plan(eidx, rank_in_e, cnt, -(-nblk // 128) * 128)

    xs = _dispatch(pos, h2_p, jnp.zeros((nblk * MOE_BM, D // 2), u32), 0)
    xs = _dispatch(pos, h2_s, xs, tp)
    ys = _experts(blk_e[0, :nblk], xs, we_gate[0], we_up[0], we_down[0])

    fw = final_norm_w.reshape(1, D)
    wts_t = wts.T
    y_prompt = _final(pos, wts_t, x2_p, ys, mod_p, fw, bp, lp, 0, 1, FINAL_TT)
    y_sample = _final(pos, wts_t, x2_s, ys, mod_s, fw, bs, ls, tp, FINAL_TT // ls, ls)
    return (y_prompt, y_sample, conv_p, gla_p, conv_s, gla_s)
```

```python
import functools
import math

import jax
import jax.numpy as jnp
from jax import lax
from jax.experimental import pallas as pl
from jax.experimental.pallas import tpu as pltpu

f32 = jnp.float32
bf16 = jnp.bfloat16
i32 = jnp.int32
i16 = jnp.int16

D = 1024
H = 4
DKH = 128
DVH = 256
DK = H * DKH
DV = H * DVH
RANK_PAD = 128
GATE_NORMALIZER = 16.0
GLA_CHUNK = 64
E = 64
TOPK = 8
NGROUPS = 8
TOPK_GROUPS = 4
DE = 256
ROUTED_SCALE = 2.5
EPS = 1e-6
NEG = float("-inf")

VMEM_LIMIT = 58 * 1024 * 1024
MOE_BM = 512
TILE = 256
MAX_TILES = 128
RUN_ALIGN = 16
SLOTS = TILE * TOPK + 1024
RUN_PIECES = (256, 128, 64, 32, 16)
TAIL_PIECES = (256, 128, 64, 32, 16)


def _dot(a, b):
    return jnp.dot(a, b, preferred_element_type=f32)


def _split3(x):
    hi = x.astype(bf16)
    r1 = x - hi.astype(f32)
    mid = r1.astype(bf16)
    lo = (r1 - mid.astype(f32)).astype(bf16)
    return hi, mid, lo


def _dot01(m01, x):
    hi, mid, lo = _split3(x)
    return _dot(m01, hi) + _dot(m01, mid) + _dot(m01, lo)


def _dotx01(x, m01):
    hi, mid, lo = _split3(x)
    return _dot(hi, m01) + _dot(mid, m01) + _dot(lo, m01)


def _rms(v, w):
    ms = jnp.mean(v * v, axis=-1, keepdims=True)
    return v * lax.rsqrt(ms + EPS) * w


def _const_spec(shape):
    n = len(shape)
    return pl.BlockSpec(shape, lambda *_: (0,) * n, pipeline_mode=pl.Buffered(1))


def _adaln_body(c_ref, w_ref, b_ref, o_ref):
    c = c_ref[...]
    a = (c * jax.nn.sigmoid(c)).astype(bf16)
    o_ref[...] = _dot(a, w_ref[...].astype(bf16)) + b_ref[...]


def _adaln(c_all, w_ada, b_ada):
    nrow = c_all.shape[0]
    tn = 1536
    return pl.pallas_call(
        _adaln_body,
        grid=(6 * D // tn,),
        in_specs=[pl.BlockSpec((nrow, D), lambda j: (0, 0)),
                  pl.BlockSpec((D, tn), lambda j: (0, j)),
                  pl.BlockSpec((1, tn), lambda j: (0, j))],
        out_specs=pl.BlockSpec((nrow, tn), lambda j: (0, j)),
        out_shape=jax.ShapeDtypeStruct((nrow, 6 * D), f32),
        compiler_params=pltpu.CompilerParams(dimension_semantics=("arbitrary",), vmem_limit_bytes=VMEM_LIMIT),
        name="adaln",
    )(c_all, w_ada, b_ada)


def _mixer_body(nb, lt, chunk, has_state, *refs):
    if has_state:
        (x_ref, mod_ref, cs_ref, gs_ref, *rest) = refs
    else:
        (x_ref, mod_ref, *rest) = refs
        cs_ref = gs_ref = None
    (n1_ref, wmain_ref, wz_ref, wu_ref, wgk_ref, bgk_ref, cw_ref, gnw_ref, woc_ref, wog_ref, wo_ref,
     n2_ref, rwt_ref, wsg_ref, wsu_ref, wsd_ref,
     x2_ref, h2_ref, lg_ref, nc_ref, ng_ref, carry_ref, st_ref, og_ref) = rest
    m = nb * lt
    i = pl.program_id(1)
    last = pl.num_programs(1) - 1

    def rows(v):
        w = v.shape[-1]
        if nb == 1:
            return v.reshape(1, w)
        return jnp.broadcast_to(v, (nb, lt, w)).reshape(m, w)

    @pl.when(i == 0)
    def _():
        if has_state:
            cs = cs_ref[...]
            carry_ref[0] = cs[:, 0:1, :]
            carry_ref[1] = cs[:, 1:2, :]
            for hh in range(H):
                st_ref[hh] = jnp.swapaxes(gs_ref[:, hh], 1, 2)
        else:
            carry_ref[...] = jnp.zeros_like(carry_ref)
            st_ref[...] = jnp.zeros_like(st_ref)

    x = x_ref[...].reshape(m, D)
    sh1, sc1, g1, sh2, sc2, g2 = [rows(mod_ref[j]) for j in range(6)]
    hb = (_rms(x, n1_ref[...]) * (1.0 + sc1) + sh1).astype(bf16)

    def proj(lo, hi):
        return _dot(hb, wmain_ref[:, lo:hi])

    pre = proj(D, 2 * D) * proj(2 * D, 3 * D)
    c0 = rows(carry_ref[0])
    c1 = rows(carry_ref[1])
    l_idx = lax.broadcasted_iota(i32, (m, D), 0) & (lt - 1)
    r1 = pltpu.roll(pre, 1, 0)
    r2 = pltpu.roll(pre, 2, 0)
    prev1 = jnp.where(l_idx == 0, c1, r1)
    prev2 = jnp.where(l_idx == 0, c0, jnp.where(l_idx == 1, c1, r2))
    cw = cw_ref[...]
    conv = cw[0:1] * prev2 + cw[1:2] * prev1 + cw[2:3] * pre
    y_a = _dot((proj(0, D) * conv).astype(bf16), woc_ref[...])
    pre3 = pre.reshape(nb, lt, D)
    tail = pre3[:, lt - 2:lt, :]
    carry_ref[0] = tail[:, 0:1, :]
    carry_ref[1] = tail[:, 1:2, :]
    nc_ref[0] = tail

    q = proj(3 * D, 3 * D + DK)
    k = proj(3 * D + DK, 3 * D + 2 * DK)
    v = proj(4 * D, 5 * D)
    g = proj(5 * D, 6 * D)
    z = _dot(hb, wz_ref[...]).astype(bf16)
    pa = _dot(z, wgk_ref[...]) + bgk_ref[...]
    la = (jnp.minimum(pa, 0.0) - jnp.log1p(jnp.exp(-jnp.abs(pa)))) / GATE_NORMALIZER
    rr = lax.broadcasted_iota(i32, (m, m), 0)
    cc = lax.broadcasted_iota(i32, (m, m), 1)
    same_chunk = (rr & ~(chunk - 1)) == (cc & ~(chunk - 1))
    tri = (same_chunk & (cc <= rr)).astype(bf16)
    bcum = _dot01(tri, la)
    ngrp = m // chunk
    bend = bcum.reshape(ngrp, chunk, DK)[:, chunk - 1:chunk, :]
    bend_rows = jnp.broadcast_to(bend, (ngrp, chunk, DK)).reshape(m, DK)
    q_dec = (q * (DKH ** -0.5) * jnp.exp(bcum)).astype(bf16).reshape(nb, lt, DK)
    k_dec = (k * jnp.exp(-bcum)).astype(bf16).reshape(nb, lt, DK)
    k_end = (k * jnp.exp(bend_rows - bcum)).astype(bf16).reshape(nb, lt, DK)
    a_end = jnp.exp(bend_rows).reshape(nb, lt, DK)
    v3 = v.astype(bf16).reshape(nb, lt, DV)
    g3 = g.reshape(nb, lt, DV)
    causal = lax.broadcasted_iota(i32, (chunk, chunk), 1) <= lax.broadcasted_iota(i32, (chunk, chunk), 0)
    gnw = gnw_ref[...]
    for j in range(lt // chunk):
        js = slice(j * chunk, (j + 1) * chunk)
        for hh in range(H):
            ks = slice(hh * DKH, (hh + 1) * DKH)
            vs = slice(hh * DVH, (hh + 1) * DVH)
            qd, kd, ke, vv = q_dec[:, js, ks], k_dec[:, js, ks], k_end[:, js, ks], v3[:, js, vs]
            st = st_ref[hh]
            o = jnp.einsum('bcd,bed->bce', qd, st.astype(bf16), preferred_element_type=f32)
            sc = jnp.einsum('bcd,bsd->bcs', qd, kd, preferred_element_type=f32)
            sc = jnp.where(causal, sc, 0.0).astype(bf16)
            o = o + jnp.einsum('bcs,bse->bce', sc, vv, preferred_element_type=f32)
            a_row = a_end[:, j * chunk:j * chunk + 1, ks]
            st_ref[hh] = st * a_row + jnp.einsum('bse,bsd->bed', vv, ke, preferred_element_type=f32)
            on = _rms(o, gnw)
            gh = g3[:, js, vs]
            og_ref[:, js, vs] = (on * (gh * jax.nn.sigmoid(gh))).astype(bf16)

    @pl.when(i == last)
    def _():
        for hh in range(H):
            ng_ref[0, :, hh] = jnp.swapaxes(st_ref[hh], 1, 2)

    y_b = _dot(og_ref[...].reshape(m, DV), wog_ref[...])
    u_a = _dot(hb, wu_ref[:, 0:D])
    u_b = _dot(hb, wu_ref[:, D:2 * D])
    merged = (jax.nn.sigmoid(u_a) * y_a + jax.nn.sigmoid(u_b) * y_b).astype(bf16)
    x1 = x + g1 * _dot(merged, wo_ref[...])

    h2 = (_rms(x1, n2_ref[...]) * (1.0 + sc2) + sh2).astype(bf16)
    sg = _dot(h2, wsg_ref[...])
    act = (sg * jax.nn.sigmoid(sg) * _dot(h2, wsu_ref[...])).astype(bf16)
    x2_ref[...] = x1 + g2 * _dot(act, wsd_ref[...])
    h2_ref[...] = h2
    lg_ref[0] = lax.dot_general(rwt_ref[...], h2, (((1,), (1,)), ((), ())), preferred_element_type=f32)


def _mixer(x, mod4, states, weights, nb, lt, chunk):
    bsz, seqlen, _ = x.shape
    has_state = states is not None
    m = nb * lt
    n_lt = seqlen // lt
    nsteps = (bsz // nb) * n_lt
    t_all = bsz * seqlen

    def tok_blk(b, i):
        return b * n_lt + i

    in_specs = [pl.BlockSpec((nb, lt, D), lambda b, i: (b, i, 0)),
                pl.BlockSpec((6, nb, 1, D), lambda b, i: (0, b, 0, 0))]
    args = [x, mod4]
    if has_state:
        in_specs += [pl.BlockSpec((nb, 2, D), lambda b, i: (b, 0, 0)),
                     pl.BlockSpec((nb, H, DKH, DVH), lambda b, i: (b, 0, 0, 0))]
        args += list(states)
    in_specs += [_const_spec(w.shape) for w in weights]
    args += list(weights)
    out_specs = [pl.BlockSpec((m, D), lambda b, i: (tok_blk(b, i), 0)),
                 pl.BlockSpec((m, D), lambda b, i: (tok_blk(b, i), 0)),
                 pl.BlockSpec((1, E, m), lambda b, i: (b * n_lt + i, 0, 0)),
                 pl.BlockSpec((1, nb, 2, D), lambda b, i: (0, b, 0, 0)),
                 pl.BlockSpec((1, nb, H, DKH, DVH), lambda b, i: (0, b, 0, 0, 0))]
    out_shape = [jax.ShapeDtypeStruct((t_all, D), f32),
                 jax.ShapeDtypeStruct((t_all, D), bf16),
                 jax.ShapeDtypeStruct((nsteps, E, m), f32),
                 jax.ShapeDtypeStruct((1, bsz, 2, D), f32),
                 jax.ShapeDtypeStruct((1, bsz, H, DKH, DVH), f32)]
    return pl.pallas_call(
        functools.partial(_mixer_body, nb, lt, chunk, has_state),
        grid=(bsz // nb, n_lt),
        in_specs=in_specs,
        out_specs=out_specs,
        out_shape=out_shape,
        scratch_shapes=[pltpu.VMEM((2, nb, 1, D), f32),
                        pltpu.VMEM((H, nb, DVH, DKH), f32),
                        pltpu.VMEM((nb, lt, DV), bf16)],
        compiler_params=pltpu.CompilerParams(dimension_semantics=("arbitrary", "arbitrary"),
                                             vmem_limit_bytes=VMEM_LIMIT),
        name="mixer_state" if has_state else "mixer_prompt",
    )(*args)


def _route_body(lg_ref, bias_ref, eidx_ref, w_ref, lrank_ref, cnt_ref):
    step = pl.program_id(0)
    tl = lg_ref.shape[1]

    scores = jax.nn.sigmoid(lg_ref[...])
    choice = scores + bias_ref[...]
    gsz = E // NGROUPS
    grp = choice.reshape(NGROUPS, gsz, tl)
    mi = lax.broadcasted_iota(i32, (NGROUPS, gsz, tl), 1)
    m1 = jnp.max(grp, axis=1, keepdims=True)
    first = jnp.min(jnp.where(grp == m1, mi, gsz), axis=1, keepdims=True)
    m2 = jnp.max(jnp.where(mi == first, NEG, grp), axis=1, keepdims=True)
    gscore = (m1 + m2).reshape(NGROUPS, tl)
    gi = lax.broadcasted_iota(i32, (NGROUPS, tl), 0)
    gsel = jnp.zeros((NGROUPS, tl), f32)
    work = gscore
    for _ in range(TOPK_GROUPS):
        mx = jnp.max(work, axis=0, keepdims=True)
        f = jnp.min(jnp.where(work == mx, gi, NGROUPS), axis=0, keepdims=True)
        hit = gi == f
        gsel = jnp.where(hit, 1.0, gsel)
        work = jnp.where(hit, NEG, work)
    emask = jnp.broadcast_to(gsel.reshape(NGROUPS, 1, tl), (NGROUPS, gsz, tl)).reshape(E, tl)
    masked = jnp.where(emask > 0.0, choice, NEG)
    ei = lax.broadcasted_iota(i32, (E, tl), 0)
    sel = jnp.zeros((E, tl), f32)
    hits, ws = [], []
    for kk in range(TOPK):
        mx = jnp.max(masked, axis=0, keepdims=True)
        f = jnp.min(jnp.where(masked == mx, ei, E), axis=0, keepdims=True)
        hit = ei == f
        ws.append(jnp.sum(jnp.where(hit, scores, 0.0), axis=0, keepdims=True))
        masked = jnp.where(hit, NEG, masked)
        sel = jnp.where(hit, 1.0, sel)
        hits.append(hit)
        eidx_ref[kk:kk + 1, :] = f
    wsum = ws[0]
    for t in ws[1:]:
        wsum = wsum + t
    for kk in range(TOPK):
        w_ref[kk:kk + 1, :] = ws[kk] / wsum * ROUTED_SCALE

    ui = lax.broadcasted_iota(i32, (tl, tl), 0)
    uj = lax.broadcasted_iota(i32, (tl, tl), 1)
    upper = (ui <= uj).astype(bf16)
    lrank = _dot(sel.astype(bf16), upper) - sel
    for kk in range(TOPK):
        lrank_ref[kk:kk + 1, :] = jnp.sum(jnp.where(hits[kk], lrank, 0.0), axis=0, keepdims=True).astype(i32)

    @pl.when(step == 0)
    def _():
        cnt_ref[...] = jnp.zeros_like(cnt_ref)
    lane = lax.broadcasted_iota(i32, cnt_ref.shape, 1)
    cnt_ref[...] = jnp.where(lane == step, jnp.sum(sel, axis=1, keepdims=True), cnt_ref[...])


def _route(logits_t, bias_col):
    t_all = logits_t.shape[1]
    tl = TILE
    assert t_all // tl <= MAX_TILES
    tok = lambda s: (0, s)
    return pl.pallas_call(
        _route_body,
        grid=(t_all // tl,),
        in_specs=[pl.BlockSpec((E, tl), tok), pl.BlockSpec((E, 1), lambda s: (0, 0))],
        out_specs=[pl.BlockSpec((TOPK, tl), tok), pl.BlockSpec((TOPK, tl), tok), pl.BlockSpec((TOPK, tl), tok),
                   pl.BlockSpec((E, MAX_TILES), lambda s: (0, 0))],
        out_shape=[jax.ShapeDtypeStruct((TOPK, t_all), i32), jax.ShapeDtypeStruct((TOPK, t_all), f32),
                   jax.ShapeDtypeStruct((TOPK, t_all), i32), jax.ShapeDtypeStruct((E, MAX_TILES), f32)],
        compiler_params=pltpu.CompilerParams(dimension_semantics=("arbitrary",), vmem_limit_bytes=VMEM_LIMIT),
        name="route",
    )(logits_t, bias_col)


def _plan_body(nblk, eidx_ref, lrank_ref, cnt_ref, slot_ref, n8_ref, l_ref, g_ref, blk_ref):
    i = pl.program_id(0)
    tl = eidx_ref.shape[1]
    cnt = cnt_ref[...]
    n8 = jnp.floor((cnt + (RUN_ALIGN - 1)) * (1.0 / RUN_ALIGN)) * RUN_ALIGN
    ti = lax.broadcasted_iota(i32, (MAX_TILES, MAX_TILES), 0)
    tj = lax.broadcasted_iota(i32, (MAX_TILES, MAX_TILES), 1)
    before = _dotx01(n8, (ti < tj).astype(bf16))
    region = jnp.sum(n8, axis=1, keepdims=True)
    nblocks = jnp.floor((region + (MOE_BM - 1)) * (1.0 / MOE_BM))
    ri = lax.broadcasted_iota(i32, (E, E), 0)
    ci = lax.broadcasted_iota(i32, (E, E), 1)
    below = (ci < ri).astype(bf16)
    bstart = _dot01(below, jnp.broadcast_to(nblocks, (E, MAX_TILES)))
    loff = _dot01(below, n8)
    n8_ref[...] = n8.astype(i32)
    l_ref[...] = loff.astype(i32)
    g_ref[...] = (bstart * MOE_BM + before).astype(i32)

    lane = lax.broadcasted_iota(i32, (E, MAX_TILES), 1)
    lcol = jnp.sum(jnp.where(lane == i, loff, 0.0), axis=1, keepdims=True)
    ei = lax.broadcasted_iota(i32, (E, tl), 0)
    for kk in range(TOPK):
        hit = ei == eidx_ref[kk:kk + 1, :]
        start = jnp.sum(jnp.where(hit, lcol, 0.0), axis=0, keepdims=True)
        slot_ref[kk:kk + 1, :] = start.astype(i32) + lrank_ref[kk:kk + 1, :]

    @pl.when(i == 0)
    def _():
        bend = bstart[:, 0:1] + nblocks
        bi = lax.broadcasted_iota(i32, (E, nblk), 1).astype(f32)
        owner = jnp.sum((bend <= bi).astype(f32), axis=0, keepdims=True)
        used = jnp.max(bend, axis=0, keepdims=True)
        owner = jnp.where(bi[0:1] < used, jnp.minimum(owner, E - 1.0), -1.0)
        blk_ref[...] = owner.astype(i32)


def _plan(eidx, lrank, cnt, nblk):
    t_all = eidx.shape[1]
    tl = TILE
    tok = lambda s: (0, s)
    table = pl.BlockSpec((E, MAX_TILES), lambda s: (0, 0))
    return pl.pallas_call(
        functools.partial(_plan_body, nblk),
        grid=(t_all // tl,),
        in_specs=[pl.BlockSpec((TOPK, tl), tok), pl.BlockSpec((TOPK, tl), tok), table],
        out_specs=[pl.BlockSpec((TOPK, tl), tok), table, table, table, pl.BlockSpec((1, nblk), lambda s: (0, 0))],
        out_shape=[jax.ShapeDtypeStruct((TOPK, t_all), i32)] + [jax.ShapeDtypeStruct((E, MAX_TILES), i32)] * 3
        + [jax.ShapeDtypeStruct((1, nblk), i32)],
        compiler_params=pltpu.CompilerParams(dimension_semantics=("arbitrary",), vmem_limit_bytes=VMEM_LIMIT),
        name="plan",
    )(eidx, lrank, cnt)


def _for_each_piece(total, pieces, fn):
    for j, b in enumerate(pieces):
        @pl.when((total & b) != 0)
        def _():
            fn(pl.multiple_of(total & ~(2 * b - 1), pieces[-1]), b, j)


def _for_each_run(n8_ref, l_ref, g_ref, tile, fn):
    def per_expert(e, c):
        lo = pl.multiple_of(l_ref[e, tile], RUN_ALIGN)
        go = pl.multiple_of(g_ref[e, tile], RUN_ALIGN)
        _for_each_piece(n8_ref[e, tile], RUN_PIECES, lambda off, b, j: fn(lo + off, go + off, b, j))
        return c
    lax.fori_loop(0, E, per_expert, 0)


def _start(copy, j):
    copy.start(priority=j % 2)


def _dispatch_body(tiles_a, n8_ref, l_ref, g_ref, slot_ref, ha_ref, hb_ref, xs_ref, buf_ref, zero_ref, sem, zsem):
    tile = pl.program_id(0)
    i = tile
    nsteps = pl.num_programs(0)
    cur = i % 2

    slot = slot_ref[...].astype(i16)
    srow = lax.broadcasted_iota(i32, (SLOTS, TILE), 0).astype(i16)
    onehot = jnp.zeros((SLOTS, TILE), bf16)
    for kk in range(TOPK):
        onehot = jnp.where(srow == slot[kk:kk + 1, :], jnp.ones((), bf16), onehot)
    h = jnp.where(tile < tiles_a, ha_ref[...], hb_ref[...])
    buf_ref[cur] = _dot(onehot, h).astype(bf16)

    def run_copy(sl):
        return lambda srow_, grow, b: pltpu.make_async_copy(
            buf_ref.at[sl, pl.ds(srow_, b)], xs_ref.at[pl.ds(grow, b)], sem.at[sl])

    _for_each_run(n8_ref, l_ref, g_ref, tile, lambda s, g, b, j: _start(run_copy(cur)(s, g, b), j))

    @pl.when(i > 0)
    def _():
        _for_each_run(n8_ref, l_ref, g_ref, tile - 1, lambda s, g, b, j: run_copy(1 - cur)(s, g, b).wait())

    @pl.when(i == nsteps - 1)
    def _():
        _for_each_run(n8_ref, l_ref, g_ref, tile, lambda s, g, b, j: run_copy(cur)(s, g, b).wait())
        zero_ref[...] = jnp.zeros_like(zero_ref)

        def tails(fn):
            def per_expert(e, c):
                end = pl.multiple_of(g_ref[e, tile] + n8_ref[e, tile], RUN_ALIGN)
                _for_each_piece((-end) & (MOE_BM - 1), TAIL_PIECES, lambda off, b, j: fn(pltpu.make_async_copy(
                    zero_ref.at[pl.ds(0, b)], xs_ref.at[pl.ds(end + off, b)], zsem)))
                return c
            lax.fori_loop(0, E, per_expert, 0)
        tails(lambda cp: cp.start())
        tails(lambda cp: cp.wait())

        used = (g_ref[E - 1, tile] + n8_ref[E - 1, tile] + (MOE_BM - 1)) // MOE_BM

        def spare(fn):
            def per_block(blk, c):
                fn(pltpu.make_async_copy(zero_ref, xs_ref.at[pl.ds(pl.multiple_of(blk * MOE_BM, MOE_BM), MOE_BM)],
                                         zsem))
                return c
            lax.fori_loop(used, xs_ref.shape[0] // MOE_BM, per_block, 0)
        spare(lambda cp: cp.start())
        spare(lambda cp: cp.wait())


def _dispatch(tables, slot, h2_a, h2_b, nrows):
    tiles_a, tiles_b = h2_a.shape[0] // TILE, h2_b.shape[0] // TILE
    return pl.pallas_call(
        functools.partial(_dispatch_body, tiles_a),
        grid_spec=pltpu.PrefetchScalarGridSpec(
            num_scalar_prefetch=len(tables),
            grid=(tiles_a + tiles_b,),
            in_specs=[pl.BlockSpec((TOPK, TILE), lambda i, *_: (0, i)),
                      pl.BlockSpec((TILE, D), lambda i, *_: (jnp.minimum(i, tiles_a - 1), 0)),
                      pl.BlockSpec((TILE, D), lambda i, *_: (jnp.maximum(i - tiles_a, 0), 0))],
            out_specs=pl.BlockSpec(memory_space=pl.ANY),
            scratch_shapes=[pltpu.VMEM((2, SLOTS, D), bf16), pltpu.VMEM((MOE_BM, D), bf16),
                            pltpu.SemaphoreType.DMA((2,)), pltpu.SemaphoreType.DMA]),
        out_shape=jax.ShapeDtypeStruct((nrows, D), bf16),
        compiler_params=pltpu.CompilerParams(dimension_semantics=("arbitrary",), vmem_limit_bytes=VMEM_LIMIT),
        name="dispatch",
    )(*tables, slot, h2_a, h2_b)


def _expert_body(blk_ref, xs_ref, wg_ref, wu_ref, wd_ref, o_ref):
    i = pl.program_id(0)

    @pl.when(blk_ref[i] >= 0)
    def _():
        xb = xs_ref[...]
        hg = _dot(xb, wg_ref[0].astype(bf16))
        act = (hg * jax.nn.sigmoid(hg) * _dot(xb, wu_ref[0].astype(bf16))).astype(bf16)
        o_ref[...] = _dot(act, wd_ref[0].astype(bf16)).astype(bf16)

    @pl.when(blk_ref[i] < 0)
    def _():
        o_ref[...] = jnp.zeros_like(o_ref)


def _experts(blk_e, xs, we_gate, we_up, we_down):
    nblk = blk_e.shape[0]
    ex = lambda i, blk: (jnp.maximum(blk[i], 0), 0, 0)
    return pl.pallas_call(
        _expert_body,
        grid_spec=pltpu.PrefetchScalarGridSpec(
            num_scalar_prefetch=1,
            grid=(nblk,),
            in_specs=[pl.BlockSpec((MOE_BM, D), lambda i, blk: (jnp.where(blk[i] >= 0, i, 0), 0)),
                      pl.BlockSpec((1, D, DE), ex), pl.BlockSpec((1, D, DE), ex), pl.BlockSpec((1, DE, D), ex)],
            out_specs=pl.BlockSpec((MOE_BM, D), lambda i, blk: (i, 0))),
        out_shape=jax.ShapeDtypeStruct((nblk * MOE_BM, D), bf16),
        compiler_params=pltpu.CompilerParams(dimension_semantics=("arbitrary",), vmem_limit_bytes=VMEM_LIMIT),
        name="experts",
    )(blk_e, xs, we_gate, we_up, we_down)


def _final_body(nb, lt, tile0, n8_ref, l_ref, g_ref, slot_ref, wt_ref, x2_ref, g2_ref, fw_ref, ys_ref, o_ref,
                buf_ref, sem):
    m = nb * lt
    n_inner = pl.num_programs(1)
    step = pl.program_id(0) * n_inner + pl.program_id(1)
    nsteps = pl.num_programs(0) * n_inner
    tile = tile0 + step
    cur = step % 2

    def run_copy(sl):
        return lambda srow, grow, b: pltpu.make_async_copy(
            ys_ref.at[pl.ds(grow, b)], buf_ref.at[sl, pl.ds(srow, b)], sem.at[sl])

    @pl.when(step == 0)
    def _():
        buf_ref[...] = jnp.zeros_like(buf_ref)
        _for_each_run(n8_ref, l_ref, g_ref, tile, lambda s, g, b, j: _start(run_copy(0)(s, g, b), j))

    @pl.when(step + 1 < nsteps)
    def _():
        _for_each_run(n8_ref, l_ref, g_ref, tile + 1, lambda s, g, b, j: _start(run_copy(1 - cur)(s, g, b), j))

    _for_each_run(n8_ref, l_ref, g_ref, tile, lambda s, g, b, j: run_copy(cur)(s, g, b).wait())

    slot = slot_ref[...]
    wt = wt_ref[...]
    scol = lax.broadcasted_iota(i32, (m, SLOTS), 1)
    wsel = jnp.zeros((m, SLOTS), f32)
    for kk in range(TOPK):
        wsel = jnp.where(scol == slot[:, kk:kk + 1], wt[:, kk:kk + 1], wsel)
    w_hi = wsel.astype(bf16)
    w_lo = (wsel - w_hi.astype(f32)).astype(bf16)
    yb = buf_ref[cur]
    routed = _dot(w_hi, yb) + _dot(w_lo, yb)
    g2 = g2_ref[0]
    if nb == 1:
        g2 = g2.reshape(1, D)
    else:
        g2 = jnp.broadcast_to(g2, (nb, lt, D)).reshape(m, D)
    y = x2_ref[...] + g2 * routed
    o_ref[...] = _rms(y, fw_ref[...]).reshape(nb, lt, D)


def _final(tables, slot_t, wts_t, x2, ys, mod4, fw, bsz, seqlen, row0, nb, lt):
    m = nb * lt
    assert m == TILE and row0 % TILE == 0
    n_lt = seqlen // lt
    tile0 = row0 // TILE
    tile = lambda b, i: tile0 + b * n_lt + i
    return pl.pallas_call(
        functools.partial(_final_body, nb, lt, tile0),
        grid_spec=pltpu.PrefetchScalarGridSpec(
            num_scalar_prefetch=len(tables),
            grid=(bsz // nb, n_lt),
            in_specs=[pl.BlockSpec((m, TOPK), lambda b, i, *_: (tile(b, i), 0)),
                      pl.BlockSpec((m, TOPK), lambda b, i, *_: (tile(b, i), 0)),
                      pl.BlockSpec((m, D), lambda b, i, *_: (b * n_lt + i, 0)),
                      pl.BlockSpec((1, nb, 1, D), lambda b, i, *_: (5, b, 0, 0)),
                      pl.BlockSpec((1, D), lambda b, i, *_: (0, 0)),
                      pl.BlockSpec(memory_space=pl.ANY)],
            out_specs=pl.BlockSpec((nb, lt, D), lambda b, i, *_: (b, i, 0)),
            scratch_shapes=[pltpu.VMEM((2, SLOTS, D), bf16), pltpu.SemaphoreType.DMA((2,))]),
        out_shape=jax.ShapeDtypeStruct((bsz, seqlen, D), f32),
        compiler_params=pltpu.CompilerParams(dimension_semantics=("arbitrary", "arbitrary"),
                                             vmem_limit_bytes=VMEM_LIMIT),
        name="final",
    )(*tables, slot_t, wts_t, x2, mod4, fw, ys)


def kernel(x_prompt, x_sample, state_conv, state_gla, c_prompt, c_sample, w_ada, b_ada, norm1_w, w_in, conv_w,
           w_gk, b_gk, gla_norm_w, w_out_conv, w_out_gla, w_o, norm2_w, router_w, router_bias, we_gate, we_up,
           we_down, ws_gate, ws_up, ws_down, final_norm_w):
    assert w_ada.shape[0] == 1, "single-layer step"
    bp, lp, _ = x_prompt.shape
    bs, ls, _ = x_sample.shape
    tp, ts = bp * lp, bs * ls
    t_all = tp + ts

    w_in0 = w_in[0]
    n_main = 3 * D + 2 * DK + 2 * DV
    rank = w_gk.shape[1]
    weights = (
        norm1_w[0].reshape(1, D),
        w_in0[:, :n_main].astype(bf16),
        jnp.pad(w_in0[:, n_main:n_main + rank], ((0, 0), (0, RANK_PAD - rank))).astype(bf16),
        w_in0[:, n_main + rank:].astype(bf16),
        jnp.pad(w_gk[0], ((0, RANK_PAD - rank), (0, 0))).astype(bf16),
        b_gk[0].reshape(1, DK),
        conv_w[0],
        gla_norm_w[0].reshape(1, DVH),
        w_out_conv[0].astype(bf16),
        w_out_gla[0].astype(bf16),
        w_o[0].astype(bf16),
        norm2_w[0].reshape(1, D),
        router_w[0].T.astype(bf16),
        ws_gate[0].astype(bf16),
        ws_up[0].astype(bf16),
        ws_down[0].astype(bf16),
    )

    mod = _adaln(jnp.concatenate([c_prompt, c_sample], axis=0), w_ada[0], b_ada[0].reshape(1, 6 * D))
    mod4 = mod.reshape(bp + bs, 6, 1, D).transpose(1, 0, 2, 3)
    mod_p, mod_s = mod4[:, :bp], mod4[:, bp:]

    lt_p = 256
    x2_p, h2_p, lg_p, conv_p, gla_p = _mixer(x_prompt, mod_p, None, weights, 1, lt_p, GLA_CHUNK)
    nb_s = 8
    x2_s, h2_s, lg_s, conv_s, gla_s = _mixer(x_sample, mod_s, (state_conv[0], state_gla[0]), weights,
                                             nb_s, ls, math.gcd(ls, GLA_CHUNK))
    lg = jnp.concatenate([lg_p.transpose(1, 0, 2).reshape(E, tp), lg_s.transpose(1, 0, 2).reshape(E, ts)], axis=1)

    eidx, wts, lrank, cnt = _route(lg, router_bias[0].reshape(E, 1))
    n_tiles = t_all // TILE
    max_rows = t_all * TOPK + n_tiles * E * (RUN_ALIGN - 1) + E * (MOE_BM - RUN_ALIGN)
    nblk = -(-max_rows // MOE_BM)
    slot, n8, loff, goff, blk_e = _plan(eidx, lrank, cnt, -(-nblk // 128) * 128)
    tables = (n8, loff, goff)

    xs = _dispatch(tables, slot, h2_p, h2_s, nblk * MOE_BM)
    ys = _experts(blk_e[0, :nblk], xs, we_gate[0], we_up[0], we_down[0])

    fw = final_norm_w.reshape(1, D)
    slot_t, wts_t = slot.T, wts.T
    y_prompt = _final(tables, slot_t, wts_t, x2_p, ys, mod_p, fw, bp, lp, 0, 1, TILE)
    y_sample = _final(tables, slot_t, wts_t, x2_s, ys, mod_s, fw, bs, ls, tp, TILE // ls, ls)
    return (y_prompt, y_sample, conv_p, gla_p, conv_s, gla_s)
```

```python
import functools
import math

import jax
import jax.numpy as jnp
from jax import lax
from jax.experimental import pallas as pl
from jax.experimental.pallas import tpu as pltpu

f32 = jnp.float32
bf16 = jnp.bfloat16
i32 = jnp.int32
i16 = jnp.int16

D = 1024
H = 4
DKH = 128
DVH = 256
DK = H * DKH
DV = H * DVH
RANK_PAD = 128
GATE_NORMALIZER = 16.0
GLA_CHUNK = 64
E = 64
TOPK = 8
NGROUPS = 8
TOPK_GROUPS = 4
DE = 256
ROUTED_SCALE = 2.5
EPS = 1e-6
NEG = float("-inf")

VMEM_LIMIT = 58 * 1024 * 1024
MOE_BM = 512
TILE = 256
MAX_TILES = 128
RUN_ALIGN = 16
SLOTS = TILE * TOPK + 1024
RUN_PIECES = (256, 128, 64, 32, 16)
TAIL_PIECES = (256, 128, 64, 32, 16)
WAIT_PIECES = (2048, 1024, 512, 256, 128, 64, 32, 16)

def _dot(a, b):
    return jnp.dot(a, b, preferred_element_type=f32)


def _split3(x):
    hi = x.astype(bf16)
    r1 = x - hi.astype(f32)
    mid = r1.astype(bf16)
    lo = (r1 - mid.astype(f32)).astype(bf16)
    return hi, mid, lo


def _dot01(m01, x):
    hi, mid, lo = _split3(x)
    return _dot(m01, hi) + _dot(m01, mid) + _dot(m01, lo)


def _dotx01(x, m01):
    hi, mid, lo = _split3(x)
    return _dot(hi, m01) + _dot(mid, m01) + _dot(lo, m01)


def _rms(v, w):
    ms = jnp.mean(v * v, axis=-1, keepdims=True)
    return v * lax.rsqrt(ms + EPS) * w


def _const_spec(shape):
    n = len(shape)
    return pl.BlockSpec(shape, lambda *_: (0,) * n, pipeline_mode=pl.Buffered(1))


def _adaln_body(c_ref, w_ref, b_ref, o_ref):
    c = c_ref[...]
    a = (c * jax.nn.sigmoid(c)).astype(bf16)
    o_ref[...] = _dot(a, w_ref[...].astype(bf16)) + b_ref[...]


def _adaln(c_all, w_ada, b_ada):
    nrow = c_all.shape[0]
    tn = 1536
    return pl.pallas_call(
        _adaln_body,
        grid=(6 * D // tn,),
        in_specs=[pl.BlockSpec((nrow, D), lambda j: (0, 0)),
                  pl.BlockSpec((D, tn), lambda j: (0, j)),
                  pl.BlockSpec((1, tn), lambda j: (0, j))],
        out_specs=pl.BlockSpec((nrow, tn), lambda j: (0, j)),
        out_shape=jax.ShapeDtypeStruct((nrow, 6 * D), f32),
        compiler_params=pltpu.CompilerParams(dimension_semantics=("arbitrary",), vmem_limit_bytes=VMEM_LIMIT),
        name="adaln",
    )(c_all, w_ada, b_ada)


def _mixer_body(nb, lt, chunk, has_state, *refs):
    if has_state:
        (x_ref, mod_ref, cs_ref, gs_ref, *rest) = refs
    else:
        (x_ref, mod_ref, *rest) = refs
        cs_ref = gs_ref = None
    (n1_ref, wmain_ref, wz_ref, wu_ref, wgk_ref, bgk_ref, cw_ref, gnw_ref, woc_ref, wog_ref, wo_ref,
     n2_ref, rwt_ref, wsg_ref, wsu_ref, wsd_ref,
     x2_ref, h2_ref, lg_ref, nc_ref, ng_ref, carry_ref, st_ref, og_ref) = rest
    m = nb * lt
    i = pl.program_id(1)
    last = pl.num_programs(1) - 1

    def rows(v):
        w = v.shape[-1]
        if nb == 1:
            return v.reshape(1, w)
        return jnp.broadcast_to(v, (nb, lt, w)).reshape(m, w)

    @pl.when(i == 0)
    def _():
        if has_state:
            cs = cs_ref[...]
            carry_ref[0] = cs[:, 0:1, :]
            carry_ref[1] = cs[:, 1:2, :]
            for hh in range(H):
                st_ref[hh] = jnp.swapaxes(gs_ref[:, hh], 1, 2)
        else:
            carry_ref[...] = jnp.zeros_like(carry_ref)
            st_ref[...] = jnp.zeros_like(st_ref)

    x = x_ref[...].reshape(m, D)
    sh1, sc1, g1, sh2, sc2, g2 = [rows(mod_ref[j]) for j in range(6)]
    hb = (_rms(x, n1_ref[...]) * (1.0 + sc1) + sh1).astype(bf16)

    def proj(lo, hi):
        return _dot(hb, wmain_ref[:, lo:hi])

    pre = proj(D, 2 * D) * proj(2 * D, 3 * D)
    c0 = rows(carry_ref[0])
    c1 = rows(carry_ref[1])
    l_idx = lax.broadcasted_iota(i32, (m, D), 0) & (lt - 1)
    r1 = pltpu.roll(pre, 1, 0)
    r2 = pltpu.roll(pre, 2, 0)
    prev1 = jnp.where(l_idx == 0, c1, r1)
    prev2 = jnp.where(l_idx == 0, c0, jnp.where(l_idx == 1, c1, r2))
    cw = cw_ref[...]
    conv = cw[0:1] * prev2 + cw[1:2] * prev1 + cw[2:3] * pre
    y_a = _dot((proj(0, D) * conv).astype(bf16), woc_ref[...])
    pre3 = pre.reshape(nb, lt, D)
    tail = pre3[:, lt - 2:lt, :]
    carry_ref[0] = tail[:, 0:1, :]
    carry_ref[1] = tail[:, 1:2, :]
    nc_ref[0] = tail

    q = proj(3 * D, 3 * D + DK)
    k = proj(3 * D + DK, 3 * D + 2 * DK)
    v = proj(4 * D, 5 * D)
    g = proj(5 * D, 6 * D)
    z = _dot(hb, wz_ref[...]).astype(bf16)
    pa = _dot(z, wgk_ref[...]) + bgk_ref[...]
    la = (jnp.minimum(pa, 0.0) - jnp.log1p(jnp.exp(-jnp.abs(pa)))) / GATE_NORMALIZER
    rr = lax.broadcasted_iota(i32, (m, m), 0)
    cc = lax.broadcasted_iota(i32, (m, m), 1)
    same_chunk = (rr & ~(chunk - 1)) == (cc & ~(chunk - 1))
    tri = (same_chunk & (cc <= rr)).astype(bf16)
    bcum = _dot01(tri, la)
    ngrp = m // chunk
    bend = bcum.reshape(ngrp, chunk, DK)[:, chunk - 1:chunk, :]
    bend_rows = jnp.broadcast_to(bend, (ngrp, chunk, DK)).reshape(m, DK)
    q_dec = (q * (DKH ** -0.5) * jnp.exp(bcum)).astype(bf16).reshape(nb, lt, DK)
    k_dec = (k * jnp.exp(-bcum)).astype(bf16).reshape(nb, lt, DK)
    k_end = (k * jnp.exp(bend_rows - bcum)).astype(bf16).reshape(nb, lt, DK)
    a_end = jnp.exp(bend_rows).reshape(nb, lt, DK)
    v3 = v.astype(bf16).reshape(nb, lt, DV)
    g3 = g.reshape(nb, lt, DV)
    causal = lax.broadcasted_iota(i32, (chunk, chunk), 1) <= lax.broadcasted_iota(i32, (chunk, chunk), 0)
    gnw = gnw_ref[...]
    for j in range(lt // chunk):
        js = slice(j * chunk, (j + 1) * chunk)
        for hh in range(H):
            ks = slice(hh * DKH, (hh + 1) * DKH)
            vs = slice(hh * DVH, (hh + 1) * DVH)
            qd, kd, ke, vv = q_dec[:, js, ks], k_dec[:, js, ks], k_end[:, js, ks], v3[:, js, vs]
            st = st_ref[hh]
            o = jnp.einsum('bcd,bed->bce', qd, st.astype(bf16), preferred_element_type=f32)
            sc = jnp.einsum('bcd,bsd->bcs', qd, kd, preferred_element_type=f32)
            sc = jnp.where(causal, sc, 0.0).astype(bf16)
            o = o + jnp.einsum('bcs,bse->bce', sc, vv, preferred_element_type=f32)
            a_row = a_end[:, j * chunk:j * chunk + 1, ks]
            st_ref[hh] = st * a_row + jnp.einsum('bse,bsd->bed', vv, ke, preferred_element_type=f32)
            on = _rms(o, gnw)
            gh = g3[:, js, vs]
            og_ref[:, js, vs] = (on * (gh * jax.nn.sigmoid(gh))).astype(bf16)

    @pl.when(i == last)
    def _():
        for hh in range(H):
            ng_ref[0, :, hh] = jnp.swapaxes(st_ref[hh], 1, 2)

    y_b = _dot(og_ref[...].reshape(m, DV), wog_ref[...])
    u_a = _dot(hb, wu_ref[:, 0:D])
    u_b = _dot(hb, wu_ref[:, D:2 * D])
    merged = (jax.nn.sigmoid(u_a) * y_a + jax.nn.sigmoid(u_b) * y_b).astype(bf16)
    x1 = x + g1 * _dot(merged, wo_ref[...])

    h2 = (_rms(x1, n2_ref[...]) * (1.0 + sc2) + sh2).astype(bf16)
    sg = _dot(h2, wsg_ref[...])
    act = (sg * jax.nn.sigmoid(sg) * _dot(h2, wsu_ref[...])).astype(bf16)
    x2_ref[...] = x1 + g2 * _dot(act, wsd_ref[...])
    h2_ref[...] = h2
    lg_ref[0] = lax.dot_general(rwt_ref[...], h2, (((1,), (1,)), ((), ())), preferred_element_type=f32)


def _mixer(x, mod4, states, weights, nb, lt, chunk):
    bsz, seqlen, _ = x.shape
    has_state = states is not None
    m = nb * lt
    n_lt = seqlen // lt
    nsteps = (bsz // nb) * n_lt
    t_all = bsz * seqlen

    def tok_blk(b, i):
        return b * n_lt + i

    in_specs = [pl.BlockSpec((nb, lt, D), lambda b, i: (b, i, 0)),
                pl.BlockSpec((6, nb, 1, D), lambda b, i: (0, b, 0, 0))]
    args = [x, mod4]
    if has_state:
        in_specs += [pl.BlockSpec((nb, 2, D), lambda b, i: (b, 0, 0)),
                     pl.BlockSpec((nb, H, DKH, DVH), lambda b, i: (b, 0, 0, 0))]
        args += list(states)
    in_specs += [_const_spec(w.shape) for w in weights]
    args += list(weights)
    out_specs = [pl.BlockSpec((m, D), lambda b, i: (tok_blk(b, i), 0)),
                 pl.BlockSpec((m, D), lambda b, i: (tok_blk(b, i), 0)),
                 pl.BlockSpec((1, E, m), lambda b, i: (b * n_lt + i, 0, 0)),
                 pl.BlockSpec((1, nb, 2, D), lambda b, i: (0, b, 0, 0)),
                 pl.BlockSpec((1, nb, H, DKH, DVH), lambda b, i: (0, b, 0, 0, 0))]
    out_shape = [jax.ShapeDtypeStruct((t_all, D), f32),
                 jax.ShapeDtypeStruct((t_all, D), bf16),
                 jax.ShapeDtypeStruct((nsteps, E, m), f32),
                 jax.ShapeDtypeStruct((1, bsz, 2, D), f32),
                 jax.ShapeDtypeStruct((1, bsz, H, DKH, DVH), f32)]
    return pl.pallas_call(
        functools.partial(_mixer_body, nb, lt, chunk, has_state),
        grid=(bsz // nb, n_lt),
        in_specs=in_specs,
        out_specs=out_specs,
        out_shape=out_shape,
        scratch_shapes=[pltpu.VMEM((2, nb, 1, D), f32),
                        pltpu.VMEM((H, nb, DVH, DKH), f32),
                        pltpu.VMEM((nb, lt, DV), bf16)],
        compiler_params=pltpu.CompilerParams(dimension_semantics=("arbitrary", "arbitrary"),
                                             vmem_limit_bytes=VMEM_LIMIT),
        name="mixer_state" if has_state else "mixer_prompt",
    )(*args)


def _route_body(lg_ref, bias_ref, eidx_ref, w_ref, lrank_ref, cnt_ref):
    step = pl.program_id(0)
    tl = lg_ref.shape[1]

    scores = jax.nn.sigmoid(lg_ref[...])
    choice = scores + bias_ref[...]
    gsz = E // NGROUPS
    grp = choice.reshape(NGROUPS, gsz, tl)
    mi = lax.broadcasted_iota(i32, (NGROUPS, gsz, tl), 1)
    m1 = jnp.max(grp, axis=1, keepdims=True)
    first = jnp.min(jnp.where(grp == m1, mi, gsz), axis=1, keepdims=True)
    m2 = jnp.max(jnp.where(mi == first, NEG, grp), axis=1, keepdims=True)
    gscore = (m1 + m2).reshape(NGROUPS, tl)
    gi = lax.broadcasted_iota(i32, (NGROUPS, tl), 0)
    gsel = jnp.zeros((NGROUPS, tl), f32)
    work = gscore
    for _ in range(TOPK_GROUPS):
        mx = jnp.max(work, axis=0, keepdims=True)
        f = jnp.min(jnp.where(work == mx, gi, NGROUPS), axis=0, keepdims=True)
        hit = gi == f
        gsel = jnp.where(hit, 1.0, gsel)
        work = jnp.where(hit, NEG, work)
    emask = jnp.broadcast_to(gsel.reshape(NGROUPS, 1, tl), (NGROUPS, gsz, tl)).reshape(E, tl)
    masked = jnp.where(emask > 0.0, choice, NEG)
    ei = lax.broadcasted_iota(i32, (E, tl), 0)
    sel = jnp.zeros((E, tl), f32)
    hits, ws = [], []
    for kk in range(TOPK):
        mx = jnp.max(masked, axis=0, keepdims=True)
        f = jnp.min(jnp.where(masked == mx, ei, E), axis=0, keepdims=True)
        hit = ei == f
        ws.append(jnp.sum(jnp.where(hit, scores, 0.0), axis=0, keepdims=True))
        masked = jnp.where(hit, NEG, masked)
        sel = jnp.where(hit, 1.0, sel)
        hits.append(hit)
        eidx_ref[kk:kk + 1, :] = f
    wsum = ws[0]
    for t in ws[1:]:
        wsum = wsum + t
    for kk in range(TOPK):
        w_ref[kk:kk + 1, :] = ws[kk] / wsum * ROUTED_SCALE

    ui = lax.broadcasted_iota(i32, (tl, tl), 0)
    uj = lax.broadcasted_iota(i32, (tl, tl), 1)
    upper = (ui <= uj).astype(bf16)
    lrank = _dot(sel.astype(bf16), upper) - sel
    for kk in range(TOPK):
        lrank_ref[kk:kk + 1, :] = jnp.sum(jnp.where(hits[kk], lrank, 0.0), axis=0, keepdims=True).astype(i32)

    @pl.when(step == 0)
    def _():
        cnt_ref[...] = jnp.zeros_like(cnt_ref)
    lane = lax.broadcasted_iota(i32, cnt_ref.shape, 1)
    cnt_ref[...] = jnp.where(lane == step, jnp.sum(sel, axis=1, keepdims=True), cnt_ref[...])


def _route(logits_t, bias_col):
    t_all = logits_t.shape[1]
    tl = TILE
    assert t_all // tl <= MAX_TILES
    tok = lambda s: (0, s)
    return pl.pallas_call(
        _route_body,
        grid=(t_all // tl,),
        in_specs=[pl.BlockSpec((E, tl), tok), pl.BlockSpec((E, 1), lambda s: (0, 0))],
        out_specs=[pl.BlockSpec((TOPK, tl), tok), pl.BlockSpec((TOPK, tl), tok), pl.BlockSpec((TOPK, tl), tok),
                   pl.BlockSpec((E, MAX_TILES), lambda s: (0, 0))],
        out_shape=[jax.ShapeDtypeStruct((TOPK, t_all), i32), jax.ShapeDtypeStruct((TOPK, t_all), f32),
                   jax.ShapeDtypeStruct((TOPK, t_all), i32), jax.ShapeDtypeStruct((E, MAX_TILES), f32)],
        compiler_params=pltpu.CompilerParams(dimension_semantics=("arbitrary",), vmem_limit_bytes=VMEM_LIMIT),
        name="route",
    )(logits_t, bias_col)


def _plan_body(nblk, eidx_ref, lrank_ref, cnt_ref, slot_ref, n8_ref, l_ref, g_ref, blk_ref):
    i = pl.program_id(0)
    tl = eidx_ref.shape[1]

    @pl.when(i == 0)
    def _():
        cnt = cnt_ref[...]
        n8 = jnp.floor((cnt + (RUN_ALIGN - 1)) * (1.0 / RUN_ALIGN)) * RUN_ALIGN
        ti = lax.broadcasted_iota(i32, (MAX_TILES, MAX_TILES), 0)
        tj = lax.broadcasted_iota(i32, (MAX_TILES, MAX_TILES), 1)
        before = _dotx01(n8, (ti < tj).astype(bf16))
        region = jnp.sum(n8, axis=1, keepdims=True)
        nblocks = jnp.floor((region + (MOE_BM - 1)) * (1.0 / MOE_BM))
        ri = lax.broadcasted_iota(i32, (E, E), 0)
        ci = lax.broadcasted_iota(i32, (E, E), 1)
        below = (ci < ri).astype(bf16)
        bstart = _dot01(below, jnp.broadcast_to(nblocks, (E, MAX_TILES)))
        n8_ref[...] = n8.astype(i32)
        l_ref[...] = _dot01(below, n8).astype(i32)
        g_ref[...] = (bstart * MOE_BM + before).astype(i32)
        bend = bstart[:, 0:1] + nblocks
        bi = lax.broadcasted_iota(i32, (E, nblk), 1).astype(f32)
        owner = jnp.sum((bend <= bi).astype(f32), axis=0, keepdims=True)
        used = jnp.max(bend, axis=0, keepdims=True)
        owner = jnp.where(bi[0:1] < used, jnp.minimum(owner, E - 1.0), -1.0)
        blk_ref[...] = owner.astype(i32)

    lane = lax.broadcasted_iota(i32, (E, MAX_TILES), 1)
    lcol = jnp.sum(jnp.where(lane == i, l_ref[...].astype(f32), 0.0), axis=1, keepdims=True)
    ei = lax.broadcasted_iota(i32, (E, tl), 0)
    for kk in range(TOPK):
        hit = ei == eidx_ref[kk:kk + 1, :]
        start = jnp.sum(jnp.where(hit, lcol, 0.0), axis=0, keepdims=True)
        slot_ref[kk:kk + 1, :] = start.astype(i32) + lrank_ref[kk:kk + 1, :]


def _plan(eidx, lrank, cnt, nblk):
    t_all = eidx.shape[1]
    tl = TILE
    tok = lambda s: (0, s)
    table = pl.BlockSpec((E, MAX_TILES), lambda s: (0, 0))
    return pl.pallas_call(
        functools.partial(_plan_body, nblk),
        grid=(t_all // tl,),
        in_specs=[pl.BlockSpec((TOPK, tl), tok), pl.BlockSpec((TOPK, tl), tok), table],
        out_specs=[pl.BlockSpec((TOPK, tl), tok), table, table, table, pl.BlockSpec((1, nblk), lambda s: (0, 0))],
        out_shape=[jax.ShapeDtypeStruct((TOPK, t_all), i32)] + [jax.ShapeDtypeStruct((E, MAX_TILES), i32)] * 3
        + [jax.ShapeDtypeStruct((1, nblk), i32)],
        compiler_params=pltpu.CompilerParams(dimension_semantics=("arbitrary",), vmem_limit_bytes=VMEM_LIMIT),
        name="plan",
    )(eidx, lrank, cnt)


def _for_each_piece(total, pieces, fn):
    for j, b in enumerate(pieces):
        @pl.when((total & b) != 0)
        def _():
            fn(pl.multiple_of(total & ~(2 * b - 1), pieces[-1]), b, j)


def _for_each_run(n8_ref, l_ref, g_ref, tile, fn):
    def per_expert(e, c):
        lo = pl.multiple_of(l_ref[e, tile], RUN_ALIGN)
        go = pl.multiple_of(g_ref[e, tile], RUN_ALIGN)
        _for_each_piece(n8_ref[e, tile], RUN_PIECES, lambda off, b, j: fn(lo + off, go + off, b, j))
        return c
    lax.fori_loop(0, E, per_expert, 0)


def _start(copy, j):
    copy.start(priority=j % 2)


def _wait_rows(make_copy, rows):
    _for_each_piece(rows, WAIT_PIECES, lambda off, b, j: make_copy(b).wait())


def _dispatch_body(tiles_a, n8_ref, l_ref, g_ref, slot_ref, ha_ref, hb_ref, xs_ref, buf_ref, zero_ref, sem, zsem):
    tile = pl.program_id(0)
    i = tile
    nsteps = pl.num_programs(0)
    cur = i % 2

    slot = slot_ref[...].astype(i16)
    srow = lax.broadcasted_iota(i32, (SLOTS, TILE), 0).astype(i16)
    onehot = jnp.zeros((SLOTS, TILE), bf16)
    for kk in range(TOPK):
        onehot = jnp.where(srow == slot[kk:kk + 1, :], jnp.ones((), bf16), onehot)
    h = jnp.where(tile < tiles_a, ha_ref[...], hb_ref[...])
    buf_ref[cur] = _dot(onehot, h).astype(bf16)

    def run_copy(sl):
        return lambda srow_, grow, b: pltpu.make_async_copy(
            buf_ref.at[sl, pl.ds(srow_, b)], xs_ref.at[pl.ds(grow, b)], sem.at[sl])

    _for_each_run(n8_ref, l_ref, g_ref, tile, lambda s, g, b, j: _start(run_copy(cur)(s, g, b), j))

    def wait_tile(t, sl):
        _wait_rows(lambda b: run_copy(sl)(0, 0, b), l_ref[E - 1, t] + n8_ref[E - 1, t])

    @pl.when(i > 0)
    def _():
        wait_tile(tile - 1, 1 - cur)

    @pl.when(i == nsteps - 1)
    def _():
        wait_tile(tile, cur)
        zero_ref[...] = jnp.zeros_like(zero_ref)

        def tails(fn):
            def per_expert(e, c):
                end = pl.multiple_of(g_ref[e, tile] + n8_ref[e, tile], RUN_ALIGN)
                _for_each_piece((-end) & (MOE_BM - 1), TAIL_PIECES, lambda off, b, j: fn(pltpu.make_async_copy(
                    zero_ref.at[pl.ds(0, b)], xs_ref.at[pl.ds(end + off, b)], zsem)))
                return c
            lax.fori_loop(0, E, per_expert, 0)
        tails(lambda cp: cp.start())
        tails(lambda cp: cp.wait())

        used = (g_ref[E - 1, tile] + n8_ref[E - 1, tile] + (MOE_BM - 1)) // MOE_BM

        def spare(fn):
            def per_block(blk, c):
                fn(pltpu.make_async_copy(zero_ref, xs_ref.at[pl.ds(pl.multiple_of(blk * MOE_BM, MOE_BM), MOE_BM)],
                                         zsem))
                return c
            lax.fori_loop(used, xs_ref.shape[0] // MOE_BM, per_block, 0)
        spare(lambda cp: cp.start())
        spare(lambda cp: cp.wait())


def _dispatch(tables, slot, h2_a, h2_b, nrows):
    tiles_a, tiles_b = h2_a.shape[0] // TILE, h2_b.shape[0] // TILE
    return pl.pallas_call(
        functools.partial(_dispatch_body, tiles_a),
        grid_spec=pltpu.PrefetchScalarGridSpec(
            num_scalar_prefetch=len(tables),
            grid=(tiles_a + tiles_b,),
            in_specs=[pl.BlockSpec((TOPK, TILE), lambda i, *_: (0, i)),
                      pl.BlockSpec((TILE, D), lambda i, *_: (jnp.minimum(i, tiles_a - 1), 0)),
                      pl.BlockSpec((TILE, D), lambda i, *_: (jnp.maximum(i - tiles_a, 0), 0))],
            out_specs=pl.BlockSpec(memory_space=pl.ANY),
            scratch_shapes=[pltpu.VMEM((2, SLOTS, D), bf16), pltpu.VMEM((MOE_BM, D), bf16),
                            pltpu.SemaphoreType.DMA((2,)), pltpu.SemaphoreType.DMA]),
        out_shape=jax.ShapeDtypeStruct((nrows, D), bf16),
        compiler_params=pltpu.CompilerParams(dimension_semantics=("arbitrary",), vmem_limit_bytes=VMEM_LIMIT),
        name="dispatch",
    )(*tables, slot, h2_a, h2_b)


def _expert_body(blk_ref, xs_ref, wg_ref, wu_ref, wd_ref, o_ref, wgb_ref, wub_ref, wdb_ref):
    i = pl.program_id(0)
    e = blk_ref[i]

    @pl.when((e >= 0) & ((i == 0) | (e != blk_ref[jnp.maximum(i - 1, 0)])))
    def _():
        wgb_ref[...] = wg_ref[0].astype(bf16)
        wub_ref[...] = wu_ref[0].astype(bf16)
        wdb_ref[...] = wd_ref[0].astype(bf16)

    @pl.when(e >= 0)
    def _():
        xb = xs_ref[...]
        hg = _dot(xb, wgb_ref[...])
        act = (hg * jax.nn.sigmoid(hg) * _dot(xb, wub_ref[...])).astype(bf16)
        o_ref[...] = _dot(act, wdb_ref[...]).astype(bf16)

    @pl.when(e < 0)
    def _():
        o_ref[...] = jnp.zeros_like(o_ref)


def _experts(blk_e, xs, we_gate, we_up, we_down):
    nblk = blk_e.shape[0]
    ex = lambda i, blk: (jnp.maximum(blk[i], 0), 0, 0)
    return pl.pallas_call(
        _expert_body,
        grid_spec=pltpu.PrefetchScalarGridSpec(
            num_scalar_prefetch=1,
            grid=(nblk,),
            in_specs=[pl.BlockSpec((MOE_BM, D), lambda i, blk: (jnp.where(blk[i] >= 0, i, 0), 0)),
                      pl.BlockSpec((1, D, DE), ex), pl.BlockSpec((1, D, DE), ex), pl.BlockSpec((1, DE, D), ex)],
            out_specs=pl.BlockSpec((MOE_BM, D), lambda i, blk: (i, 0)),
            scratch_shapes=[pltpu.VMEM((D, DE), bf16), pltpu.VMEM((D, DE), bf16), pltpu.VMEM((DE, D), bf16)]),
        out_shape=jax.ShapeDtypeStruct((nblk * MOE_BM, D), bf16),
        compiler_params=pltpu.CompilerParams(dimension_semantics=("arbitrary",), vmem_limit_bytes=VMEM_LIMIT),
        name="experts",
    )(blk_e, xs, we_gate, we_up, we_down)


def _final_body(nb, lt, tile0, n8_ref, l_ref, g_ref, slot_ref, wt_ref, x2_ref, g2_ref, fw_ref, ys_ref, o_ref,
                buf_ref, sem):
    m = nb * lt
    n_inner = pl.num_programs(1)
    step = pl.program_id(0) * n_inner + pl.program_id(1)
    nsteps = pl.num_programs(0) * n_inner
    tile = tile0 + step
    cur = step % 2

    def run_copy(sl):
        return lambda srow, grow, b: pltpu.make_async_copy(
            ys_ref.at[pl.ds(grow, b)], buf_ref.at[sl, pl.ds(srow, b)], sem.at[sl])

    @pl.when(step == 0)
    def _():
        buf_ref[...] = jnp.zeros_like(buf_ref)
        _for_each_run(n8_ref, l_ref, g_ref, tile, lambda s, g, b, j: _start(run_copy(0)(s, g, b), j))

    @pl.when(step + 1 < nsteps)
    def _():
        _for_each_run(n8_ref, l_ref, g_ref, tile + 1, lambda s, g, b, j: _start(run_copy(1 - cur)(s, g, b), j))

    _wait_rows(lambda b: run_copy(cur)(0, 0, b), l_ref[E - 1, tile] + n8_ref[E - 1, tile])

    slot = slot_ref[...].astype(i16)
    wt = wt_ref[...]
    wt_hi = wt.astype(bf16)
    wt_lo = (wt - wt_hi.astype(f32)).astype(bf16)
    scol = lax.broadcasted_iota(i32, (m, SLOTS), 1).astype(i16)
    w_hi = jnp.zeros((m, SLOTS), bf16)
    w_lo = jnp.zeros((m, SLOTS), bf16)
    for kk in range(TOPK):
        hit = scol == slot[:, kk:kk + 1]
        w_hi = jnp.where(hit, wt_hi[:, kk:kk + 1], w_hi)
        w_lo = jnp.where(hit, wt_lo[:, kk:kk + 1], w_lo)
    yb = buf_ref[cur]
    routed = _dot(w_hi, yb) + _dot(w_lo, yb)
    g2 = g2_ref[0]
    if nb == 1:
        g2 = g2.reshape(1, D)
    else:
        g2 = jnp.broadcast_to(g2, (nb, lt, D)).reshape(m, D)
    y = x2_ref[...] + g2 * routed
    o_ref[...] = _rms(y, fw_ref[...]).reshape(nb, lt, D)


def _final(tables, slot_t, wts_t, x2, ys, mod4, fw, bsz, seqlen, row0, nb, lt):
    m = nb * lt
    assert m == TILE and row0 % TILE == 0
    n_lt = seqlen // lt
    tile0 = row0 // TILE
    tile = lambda b, i: tile0 + b * n_lt + i
    return pl.pallas_call(
        functools.partial(_final_body, nb, lt, tile0),
        grid_spec=pltpu.PrefetchScalarGridSpec(
            num_scalar_prefetch=len(tables),
            grid=(bsz // nb, n_lt),
            in_specs=[pl.BlockSpec((m, TOPK), lambda b, i, *_: (tile(b, i), 0)),
                      pl.BlockSpec((m, TOPK), lambda b, i, *_: (tile(b, i), 0)),
                      pl.BlockSpec((m, D), lambda b, i, *_: (b * n_lt + i, 0)),
                      pl.BlockSpec((1, nb, 1, D), lambda b, i, *_: (5, b, 0, 0)),
                      pl.BlockSpec((1, D), lambda b, i, *_: (0, 0)),
                      pl.BlockSpec(memory_space=pl.ANY)],
            out_specs=pl.BlockSpec((nb, lt, D), lambda b, i, *_: (b, i, 0)),
            scratch_shapes=[pltpu.VMEM((2, SLOTS, D), bf16), pltpu.SemaphoreType.DMA((2,))]),
        out_shape=jax.ShapeDtypeStruct((bsz, seqlen, D), f32),
        compiler_params=pltpu.CompilerParams(dimension_semantics=("arbitrary", "arbitrary"),
                                             vmem_limit_bytes=VMEM_LIMIT),
        name="final",
    )(*tables, slot_t, wts_t, x2, mod4, fw, ys)


def kernel(x_prompt, x_sample, state_conv, state_gla, c_prompt, c_sample, w_ada, b_ada, norm1_w, w_in, conv_w,
           w_gk, b_gk, gla_norm_w, w_out_conv, w_out_gla, w_o, norm2_w, router_w, router_bias, we_gate, we_up,
           we_down, ws_gate, ws_up, ws_down, final_norm_w):
    assert w_ada.shape[0] == 1, "single-layer step"
    bp, lp, _ = x_prompt.shape
    bs, ls, _ = x_sample.shape
    tp, ts = bp * lp, bs * ls
    t_all = tp + ts

    w_in0 = w_in[0]
    n_main = 3 * D + 2 * DK + 2 * DV
    rank = w_gk.shape[1]
    weights = (
        norm1_w[0].reshape(1, D),
        w_in0[:, :n_main].astype(bf16),
        jnp.pad(w_in0[:, n_main:n_main + rank], ((0, 0), (0, RANK_PAD - rank))).astype(bf16),
        w_in0[:, n_main + rank:].astype(bf16),
        jnp.pad(w_gk[0], ((0, RANK_PAD - rank), (0, 0))).astype(bf16),
        b_gk[0].reshape(1, DK),
        conv_w[0],
        gla_norm_w[0].reshape(1, DVH),
        w_out_conv[0].astype(bf16),
        w_out_gla[0].astype(bf16),
        w_o[0].astype(bf16),
        norm2_w[0].reshape(1, D),
        router_w[0].T.astype(bf16),
        ws_gate[0].astype(bf16),
        ws_up[0].astype(bf16),
        ws_down[0].astype(bf16),
    )

    mod = _adaln(jnp.concatenate([c_prompt, c_sample], axis=0), w_ada[0], b_ada[0].reshape(1, 6 * D))
    mod4 = mod.reshape(bp + bs, 6, 1, D).transpose(1, 0, 2, 3)
    mod_p, mod_s = mod4[:, :bp], mod4[:, bp:]

    lt_p = 512
    x2_p, h2_p, lg_p, conv_p, gla_p = _mixer(x_prompt, mod_p, None, weights, 1, lt_p, GLA_CHUNK)
    nb_s = 8
    x2_s, h2_s, lg_s, conv_s, gla_s = _mixer(x_sample, mod_s, (state_conv[0], state_gla[0]), weights,
                                             nb_s, ls, math.gcd(ls, GLA_CHUNK))
    lg = jnp.concatenate([lg_p.transpose(1, 0, 2).reshape(E, tp), lg_s.transpose(1, 0, 2).reshape(E, ts)], axis=1)

    eidx, wts, lrank, cnt = _route(lg, router_bias[0].reshape(E, 1))
    n_tiles = t_all // TILE
    max_rows = t_all * TOPK + n_tiles * E * (RUN_ALIGN - 1) + E * (MOE_BM - RUN_ALIGN)
    nblk = -(-max_rows // MOE_BM)
    slot, n8, loff, goff, blk_e = _plan(eidx, lrank, cnt, -(-nblk // 128) * 128)
    tables = (n8, loff, goff)

    xs = _dispatch(tables, slot, h2_p, h2_s, nblk * MOE_BM)
    ys = _experts(blk_e[0, :nblk], xs, we_gate[0], we_up[0], we_down[0])

    fw = final_norm_w.reshape(1, D)
    slot_t, wts_t = slot.T, wts.T
    y_prompt = _final(tables, slot_t, wts_t, x2_p, ys, mod_p, fw, bp, lp, 0, 1, TILE)
    y_sample = _final(tables, slot_t, wts_t, x2_s, ys, mod_s, fw, bs, ls, tp, TILE // ls, ls)
    return (y_prompt, y_sample, conv_p, gla_p, conv_s, gla_s)
```

```python
import functools
import math

import jax
import jax.numpy as jnp
from jax import lax
from jax.experimental import pallas as pl
from jax.experimental.pallas import tpu as pltpu

f32 = jnp.float32
bf16 = jnp.bfloat16
i32 = jnp.int32
i16 = jnp.int16

D = 1024
H = 4
DKH = 128
DVH = 256
DK = H * DKH
DV = H * DVH
RANK_PAD = 128
GATE_NORMALIZER = 16.0
GLA_CHUNK = 64
E = 64
TOPK = 8
NGROUPS = 8
TOPK_GROUPS = 4
DE = 256
ROUTED_SCALE = 2.5
EPS = 1e-6
NEG = float("-inf")

VMEM_LIMIT = 58 * 1024 * 1024
MOE_BM = 512
TILE = 256
MAX_TILES = 128
RUN_ALIGN = 16
SLOTS = TILE * TOPK + 1024
SLOTS_MAIN = TILE * TOPK + 512
RUN_PIECES = (256, 128, 64, 32, 16)
TAIL_PIECES = (256, 128, 64, 32, 16)
WAIT_PIECES = (2048, 1024, 512, 256, 128, 64, 32, 16)

def _dot(a, b):
    return jnp.dot(a, b, preferred_element_type=f32)


def _split3(x):
    hi = x.astype(bf16)
    r1 = x - hi.astype(f32)
    mid = r1.astype(bf16)
    lo = (r1 - mid.astype(f32)).astype(bf16)
    return hi, mid, lo


def _dot01(m01, x):
    hi, mid, lo = _split3(x)
    return _dot(m01, hi) + _dot(m01, mid) + _dot(m01, lo)


def _dotx01(x, m01):
    hi, mid, lo = _split3(x)
    return _dot(hi, m01) + _dot(mid, m01) + _dot(lo, m01)


def _rms(v, w):
    ms = jnp.mean(v * v, axis=-1, keepdims=True)
    return v * lax.rsqrt(ms + EPS) * w


def _const_spec(shape):
    n = len(shape)
    return pl.BlockSpec(shape, lambda *_: (0,) * n, pipeline_mode=pl.Buffered(1))


def _adaln_body(c_ref, w_ref, b_ref, o_ref):
    c = c_ref[...]
    a = (c * jax.nn.sigmoid(c)).astype(bf16)
    o_ref[...] = _dot(a, w_ref[...].astype(bf16)) + b_ref[...]


def _adaln(c_all, w_ada, b_ada):
    nrow = c_all.shape[0]
    tn = 1536
    return pl.pallas_call(
        _adaln_body,
        grid=(6 * D // tn,),
        in_specs=[pl.BlockSpec((nrow, D), lambda j: (0, 0)),
                  pl.BlockSpec((D, tn), lambda j: (0, j)),
                  pl.BlockSpec((1, tn), lambda j: (0, j))],
        out_specs=pl.BlockSpec((nrow, tn), lambda j: (0, j)),
        out_shape=jax.ShapeDtypeStruct((nrow, 6 * D), f32),
        compiler_params=pltpu.CompilerParams(dimension_semantics=("arbitrary",), vmem_limit_bytes=VMEM_LIMIT),
        name="adaln",
    )(c_all, w_ada, b_ada)


def _mixer_body(nb, lt, chunk, has_state, *refs):
    if has_state:
        (x_ref, mod_ref, cs_ref, gs_ref, *rest) = refs
    else:
        (x_ref, mod_ref, *rest) = refs
        cs_ref = gs_ref = None
    (n1_ref, wmain_ref, wz_ref, wu_ref, wgk_ref, bgk_ref, cw_ref, gnw_ref, woc_ref, wog_ref, wo_ref,
     n2_ref, rwt_ref, wsg_ref, wsu_ref, wsd_ref,
     x2_ref, h2_ref, lg_ref, nc_ref, ng_ref, carry_ref, st_ref, og_ref) = rest
    m = nb * lt
    i = pl.program_id(1)
    last = pl.num_programs(1) - 1

    def rows(v):
        w = v.shape[-1]
        if nb == 1:
            return v.reshape(1, w)
        return jnp.broadcast_to(v, (nb, lt, w)).reshape(m, w)

    @pl.when(i == 0)
    def _():
        if has_state:
            cs = cs_ref[...]
            carry_ref[0] = cs[:, 0:1, :]
            carry_ref[1] = cs[:, 1:2, :]
            for hh in range(H):
                st_ref[hh] = jnp.swapaxes(gs_ref[:, hh], 1, 2)
        else:
            carry_ref[...] = jnp.zeros_like(carry_ref)
            st_ref[...] = jnp.zeros_like(st_ref)

    x = x_ref[...].reshape(m, D)
    sh1, sc1, g1, sh2, sc2, g2 = [rows(mod_ref[j]) for j in range(6)]
    hb = (_rms(x, n1_ref[...]) * (1.0 + sc1) + sh1).astype(bf16)

    def proj(lo, hi):
        return _dot(hb, wmain_ref[:, lo:hi])

    pre = proj(D, 2 * D) * proj(2 * D, 3 * D)
    c0 = rows(carry_ref[0])
    c1 = rows(carry_ref[1])
    l_idx = lax.broadcasted_iota(i32, (m, D), 0) & (lt - 1)
    r1 = pltpu.roll(pre, 1, 0)
    r2 = pltpu.roll(pre, 2, 0)
    prev1 = jnp.where(l_idx == 0, c1, r1)
    prev2 = jnp.where(l_idx == 0, c0, jnp.where(l_idx == 1, c1, r2))
    cw = cw_ref[...]
    conv = cw[0:1] * prev2 + cw[1:2] * prev1 + cw[2:3] * pre
    y_a = _dot((proj(0, D) * conv).astype(bf16), woc_ref[...])
    pre3 = pre.reshape(nb, lt, D)
    tail = pre3[:, lt - 2:lt, :]
    carry_ref[0] = tail[:, 0:1, :]
    carry_ref[1] = tail[:, 1:2, :]
    nc_ref[0] = tail

    q = proj(3 * D, 3 * D + DK)
    k = proj(3 * D + DK, 3 * D + 2 * DK)
    v = proj(4 * D, 5 * D)
    g = proj(5 * D, 6 * D)
    z = _dot(hb, wz_ref[...]).astype(bf16)
    pa = _dot(z, wgk_ref[...]) + bgk_ref[...]
    la = (jnp.minimum(pa, 0.0) - jnp.log1p(jnp.exp(-jnp.abs(pa)))) / GATE_NORMALIZER
    rr = lax.broadcasted_iota(i32, (m, m), 0)
    cc = lax.broadcasted_iota(i32, (m, m), 1)
    same_chunk = (rr & ~(chunk - 1)) == (cc & ~(chunk - 1))
    tri = (same_chunk & (cc <= rr)).astype(bf16)
    bcum = _dot01(tri, la)
    ngrp = m // chunk
    bend = bcum.reshape(ngrp, chunk, DK)[:, chunk - 1:chunk, :]
    bend_rows = jnp.broadcast_to(bend, (ngrp, chunk, DK)).reshape(m, DK)
    q_dec = (q * (DKH ** -0.5) * jnp.exp(bcum)).astype(bf16).reshape(nb, lt, DK)
    k_dec = (k * jnp.exp(-bcum)).astype(bf16).reshape(nb, lt, DK)
    k_end = (k * jnp.exp(bend_rows - bcum)).astype(bf16).reshape(nb, lt, DK)
    a_end = jnp.exp(bend_rows).reshape(nb, lt, DK)
    v3 = v.astype(bf16).reshape(nb, lt, DV)
    g3 = g.reshape(nb, lt, DV)
    causal = lax.broadcasted_iota(i32, (chunk, chunk), 1) <= lax.broadcasted_iota(i32, (chunk, chunk), 0)
    gnw = gnw_ref[...]
    for j in range(lt // chunk):
        js = slice(j * chunk, (j + 1) * chunk)
        for hh in range(H):
            ks = slice(hh * DKH, (hh + 1) * DKH)
            vs = slice(hh * DVH, (hh + 1) * DVH)
            qd, kd, ke, vv = q_dec[:, js, ks], k_dec[:, js, ks], k_end[:, js, ks], v3[:, js, vs]
            st = st_ref[hh]
            o = jnp.einsum('bcd,bed->bce', qd, st.astype(bf16), preferred_element_type=f32)
            sc = jnp.einsum('bcd,bsd->bcs', qd, kd, preferred_element_type=f32)
            sc = jnp.where(causal, sc, 0.0).astype(bf16)
            o = o + jnp.einsum('bcs,bse->bce', sc, vv, preferred_element_type=f32)
            a_row = a_end[:, j * chunk:j * chunk + 1, ks]
            st_ref[hh] = st * a_row + jnp.einsum('bse,bsd->bed', vv, ke, preferred_element_type=f32)
            on = _rms(o, gnw)
            gh = g3[:, js, vs]
            og_ref[:, js, vs] = (on * (gh * jax.nn.sigmoid(gh))).astype(bf16)

    @pl.when(i == last)
    def _():
        for hh in range(H):
            ng_ref[0, :, hh] = jnp.swapaxes(st_ref[hh], 1, 2)

    y_b = _dot(og_ref[...].reshape(m, DV), wog_ref[...])
    u_a = _dot(hb, wu_ref[:, 0:D])
    u_b = _dot(hb, wu_ref[:, D:2 * D])
    merged = (jax.nn.sigmoid(u_a) * y_a + jax.nn.sigmoid(u_b) * y_b).astype(bf16)
    x1 = x + g1 * _dot(merged, wo_ref[...])

    h2 = (_rms(x1, n2_ref[...]) * (1.0 + sc2) + sh2).astype(bf16)
    sg = _dot(h2, wsg_ref[...])
    act = (sg * jax.nn.sigmoid(sg) * _dot(h2, wsu_ref[...])).astype(bf16)
    x2_ref[...] = x1 + g2 * _dot(act, wsd_ref[...])
    h2_ref[...] = h2
    lg_ref[0] = lax.dot_general(rwt_ref[...], h2, (((1,), (1,)), ((), ())), preferred_element_type=f32)


def _mixer(x, mod4, states, weights, nb, lt, chunk):
    bsz, seqlen, _ = x.shape
    has_state = states is not None
    m = nb * lt
    n_lt = seqlen // lt
    nsteps = (bsz // nb) * n_lt
    t_all = bsz * seqlen

    def tok_blk(b, i):
        return b * n_lt + i

    in_specs = [pl.BlockSpec((nb, lt, D), lambda b, i: (b, i, 0)),
                pl.BlockSpec((6, nb, 1, D), lambda b, i: (0, b, 0, 0))]
    args = [x, mod4]
    if has_state:
        in_specs += [pl.BlockSpec((nb, 2, D), lambda b, i: (b, 0, 0)),
                     pl.BlockSpec((nb, H, DKH, DVH), lambda b, i: (b, 0, 0, 0))]
        args += list(states)
    in_specs += [_const_spec(w.shape) for w in weights]
    args += list(weights)
    out_specs = [pl.BlockSpec((m, D), lambda b, i: (tok_blk(b, i), 0)),
                 pl.BlockSpec((m, D), lambda b, i: (tok_blk(b, i), 0)),
                 pl.BlockSpec((1, E, m), lambda b, i: (b * n_lt + i, 0, 0)),
                 pl.BlockSpec((1, nb, 2, D), lambda b, i: (0, b, 0, 0)),
                 pl.BlockSpec((1, nb, H, DKH, DVH), lambda b, i: (0, b, 0, 0, 0))]
    out_shape = [jax.ShapeDtypeStruct((t_all, D), f32),
                 jax.ShapeDtypeStruct((t_all, D), bf16),
                 jax.ShapeDtypeStruct((nsteps, E, m), f32),
                 jax.ShapeDtypeStruct((1, bsz, 2, D), f32),
                 jax.ShapeDtypeStruct((1, bsz, H, DKH, DVH), f32)]
    return pl.pallas_call(
        functools.partial(_mixer_body, nb, lt, chunk, has_state),
        grid=(bsz // nb, n_lt),
        in_specs=in_specs,
        out_specs=out_specs,
        out_shape=out_shape,
        scratch_shapes=[pltpu.VMEM((2, nb, 1, D), f32),
                        pltpu.VMEM((H, nb, DVH, DKH), f32),
                        pltpu.VMEM((nb, lt, DV), bf16)],
        compiler_params=pltpu.CompilerParams(dimension_semantics=("arbitrary", "arbitrary"),
                                             vmem_limit_bytes=VMEM_LIMIT),
        name="mixer_state" if has_state else "mixer_prompt",
    )(*args)


def _route_body(lg_ref, bias_ref, eidx_ref, w_ref, lrank_ref, cnt_ref):
    step = pl.program_id(0)
    tl = lg_ref.shape[1]

    scores = jax.nn.sigmoid(lg_ref[...])
    choice = scores + bias_ref[...]
    gsz = E // NGROUPS
    grp = choice.reshape(NGROUPS, gsz, tl)
    mi = lax.broadcasted_iota(i32, (NGROUPS, gsz, tl), 1)
    m1 = jnp.max(grp, axis=1, keepdims=True)
    first = jnp.min(jnp.where(grp == m1, mi, gsz), axis=1, keepdims=True)
    m2 = jnp.max(jnp.where(mi == first, NEG, grp), axis=1, keepdims=True)
    gscore = (m1 + m2).reshape(NGROUPS, tl)
    gi = lax.broadcasted_iota(i32, (NGROUPS, tl), 0)
    gsel = jnp.zeros((NGROUPS, tl), f32)
    work = gscore
    for _ in range(TOPK_GROUPS):
        mx = jnp.max(work, axis=0, keepdims=True)
        f = jnp.min(jnp.where(work == mx, gi, NGROUPS), axis=0, keepdims=True)
        hit = gi == f
        gsel = jnp.where(hit, 1.0, gsel)
        work = jnp.where(hit, NEG, work)
    emask = jnp.broadcast_to(gsel.reshape(NGROUPS, 1, tl), (NGROUPS, gsz, tl)).reshape(E, tl)
    masked = jnp.where(emask > 0.0, choice, NEG)
    ei = lax.broadcasted_iota(i32, (E, tl), 0)
    sel = jnp.zeros((E, tl), f32)
    hits, ws = [], []
    for kk in range(TOPK):
        mx = jnp.max(masked, axis=0, keepdims=True)
        f = jnp.min(jnp.where(masked == mx, ei, E), axis=0, keepdims=True)
        hit = ei == f
        ws.append(jnp.sum(jnp.where(hit, scores, 0.0), axis=0, keepdims=True))
        masked = jnp.where(hit, NEG, masked)
        sel = jnp.where(hit, 1.0, sel)
        hits.append(hit)
        eidx_ref[kk:kk + 1, :] = f
    wsum = ws[0]
    for t in ws[1:]:
        wsum = wsum + t
    for kk in range(TOPK):
        w_ref[kk:kk + 1, :] = ws[kk] / wsum * ROUTED_SCALE

    ui = lax.broadcasted_iota(i32, (tl, tl), 0)
    uj = lax.broadcasted_iota(i32, (tl, tl), 1)
    upper = (ui <= uj).astype(bf16)
    lrank = _dot(sel.astype(bf16), upper) - sel
    for kk in range(TOPK):
        lrank_ref[kk:kk + 1, :] = jnp.sum(jnp.where(hits[kk], lrank, 0.0), axis=0, keepdims=True).astype(i32)

    @pl.when(step == 0)
    def _():
        cnt_ref[...] = jnp.zeros_like(cnt_ref)
    lane = lax.broadcasted_iota(i32, cnt_ref.shape, 1)
    cnt_ref[...] = jnp.where(lane == step, jnp.sum(sel, axis=1, keepdims=True), cnt_ref[...])


def _route(logits_t, bias_col):
    t_all = logits_t.shape[1]
    tl = TILE
    assert t_all // tl <= MAX_TILES
    tok = lambda s: (0, s)
    return pl.pallas_call(
        _route_body,
        grid=(t_all // tl,),
        in_specs=[pl.BlockSpec((E, tl), tok), pl.BlockSpec((E, 1), lambda s: (0, 0))],
        out_specs=[pl.BlockSpec((TOPK, tl), tok), pl.BlockSpec((TOPK, tl), tok), pl.BlockSpec((TOPK, tl), tok),
                   pl.BlockSpec((E, MAX_TILES), lambda s: (0, 0))],
        out_shape=[jax.ShapeDtypeStruct((TOPK, t_all), i32), jax.ShapeDtypeStruct((TOPK, t_all), f32),
                   jax.ShapeDtypeStruct((TOPK, t_all), i32), jax.ShapeDtypeStruct((E, MAX_TILES), f32)],
        compiler_params=pltpu.CompilerParams(dimension_semantics=("arbitrary",), vmem_limit_bytes=VMEM_LIMIT),
        name="route",
    )(logits_t, bias_col)


def _plan_body(nblk, eidx_ref, lrank_ref, cnt_ref, slot_ref, n8_ref, l_ref, g_ref, blk_ref):
    i = pl.program_id(0)
    tl = eidx_ref.shape[1]

    @pl.when(i == 0)
    def _():
        cnt = cnt_ref[...]
        n8 = jnp.floor((cnt + (RUN_ALIGN - 1)) * (1.0 / RUN_ALIGN)) * RUN_ALIGN
        ti = lax.broadcasted_iota(i32, (MAX_TILES, MAX_TILES), 0)
        tj = lax.broadcasted_iota(i32, (MAX_TILES, MAX_TILES), 1)
        before = _dotx01(n8, (ti < tj).astype(bf16))
        region = jnp.sum(n8, axis=1, keepdims=True)
        nblocks = jnp.floor((region + (MOE_BM - 1)) * (1.0 / MOE_BM))
        ri = lax.broadcasted_iota(i32, (E, E), 0)
        ci = lax.broadcasted_iota(i32, (E, E), 1)
        below = (ci < ri).astype(bf16)
        bstart = _dot01(below, jnp.broadcast_to(nblocks, (E, MAX_TILES)))
        n8_ref[...] = n8.astype(i32)
        l_ref[...] = _dot01(below, n8).astype(i32)
        g_ref[...] = (bstart * MOE_BM + before).astype(i32)
        bend = bstart[:, 0:1] + nblocks
        bi = lax.broadcasted_iota(i32, (E, nblk), 1).astype(f32)
        owner = jnp.sum((bend <= bi).astype(f32), axis=0, keepdims=True)
        used = jnp.max(bend, axis=0, keepdims=True)
        owner = jnp.where(bi[0:1] < used, jnp.minimum(owner, E - 1.0), -1.0)
        blk_ref[...] = owner.astype(i32)

    lane = lax.broadcasted_iota(i32, (E, MAX_TILES), 1)
    lcol = jnp.sum(jnp.where(lane == i, l_ref[...].astype(f32), 0.0), axis=1, keepdims=True)
    ei = lax.broadcasted_iota(i32, (E, tl), 0)
    for kk in range(TOPK):
        hit = ei == eidx_ref[kk:kk + 1, :]
        start = jnp.sum(jnp.where(hit, lcol, 0.0), axis=0, keepdims=True)
        slot_ref[kk:kk + 1, :] = start.astype(i32) + lrank_ref[kk:kk + 1, :]


def _plan(eidx, lrank, cnt, nblk):
    t_all = eidx.shape[1]
    tl = TILE
    tok = lambda s: (0, s)
    table = pl.BlockSpec((E, MAX_TILES), lambda s: (0, 0))
    return pl.pallas_call(
        functools.partial(_plan_body, nblk),
        grid=(t_all // tl,),
        in_specs=[pl.BlockSpec((TOPK, tl), tok), pl.BlockSpec((TOPK, tl), tok), table],
        out_specs=[pl.BlockSpec((TOPK, tl), tok), table, table, table, pl.BlockSpec((1, nblk), lambda s: (0, 0))],
        out_shape=[jax.ShapeDtypeStruct((TOPK, t_all), i32)] + [jax.ShapeDtypeStruct((E, MAX_TILES), i32)] * 3
        + [jax.ShapeDtypeStruct((1, nblk), i32)],
        compiler_params=pltpu.CompilerParams(dimension_semantics=("arbitrary",), vmem_limit_bytes=VMEM_LIMIT),
        name="plan",
    )(eidx, lrank, cnt)


def _for_each_piece(total, pieces, fn):
    for j, b in enumerate(pieces):
        @pl.when((total & b) != 0)
        def _():
            fn(pl.multiple_of(total & ~(2 * b - 1), pieces[-1]), b, j)


def _for_each_run(n8_ref, l_ref, g_ref, tile, fn):
    def per_expert(e, c):
        lo = pl.multiple_of(l_ref[e, tile], RUN_ALIGN)
        go = pl.multiple_of(g_ref[e, tile], RUN_ALIGN)
        _for_each_piece(n8_ref[e, tile], RUN_PIECES, lambda off, b, j: fn(lo + off, go + off, b, j))
        return c
    lax.fori_loop(0, E, per_expert, 0)


def _start(copy, j):
    copy.start(priority=j % 2)


def _wait_rows(make_copy, rows):
    _for_each_piece(rows, WAIT_PIECES, lambda off, b, j: make_copy(b).wait())


def _dispatch_body(tiles_a, n8_ref, l_ref, g_ref, slot_ref, ha_ref, hb_ref, xs_ref, buf_ref, zero_ref, sem, zsem):
    tile = pl.program_id(0)
    i = tile
    nsteps = pl.num_programs(0)
    cur = i % 2

    slot = slot_ref[...].astype(i16)
    h = jnp.where(tile < tiles_a, ha_ref[...], hb_ref[...])
    filled = l_ref[E - 1, tile] + n8_ref[E - 1, tile]

    def order(row0, nrows):
        srow = lax.broadcasted_iota(i32, (nrows, TILE), 0).astype(i16) + jnp.int16(row0)
        onehot = jnp.zeros((nrows, TILE), bf16)
        for kk in range(TOPK):
            onehot = jnp.where(srow == slot[kk:kk + 1, :], jnp.ones((), bf16), onehot)
        buf_ref[cur, row0:row0 + nrows, :] = _dot(onehot, h).astype(bf16)

    order(0, SLOTS_MAIN)

    @pl.when(filled > SLOTS_MAIN)
    def _():
        order(SLOTS_MAIN, SLOTS - SLOTS_MAIN)

    def run_copy(sl):
        return lambda srow_, grow, b: pltpu.make_async_copy(
            buf_ref.at[sl, pl.ds(srow_, b)], xs_ref.at[pl.ds(grow, b)], sem.at[sl])

    _for_each_run(n8_ref, l_ref, g_ref, tile, lambda s, g, b, j: _start(run_copy(cur)(s, g, b), j))

    def wait_tile(t, sl):
        _wait_rows(lambda b: run_copy(sl)(0, 0, b), l_ref[E - 1, t] + n8_ref[E - 1, t])

    @pl.when(i > 0)
    def _():
        wait_tile(tile - 1, 1 - cur)

    @pl.when(i == nsteps - 1)
    def _():
        wait_tile(tile, cur)
        zero_ref[...] = jnp.zeros_like(zero_ref)

        def tails(fn):
            def per_expert(e, c):
                end = pl.multiple_of(g_ref[e, tile] + n8_ref[e, tile], RUN_ALIGN)
                _for_each_piece((-end) & (MOE_BM - 1), TAIL_PIECES, lambda off, b, j: fn(pltpu.make_async_copy(
                    zero_ref.at[pl.ds(0, b)], xs_ref.at[pl.ds(end + off, b)], zsem)))
                return c
            lax.fori_loop(0, E, per_expert, 0)
        tails(lambda cp: cp.start())
        tails(lambda cp: cp.wait())

        used = (g_ref[E - 1, tile] + n8_ref[E - 1, tile] + (MOE_BM - 1)) // MOE_BM

        def spare(fn):
            def per_block(blk, c):
                fn(pltpu.make_async_copy(zero_ref, xs_ref.at[pl.ds(pl.multiple_of(blk * MOE_BM, MOE_BM), MOE_BM)],
                                         zsem))
                return c
            lax.fori_loop(used, xs_ref.shape[0] // MOE_BM, per_block, 0)
        spare(lambda cp: cp.start())
        spare(lambda cp: cp.wait())


def _dispatch(tables, slot, h2_a, h2_b, nrows):
    tiles_a, tiles_b = h2_a.shape[0] // TILE, h2_b.shape[0] // TILE
    return pl.pallas_call(
        functools.partial(_dispatch_body, tiles_a),
        grid_spec=pltpu.PrefetchScalarGridSpec(
            num_scalar_prefetch=len(tables),
            grid=(tiles_a + tiles_b,),
            in_specs=[pl.BlockSpec((TOPK, TILE), lambda i, *_: (0, i)),
                      pl.BlockSpec((TILE, D), lambda i, *_: (jnp.minimum(i, tiles_a - 1), 0)),
                      pl.BlockSpec((TILE, D), lambda i, *_: (jnp.maximum(i - tiles_a, 0), 0))],
            out_specs=pl.BlockSpec(memory_space=pl.ANY),
            scratch_shapes=[pltpu.VMEM((2, SLOTS, D), bf16), pltpu.VMEM((MOE_BM, D), bf16),
                            pltpu.SemaphoreType.DMA((2,)), pltpu.SemaphoreType.DMA]),
        out_shape=jax.ShapeDtypeStruct((nrows, D), bf16),
        compiler_params=pltpu.CompilerParams(dimension_semantics=("arbitrary",), vmem_limit_bytes=VMEM_LIMIT),
        name="dispatch",
    )(*tables, slot, h2_a, h2_b)


def _expert_body(blk_ref, xs_ref, wg_ref, wu_ref, wd_ref, o_ref, wgu_ref, wdb_ref):
    i = pl.program_id(0)
    e = blk_ref[i]

    @pl.when((e >= 0) & ((i == 0) | (e != blk_ref[jnp.maximum(i - 1, 0)])))
    def _():
        wgu_ref[:, :DE] = wg_ref[0].astype(bf16)
        wgu_ref[:, DE:] = wu_ref[0].astype(bf16)
        wdb_ref[...] = wd_ref[0].astype(bf16)

    @pl.when(e >= 0)
    def _():
        hgu = _dot(xs_ref[...], wgu_ref[...])
        hg = hgu[:, :DE]
        act = (hg * jax.nn.sigmoid(hg) * hgu[:, DE:]).astype(bf16)
        o_ref[...] = _dot(act, wdb_ref[...]).astype(bf16)

    @pl.when(e < 0)
    def _():
        o_ref[...] = jnp.zeros_like(o_ref)


def _experts(blk_e, xs, we_gate, we_up, we_down):
    nblk = blk_e.shape[0]
    ex = lambda i, blk: (jnp.maximum(blk[i], 0), 0, 0)
    return pl.pallas_call(
        _expert_body,
        grid_spec=pltpu.PrefetchScalarGridSpec(
            num_scalar_prefetch=1,
            grid=(nblk,),
            in_specs=[pl.BlockSpec((MOE_BM, D), lambda i, blk: (jnp.where(blk[i] >= 0, i, 0), 0)),
                      pl.BlockSpec((1, D, DE), ex), pl.BlockSpec((1, D, DE), ex), pl.BlockSpec((1, DE, D), ex)],
            out_specs=pl.BlockSpec((MOE_BM, D), lambda i, blk: (i, 0)),
            scratch_shapes=[pltpu.VMEM((D, 2 * DE), bf16), pltpu.VMEM((DE, D), bf16)]),
        out_shape=jax.ShapeDtypeStruct((nblk * MOE_BM, D), bf16),
        compiler_params=pltpu.CompilerParams(dimension_semantics=("arbitrary",), vmem_limit_bytes=VMEM_LIMIT),
        name="experts",
    )(blk_e, xs, we_gate, we_up, we_down)


def _final_body(nb, lt, tile0, n8_ref, l_ref, g_ref, slot_ref, wt_ref, x2_ref, g2_ref, fw_ref, ys_ref, o_ref,
                buf_ref, routed_ref, sem):
    m = nb * lt
    n_inner = pl.num_programs(1)
    step = pl.program_id(0) * n_inner + pl.program_id(1)
    nsteps = pl.num_programs(0) * n_inner
    tile = tile0 + step
    cur = step % 2

    def run_copy(sl):
        return lambda srow, grow, b: pltpu.make_async_copy(
            ys_ref.at[pl.ds(grow, b)], buf_ref.at[sl, pl.ds(srow, b)], sem.at[sl])

    @pl.when(step == 0)
    def _():
        buf_ref[...] = jnp.zeros_like(buf_ref)
        _for_each_run(n8_ref, l_ref, g_ref, tile, lambda s, g, b, j: _start(run_copy(0)(s, g, b), j))

    @pl.when(step + 1 < nsteps)
    def _():
        _for_each_run(n8_ref, l_ref, g_ref, tile + 1, lambda s, g, b, j: _start(run_copy(1 - cur)(s, g, b), j))

    _wait_rows(lambda b: run_copy(cur)(0, 0, b), l_ref[E - 1, tile] + n8_ref[E - 1, tile])

    slot = slot_ref[...].astype(i16)
    wt = wt_ref[...]
    wt_hi = wt.astype(bf16)
    wt_lo = (wt - wt_hi.astype(f32)).astype(bf16)
    def combine(col0, ncols):
        scol = lax.broadcasted_iota(i32, (m, ncols), 1).astype(i16) + jnp.int16(col0)
        w_hi = jnp.zeros((m, ncols), bf16)
        w_lo = jnp.zeros((m, ncols), bf16)
        for kk in range(TOPK):
            hit = scol == slot[:, kk:kk + 1]
            w_hi = jnp.where(hit, wt_hi[:, kk:kk + 1], w_hi)
            w_lo = jnp.where(hit, wt_lo[:, kk:kk + 1], w_lo)
        yb = buf_ref[cur, col0:col0 + ncols, :]
        return _dot(w_hi, yb) + _dot(w_lo, yb)

    routed_ref[...] = combine(0, SLOTS_MAIN)

    @pl.when(l_ref[E - 1, tile] + n8_ref[E - 1, tile] > SLOTS_MAIN)
    def _():
        routed_ref[...] += combine(SLOTS_MAIN, SLOTS - SLOTS_MAIN)
    routed = routed_ref[...]
    g2 = g2_ref[0]
    if nb == 1:
        g2 = g2.reshape(1, D)
    else:
        g2 = jnp.broadcast_to(g2, (nb, lt, D)).reshape(m, D)
    y = x2_ref[...] + g2 * routed
    o_ref[...] = _rms(y, fw_ref[...]).reshape(nb, lt, D)


def _final(tables, slot_t, wts_t, x2, ys, mod4, fw, bsz, seqlen, row0, nb, lt):
    m = nb * lt
    assert m == TILE and row0 % TILE == 0
    n_lt = seqlen // lt
    tile0 = row0 // TILE
    tile = lambda b, i: tile0 + b * n_lt + i
    return pl.pallas_call(
        functools.partial(_final_body, nb, lt, tile0),
        grid_spec=pltpu.PrefetchScalarGridSpec(
            num_scalar_prefetch=len(tables),
            grid=(bsz // nb, n_lt),
            in_specs=[pl.BlockSpec((m, TOPK), lambda b, i, *_: (tile(b, i), 0)),
                      pl.BlockSpec((m, TOPK), lambda b, i, *_: (tile(b, i), 0)),
                      pl.BlockSpec((m, D), lambda b, i, *_: (b * n_lt + i, 0)),
                      pl.BlockSpec((1, nb, 1, D), lambda b, i, *_: (5, b, 0, 0)),
                      pl.BlockSpec((1, D), lambda b, i, *_: (0, 0)),
                      pl.BlockSpec(memory_space=pl.ANY)],
            out_specs=pl.BlockSpec((nb, lt, D), lambda b, i, *_: (b, i, 0)),
            scratch_shapes=[pltpu.VMEM((2, SLOTS, D), bf16), pltpu.VMEM((m, D), f32),
                            pltpu.SemaphoreType.DMA((2,))]),
        out_shape=jax.ShapeDtypeStruct((bsz, seqlen, D), f32),
        compiler_params=pltpu.CompilerParams(dimension_semantics=("arbitrary", "arbitrary"),
                                             vmem_limit_bytes=VMEM_LIMIT),
        name="final",
    )(*tables, slot_t, wts_t, x2, mod4, fw, ys)


def kernel(x_prompt, x_sample, state_conv, state_gla, c_prompt, c_sample, w_ada, b_ada, norm1_w, w_in, conv_w,
           w_gk, b_gk, gla_norm_w, w_out_conv, w_out_gla, w_o, norm2_w, router_w, router_bias, we_gate, we_up,
           we_down, ws_gate, ws_up, ws_down, final_norm_w):
    assert w_ada.shape[0] == 1, "single-layer step"
    bp, lp, _ = x_prompt.shape
    bs, ls, _ = x_sample.shape
    tp, ts = bp * lp, bs * ls
    t_all = tp + ts

    w_in0 = w_in[0]
    n_main = 3 * D + 2 * DK + 2 * DV
    rank = w_gk.shape[1]
    weights = (
        norm1_w[0].reshape(1, D),
        w_in0[:, :n_main].astype(bf16),
        jnp.pad(w_in0[:, n_main:n_main + rank], ((0, 0), (0, RANK_PAD - rank))).astype(bf16),
        w_in0[:, n_main + rank:].astype(bf16),
        jnp.pad(w_gk[0], ((0, RANK_PAD - rank), (0, 0))).astype(bf16),
        b_gk[0].reshape(1, DK),
        conv_w[0],
        gla_norm_w[0].reshape(1, DVH),
        w_out_conv[0].astype(bf16),
        w_out_gla[0].astype(bf16),
        w_o[0].astype(bf16),
        norm2_w[0].reshape(1, D),
        router_w[0].T.astype(bf16),
        ws_gate[0].astype(bf16),
        ws_up[0].astype(bf16),
        ws_down[0].astype(bf16),
    )

    mod = _adaln(jnp.concatenate([c_prompt, c_sample], axis=0), w_ada[0], b_ada[0].reshape(1, 6 * D))
    mod4 = mod.reshape(bp + bs, 6, 1, D).transpose(1, 0, 2, 3)
    mod_p, mod_s = mod4[:, :bp], mod4[:, bp:]

    lt_p = 512
    x2_p, h2_p, lg_p, conv_p, gla_p = _mixer(x_prompt, mod_p, None, weights, 1, lt_p, GLA_CHUNK)
    nb_s = 8
    x2_s, h2_s, lg_s, conv_s, gla_s = _mixer(x_sample, mod_s, (state_conv[0], state_gla[0]), weights,
                                             nb_s, ls, math.gcd(ls, GLA_CHUNK))
    lg = jnp.concatenate([lg_p.transpose(1, 0, 2).reshape(E, tp), lg_s.transpose(1, 0, 2).reshape(E, ts)], axis=1)

    eidx, wts, lrank, cnt = _route(lg, router_bias[0].reshape(E, 1))
    n_tiles = t_all // TILE
    max_rows = t_all * TOPK + n_tiles * E * (RUN_ALIGN - 1) + E * (MOE_BM - RUN_ALIGN)
    nblk = -(-max_rows // MOE_BM)
    slot, n8, loff, goff, blk_e = _plan(eidx, lrank, cnt, -(-nblk // 128) * 128)
    tables = (n8, loff, goff)

    xs = _dispatch(tables, slot, h2_p, h2_s, nblk * MOE_BM)
    ys = _experts(blk_e[0, :nblk], xs, we_gate[0], we_up[0], we_down[0])

    fw = final_norm_w.reshape(1, D)
    slot_t, wts_t = slot.T, wts.T
    y_prompt = _final(tables, slot_t, wts_t, x2_p, ys, mod_p, fw, bp, lp, 0, 1, TILE)
    y_sample = _final(tables, slot_t, wts_t, x2_s, ys, mod_s, fw, bs, ls, tp, TILE // ls, ls)
    return (y_prompt, y_sample, conv_p, gla_p, conv_s, gla_s)
```

```python
import functools
import math

import jax
import jax.numpy as jnp
from jax import lax
from jax.experimental import pallas as pl
from jax.experimental.pallas import tpu as pltpu

f32 = jnp.float32
bf16 = jnp.bfloat16
i32 = jnp.int32
i16 = jnp.int16

D = 1024
H = 4
DKH = 128
DVH = 256
DK = H * DKH
DV = H * DVH
RANK_PAD = 128
GATE_NORMALIZER = 16.0
GLA_CHUNK = 64
E = 64
TOPK = 8
NGROUPS = 8
TOPK_GROUPS = 4
DE = 256
ROUTED_SCALE = 2.5
EPS = 1e-6
NEG = float("-inf")

VMEM_LIMIT = 58 * 1024 * 1024
MOE_BM = 512
TILE = 256
MAX_TILES = 128
RUN_ALIGN = 16
SLOTS = TILE * TOPK + 1024
SLOTS_MAIN = TILE * TOPK + 512
RUN_PIECES = (256, 128, 64, 32, 16)
TAIL_PIECES = (256, 128, 64, 32, 16)
WAIT_PIECES = (2048, 1024, 512, 256, 128, 64, 32, 16)

def _dot(a, b):
    return jnp.dot(a, b, preferred_element_type=f32)


def _split3(x):
    hi = x.astype(bf16)
    r1 = x - hi.astype(f32)
    mid = r1.astype(bf16)
    lo = (r1 - mid.astype(f32)).astype(bf16)
    return hi, mid, lo


def _dot01(m01, x):
    hi, mid, lo = _split3(x)
    return _dot(m01, hi) + _dot(m01, mid) + _dot(m01, lo)


def _dotx01(x, m01):
    hi, mid, lo = _split3(x)
    return _dot(hi, m01) + _dot(mid, m01) + _dot(lo, m01)


def _rms(v, w):
    ms = jnp.mean(v * v, axis=-1, keepdims=True)
    return v * lax.rsqrt(ms + EPS) * w


def _const_spec(shape):
    n = len(shape)
    return pl.BlockSpec(shape, lambda *_: (0,) * n, pipeline_mode=pl.Buffered(1))


def _adaln_body(c_ref, w_ref, b_ref, o_ref):
    c = c_ref[...]
    a = (c * jax.nn.sigmoid(c)).astype(bf16)
    o_ref[...] = _dot(a, w_ref[...].astype(bf16)) + b_ref[...]


def _adaln(c_all, w_ada, b_ada):
    nrow = c_all.shape[0]
    tn = 1536
    return pl.pallas_call(
        _adaln_body,
        grid=(6 * D // tn,),
        in_specs=[pl.BlockSpec((nrow, D), lambda j: (0, 0)),
                  pl.BlockSpec((D, tn), lambda j: (0, j)),
                  pl.BlockSpec((1, tn), lambda j: (0, j))],
        out_specs=pl.BlockSpec((nrow, tn), lambda j: (0, j)),
        out_shape=jax.ShapeDtypeStruct((nrow, 6 * D), f32),
        compiler_params=pltpu.CompilerParams(dimension_semantics=("arbitrary",), vmem_limit_bytes=VMEM_LIMIT),
        name="adaln",
    )(c_all, w_ada, b_ada)


def _mixer_body(nb, lt, chunk, has_state, *refs):
    if has_state:
        (x_ref, mod_ref, cs_ref, gs_ref, *rest) = refs
    else:
        (x_ref, mod_ref, *rest) = refs
        cs_ref = gs_ref = None
    (n1_ref, wmain_ref, wz_ref, wu_ref, wgk_ref, bgk_ref, cw_ref, gnw_ref, woc_ref, wog_ref, wo_ref,
     n2_ref, rwt_ref, wsg_ref, wsu_ref, wsd_ref,
     x2_ref, h2_ref, lg_ref, nc_ref, ng_ref, carry_ref, st_ref, og_ref) = rest
    m = nb * lt
    i = pl.program_id(1)
    last = pl.num_programs(1) - 1

    def rows(v):
        w = v.shape[-1]
        if nb == 1:
            return v.reshape(1, w)
        return jnp.broadcast_to(v, (nb, lt, w)).reshape(m, w)

    @pl.when(i == 0)
    def _():
        if has_state:
            cs = cs_ref[...]
            carry_ref[0] = cs[:, 0:1, :]
            carry_ref[1] = cs[:, 1:2, :]
            for hh in range(H):
                st_ref[hh] = jnp.swapaxes(gs_ref[:, hh], 1, 2)
        else:
            carry_ref[...] = jnp.zeros_like(carry_ref)
            st_ref[...] = jnp.zeros_like(st_ref)

    x = x_ref[...].reshape(m, D)
    sh1, sc1, g1, sh2, sc2, g2 = [rows(mod_ref[j]) for j in range(6)]
    hb = (_rms(x, n1_ref[...]) * (1.0 + sc1) + sh1).astype(bf16)

    def proj(lo, hi):
        return _dot(hb, wmain_ref[:, lo:hi])

    pre = proj(D, 2 * D) * proj(2 * D, 3 * D)
    c0 = rows(carry_ref[0])
    c1 = rows(carry_ref[1])
    l_idx = lax.broadcasted_iota(i32, (m, D), 0) & (lt - 1)
    r1 = pltpu.roll(pre, 1, 0)
    r2 = pltpu.roll(pre, 2, 0)
    prev1 = jnp.where(l_idx == 0, c1, r1)
    prev2 = jnp.where(l_idx == 0, c0, jnp.where(l_idx == 1, c1, r2))
    cw = cw_ref[...]
    conv = cw[0:1] * prev2 + cw[1:2] * prev1 + cw[2:3] * pre
    y_a = _dot((proj(0, D) * conv).astype(bf16), woc_ref[...])
    pre3 = pre.reshape(nb, lt, D)
    tail = pre3[:, lt - 2:lt, :]
    carry_ref[0] = tail[:, 0:1, :]
    carry_ref[1] = tail[:, 1:2, :]
    nc_ref[0] = tail

    q = proj(3 * D, 3 * D + DK)
    k = proj(3 * D + DK, 3 * D + 2 * DK)
    v = proj(4 * D, 5 * D)
    g = proj(5 * D, 6 * D)
    z = _dot(hb, wz_ref[...]).astype(bf16)
    pa = _dot(z, wgk_ref[...]) + bgk_ref[...]
    la = (jnp.minimum(pa, 0.0) - jnp.log1p(jnp.exp(-jnp.abs(pa)))) / GATE_NORMALIZER
    rr = lax.broadcasted_iota(i32, (m, m), 0)
    cc = lax.broadcasted_iota(i32, (m, m), 1)
    same_chunk = (rr & ~(chunk - 1)) == (cc & ~(chunk - 1))
    tri = (same_chunk & (cc <= rr)).astype(bf16)
    bcum = _dot01(tri, la)
    ngrp = m // chunk
    bend = bcum.reshape(ngrp, chunk, DK)[:, chunk - 1:chunk, :]
    bend_rows = jnp.broadcast_to(bend, (ngrp, chunk, DK)).reshape(m, DK)
    q_dec = (q * (DKH ** -0.5) * jnp.exp(bcum)).astype(bf16).reshape(nb, lt, DK)
    k_dec = (k * jnp.exp(-bcum)).astype(bf16).reshape(nb, lt, DK)
    k_end = (k * jnp.exp(bend_rows - bcum)).astype(bf16).reshape(nb, lt, DK)
    a_end = jnp.exp(bend_rows).reshape(nb, lt, DK)
    v3 = v.astype(bf16).reshape(nb, lt, DV)
    g3 = g.reshape(nb, lt, DV)
    causal = lax.broadcasted_iota(i32, (chunk, chunk), 1) <= lax.broadcasted_iota(i32, (chunk, chunk), 0)
    gnw = gnw_ref[...]
    for j in range(lt // chunk):
        js = slice(j * chunk, (j + 1) * chunk)
        for hh in range(H):
            ks = slice(hh * DKH, (hh + 1) * DKH)
            vs = slice(hh * DVH, (hh + 1) * DVH)
            qd, kd, ke, vv = q_dec[:, js, ks], k_dec[:, js, ks], k_end[:, js, ks], v3[:, js, vs]
            st = st_ref[hh]
            o = jnp.einsum('bcd,bed->bce', qd, st.astype(bf16), preferred_element_type=f32)
            sc = jnp.einsum('bcd,bsd->bcs', qd, kd, preferred_element_type=f32)
            sc = jnp.where(causal, sc, 0.0).astype(bf16)
            o = o + jnp.einsum('bcs,bse->bce', sc, vv, preferred_element_type=f32)
            a_row = a_end[:, j * chunk:j * chunk + 1, ks]
            st_ref[hh] = st * a_row + jnp.einsum('bse,bsd->bed', vv, ke, preferred_element_type=f32)
            on = _rms(o, gnw)
            gh = g3[:, js, vs]
            og_ref[:, js, vs] = (on * (gh * jax.nn.sigmoid(gh))).astype(bf16)

    @pl.when(i == last)
    def _():
        for hh in range(H):
            ng_ref[0, :, hh] = jnp.swapaxes(st_ref[hh], 1, 2)

    y_b = _dot(og_ref[...].reshape(m, DV), wog_ref[...])
    u_a = _dot(hb, wu_ref[:, 0:D])
    u_b = _dot(hb, wu_ref[:, D:2 * D])
    merged = (jax.nn.sigmoid(u_a) * y_a + jax.nn.sigmoid(u_b) * y_b).astype(bf16)
    x1 = x + g1 * _dot(merged, wo_ref[...])

    h2 = (_rms(x1, n2_ref[...]) * (1.0 + sc2) + sh2).astype(bf16)
    sg = _dot(h2, wsg_ref[...])
    act = (sg * jax.nn.sigmoid(sg) * _dot(h2, wsu_ref[...])).astype(bf16)
    x2_ref[...] = x1 + g2 * _dot(act, wsd_ref[...])
    h2_ref[...] = h2
    lg_ref[0] = lax.dot_general(rwt_ref[...], h2, (((1,), (1,)), ((), ())), preferred_element_type=f32)


def _mixer(x, mod4, states, weights, nb, lt, chunk):
    bsz, seqlen, _ = x.shape
    has_state = states is not None
    m = nb * lt
    n_lt = seqlen // lt
    nsteps = (bsz // nb) * n_lt
    t_all = bsz * seqlen

    def tok_blk(b, i):
        return b * n_lt + i

    in_specs = [pl.BlockSpec((nb, lt, D), lambda b, i: (b, i, 0)),
                pl.BlockSpec((6, nb, 1, D), lambda b, i: (0, b, 0, 0))]
    args = [x, mod4]
    if has_state:
        in_specs += [pl.BlockSpec((nb, 2, D), lambda b, i: (b, 0, 0)),
                     pl.BlockSpec((nb, H, DKH, DVH), lambda b, i: (b, 0, 0, 0))]
        args += list(states)
    in_specs += [_const_spec(w.shape) for w in weights]
    args += list(weights)
    out_specs = [pl.BlockSpec((m, D), lambda b, i: (tok_blk(b, i), 0)),
                 pl.BlockSpec((m, D), lambda b, i: (tok_blk(b, i), 0)),
                 pl.BlockSpec((1, E, m), lambda b, i: (b * n_lt + i, 0, 0)),
                 pl.BlockSpec((1, nb, 2, D), lambda b, i: (0, b, 0, 0)),
                 pl.BlockSpec((1, nb, H, DKH, DVH), lambda b, i: (0, b, 0, 0, 0))]
    out_shape = [jax.ShapeDtypeStruct((t_all, D), f32),
                 jax.ShapeDtypeStruct((t_all, D), bf16),
                 jax.ShapeDtypeStruct((nsteps, E, m), f32),
                 jax.ShapeDtypeStruct((1, bsz, 2, D), f32),
                 jax.ShapeDtypeStruct((1, bsz, H, DKH, DVH), f32)]
    return pl.pallas_call(
        functools.partial(_mixer_body, nb, lt, chunk, has_state),
        grid=(bsz // nb, n_lt),
        in_specs=in_specs,
        out_specs=out_specs,
        out_shape=out_shape,
        scratch_shapes=[pltpu.VMEM((2, nb, 1, D), f32),
                        pltpu.VMEM((H, nb, DVH, DKH), f32),
                        pltpu.VMEM((nb, lt, DV), bf16)],
        compiler_params=pltpu.CompilerParams(dimension_semantics=("arbitrary", "arbitrary"),
                                             vmem_limit_bytes=VMEM_LIMIT),
        name="mixer_state" if has_state else "mixer_prompt",
    )(*args)


def _route_body(lg_ref, bias_ref, eidx_ref, w_ref, lrank_ref, cnt_ref):
    step = pl.program_id(0)
    tl = lg_ref.shape[1]

    scores = jax.nn.sigmoid(lg_ref[...])
    choice = scores + bias_ref[...]
    gsz = E // NGROUPS
    grp = choice.reshape(NGROUPS, gsz, tl)
    mi = lax.broadcasted_iota(i32, (NGROUPS, gsz, tl), 1)
    m1 = jnp.max(grp, axis=1, keepdims=True)
    first = jnp.min(jnp.where(grp == m1, mi, gsz), axis=1, keepdims=True)
    m2 = jnp.max(jnp.where(mi == first, NEG, grp), axis=1, keepdims=True)
    gscore = (m1 + m2).reshape(NGROUPS, tl)
    gi = lax.broadcasted_iota(i32, (NGROUPS, tl), 0)
    gsel = jnp.zeros((NGROUPS, tl), f32)
    work = gscore
    for _ in range(TOPK_GROUPS):
        mx = jnp.max(work, axis=0, keepdims=True)
        f = jnp.min(jnp.where(work == mx, gi, NGROUPS), axis=0, keepdims=True)
        hit = gi == f
        gsel = jnp.where(hit, 1.0, gsel)
        work = jnp.where(hit, NEG, work)
    emask = jnp.broadcast_to(gsel.reshape(NGROUPS, 1, tl), (NGROUPS, gsz, tl)).reshape(E, tl)
    masked = jnp.where(emask > 0.0, choice, NEG)
    ei = lax.broadcasted_iota(i32, (E, tl), 0)
    sel = jnp.zeros((E, tl), f32)
    hits, ws = [], []
    for kk in range(TOPK):
        mx = jnp.max(masked, axis=0, keepdims=True)
        f = jnp.min(jnp.where(masked == mx, ei, E), axis=0, keepdims=True)
        hit = ei == f
        ws.append(jnp.sum(jnp.where(hit, scores, 0.0), axis=0, keepdims=True))
        masked = jnp.where(hit, NEG, masked)
        sel = jnp.where(hit, 1.0, sel)
        hits.append(hit)
        eidx_ref[kk:kk + 1, :] = f
    wsum = ws[0]
    for t in ws[1:]:
        wsum = wsum + t
    for kk in range(TOPK):
        w_ref[kk:kk + 1, :] = ws[kk] / wsum * ROUTED_SCALE

    ui = lax.broadcasted_iota(i32, (tl, tl), 0)
    uj = lax.broadcasted_iota(i32, (tl, tl), 1)
    upper = (ui <= uj).astype(bf16)
    lrank = _dot(sel.astype(bf16), upper) - sel
    for kk in range(TOPK):
        lrank_ref[kk:kk + 1, :] = jnp.sum(jnp.where(hits[kk], lrank, 0.0), axis=0, keepdims=True).astype(i32)

    @pl.when(step == 0)
    def _():
        cnt_ref[...] = jnp.zeros_like(cnt_ref)
    lane = lax.broadcasted_iota(i32, cnt_ref.shape, 1)
    cnt_ref[...] = jnp.where(lane == step, jnp.sum(sel, axis=1, keepdims=True), cnt_ref[...])


def _route(logits_t, bias_col):
    t_all = logits_t.shape[1]
    tl = TILE
    assert t_all // tl <= MAX_TILES
    tok = lambda s: (0, s)
    return pl.pallas_call(
        _route_body,
        grid=(t_all // tl,),
        in_specs=[pl.BlockSpec((E, tl), tok), pl.BlockSpec((E, 1), lambda s: (0, 0))],
        out_specs=[pl.BlockSpec((TOPK, tl), tok), pl.BlockSpec((TOPK, tl), tok), pl.BlockSpec((TOPK, tl), tok),
                   pl.BlockSpec((E, MAX_TILES), lambda s: (0, 0))],
        out_shape=[jax.ShapeDtypeStruct((TOPK, t_all), i32), jax.ShapeDtypeStruct((TOPK, t_all), f32),
                   jax.ShapeDtypeStruct((TOPK, t_all), i32), jax.ShapeDtypeStruct((E, MAX_TILES), f32)],
        compiler_params=pltpu.CompilerParams(dimension_semantics=("arbitrary",), vmem_limit_bytes=VMEM_LIMIT),
        name="route",
    )(logits_t, bias_col)


def _plan_body(eidx_ref, lrank_ref, cnt_ref, slot_ref, n8_ref, l_ref, g_ref, blk_ref):
    i = pl.program_id(0)
    tl = eidx_ref.shape[1]

    @pl.when(i == 0)
    def _():
        cnt = cnt_ref[...]
        n8 = jnp.floor((cnt + (RUN_ALIGN - 1)) * (1.0 / RUN_ALIGN)) * RUN_ALIGN
        ti = lax.broadcasted_iota(i32, (MAX_TILES, MAX_TILES), 0)
        tj = lax.broadcasted_iota(i32, (MAX_TILES, MAX_TILES), 1)
        before = _dotx01(n8, (ti < tj).astype(bf16))
        region = jnp.sum(n8, axis=1, keepdims=True)
        nblocks = jnp.floor((region + (MOE_BM - 1)) * (1.0 / MOE_BM))
        ri = lax.broadcasted_iota(i32, (E, E), 0)
        ci = lax.broadcasted_iota(i32, (E, E), 1)
        below = (ci < ri).astype(bf16)
        bstart = _dot01(below, jnp.broadcast_to(nblocks, (E, MAX_TILES)))
        n8_ref[...] = n8.astype(i32)
        l_ref[...] = _dot01(below, n8).astype(i32)
        g_ref[...] = (bstart * MOE_BM + before).astype(i32)
        lane0 = lax.broadcasted_iota(i32, (E, MAX_TILES), 1) == 0
        blk_ref[...] = jnp.where(lane0, bstart, nblocks).astype(i32)

    lane = lax.broadcasted_iota(i32, (E, MAX_TILES), 1)
    lcol = jnp.sum(jnp.where(lane == i, l_ref[...].astype(f32), 0.0), axis=1, keepdims=True)
    ei = lax.broadcasted_iota(i32, (E, tl), 0)
    for kk in range(TOPK):
        hit = ei == eidx_ref[kk:kk + 1, :]
        start = jnp.sum(jnp.where(hit, lcol, 0.0), axis=0, keepdims=True)
        slot_ref[kk:kk + 1, :] = start.astype(i32) + lrank_ref[kk:kk + 1, :]


def _plan(eidx, lrank, cnt):
    t_all = eidx.shape[1]
    tl = TILE
    tok = lambda s: (0, s)
    table = pl.BlockSpec((E, MAX_TILES), lambda s: (0, 0))
    return pl.pallas_call(
        _plan_body,
        grid=(t_all // tl,),
        in_specs=[pl.BlockSpec((TOPK, tl), tok), pl.BlockSpec((TOPK, tl), tok), table],
        out_specs=[pl.BlockSpec((TOPK, tl), tok), table, table, table, table],
        out_shape=[jax.ShapeDtypeStruct((TOPK, t_all), i32)] + [jax.ShapeDtypeStruct((E, MAX_TILES), i32)] * 4,
        compiler_params=pltpu.CompilerParams(dimension_semantics=("arbitrary",), vmem_limit_bytes=VMEM_LIMIT),
        name="plan",
    )(eidx, lrank, cnt)


def _for_each_piece(total, pieces, fn):
    for j, b in enumerate(pieces):
        @pl.when((total & b) != 0)
        def _():
            fn(pl.multiple_of(total & ~(2 * b - 1), pieces[-1]), b, j)


def _for_each_run(n8_ref, l_ref, g_ref, tile, fn):
    def per_expert(e, c):
        lo = pl.multiple_of(l_ref[e, tile], RUN_ALIGN)
        go = pl.multiple_of(g_ref[e, tile], RUN_ALIGN)
        _for_each_piece(n8_ref[e, tile], RUN_PIECES, lambda off, b, j: fn(lo + off, go + off, b, j))
        return c
    lax.fori_loop(0, E, per_expert, 0)


def _start(copy, j):
    copy.start(priority=j % 2)


def _wait_rows(make_copy, rows):
    _for_each_piece(rows, WAIT_PIECES, lambda off, b, j: make_copy(b).wait())


def _dispatch_body(tiles_a, n8_ref, l_ref, g_ref, slot_ref, ha_ref, hb_ref, xs_ref, buf_ref, zero_ref, sem, zsem):
    tile = pl.program_id(0)
    i = tile
    nsteps = pl.num_programs(0)
    cur = i % 2

    slot = slot_ref[...].astype(i16)
    h = jnp.where(tile < tiles_a, ha_ref[...], hb_ref[...])
    filled = l_ref[E - 1, tile] + n8_ref[E - 1, tile]

    def order(row0, nrows):
        srow = lax.broadcasted_iota(i32, (nrows, TILE), 0).astype(i16) + jnp.int16(row0)
        onehot = jnp.zeros((nrows, TILE), bf16)
        for kk in range(TOPK):
            onehot = jnp.where(srow == slot[kk:kk + 1, :], jnp.ones((), bf16), onehot)
        buf_ref[cur, row0:row0 + nrows, :] = _dot(onehot, h).astype(bf16)

    order(0, SLOTS_MAIN)

    @pl.when(filled > SLOTS_MAIN)
    def _():
        order(SLOTS_MAIN, SLOTS - SLOTS_MAIN)

    def run_copy(sl):
        return lambda srow_, grow, b: pltpu.make_async_copy(
            buf_ref.at[sl, pl.ds(srow_, b)], xs_ref.at[pl.ds(grow, b)], sem.at[sl])

    _for_each_run(n8_ref, l_ref, g_ref, tile, lambda s, g, b, j: _start(run_copy(cur)(s, g, b), j))

    def wait_tile(t, sl):
        _wait_rows(lambda b: run_copy(sl)(0, 0, b), l_ref[E - 1, t] + n8_ref[E - 1, t])

    @pl.when(i > 0)
    def _():
        wait_tile(tile - 1, 1 - cur)

    last = nsteps - 1

    def zero_fill(fn):
        def per_expert(e, c):
            end = pl.multiple_of(g_ref[e, last] + n8_ref[e, last], RUN_ALIGN)
            _for_each_piece((-end) & (MOE_BM - 1), TAIL_PIECES, lambda off, b, j: fn(pltpu.make_async_copy(
                zero_ref.at[pl.ds(0, b)], xs_ref.at[pl.ds(end + off, b)], zsem)))
            return c
        lax.fori_loop(0, E, per_expert, 0)
        used = (g_ref[E - 1, last] + n8_ref[E - 1, last] + (MOE_BM - 1)) // MOE_BM

        def per_block(blk, c):
            fn(pltpu.make_async_copy(zero_ref, xs_ref.at[pl.ds(pl.multiple_of(blk * MOE_BM, MOE_BM), MOE_BM)], zsem))
            return c
        lax.fori_loop(used, xs_ref.shape[0] // MOE_BM, per_block, 0)

    @pl.when(i == 0)
    def _():
        zero_ref[...] = jnp.zeros_like(zero_ref)
        zero_fill(lambda cp: cp.start())

    @pl.when(i == last)
    def _():
        wait_tile(tile, cur)
        zero_fill(lambda cp: cp.wait())


def _dispatch(tables, slot, h2_a, h2_b, nrows):
    tiles_a, tiles_b = h2_a.shape[0] // TILE, h2_b.shape[0] // TILE
    return pl.pallas_call(
        functools.partial(_dispatch_body, tiles_a),
        grid_spec=pltpu.PrefetchScalarGridSpec(
            num_scalar_prefetch=len(tables),
            grid=(tiles_a + tiles_b,),
            in_specs=[pl.BlockSpec((TOPK, TILE), lambda i, *_: (0, i)),
                      pl.BlockSpec((TILE, D), lambda i, *_: (jnp.minimum(i, tiles_a - 1), 0)),
                      pl.BlockSpec((TILE, D), lambda i, *_: (jnp.maximum(i - tiles_a, 0), 0))],
            out_specs=pl.BlockSpec(memory_space=pl.ANY),
            scratch_shapes=[pltpu.VMEM((2, SLOTS, D), bf16), pltpu.VMEM((MOE_BM, D), bf16),
                            pltpu.SemaphoreType.DMA((2,)), pltpu.SemaphoreType.DMA]),
        out_shape=jax.ShapeDtypeStruct((nrows, D), bf16),
        compiler_params=pltpu.CompilerParams(dimension_semantics=("arbitrary",), vmem_limit_bytes=VMEM_LIMIT),
        name="dispatch",
    )(*tables, slot, h2_a, h2_b)


def _expert_body(blk_ref, wg_ref, wu_ref, wd_ref, xs_ref, ys_ref, xbuf_ref, ybuf_ref, wgu_ref, wdb_ref, isem, osem):
    e = pl.program_id(0)
    first, count = blk_ref[e, 0], blk_ref[e, 1]
    wgu_ref[:, :DE] = wg_ref[0].astype(bf16)
    wgu_ref[:, DE:] = wu_ref[0].astype(bf16)
    wdb_ref[...] = wd_ref[0].astype(bf16)

    def rows(j):
        return pl.ds(pl.multiple_of((first + j) * MOE_BM, MOE_BM), MOE_BM)

    def load(j, sl):
        return pltpu.make_async_copy(xs_ref.at[rows(j)], xbuf_ref.at[sl], isem.at[sl])

    def store(j, sl):
        return pltpu.make_async_copy(ybuf_ref.at[sl], ys_ref.at[rows(j)], osem.at[sl])

    @pl.when(count > 0)
    def _():
        load(0, 0).start()

    def block(j, c):
        sl = j % 2

        @pl.when(j + 1 < count)
        def _():
            load(j + 1, 1 - sl).start()
        load(j, sl).wait()

        @pl.when(j >= 2)
        def _():
            store(j - 2, sl).wait()
        hgu = _dot(xbuf_ref[sl], wgu_ref[...])
        hg = hgu[:, :DE]
        act = (hg * jax.nn.sigmoid(hg) * hgu[:, DE:]).astype(bf16)
        ybuf_ref[sl] = _dot(act, wdb_ref[...]).astype(bf16)
        store(j, sl).start()
        return c
    lax.fori_loop(0, count, block, 0)

    @pl.when(count >= 2)
    def _():
        store(count - 2, count % 2).wait()

    @pl.when(count >= 1)
    def _():
        store(count - 1, (count - 1) % 2).wait()


def _experts(blk, xs, we_gate, we_up, we_down):
    ex = lambda e, blk_: (e, 0, 0)
    return pl.pallas_call(
        _expert_body,
        grid_spec=pltpu.PrefetchScalarGridSpec(
            num_scalar_prefetch=1,
            grid=(E,),
            in_specs=[pl.BlockSpec((1, D, DE), ex), pl.BlockSpec((1, D, DE), ex), pl.BlockSpec((1, DE, D), ex),
                      pl.BlockSpec(memory_space=pl.ANY)],
            out_specs=pl.BlockSpec(memory_space=pl.ANY),
            scratch_shapes=[pltpu.VMEM((2, MOE_BM, D), bf16), pltpu.VMEM((2, MOE_BM, D), bf16),
                            pltpu.VMEM((D, 2 * DE), bf16), pltpu.VMEM((DE, D), bf16),
                            pltpu.SemaphoreType.DMA((2,)), pltpu.SemaphoreType.DMA((2,))]),
        out_shape=jax.ShapeDtypeStruct(xs.shape, xs.dtype),
        input_output_aliases={4: 0},
        compiler_params=pltpu.CompilerParams(dimension_semantics=("arbitrary",), vmem_limit_bytes=VMEM_LIMIT),
        name="experts",
    )(blk, we_gate, we_up, we_down, xs)


def _final_body(nb, lt, tile0, n8_ref, l_ref, g_ref, slot_ref, wt_ref, x2_ref, g2_ref, fw_ref, ys_ref, o_ref,
                buf_ref, routed_ref, sem):
    m = nb * lt
    n_inner = pl.num_programs(1)
    step = pl.program_id(0) * n_inner + pl.program_id(1)
    nsteps = pl.num_programs(0) * n_inner
    tile = tile0 + step
    cur = step % 2

    def run_copy(sl):
        return lambda srow, grow, b: pltpu.make_async_copy(
            ys_ref.at[pl.ds(grow, b)], buf_ref.at[sl, pl.ds(srow, b)], sem.at[sl])

    @pl.when(step == 0)
    def _():
        buf_ref[...] = jnp.zeros_like(buf_ref)
        _for_each_run(n8_ref, l_ref, g_ref, tile, lambda s, g, b, j: _start(run_copy(0)(s, g, b), j))

    @pl.when(step + 1 < nsteps)
    def _():
        _for_each_run(n8_ref, l_ref, g_ref, tile + 1, lambda s, g, b, j: _start(run_copy(1 - cur)(s, g, b), j))

    _wait_rows(lambda b: run_copy(cur)(0, 0, b), l_ref[E - 1, tile] + n8_ref[E - 1, tile])

    slot = slot_ref[...].astype(i16)
    wt = wt_ref[...]
    wt_hi = wt.astype(bf16)
    wt_lo = (wt - wt_hi.astype(f32)).astype(bf16)
    def combine(col0, ncols):
        scol = lax.broadcasted_iota(i32, (m, ncols), 1).astype(i16) + jnp.int16(col0)
        w_hi = jnp.zeros((m, ncols), bf16)
        w_lo = jnp.zeros((m, ncols), bf16)
        for kk in range(TOPK):
            hit = scol == slot[:, kk:kk + 1]
            w_hi = jnp.where(hit, wt_hi[:, kk:kk + 1], w_hi)
            w_lo = jnp.where(hit, wt_lo[:, kk:kk + 1], w_lo)
        yb = buf_ref[cur, col0:col0 + ncols, :]
        return _dot(w_hi, yb) + _dot(w_lo, yb)

    routed_ref[...] = combine(0, SLOTS_MAIN)

    @pl.when(l_ref[E - 1, tile] + n8_ref[E - 1, tile] > SLOTS_MAIN)
    def _():
        routed_ref[...] += combine(SLOTS_MAIN, SLOTS - SLOTS_MAIN)
    routed = routed_ref[...]
    g2 = g2_ref[0]
    if nb == 1:
        g2 = g2.reshape(1, D)
    else:
        g2 = jnp.broadcast_to(g2, (nb, lt, D)).reshape(m, D)
    y = x2_ref[...] + g2 * routed
    o_ref[...] = _rms(y, fw_ref[...]).reshape(nb, lt, D)


def _final(tables, slot_t, wts_t, x2, ys, mod4, fw, bsz, seqlen, row0, nb, lt):
    m = nb * lt
    assert m == TILE and row0 % TILE == 0
    n_lt = seqlen // lt
    tile0 = row0 // TILE
    tile = lambda b, i: tile0 + b * n_lt + i
    return pl.pallas_call(
        functools.partial(_final_body, nb, lt, tile0),
        grid_spec=pltpu.PrefetchScalarGridSpec(
            num_scalar_prefetch=len(tables),
            grid=(bsz // nb, n_lt),
            in_specs=[pl.BlockSpec((m, TOPK), lambda b, i, *_: (tile(b, i), 0)),
                      pl.BlockSpec((m, TOPK), lambda b, i, *_: (tile(b, i), 0)),
                      pl.BlockSpec((m, D), lambda b, i, *_: (b * n_lt + i, 0)),
                      pl.BlockSpec((1, nb, 1, D), lambda b, i, *_: (5, b, 0, 0)),
                      pl.BlockSpec((1, D), lambda b, i, *_: (0, 0)),
                      pl.BlockSpec(memory_space=pl.ANY)],
            out_specs=pl.BlockSpec((nb, lt, D), lambda b, i, *_: (b, i, 0)),
            scratch_shapes=[pltpu.VMEM((2, SLOTS, D), bf16), pltpu.VMEM((m, D), f32),
                            pltpu.SemaphoreType.DMA((2,))]),
        out_shape=jax.ShapeDtypeStruct((bsz, seqlen, D), f32),
        compiler_params=pltpu.CompilerParams(dimension_semantics=("arbitrary", "arbitrary"),
                                             vmem_limit_bytes=VMEM_LIMIT),
        name="final",
    )(*tables, slot_t, wts_t, x2, mod4, fw, ys)


def kernel(x_prompt, x_sample, state_conv, state_gla, c_prompt, c_sample, w_ada, b_ada, norm1_w, w_in, conv_w,
           w_gk, b_gk, gla_norm_w, w_out_conv, w_out_gla, w_o, norm2_w, router_w, router_bias, we_gate, we_up,
           we_down, ws_gate, ws_up, ws_down, final_norm_w):
    assert w_ada.shape[0] == 1, "single-layer step"
    bp, lp, _ = x_prompt.shape
    bs, ls, _ = x_sample.shape
    tp, ts = bp * lp, bs * ls
    t_all = tp + ts

    w_in0 = w_in[0]
    n_main = 3 * D + 2 * DK + 2 * DV
    rank = w_gk.shape[1]
    weights = (
        norm1_w[0].reshape(1, D),
        w_in0[:, :n_main].astype(bf16),
        jnp.pad(w_in0[:, n_main:n_main + rank], ((0, 0), (0, RANK_PAD - rank))).astype(bf16),
        w_in0[:, n_main + rank:].astype(bf16),
        jnp.pad(w_gk[0], ((0, RANK_PAD - rank), (0, 0))).astype(bf16),
        b_gk[0].reshape(1, DK),
        conv_w[0],
        gla_norm_w[0].reshape(1, DVH),
        w_out_conv[0].astype(bf16),
        w_out_gla[0].astype(bf16),
        w_o[0].astype(bf16),
        norm2_w[0].reshape(1, D),
        router_w[0].T.astype(bf16),
        ws_gate[0].astype(bf16),
        ws_up[0].astype(bf16),
        ws_down[0].astype(bf16),
    )

    mod = _adaln(jnp.concatenate([c_prompt, c_sample], axis=0), w_ada[0], b_ada[0].reshape(1, 6 * D))
    mod4 = mod.reshape(bp + bs, 6, 1, D).transpose(1, 0, 2, 3)
    mod_p, mod_s = mod4[:, :bp], mod4[:, bp:]

    lt_p = 512
    x2_p, h2_p, lg_p, conv_p, gla_p = _mixer(x_prompt, mod_p, None, weights, 1, lt_p, GLA_CHUNK)
    nb_s = 8
    x2_s, h2_s, lg_s, conv_s, gla_s = _mixer(x_sample, mod_s, (state_conv[0], state_gla[0]), weights,
                                             nb_s, ls, math.gcd(ls, GLA_CHUNK))
    lg = jnp.concatenate([lg_p.transpose(1, 0, 2).reshape(E, tp), lg_s.transpose(1, 0, 2).reshape(E, ts)], axis=1)

    eidx, wts, lrank, cnt = _route(lg, router_bias[0].reshape(E, 1))
    n_tiles = t_all // TILE
    max_rows = t_all * TOPK + n_tiles * E * (RUN_ALIGN - 1) + E * (MOE_BM - RUN_ALIGN)
    nblk = -(-max_rows // MOE_BM)
    slot, n8, loff, goff, blk = _plan(eidx, lrank, cnt)
    tables = (n8, loff, goff)

    xs = _dispatch(tables, slot, h2_p, h2_s, nblk * MOE_BM)
    ys = _experts(blk, xs, we_gate[0], we_up[0], we_down[0])

    fw = final_norm_w.reshape(1, D)
    slot_t, wts_t = slot.T, wts.T
    y_prompt = _final(tables, slot_t, wts_t, x2_p, ys, mod_p, fw, bp, lp, 0, 1, TILE)
    y_sample = _final(tables, slot_t, wts_t, x2_s, ys, mod_s, fw, bs, ls, tp, TILE // ls, ls)
    return (y_prompt, y_sample, conv_p, gla_p, conv_s, gla_s)
```

```python
import functools
import math

import jax
import jax.numpy as jnp
from jax import lax
from jax.experimental import pallas as pl
from jax.experimental.pallas import tpu as pltpu

f32 = jnp.float32
bf16 = jnp.bfloat16
i32 = jnp.int32
i16 = jnp.int16

D = 1024
H = 4
DKH = 128
DVH = 256
DK = H * DKH
DV = H * DVH
RANK_PAD = 128
GATE_NORMALIZER = 16.0
GLA_CHUNK = 64
E = 64
TOPK = 8
NGROUPS = 8
TOPK_GROUPS = 4
DE = 256
ROUTED_SCALE = 2.5
EPS = 1e-6
NEG = float("-inf")

VMEM_LIMIT = 58 * 1024 * 1024
MOE_BM = 512
EXPERT_DMA_PARTS = 4
TILE = 256
MAX_TILES = 128
RUN_ALIGN = 16
SLOTS = TILE * TOPK + 1024
SLOTS_MAIN = TILE * TOPK + 512
RUN_PIECES = (256, 128, 64, 32, 16)
TAIL_PIECES = (256, 128, 64, 32, 16)
WAIT_PIECES = (2048, 1024, 512, 256, 128, 64, 32, 16)

def _dot(a, b):
    return jnp.dot(a, b, preferred_element_type=f32)


def _split3(x):
    hi = x.astype(bf16)
    r1 = x - hi.astype(f32)
    mid = r1.astype(bf16)
    lo = (r1 - mid.astype(f32)).astype(bf16)
    return hi, mid, lo


def _dot01(m01, x):
    hi, mid, lo = _split3(x)
    return _dot(m01, hi) + _dot(m01, mid) + _dot(m01, lo)


def _dotx01(x, m01):
    hi, mid, lo = _split3(x)
    return _dot(hi, m01) + _dot(mid, m01) + _dot(lo, m01)


def _rms(v, w):
    ms = jnp.mean(v * v, axis=-1, keepdims=True)
    return v * lax.rsqrt(ms + EPS) * w


def _const_spec(shape):
    n = len(shape)
    return pl.BlockSpec(shape, lambda *_: (0,) * n, pipeline_mode=pl.Buffered(1))


def _adaln_body(c_ref, w_ref, b_ref, o_ref):
    c = c_ref[...]
    a = (c * jax.nn.sigmoid(c)).astype(bf16)
    o_ref[...] = _dot(a, w_ref[...].astype(bf16)) + b_ref[...]


def _adaln(c_all, w_ada, b_ada):
    nrow = c_all.shape[0]
    tn = 1536
    return pl.pallas_call(
        _adaln_body,
        grid=(6 * D // tn,),
        in_specs=[pl.BlockSpec((nrow, D), lambda j: (0, 0)),
                  pl.BlockSpec((D, tn), lambda j: (0, j)),
                  pl.BlockSpec((1, tn), lambda j: (0, j))],
        out_specs=pl.BlockSpec((nrow, tn), lambda j: (0, j)),
        out_shape=jax.ShapeDtypeStruct((nrow, 6 * D), f32),
        compiler_params=pltpu.CompilerParams(dimension_semantics=("arbitrary",), vmem_limit_bytes=VMEM_LIMIT),
        name="adaln",
    )(c_all, w_ada, b_ada)


def _mixer_body(nb, lt, chunk, has_state, *refs):
    if has_state:
        (x_ref, mod_ref, cs_ref, gs_ref, *rest) = refs
    else:
        (x_ref, mod_ref, *rest) = refs
        cs_ref = gs_ref = None
    (n1_ref, wmain_ref, wz_ref, wu_ref, wgk_ref, bgk_ref, cw_ref, gnw_ref, woc_ref, wog_ref, wo_ref,
     n2_ref, rwt_ref, wsg_ref, wsu_ref, wsd_ref,
     x2_ref, h2_ref, lg_ref, nc_ref, ng_ref, carry_ref, st_ref, og_ref) = rest
    m = nb * lt
    i = pl.program_id(1)
    last = pl.num_programs(1) - 1

    def rows(v):
        w = v.shape[-1]
        if nb == 1:
            return v.reshape(1, w)
        return jnp.broadcast_to(v, (nb, lt, w)).reshape(m, w)

    @pl.when(i == 0)
    def _():
        if has_state:
            cs = cs_ref[...]
            carry_ref[0] = cs[:, 0:1, :]
            carry_ref[1] = cs[:, 1:2, :]
            for hh in range(H):
                st_ref[hh] = jnp.swapaxes(gs_ref[:, hh], 1, 2)
        else:
            carry_ref[...] = jnp.zeros_like(carry_ref)
            st_ref[...] = jnp.zeros_like(st_ref)

    x = x_ref[...].reshape(m, D)
    sh1, sc1, g1, sh2, sc2, g2 = [rows(mod_ref[j]) for j in range(6)]
    hb = (_rms(x, n1_ref[...]) * (1.0 + sc1) + sh1).astype(bf16)

    def proj(lo, hi):
        return _dot(hb, wmain_ref[:, lo:hi])

    pre = proj(D, 2 * D) * proj(2 * D, 3 * D)
    c0 = rows(carry_ref[0])
    c1 = rows(carry_ref[1])
    l_idx = lax.broadcasted_iota(i32, (m, D), 0) & (lt - 1)
    r1 = pltpu.roll(pre, 1, 0)
    r2 = pltpu.roll(pre, 2, 0)
    prev1 = jnp.where(l_idx == 0, c1, r1)
    prev2 = jnp.where(l_idx == 0, c0, jnp.where(l_idx == 1, c1, r2))
    cw = cw_ref[...]
    conv = cw[0:1] * prev2 + cw[1:2] * prev1 + cw[2:3] * pre
    y_a = _dot((proj(0, D) * conv).astype(bf16), woc_ref[...])
    pre3 = pre.reshape(nb, lt, D)
    tail = pre3[:, lt - 2:lt, :]
    carry_ref[0] = tail[:, 0:1, :]
    carry_ref[1] = tail[:, 1:2, :]
    nc_ref[0] = tail

    q = proj(3 * D, 3 * D + DK)
    k = proj(3 * D + DK, 3 * D + 2 * DK)
    v = proj(4 * D, 5 * D)
    g = proj(5 * D, 6 * D)
    z = _dot(hb, wz_ref[...]).astype(bf16)
    pa = _dot(z, wgk_ref[...]) + bgk_ref[...]
    la = (jnp.minimum(pa, 0.0) - jnp.log1p(jnp.exp(-jnp.abs(pa)))) / GATE_NORMALIZER
    rr = lax.broadcasted_iota(i32, (m, m), 0)
    cc = lax.broadcasted_iota(i32, (m, m), 1)
    same_chunk = (rr & ~(chunk - 1)) == (cc & ~(chunk - 1))
    tri = (same_chunk & (cc <= rr)).astype(bf16)
    bcum = _dot01(tri, la)
    ngrp = m // chunk
    bend = bcum.reshape(ngrp, chunk, DK)[:, chunk - 1:chunk, :]
    bend_rows = jnp.broadcast_to(bend, (ngrp, chunk, DK)).reshape(m, DK)
    q_dec = (q * (DKH ** -0.5) * jnp.exp(bcum)).astype(bf16).reshape(nb, lt, DK)
    k_dec = (k * jnp.exp(-bcum)).astype(bf16).reshape(nb, lt, DK)
    k_end = (k * jnp.exp(bend_rows - bcum)).astype(bf16).reshape(nb, lt, DK)
    a_end = jnp.exp(bend_rows).reshape(nb, lt, DK)
    v3 = v.astype(bf16).reshape(nb, lt, DV)
    g3 = g.reshape(nb, lt, DV)
    causal = lax.broadcasted_iota(i32, (chunk, chunk), 1) <= lax.broadcasted_iota(i32, (chunk, chunk), 0)
    gnw = gnw_ref[...]
    for j in range(lt // chunk):
        js = slice(j * chunk, (j + 1) * chunk)
        for hh in range(H):
            ks = slice(hh * DKH, (hh + 1) * DKH)
            vs = slice(hh * DVH, (hh + 1) * DVH)
            qd, kd, ke, vv = q_dec[:, js, ks], k_dec[:, js, ks], k_end[:, js, ks], v3[:, js, vs]
            st = st_ref[hh]
            o = jnp.einsum('bcd,bed->bce', qd, st.astype(bf16), preferred_element_type=f32)
            sc = jnp.einsum('bcd,bsd->bcs', qd, kd, preferred_element_type=f32)
            sc = jnp.where(causal, sc, 0.0).astype(bf16)
            o = o + jnp.einsum('bcs,bse->bce', sc, vv, preferred_element_type=f32)
            a_row = a_end[:, j * chunk:j * chunk + 1, ks]
            st_ref[hh] = st * a_row + jnp.einsum('bse,bsd->bed', vv, ke, preferred_element_type=f32)
            on = _rms(o, gnw)
            gh = g3[:, js, vs]
            og_ref[:, js, vs] = (on * (gh * jax.nn.sigmoid(gh))).astype(bf16)

    @pl.when(i == last)
    def _():
        for hh in range(H):
            ng_ref[0, :, hh] = jnp.swapaxes(st_ref[hh], 1, 2)

    y_b = _dot(og_ref[...].reshape(m, DV), wog_ref[...])
    u_a = _dot(hb, wu_ref[:, 0:D])
    u_b = _dot(hb, wu_ref[:, D:2 * D])
    merged = (jax.nn.sigmoid(u_a) * y_a + jax.nn.sigmoid(u_b) * y_b).astype(bf16)
    x1 = x + g1 * _dot(merged, wo_ref[...])

    h2 = (_rms(x1, n2_ref[...]) * (1.0 + sc2) + sh2).astype(bf16)
    sg = _dot(h2, wsg_ref[...])
    act = (sg * jax.nn.sigmoid(sg) * _dot(h2, wsu_ref[...])).astype(bf16)
    x2_ref[...] = x1 + g2 * _dot(act, wsd_ref[...])
    h2_ref[...] = h2
    lg_ref[0] = lax.dot_general(rwt_ref[...], h2, (((1,), (1,)), ((), ())), preferred_element_type=f32)


def _mixer(x, mod4, states, weights, nb, lt, chunk):
    bsz, seqlen, _ = x.shape
    has_state = states is not None
    m = nb * lt
    n_lt = seqlen // lt
    nsteps = (bsz // nb) * n_lt
    t_all = bsz * seqlen

    def tok_blk(b, i):
        return b * n_lt + i

    in_specs = [pl.BlockSpec((nb, lt, D), lambda b, i: (b, i, 0)),
                pl.BlockSpec((6, nb, 1, D), lambda b, i: (0, b, 0, 0))]
    args = [x, mod4]
    if has_state:
        in_specs += [pl.BlockSpec((nb, 2, D), lambda b, i: (b, 0, 0)),
                     pl.BlockSpec((nb, H, DKH, DVH), lambda b, i: (b, 0, 0, 0))]
        args += list(states)
    in_specs += [_const_spec(w.shape) for w in weights]
    args += list(weights)
    out_specs = [pl.BlockSpec((m, D), lambda b, i: (tok_blk(b, i), 0)),
                 pl.BlockSpec((m, D), lambda b, i: (tok_blk(b, i), 0)),
                 pl.BlockSpec((1, E, m), lambda b, i: (b * n_lt + i, 0, 0)),
                 pl.BlockSpec((1, nb, 2, D), lambda b, i: (0, b, 0, 0)),
                 pl.BlockSpec((1, nb, H, DKH, DVH), lambda b, i: (0, b, 0, 0, 0))]
    out_shape = [jax.ShapeDtypeStruct((t_all, D), f32),
                 jax.ShapeDtypeStruct((t_all, D), bf16),
                 jax.ShapeDtypeStruct((nsteps, E, m), f32),
                 jax.ShapeDtypeStruct((1, bsz, 2, D), f32),
                 jax.ShapeDtypeStruct((1, bsz, H, DKH, DVH), f32)]
    return pl.pallas_call(
        functools.partial(_mixer_body, nb, lt, chunk, has_state),
        grid=(bsz // nb, n_lt),
        in_specs=in_specs,
        out_specs=out_specs,
        out_shape=out_shape,
        scratch_shapes=[pltpu.VMEM((2, nb, 1, D), f32),
                        pltpu.VMEM((H, nb, DVH, DKH), f32),
                        pltpu.VMEM((nb, lt, DV), bf16)],
        compiler_params=pltpu.CompilerParams(dimension_semantics=("arbitrary", "arbitrary"),
                                             vmem_limit_bytes=VMEM_LIMIT),
        name="mixer_state" if has_state else "mixer_prompt",
    )(*args)


def _route_body(lg_ref, bias_ref, eidx_ref, w_ref, lrank_ref, cnt_ref):
    step = pl.program_id(0)
    tl = lg_ref.shape[1]

    scores = jax.nn.sigmoid(lg_ref[...])
    choice = scores + bias_ref[...]
    gsz = E // NGROUPS
    grp = choice.reshape(NGROUPS, gsz, tl)
    mi = lax.broadcasted_iota(i32, (NGROUPS, gsz, tl), 1)
    m1 = jnp.max(grp, axis=1, keepdims=True)
    first = jnp.min(jnp.where(grp == m1, mi, gsz), axis=1, keepdims=True)
    m2 = jnp.max(jnp.where(mi == first, NEG, grp), axis=1, keepdims=True)
    gscore = (m1 + m2).reshape(NGROUPS, tl)
    gi = lax.broadcasted_iota(i32, (NGROUPS, tl), 0)
    gsel = jnp.zeros((NGROUPS, tl), f32)
    work = gscore
    for _ in range(TOPK_GROUPS):
        mx = jnp.max(work, axis=0, keepdims=True)
        f = jnp.min(jnp.where(work == mx, gi, NGROUPS), axis=0, keepdims=True)
        hit = gi == f
        gsel = jnp.where(hit, 1.0, gsel)
        work = jnp.where(hit, NEG, work)
    emask = jnp.broadcast_to(gsel.reshape(NGROUPS, 1, tl), (NGROUPS, gsz, tl)).reshape(E, tl)
    masked = jnp.where(emask > 0.0, choice, NEG)
    ei = lax.broadcasted_iota(i32, (E, tl), 0)
    sel = jnp.zeros((E, tl), f32)
    hits, ws = [], []
    for kk in range(TOPK):
        mx = jnp.max(masked, axis=0, keepdims=True)
        f = jnp.min(jnp.where(masked == mx, ei, E), axis=0, keepdims=True)
        hit = ei == f
        ws.append(jnp.sum(jnp.where(hit, scores, 0.0), axis=0, keepdims=True))
        masked = jnp.where(hit, NEG, masked)
        sel = jnp.where(hit, 1.0, sel)
        hits.append(hit)
        eidx_ref[kk:kk + 1, :] = f
    wsum = ws[0]
    for t in ws[1:]:
        wsum = wsum + t
    for kk in range(TOPK):
        w_ref[kk:kk + 1, :] = ws[kk] / wsum * ROUTED_SCALE

    ui = lax.broadcasted_iota(i32, (tl, tl), 0)
    uj = lax.broadcasted_iota(i32, (tl, tl), 1)
    upper = (ui <= uj).astype(bf16)
    lrank = _dot(sel.astype(bf16), upper) - sel
    for kk in range(TOPK):
        lrank_ref[kk:kk + 1, :] = jnp.sum(jnp.where(hits[kk], lrank, 0.0), axis=0, keepdims=True).astype(i32)

    @pl.when(step == 0)
    def _():
        cnt_ref[...] = jnp.zeros_like(cnt_ref)
    lane = lax.broadcasted_iota(i32, cnt_ref.shape, 1)
    cnt_ref[...] = jnp.where(lane == step, jnp.sum(sel, axis=1, keepdims=True), cnt_ref[...])


def _route(logits_t, bias_col):
    t_all = logits_t.shape[1]
    tl = TILE
    assert t_all // tl <= MAX_TILES
    tok = lambda s: (0, s)
    return pl.pallas_call(
        _route_body,
        grid=(t_all // tl,),
        in_specs=[pl.BlockSpec((E, tl), tok), pl.BlockSpec((E, 1), lambda s: (0, 0))],
        out_specs=[pl.BlockSpec((TOPK, tl), tok), pl.BlockSpec((TOPK, tl), tok), pl.BlockSpec((TOPK, tl), tok),
                   pl.BlockSpec((E, MAX_TILES), lambda s: (0, 0))],
        out_shape=[jax.ShapeDtypeStruct((TOPK, t_all), i32), jax.ShapeDtypeStruct((TOPK, t_all), f32),
                   jax.ShapeDtypeStruct((TOPK, t_all), i32), jax.ShapeDtypeStruct((E, MAX_TILES), f32)],
        compiler_params=pltpu.CompilerParams(dimension_semantics=("arbitrary",), vmem_limit_bytes=VMEM_LIMIT),
        name="route",
    )(logits_t, bias_col)


def _plan_body(eidx_ref, lrank_ref, cnt_ref, slot_ref, n8_ref, l_ref, g_ref, blk_ref):
    i = pl.program_id(0)
    tl = eidx_ref.shape[1]

    @pl.when(i == 0)
    def _():
        cnt = cnt_ref[...]
        n8 = jnp.floor((cnt + (RUN_ALIGN - 1)) * (1.0 / RUN_ALIGN)) * RUN_ALIGN
        ti = lax.broadcasted_iota(i32, (MAX_TILES, MAX_TILES), 0)
        tj = lax.broadcasted_iota(i32, (MAX_TILES, MAX_TILES), 1)
        before = _dotx01(n8, (ti < tj).astype(bf16))
        region = jnp.sum(n8, axis=1, keepdims=True)
        nblocks = jnp.floor((region + (MOE_BM - 1)) * (1.0 / MOE_BM))
        ri = lax.broadcasted_iota(i32, (E, E), 0)
        ci = lax.broadcasted_iota(i32, (E, E), 1)
        below = (ci < ri).astype(bf16)
        bstart = _dot01(below, jnp.broadcast_to(nblocks, (E, MAX_TILES)))
        n8_ref[...] = n8.astype(i32)
        l_ref[...] = _dot01(below, n8).astype(i32)
        g_ref[...] = (bstart * MOE_BM + before).astype(i32)
        lane0 = lax.broadcasted_iota(i32, (E, MAX_TILES), 1) == 0
        blk_ref[...] = jnp.where(lane0, bstart, nblocks).astype(i32)

    lane = lax.broadcasted_iota(i32, (E, MAX_TILES), 1)
    lcol = jnp.sum(jnp.where(lane == i, l_ref[...].astype(f32), 0.0), axis=1, keepdims=True)
    ei = lax.broadcasted_iota(i32, (E, tl), 0)
    for kk in range(TOPK):
        hit = ei == eidx_ref[kk:kk + 1, :]
        start = jnp.sum(jnp.where(hit, lcol, 0.0), axis=0, keepdims=True)
        slot_ref[kk:kk + 1, :] = start.astype(i32) + lrank_ref[kk:kk + 1, :]


def _plan(eidx, lrank, cnt):
    t_all = eidx.shape[1]
    tl = TILE
    tok = lambda s: (0, s)
    table = pl.BlockSpec((E, MAX_TILES), lambda s: (0, 0))
    return pl.pallas_call(
        _plan_body,
        grid=(t_all // tl,),
        in_specs=[pl.BlockSpec((TOPK, tl), tok), pl.BlockSpec((TOPK, tl), tok), table],
        out_specs=[pl.BlockSpec((TOPK, tl), tok), table, table, table, table],
        out_shape=[jax.ShapeDtypeStruct((TOPK, t_all), i32)] + [jax.ShapeDtypeStruct((E, MAX_TILES), i32)] * 4,
        compiler_params=pltpu.CompilerParams(dimension_semantics=("arbitrary",), vmem_limit_bytes=VMEM_LIMIT),
        name="plan",
    )(eidx, lrank, cnt)


def _for_each_piece(total, pieces, fn):
    for j, b in enumerate(pieces):
        @pl.when((total & b) != 0)
        def _():
            fn(pl.multiple_of(total & ~(2 * b - 1), pieces[-1]), b, j)


def _for_each_run(n8_ref, l_ref, g_ref, tile, fn):
    def per_expert(e, c):
        lo = pl.multiple_of(l_ref[e, tile], RUN_ALIGN)
        go = pl.multiple_of(g_ref[e, tile], RUN_ALIGN)
        _for_each_piece(n8_ref[e, tile], RUN_PIECES, lambda off, b, j: fn(lo + off, go + off, b, j))
        return c
    lax.fori_loop(0, E, per_expert, 0)


def _start(copy, j):
    copy.start(priority=j % 2)


def _wait_rows(make_copy, rows):
    _for_each_piece(rows, WAIT_PIECES, lambda off, b, j: make_copy(b).wait())


def _dispatch_body(tiles_a, n8_ref, l_ref, g_ref, slot_ref, ha_ref, hb_ref, xs_ref, buf_ref, zero_ref, sem, zsem):
    tile = pl.program_id(0)
    i = tile
    nsteps = pl.num_programs(0)
    cur = i % 2

    slot = slot_ref[...].astype(i16)
    h = jnp.where(tile < tiles_a, ha_ref[...], hb_ref[...])
    filled = l_ref[E - 1, tile] + n8_ref[E - 1, tile]

    def order(row0, nrows):
        srow = lax.broadcasted_iota(i32, (nrows, TILE), 0).astype(i16) + jnp.int16(row0)
        onehot = jnp.zeros((nrows, TILE), bf16)
        for kk in range(TOPK):
            onehot = jnp.where(srow == slot[kk:kk + 1, :], jnp.ones((), bf16), onehot)
        buf_ref[cur, row0:row0 + nrows, :] = _dot(onehot, h).astype(bf16)

    order(0, SLOTS_MAIN)

    @pl.when(filled > SLOTS_MAIN)
    def _():
        order(SLOTS_MAIN, SLOTS - SLOTS_MAIN)

    def run_copy(sl):
        return lambda srow_, grow, b: pltpu.make_async_copy(
            buf_ref.at[sl, pl.ds(srow_, b)], xs_ref.at[pl.ds(grow, b)], sem.at[sl])

    _for_each_run(n8_ref, l_ref, g_ref, tile, lambda s, g, b, j: _start(run_copy(cur)(s, g, b), j))

    def wait_tile(t, sl):
        _wait_rows(lambda b: run_copy(sl)(0, 0, b), l_ref[E - 1, t] + n8_ref[E - 1, t])

    @pl.when(i > 0)
    def _():
        wait_tile(tile - 1, 1 - cur)

    last = nsteps - 1

    def zero_fill(fn):
        def per_expert(e, c):
            end = pl.multiple_of(g_ref[e, last] + n8_ref[e, last], RUN_ALIGN)
            _for_each_piece((-end) & (MOE_BM - 1), TAIL_PIECES, lambda off, b, j: fn(pltpu.make_async_copy(
                zero_ref.at[pl.ds(0, b)], xs_ref.at[pl.ds(end + off, b)], zsem)))
            return c
        lax.fori_loop(0, E, per_expert, 0)
        used = (g_ref[E - 1, last] + n8_ref[E - 1, last] + (MOE_BM - 1)) // MOE_BM

        def per_block(blk, c):
            fn(pltpu.make_async_copy(zero_ref, xs_ref.at[pl.ds(pl.multiple_of(blk * MOE_BM, MOE_BM), MOE_BM)], zsem))
            return c
        lax.fori_loop(used, xs_ref.shape[0] // MOE_BM, per_block, 0)

    @pl.when(i == 0)
    def _():
        zero_ref[...] = jnp.zeros_like(zero_ref)
        zero_fill(lambda cp: cp.start())

    @pl.when(i == last)
    def _():
        wait_tile(tile, cur)
        zero_fill(lambda cp: cp.wait())


def _dispatch(tables, slot, h2_a, h2_b, nrows):
    tiles_a, tiles_b = h2_a.shape[0] // TILE, h2_b.shape[0] // TILE
    return pl.pallas_call(
        functools.partial(_dispatch_body, tiles_a),
        grid_spec=pltpu.PrefetchScalarGridSpec(
            num_scalar_prefetch=len(tables),
            grid=(tiles_a + tiles_b,),
            in_specs=[pl.BlockSpec((TOPK, TILE), lambda i, *_: (0, i)),
                      pl.BlockSpec((TILE, D), lambda i, *_: (jnp.minimum(i, tiles_a - 1), 0)),
                      pl.BlockSpec((TILE, D), lambda i, *_: (jnp.maximum(i - tiles_a, 0), 0))],
            out_specs=pl.BlockSpec(memory_space=pl.ANY),
            scratch_shapes=[pltpu.VMEM((2, SLOTS, D), bf16), pltpu.VMEM((MOE_BM, D), bf16),
                            pltpu.SemaphoreType.DMA((2,)), pltpu.SemaphoreType.DMA]),
        out_shape=jax.ShapeDtypeStruct((nrows, D), bf16),
        compiler_params=pltpu.CompilerParams(dimension_semantics=("arbitrary",), vmem_limit_bytes=VMEM_LIMIT),
        name="dispatch",
    )(*tables, slot, h2_a, h2_b)


def _expert_body(blk_ref, wg_ref, wu_ref, wd_ref, xs_ref, ys_ref, xbuf_ref, ybuf_ref, wgu_ref, wdb_ref, isem, osem):
    e = pl.program_id(0)
    first, count = blk_ref[e, 0], blk_ref[e, 1]
    wgu_ref[:, :DE] = wg_ref[0].astype(bf16)
    wgu_ref[:, DE:] = wu_ref[0].astype(bf16)
    wdb_ref[...] = wd_ref[0].astype(bf16)

    part = MOE_BM // EXPERT_DMA_PARTS

    def load(j, sl):
        return [pltpu.make_async_copy(
            xs_ref.at[pl.ds(pl.multiple_of((first + j) * MOE_BM + p * part, part), part)],
            xbuf_ref.at[sl, pl.ds(p * part, part)], isem.at[sl]) for p in range(EXPERT_DMA_PARTS)]

    def store(j, sl):
        return [pltpu.make_async_copy(
            ybuf_ref.at[sl, pl.ds(p * part, part)],
            ys_ref.at[pl.ds(pl.multiple_of((first + j) * MOE_BM + p * part, part), part)], osem.at[sl])
            for p in range(EXPERT_DMA_PARTS)]

    def start(copies, prio):
        for cp in copies:
            cp.start(priority=prio)

    def wait(copies):
        for cp in copies:
            cp.wait()

    @pl.when(count > 0)
    def _():
        start(load(0, 0), 0)

    def block(j, c):
        sl = j % 2

        @pl.when(j + 1 < count)
        def _():
            start(load(j + 1, 1 - sl), 0)
        wait(load(j, sl))

        @pl.when(j >= 2)
        def _():
            wait(store(j - 2, sl))
        hgu = _dot(xbuf_ref[sl], wgu_ref[...])
        hg = hgu[:, :DE]
        act = (hg * jax.nn.sigmoid(hg) * hgu[:, DE:]).astype(bf16)
        ybuf_ref[sl] = _dot(act, wdb_ref[...]).astype(bf16)
        start(store(j, sl), 1)
        return c
    lax.fori_loop(0, count, block, 0)

    @pl.when(count >= 2)
    def _():
        wait(store(count - 2, count % 2))

    @pl.when(count >= 1)
    def _():
        wait(store(count - 1, (count - 1) % 2))


def _experts(blk, xs, we_gate, we_up, we_down):
    ex = lambda e, blk_: (e, 0, 0)
    return pl.pallas_call(
        _expert_body,
        grid_spec=pltpu.PrefetchScalarGridSpec(
            num_scalar_prefetch=1,
            grid=(E,),
            in_specs=[pl.BlockSpec((1, D, DE), ex), pl.BlockSpec((1, D, DE), ex), pl.BlockSpec((1, DE, D), ex),
                      pl.BlockSpec(memory_space=pl.ANY)],
            out_specs=pl.BlockSpec(memory_space=pl.ANY),
            scratch_shapes=[pltpu.VMEM((2, MOE_BM, D), bf16), pltpu.VMEM((2, MOE_BM, D), bf16),
                            pltpu.VMEM((D, 2 * DE), bf16), pltpu.VMEM((DE, D), bf16),
                            pltpu.SemaphoreType.DMA((2,)), pltpu.SemaphoreType.DMA((2,))]),
        out_shape=jax.ShapeDtypeStruct(xs.shape, xs.dtype),
        input_output_aliases={4: 0},
        compiler_params=pltpu.CompilerParams(dimension_semantics=("arbitrary",), vmem_limit_bytes=VMEM_LIMIT),
        name="experts",
    )(blk, we_gate, we_up, we_down, xs)


def _final_body(nb, lt, tile0, n8_ref, l_ref, g_ref, slot_ref, wt_ref, x2_ref, g2_ref, fw_ref, ys_ref, o_ref,
                buf_ref, routed_ref, sem):
    m = nb * lt
    n_inner = pl.num_programs(1)
    step = pl.program_id(0) * n_inner + pl.program_id(1)
    nsteps = pl.num_programs(0) * n_inner
    tile = tile0 + step
    cur = step % 2

    def run_copy(sl):
        return lambda srow, grow, b: pltpu.make_async_copy(
            ys_ref.at[pl.ds(grow, b)], buf_ref.at[sl, pl.ds(srow, b)], sem.at[sl])

    @pl.when(step == 0)
    def _():
        buf_ref[...] = jnp.zeros_like(buf_ref)
        _for_each_run(n8_ref, l_ref, g_ref, tile, lambda s, g, b, j: _start(run_copy(0)(s, g, b), j))

    @pl.when(step + 1 < nsteps)
    def _():
        _for_each_run(n8_ref, l_ref, g_ref, tile + 1, lambda s, g, b, j: _start(run_copy(1 - cur)(s, g, b), j))

    _wait_rows(lambda b: run_copy(cur)(0, 0, b), l_ref[E - 1, tile] + n8_ref[E - 1, tile])

    slot = slot_ref[...].astype(i16)
    wt = wt_ref[...]
    wt_hi = wt.astype(bf16)
    wt_lo = (wt - wt_hi.astype(f32)).astype(bf16)
    def combine(col0, ncols):
        scol = lax.broadcasted_iota(i32, (m, ncols), 1).astype(i16) + jnp.int16(col0)
        w_hi = jnp.zeros((m, ncols), bf16)
        w_lo = jnp.zeros((m, ncols), bf16)
        for kk in range(TOPK):
            hit = scol == slot[:, kk:kk + 1]
            w_hi = jnp.where(hit, wt_hi[:, kk:kk + 1], w_hi)
            w_lo = jnp.where(hit, wt_lo[:, kk:kk + 1], w_lo)
        yb = buf_ref[cur, col0:col0 + ncols, :]
        return _dot(w_hi, yb) + _dot(w_lo, yb)

    routed_ref[...] = combine(0, SLOTS_MAIN)

    @pl.when(l_ref[E - 1, tile] + n8_ref[E - 1, tile] > SLOTS_MAIN)
    def _():
        routed_ref[...] += combine(SLOTS_MAIN, SLOTS - SLOTS_MAIN)
    routed = routed_ref[...]
    g2 = g2_ref[0]
    if nb == 1:
        g2 = g2.reshape(1, D)
    else:
        g2 = jnp.broadcast_to(g2, (nb, lt, D)).reshape(m, D)
    y = x2_ref[...] + g2 * routed
    o_ref[...] = _rms(y, fw_ref[...]).reshape(nb, lt, D)


def _final(tables, slot_t, wts_t, x2, ys, mod4, fw, bsz, seqlen, row0, nb, lt):
    m = nb * lt
    assert m == TILE and row0 % TILE == 0
    n_lt = seqlen // lt
    tile0 = row0 // TILE
    tile = lambda b, i: tile0 + b * n_lt + i
    return pl.pallas_call(
        functools.partial(_final_body, nb, lt, tile0),
        grid_spec=pltpu.PrefetchScalarGridSpec(
            num_scalar_prefetch=len(tables),
            grid=(bsz // nb, n_lt),
            in_specs=[pl.BlockSpec((m, TOPK), lambda b, i, *_: (tile(b, i), 0)),
                      pl.BlockSpec((m, TOPK), lambda b, i, *_: (tile(b, i), 0)),
                      pl.BlockSpec((m, D), lambda b, i, *_: (b * n_lt + i, 0)),
                      pl.BlockSpec((1, nb, 1, D), lambda b, i, *_: (5, b, 0, 0)),
                      pl.BlockSpec((1, D), lambda b, i, *_: (0, 0)),
                      pl.BlockSpec(memory_space=pl.ANY)],
            out_specs=pl.BlockSpec((nb, lt, D), lambda b, i, *_: (b, i, 0)),
            scratch_shapes=[pltpu.VMEM((2, SLOTS, D), bf16), pltpu.VMEM((m, D), f32),
                            pltpu.SemaphoreType.DMA((2,))]),
        out_shape=jax.ShapeDtypeStruct((bsz, seqlen, D), f32),
        compiler_params=pltpu.CompilerParams(dimension_semantics=("arbitrary", "arbitrary"),
                                             vmem_limit_bytes=VMEM_LIMIT),
        name="final",
    )(*tables, slot_t, wts_t, x2, mod4, fw, ys)


def kernel(x_prompt, x_sample, state_conv, state_gla, c_prompt, c_sample, w_ada, b_ada, norm1_w, w_in, conv_w,
           w_gk, b_gk, gla_norm_w, w_out_conv, w_out_gla, w_o, norm2_w, router_w, router_bias, we_gate, we_up,
           we_down, ws_gate, ws_up, ws_down, final_norm_w):
    assert w_ada.shape[0] == 1, "single-layer step"
    bp, lp, _ = x_prompt.shape
    bs, ls, _ = x_sample.shape
    tp, ts = bp * lp, bs * ls
    t_all = tp + ts

    w_in0 = w_in[0]
    n_main = 3 * D + 2 * DK + 2 * DV
    rank = w_gk.shape[1]
    weights = (
        norm1_w[0].reshape(1, D),
        w_in0[:, :n_main].astype(bf16),
        jnp.pad(w_in0[:, n_main:n_main + rank], ((0, 0), (0, RANK_PAD - rank))).astype(bf16),
        w_in0[:, n_main + rank:].astype(bf16),
        jnp.pad(w_gk[0], ((0, RANK_PAD - rank), (0, 0))).astype(bf16),
        b_gk[0].reshape(1, DK),
        conv_w[0],
        gla_norm_w[0].reshape(1, DVH),
        w_out_conv[0].astype(bf16),
        w_out_gla[0].astype(bf16),
        w_o[0].astype(bf16),
        norm2_w[0].reshape(1, D),
        router_w[0].T.astype(bf16),
        ws_gate[0].astype(bf16),
        ws_up[0].astype(bf16),
        ws_down[0].astype(bf16),
    )

    mod = _adaln(jnp.concatenate([c_prompt, c_sample], axis=0), w_ada[0], b_ada[0].reshape(1, 6 * D))
    mod4 = mod.reshape(bp + bs, 6, 1, D).transpose(1, 0, 2, 3)
    mod_p, mod_s = mod4[:, :bp], mod4[:, bp:]

    lt_p = 512
    x2_p, h2_p, lg_p, conv_p, gla_p = _mixer(x_prompt, mod_p, None, weights, 1, lt_p, GLA_CHUNK)
    nb_s = 8
    x2_s, h2_s, lg_s, conv_s, gla_s = _mixer(x_sample, mod_s, (state_conv[0], state_gla[0]), weights,
                                             nb_s, ls, math.gcd(ls, GLA_CHUNK))
    lg = jnp.concatenate([lg_p.transpose(1, 0, 2).reshape(E, tp), lg_s.transpose(1, 0, 2).reshape(E, ts)], axis=1)

    eidx, wts, lrank, cnt = _route(lg, router_bias[0].reshape(E, 1))
    n_tiles = t_all // TILE
    max_rows = t_all * TOPK + n_tiles * E * (RUN_ALIGN - 1) + E * (MOE_BM - RUN_ALIGN)
    nblk = -(-max_rows // MOE_BM)
    slot, n8, loff, goff, blk = _plan(eidx, lrank, cnt)
    tables = (n8, loff, goff)

    xs = _dispatch(tables, slot, h2_p, h2_s, nblk * MOE_BM)
    ys = _experts(blk, xs, we_gate[0], we_up[0], we_down[0])

    fw = final_norm_w.reshape(1, D)
    slot_t, wts_t = slot.T, wts.T
    y_prompt = _final(tables, slot_t, wts_t, x2_p, ys, mod_p, fw, bp, lp, 0, 1, TILE)
    y_sample = _final(tables, slot_t, wts_t, x2_s, ys, mod_s, fw, bs, ls, tp, TILE // ls, ls)
    return (y_prompt, y_sample, conv_p, gla_p, conv_s, gla_s)
```

```python
import functools
import math

import jax
import jax.numpy as jnp
from jax import lax
from jax.experimental import pallas as pl
from jax.experimental.pallas import tpu as pltpu

f32 = jnp.float32
bf16 = jnp.bfloat16
i32 = jnp.int32
i16 = jnp.int16

D = 1024
H = 4
DKH = 128
DVH = 256
DK = H * DKH
DV = H * DVH
RANK_PAD = 128
GATE_NORMALIZER = 16.0
GLA_CHUNK = 64
E = 64
TOPK = 8
NGROUPS = 8
TOPK_GROUPS = 4
DE = 256
ROUTED_SCALE = 2.5
EPS = 1e-6
NEG = float("-inf")

VMEM_LIMIT = 58 * 1024 * 1024
MOE_BM = 512
TILE = 256
MAX_TILES = 128
RUN_ALIGN = 16
SLOTS = TILE * TOPK + 1024
SLOTS_MAIN = TILE * TOPK + 512
RUN_PIECES = (256, 128, 64, 32, 16)
TAIL_PIECES = (256, 128, 64, 32, 16)
WAIT_PIECES = (2048, 1024, 512, 256, 128, 64, 32, 16)

def _dot(a, b):
    return jnp.dot(a, b, preferred_element_type=f32)


def _split3(x):
    hi = x.astype(bf16)
    r1 = x - hi.astype(f32)
    mid = r1.astype(bf16)
    lo = (r1 - mid.astype(f32)).astype(bf16)
    return hi, mid, lo


def _dot01(m01, x):
    hi, mid, lo = _split3(x)
    return _dot(m01, hi) + _dot(m01, mid) + _dot(m01, lo)


def _dotx01(x, m01):
    hi, mid, lo = _split3(x)
    return _dot(hi, m01) + _dot(mid, m01) + _dot(lo, m01)


def _rms(v, w):
    ms = jnp.mean(v * v, axis=-1, keepdims=True)
    return v * lax.rsqrt(ms + EPS) * w


def _const_spec(shape):
    n = len(shape)
    return pl.BlockSpec(shape, lambda *_: (0,) * n, pipeline_mode=pl.Buffered(1))


def _adaln_body(c_ref, w_ref, b_ref, o_ref):
    c = c_ref[...]
    a = (c * jax.nn.sigmoid(c)).astype(bf16)
    o_ref[...] = _dot(a, w_ref[...].astype(bf16)) + b_ref[...]


def _adaln(c_all, w_ada, b_ada):
    nrow = c_all.shape[0]
    tn = 1536
    return pl.pallas_call(
        _adaln_body,
        grid=(6 * D // tn,),
        in_specs=[pl.BlockSpec((nrow, D), lambda j: (0, 0)),
                  pl.BlockSpec((D, tn), lambda j: (0, j)),
                  pl.BlockSpec((1, tn), lambda j: (0, j))],
        out_specs=pl.BlockSpec((nrow, tn), lambda j: (0, j)),
        out_shape=jax.ShapeDtypeStruct((nrow, 6 * D), f32),
        compiler_params=pltpu.CompilerParams(dimension_semantics=("arbitrary",), vmem_limit_bytes=VMEM_LIMIT),
        name="adaln",
    )(c_all, w_ada, b_ada)


def _mixer_body(nb, lt, chunk, has_state, *refs):
    if has_state:
        (x_ref, mod_ref, cs_ref, gs_ref, *rest) = refs
    else:
        (x_ref, mod_ref, *rest) = refs
        cs_ref = gs_ref = None
    (n1_ref, wmain_ref, wz_ref, wu_ref, wgk_ref, bgk_ref, cw_ref, gnw_ref, woc_ref, wog_ref, wo_ref,
     n2_ref, rwt_ref, wsg_ref, wsu_ref, wsd_ref,
     x2_ref, h2_ref, lg_ref, nc_ref, ng_ref, carry_ref, st_ref, og_ref, sin_ref) = rest
    m = nb * lt
    i = pl.program_id(1)
    last = pl.num_programs(1) - 1

    def rows(v):
        w = v.shape[-1]
        if nb == 1:
            return v.reshape(1, w)
        return jnp.broadcast_to(v, (nb, lt, w)).reshape(m, w)

    @pl.when(i == 0)
    def _():
        if has_state:
            cs = cs_ref[...]
            carry_ref[0] = cs[:, 0:1, :]
            carry_ref[1] = cs[:, 1:2, :]
            for hh in range(H):
                st_ref[hh] = jnp.swapaxes(gs_ref[:, hh], 1, 2)
        else:
            carry_ref[...] = jnp.zeros_like(carry_ref)
            st_ref[...] = jnp.zeros_like(st_ref)

    x = x_ref[...].reshape(m, D)
    sh1, sc1, g1, sh2, sc2, g2 = [rows(mod_ref[j]) for j in range(6)]
    hb = (_rms(x, n1_ref[...]) * (1.0 + sc1) + sh1).astype(bf16)

    def proj(lo, hi):
        return _dot(hb, wmain_ref[:, lo:hi])

    pre = proj(D, 2 * D) * proj(2 * D, 3 * D)
    c0 = rows(carry_ref[0])
    c1 = rows(carry_ref[1])
    l_idx = lax.broadcasted_iota(i32, (m, D), 0) & (lt - 1)
    r1 = pltpu.roll(pre, 1, 0)
    r2 = pltpu.roll(pre, 2, 0)
    prev1 = jnp.where(l_idx == 0, c1, r1)
    prev2 = jnp.where(l_idx == 0, c0, jnp.where(l_idx == 1, c1, r2))
    cw = cw_ref[...]
    conv = cw[0:1] * prev2 + cw[1:2] * prev1 + cw[2:3] * pre
    y_a = _dot((proj(0, D) * conv).astype(bf16), woc_ref[...])
    pre3 = pre.reshape(nb, lt, D)
    tail = pre3[:, lt - 2:lt, :]
    carry_ref[0] = tail[:, 0:1, :]
    carry_ref[1] = tail[:, 1:2, :]
    nc_ref[0] = tail

    q = proj(3 * D, 3 * D + DK)
    k = proj(3 * D + DK, 3 * D + 2 * DK)
    v = proj(4 * D, 5 * D)
    g = proj(5 * D, 6 * D)
    z = _dot(hb, wz_ref[...]).astype(bf16)
    pa = _dot(z, wgk_ref[...]) + bgk_ref[...]
    la = (jnp.minimum(pa, 0.0) - jnp.log1p(jnp.exp(-jnp.abs(pa)))) / GATE_NORMALIZER
    rr = lax.broadcasted_iota(i32, (m, m), 0)
    cc = lax.broadcasted_iota(i32, (m, m), 1)
    same_chunk = (rr & ~(chunk - 1)) == (cc & ~(chunk - 1))
    tri = (same_chunk & (cc <= rr)).astype(bf16)
    bcum = _dot01(tri, la)
    ngrp = m // chunk
    bend = bcum.reshape(ngrp, chunk, DK)[:, chunk - 1:chunk, :]
    bend_rows = jnp.broadcast_to(bend, (ngrp, chunk, DK)).reshape(m, DK)
    nch = lt // chunk
    assert nb == 1 or nch == 1
    hs = H * nb
    nbat = nch * hs

    def per_head(xm, w):
        x3 = xm.reshape(nb * nch, chunk, H * w)
        parts = [x3[:, :, hh * w:(hh + 1) * w] for hh in range(H)]
        return jnp.stack(parts, axis=1 if nb == 1 else 0).reshape(nbat, chunk, w)

    qd = per_head((q * (DKH ** -0.5) * jnp.exp(bcum)).astype(bf16), DKH)
    kd = per_head((k * jnp.exp(-bcum)).astype(bf16), DKH)
    ke = per_head((k * jnp.exp(bend_rows - bcum)).astype(bf16), DKH)
    vv = per_head(v.astype(bf16), DVH)
    a_end = per_head(jnp.exp(bend_rows), DKH)[:, 0:1, :]
    causal = lax.broadcasted_iota(i32, (chunk, chunk), 1) <= lax.broadcasted_iota(i32, (chunk, chunk), 0)
    sc = jnp.einsum('bcd,bsd->bcs', qd, kd, preferred_element_type=f32)
    sc = jnp.where(causal, sc, 0.0).astype(bf16)
    o = jnp.einsum('bcs,bse->bce', sc, vv, preferred_element_type=f32)
    upd = jnp.einsum('bse,bsd->bed', vv, ke, preferred_element_type=f32)
    st = st_ref[...].reshape(hs, DVH, DKH)
    for j in range(nch):
        sin_ref[j * hs:(j + 1) * hs] = st.astype(bf16)
        st = st * a_end[j * hs:(j + 1) * hs] + upd[j * hs:(j + 1) * hs]
    st_ref[...] = st.reshape(H, nb, DVH, DKH)
    o = o + jnp.einsum('bcd,bed->bce', qd, sin_ref[...], preferred_element_type=f32)
    on = _rms(o, gnw_ref[...])
    g3 = g.reshape(nb, lt, DV)
    for j in range(nch):
        js = slice(j * chunk, (j + 1) * chunk)
        for hh in range(H):
            vs = slice(hh * DVH, (hh + 1) * DVH)
            gh = g3[:, js, vs]
            b0 = (j * H + hh) * nb
            og_ref[:, js, vs] = (on[b0:b0 + nb] * (gh * jax.nn.sigmoid(gh))).astype(bf16)

    @pl.when(i == last)
    def _():
        for hh in range(H):
            ng_ref[0, :, hh] = jnp.swapaxes(st_ref[hh], 1, 2)

    y_b = _dot(og_ref[...].reshape(m, DV), wog_ref[...])
    u_a = _dot(hb, wu_ref[:, 0:D])
    u_b = _dot(hb, wu_ref[:, D:2 * D])
    merged = (jax.nn.sigmoid(u_a) * y_a + jax.nn.sigmoid(u_b) * y_b).astype(bf16)
    x1 = x + g1 * _dot(merged, wo_ref[...])

    h2 = (_rms(x1, n2_ref[...]) * (1.0 + sc2) + sh2).astype(bf16)
    sg = _dot(h2, wsg_ref[...])
    act = (sg * jax.nn.sigmoid(sg) * _dot(h2, wsu_ref[...])).astype(bf16)
    x2_ref[...] = x1 + g2 * _dot(act, wsd_ref[...])
    h2_ref[...] = h2
    lg_ref[0] = lax.dot_general(rwt_ref[...], h2, (((1,), (1,)), ((), ())), preferred_element_type=f32)


def _mixer(x, mod4, states, weights, nb, lt, chunk):
    bsz, seqlen, _ = x.shape
    has_state = states is not None
    m = nb * lt
    n_lt = seqlen // lt
    nsteps = (bsz // nb) * n_lt
    t_all = bsz * seqlen

    def tok_blk(b, i):
        return b * n_lt + i

    in_specs = [pl.BlockSpec((nb, lt, D), lambda b, i: (b, i, 0)),
                pl.BlockSpec((6, nb, 1, D), lambda b, i: (0, b, 0, 0))]
    args = [x, mod4]
    if has_state:
        in_specs += [pl.BlockSpec((nb, 2, D), lambda b, i: (b, 0, 0)),
                     pl.BlockSpec((nb, H, DKH, DVH), lambda b, i: (b, 0, 0, 0))]
        args += list(states)
    in_specs += [_const_spec(w.shape) for w in weights]
    args += list(weights)
    out_specs = [pl.BlockSpec((m, D), lambda b, i: (tok_blk(b, i), 0)),
                 pl.BlockSpec((m, D), lambda b, i: (tok_blk(b, i), 0)),
                 pl.BlockSpec((1, E, m), lambda b, i: (b * n_lt + i, 0, 0)),
                 pl.BlockSpec((1, nb, 2, D), lambda b, i: (0, b, 0, 0)),
                 pl.BlockSpec((1, nb, H, DKH, DVH), lambda b, i: (0, b, 0, 0, 0))]
    out_shape = [jax.ShapeDtypeStruct((t_all, D), f32),
                 jax.ShapeDtypeStruct((t_all, D), bf16),
                 jax.ShapeDtypeStruct((nsteps, E, m), f32),
                 jax.ShapeDtypeStruct((1, bsz, 2, D), f32),
                 jax.ShapeDtypeStruct((1, bsz, H, DKH, DVH), f32)]
    return pl.pallas_call(
        functools.partial(_mixer_body, nb, lt, chunk, has_state),
        grid=(bsz // nb, n_lt),
        in_specs=in_specs,
        out_specs=out_specs,
        out_shape=out_shape,
        scratch_shapes=[pltpu.VMEM((2, nb, 1, D), f32),
                        pltpu.VMEM((H, nb, DVH, DKH), f32),
                        pltpu.VMEM((nb, lt, DV), bf16),
                        pltpu.VMEM(((lt // chunk) * H * nb, DVH, DKH), bf16)],
        compiler_params=pltpu.CompilerParams(dimension_semantics=("arbitrary", "arbitrary"),
                                             vmem_limit_bytes=VMEM_LIMIT),
        name="mixer_state" if has_state else "mixer_prompt",
    )(*args)


def _route_body(lg_ref, bias_ref, eidx_ref, w_ref, lrank_ref, cnt_ref):
    step = pl.program_id(0)
    tl = lg_ref.shape[1]

    scores = jax.nn.sigmoid(lg_ref[...])
    choice = scores + bias_ref[...]
    gsz = E // NGROUPS
    grp = choice.reshape(NGROUPS, gsz, tl)
    mi = lax.broadcasted_iota(i32, (NGROUPS, gsz, tl), 1)
    m1 = jnp.max(grp, axis=1, keepdims=True)
    first = jnp.min(jnp.where(grp == m1, mi, gsz), axis=1, keepdims=True)
    m2 = jnp.max(jnp.where(mi == first, NEG, grp), axis=1, keepdims=True)
    gscore = (m1 + m2).reshape(NGROUPS, tl)
    gi = lax.broadcasted_iota(i32, (NGROUPS, tl), 0)
    gsel = jnp.zeros((NGROUPS, tl), f32)
    work = gscore
    for _ in range(TOPK_GROUPS):
        mx = jnp.max(work, axis=0, keepdims=True)
        f = jnp.min(jnp.where(work == mx, gi, NGROUPS), axis=0, keepdims=True)
        hit = gi == f
        gsel = jnp.where(hit, 1.0, gsel)
        work = jnp.where(hit, NEG, work)
    emask = jnp.broadcast_to(gsel.reshape(NGROUPS, 1, tl), (NGROUPS, gsz, tl)).reshape(E, tl)
    masked = jnp.where(emask > 0.0, choice, NEG)
    ei = lax.broadcasted_iota(i32, (E, tl), 0)
    sel = jnp.zeros((E, tl), f32)
    hits, ws = [], []
    for kk in range(TOPK):
        mx = jnp.max(masked, axis=0, keepdims=True)
        f = jnp.min(jnp.where(masked == mx, ei, E), axis=0, keepdims=True)
        hit = ei == f
        ws.append(jnp.sum(jnp.where(hit, scores, 0.0), axis=0, keepdims=True))
        masked = jnp.where(hit, NEG, masked)
        sel = jnp.where(hit, 1.0, sel)
        hits.append(hit)
        eidx_ref[kk:kk + 1, :] = f
    wsum = ws[0]
    for t in ws[1:]:
        wsum = wsum + t
    for kk in range(TOPK):
        w_ref[kk:kk + 1, :] = ws[kk] / wsum * ROUTED_SCALE

    ui = lax.broadcasted_iota(i32, (tl, tl), 0)
    uj = lax.broadcasted_iota(i32, (tl, tl), 1)
    upper = (ui <= uj).astype(bf16)
    lrank = _dot(sel.astype(bf16), upper) - sel
    for kk in range(TOPK):
        lrank_ref[kk:kk + 1, :] = jnp.sum(jnp.where(hits[kk], lrank, 0.0), axis=0, keepdims=True).astype(i32)

    @pl.when(step == 0)
    def _():
        cnt_ref[...] = jnp.zeros_like(cnt_ref)
    lane = lax.broadcasted_iota(i32, cnt_ref.shape, 1)
    cnt_ref[...] = jnp.where(lane == step, jnp.sum(sel, axis=1, keepdims=True), cnt_ref[...])


def _route(logits_t, bias_col):
    t_all = logits_t.shape[1]
    tl = TILE
    assert t_all // tl <= MAX_TILES
    tok = lambda s: (0, s)
    return pl.pallas_call(
        _route_body,
        grid=(t_all // tl,),
        in_specs=[pl.BlockSpec((E, tl), tok), pl.BlockSpec((E, 1), lambda s: (0, 0))],
        out_specs=[pl.BlockSpec((TOPK, tl), tok), pl.BlockSpec((TOPK, tl), tok), pl.BlockSpec((TOPK, tl), tok),
                   pl.BlockSpec((E, MAX_TILES), lambda s: (0, 0))],
        out_shape=[jax.ShapeDtypeStruct((TOPK, t_all), i32), jax.ShapeDtypeStruct((TOPK, t_all), f32),
                   jax.ShapeDtypeStruct((TOPK, t_all), i32), jax.ShapeDtypeStruct((E, MAX_TILES), f32)],
        compiler_params=pltpu.CompilerParams(dimension_semantics=("arbitrary",), vmem_limit_bytes=VMEM_LIMIT),
        name="route",
    )(logits_t, bias_col)


def _plan_body(nblk, eidx_ref, lrank_ref, cnt_ref, slot_ref, n8_ref, l_ref, g_ref, blk_ref):
    i = pl.program_id(0)
    tl = eidx_ref.shape[1]

    @pl.when(i == 0)
    def _():
        cnt = cnt_ref[...]
        n8 = jnp.floor((cnt + (RUN_ALIGN - 1)) * (1.0 / RUN_ALIGN)) * RUN_ALIGN
        ti = lax.broadcasted_iota(i32, (MAX_TILES, MAX_TILES), 0)
        tj = lax.broadcasted_iota(i32, (MAX_TILES, MAX_TILES), 1)
        before = _dotx01(n8, (ti < tj).astype(bf16))
        region = jnp.sum(n8, axis=1, keepdims=True)
        nblocks = jnp.floor((region + (MOE_BM - 1)) * (1.0 / MOE_BM))
        ri = lax.broadcasted_iota(i32, (E, E), 0)
        ci = lax.broadcasted_iota(i32, (E, E), 1)
        below = (ci < ri).astype(bf16)
        bstart = _dot01(below, jnp.broadcast_to(nblocks, (E, MAX_TILES)))
        n8_ref[...] = n8.astype(i32)
        l_ref[...] = _dot01(below, n8).astype(i32)
        g_ref[...] = (bstart * MOE_BM + before).astype(i32)
        bend = bstart[:, 0:1] + nblocks
        bi = lax.broadcasted_iota(i32, (E, nblk), 1).astype(f32)
        owner = jnp.sum((bend <= bi).astype(f32), axis=0, keepdims=True)
        used = jnp.max(bend, axis=0, keepdims=True)
        owner = jnp.where(bi[0:1] < used, jnp.minimum(owner, E - 1.0), -1.0)
        blk_ref[...] = owner.astype(i32)

    lane = lax.broadcasted_iota(i32, (E, MAX_TILES), 1)
    lcol = jnp.sum(jnp.where(lane == i, l_ref[...].astype(f32), 0.0), axis=1, keepdims=True)
    ei = lax.broadcasted_iota(i32, (E, tl), 0)
    for kk in range(TOPK):
        hit = ei == eidx_ref[kk:kk + 1, :]
        start = jnp.sum(jnp.where(hit, lcol, 0.0), axis=0, keepdims=True)
        slot_ref[kk:kk + 1, :] = start.astype(i32) + lrank_ref[kk:kk + 1, :]


def _plan(eidx, lrank, cnt, nblk):
    t_all = eidx.shape[1]
    tl = TILE
    tok = lambda s: (0, s)
    table = pl.BlockSpec((E, MAX_TILES), lambda s: (0, 0))
    return pl.pallas_call(
        functools.partial(_plan_body, nblk),
        grid=(t_all // tl,),
        in_specs=[pl.BlockSpec((TOPK, tl), tok), pl.BlockSpec((TOPK, tl), tok), table],
        out_specs=[pl.BlockSpec((TOPK, tl), tok), table, table, table, pl.BlockSpec((1, nblk), lambda s: (0, 0))],
        out_shape=[jax.ShapeDtypeStruct((TOPK, t_all), i32)] + [jax.ShapeDtypeStruct((E, MAX_TILES), i32)] * 3
        + [jax.ShapeDtypeStruct((1, nblk), i32)],
        compiler_params=pltpu.CompilerParams(dimension_semantics=("arbitrary",), vmem_limit_bytes=VMEM_LIMIT),
        name="plan",
    )(eidx, lrank, cnt)


def _for_each_piece(total, pieces, fn):
    for j, b in enumerate(pieces):
        @pl.when((total & b) != 0)
        def _():
            fn(pl.multiple_of(total & ~(2 * b - 1), pieces[-1]), b, j)


def _for_each_run(n8_ref, l_ref, g_ref, tile, fn):
    def per_expert(e, c):
        lo = pl.multiple_of(l_ref[e, tile], RUN_ALIGN)
        go = pl.multiple_of(g_ref[e, tile], RUN_ALIGN)
        _for_each_piece(n8_ref[e, tile], RUN_PIECES, lambda off, b, j: fn(lo + off, go + off, b, j))
        return c
    lax.fori_loop(0, E, per_expert, 0)


def _start(copy, j):
    copy.start(priority=j % 2)


def _wait_rows(make_copy, rows):
    _for_each_piece(rows, WAIT_PIECES, lambda off, b, j: make_copy(b).wait())


def _dispatch_body(tiles_a, n8_ref, l_ref, g_ref, slot_ref, ha_ref, hb_ref, xs_ref, buf_ref, zero_ref, sem, zsem):
    tile = pl.program_id(0)
    i = tile
    nsteps = pl.num_programs(0)
    cur = i % 2

    slot = slot_ref[...].astype(i16)
    h = jnp.where(tile < tiles_a, ha_ref[...], hb_ref[...])
    filled = l_ref[E - 1, tile] + n8_ref[E - 1, tile]

    def order(row0, nrows):
        srow = lax.broadcasted_iota(i32, (nrows, TILE), 0).astype(i16) + jnp.int16(row0)
        onehot = jnp.zeros((nrows, TILE), bf16)
        for kk in range(TOPK):
            onehot = jnp.where(srow == slot[kk:kk + 1, :], jnp.ones((), bf16), onehot)
        buf_ref[cur, row0:row0 + nrows, :] = _dot(onehot, h).astype(bf16)

    order(0, SLOTS_MAIN)

    @pl.when(filled > SLOTS_MAIN)
    def _():
        order(SLOTS_MAIN, SLOTS - SLOTS_MAIN)

    def run_copy(sl):
        return lambda srow_, grow, b: pltpu.make_async_copy(
            buf_ref.at[sl, pl.ds(srow_, b)], xs_ref.at[pl.ds(grow, b)], sem.at[sl])

    _for_each_run(n8_ref, l_ref, g_ref, tile, lambda s, g, b, j: _start(run_copy(cur)(s, g, b), j))

    def wait_tile(t, sl):
        _wait_rows(lambda b: run_copy(sl)(0, 0, b), l_ref[E - 1, t] + n8_ref[E - 1, t])

    @pl.when(i > 0)
    def _():
        wait_tile(tile - 1, 1 - cur)

    last = nsteps - 1

    def zero_fill(fn):
        def per_expert(e, c):
            end = pl.multiple_of(g_ref[e, last] + n8_ref[e, last], RUN_ALIGN)
            _for_each_piece((-end) & (MOE_BM - 1), TAIL_PIECES, lambda off, b, j: fn(pltpu.make_async_copy(
                zero_ref.at[pl.ds(0, b)], xs_ref.at[pl.ds(end + off, b)], zsem)))
            return c
        lax.fori_loop(0, E, per_expert, 0)
        used = (g_ref[E - 1, last] + n8_ref[E - 1, last] + (MOE_BM - 1)) // MOE_BM

        def per_block(blk, c):
            fn(pltpu.make_async_copy(zero_ref, xs_ref.at[pl.ds(pl.multiple_of(blk * MOE_BM, MOE_BM), MOE_BM)], zsem))
            return c
        lax.fori_loop(used, xs_ref.shape[0] // MOE_BM, per_block, 0)

    @pl.when(i == 0)
    def _():
        zero_ref[...] = jnp.zeros_like(zero_ref)
        zero_fill(lambda cp: cp.start())

    @pl.when(i == last)
    def _():
        wait_tile(tile, cur)
        zero_fill(lambda cp: cp.wait())


def _dispatch(tables, slot, h2_a, h2_b, nrows):
    tiles_a, tiles_b = h2_a.shape[0] // TILE, h2_b.shape[0] // TILE
    return pl.pallas_call(
        functools.partial(_dispatch_body, tiles_a),
        grid_spec=pltpu.PrefetchScalarGridSpec(
            num_scalar_prefetch=len(tables),
            grid=(tiles_a + tiles_b,),
            in_specs=[pl.BlockSpec((TOPK, TILE), lambda i, *_: (0, i)),
                      pl.BlockSpec((TILE, D), lambda i, *_: (jnp.minimum(i, tiles_a - 1), 0)),
                      pl.BlockSpec((TILE, D), lambda i, *_: (jnp.maximum(i - tiles_a, 0), 0))],
            out_specs=pl.BlockSpec(memory_space=pl.ANY),
            scratch_shapes=[pltpu.VMEM((2, SLOTS, D), bf16), pltpu.VMEM((MOE_BM, D), bf16),
                            pltpu.SemaphoreType.DMA((2,)), pltpu.SemaphoreType.DMA]),
        out_shape=jax.ShapeDtypeStruct((nrows, D), bf16),
        compiler_params=pltpu.CompilerParams(dimension_semantics=("arbitrary",), vmem_limit_bytes=VMEM_LIMIT),
        name="dispatch",
    )(*tables, slot, h2_a, h2_b)


def _expert_body(blk_ref, used_ref, xs_ref, wg_ref, wu_ref, wd_ref, o_ref, wgu_ref, wdb_ref):
    i = pl.program_id(0)
    e = blk_ref[i]

    @pl.when((e >= 0) & ((i == 0) | (e != blk_ref[jnp.maximum(i - 1, 0)])))
    def _():
        wgu_ref[:, :DE] = wg_ref[0].astype(bf16)
        wgu_ref[:, DE:] = wu_ref[0].astype(bf16)
        wdb_ref[...] = wd_ref[0].astype(bf16)

    @pl.when(e >= 0)
    def _():
        hgu = _dot(xs_ref[...], wgu_ref[...])
        hg = hgu[:, :DE]
        act = (hg * jax.nn.sigmoid(hg) * hgu[:, DE:]).astype(bf16)
        o_ref[...] = _dot(act, wdb_ref[...]).astype(bf16)

    @pl.when(e < 0)
    def _():
        o_ref[...] = jnp.zeros_like(o_ref)


def _experts(blk_e, xs, we_gate, we_up, we_down):
    nblk = blk_e.shape[0]
    used = jnp.minimum(jnp.sum((blk_e >= 0).astype(i32)), nblk - 1).reshape(1)
    ex = lambda i, blk, used_: (jnp.maximum(blk[i], 0), 0, 0)
    rows = lambda i, blk, used_: (jnp.where(blk[i] >= 0, i, used_[0]), 0)
    return pl.pallas_call(
        _expert_body,
        grid_spec=pltpu.PrefetchScalarGridSpec(
            num_scalar_prefetch=2,
            grid=(nblk,),
            in_specs=[pl.BlockSpec((MOE_BM, D), rows),
                      pl.BlockSpec((1, D, DE), ex), pl.BlockSpec((1, D, DE), ex), pl.BlockSpec((1, DE, D), ex)],
            out_specs=pl.BlockSpec((MOE_BM, D), rows),
            scratch_shapes=[pltpu.VMEM((D, 2 * DE), bf16), pltpu.VMEM((DE, D), bf16)]),
        out_shape=jax.ShapeDtypeStruct(xs.shape, xs.dtype),
        input_output_aliases={2: 0},
        compiler_params=pltpu.CompilerParams(dimension_semantics=("arbitrary",), vmem_limit_bytes=VMEM_LIMIT),
        name="experts",
    )(blk_e, used, xs, we_gate, we_up, we_down)


def _final_body(nb, lt, tile0, n8_ref, l_ref, g_ref, slot_ref, wt_ref, x2_ref, g2_ref, fw_ref, ys_ref, o_ref,
                buf_ref, routed_ref, sem):
    m = nb * lt
    n_inner = pl.num_programs(1)
    step = pl.program_id(0) * n_inner + pl.program_id(1)
    nsteps = pl.num_programs(0) * n_inner
    tile = tile0 + step
    cur = step % 2

    def run_copy(sl):
        return lambda srow, grow, b: pltpu.make_async_copy(
            ys_ref.at[pl.ds(grow, b)], buf_ref.at[sl, pl.ds(srow, b)], sem.at[sl])

    @pl.when(step == 0)
    def _():
        buf_ref[...] = jnp.zeros_like(buf_ref)
        _for_each_run(n8_ref, l_ref, g_ref, tile, lambda s, g, b, j: _start(run_copy(0)(s, g, b), j))

    @pl.when(step + 1 < nsteps)
    def _():
        _for_each_run(n8_ref, l_ref, g_ref, tile + 1, lambda s, g, b, j: _start(run_copy(1 - cur)(s, g, b), j))

    _wait_rows(lambda b: run_copy(cur)(0, 0, b), l_ref[E - 1, tile] + n8_ref[E - 1, tile])

    slot = slot_ref[...].astype(i16)
    wt = wt_ref[...]
    wt_hi = wt.astype(bf16)
    wt_lo = (wt - wt_hi.astype(f32)).astype(bf16)
    def combine(col0, ncols):
        scol = lax.broadcasted_iota(i32, (m, ncols), 1).astype(i16) + jnp.int16(col0)
        w_hi = jnp.zeros((m, ncols), bf16)
        w_lo = jnp.zeros((m, ncols), bf16)
        for kk in range(TOPK):
            hit = scol == slot[:, kk:kk + 1]
            w_hi = jnp.where(hit, wt_hi[:, kk:kk + 1], w_hi)
            w_lo = jnp.where(hit, wt_lo[:, kk:kk + 1], w_lo)
        yb = buf_ref[cur, col0:col0 + ncols, :]
        return _dot(w_hi, yb) + _dot(w_lo, yb)

    routed_ref[...] = combine(0, SLOTS_MAIN)

    @pl.when(l_ref[E - 1, tile] + n8_ref[E - 1, tile] > SLOTS_MAIN)
    def _():
        routed_ref[...] += combine(SLOTS_MAIN, SLOTS - SLOTS_MAIN)
    routed = routed_ref[...]
    g2 = g2_ref[0]
    if nb == 1:
        g2 = g2.reshape(1, D)
    else:
        g2 = jnp.broadcast_to(g2, (nb, lt, D)).reshape(m, D)
    y = x2_ref[...] + g2 * routed
    o_ref[...] = _rms(y, fw_ref[...]).reshape(nb, lt, D)


def _final(tables, slot_t, wts_t, x2, ys, mod4, fw, bsz, seqlen, row0, nb, lt):
    m = nb * lt
    assert m == TILE and row0 % TILE == 0
    n_lt = seqlen // lt
    tile0 = row0 // TILE
    tile = lambda b, i: tile0 + b * n_lt + i
    return pl.pallas_call(
        functools.partial(_final_body, nb, lt, tile0),
        grid_spec=pltpu.PrefetchScalarGridSpec(
            num_scalar_prefetch=len(tables),
            grid=(bsz // nb, n_lt),
            in_specs=[pl.BlockSpec((m, TOPK), lambda b, i, *_: (tile(b, i), 0)),
                      pl.BlockSpec((m, TOPK), lambda b, i, *_: (tile(b, i), 0)),
                      pl.BlockSpec((m, D), lambda b, i, *_: (b * n_lt + i, 0)),
                      pl.BlockSpec((1, nb, 1, D), lambda b, i, *_: (5, b, 0, 0)),
                      pl.BlockSpec((1, D), lambda b, i, *_: (0, 0)),
                      pl.BlockSpec(memory_space=pl.ANY)],
            out_specs=pl.BlockSpec((nb, lt, D), lambda b, i, *_: (b, i, 0)),
            scratch_shapes=[pltpu.VMEM((2, SLOTS, D), bf16), pltpu.VMEM((m, D), f32),
                            pltpu.SemaphoreType.DMA((2,))]),
        out_shape=jax.ShapeDtypeStruct((bsz, seqlen, D), f32),
        compiler_params=pltpu.CompilerParams(dimension_semantics=("arbitrary", "arbitrary"),
                                             vmem_limit_bytes=VMEM_LIMIT),
        name="final",
    )(*tables, slot_t, wts_t, x2, mod4, fw, ys)


def kernel(x_prompt, x_sample, state_conv, state_gla, c_prompt, c_sample, w_ada, b_ada, norm1_w, w_in, conv_w,
           w_gk, b_gk, gla_norm_w, w_out_conv, w_out_gla, w_o, norm2_w, router_w, router_bias, we_gate, we_up,
           we_down, ws_gate, ws_up, ws_down, final_norm_w):
    assert w_ada.shape[0] == 1, "single-layer step"
    bp, lp, _ = x_prompt.shape
    bs, ls, _ = x_sample.shape
    tp, ts = bp * lp, bs * ls
    t_all = tp + ts

    w_in0 = w_in[0]
    n_main = 3 * D + 2 * DK + 2 * DV
    rank = w_gk.shape[1]
    weights = (
        norm1_w[0].reshape(1, D),
        w_in0[:, :n_main].astype(bf16),
        jnp.pad(w_in0[:, n_main:n_main + rank], ((0, 0), (0, RANK_PAD - rank))).astype(bf16),
        w_in0[:, n_main + rank:].astype(bf16),
        jnp.pad(w_gk[0], ((0, RANK_PAD - rank), (0, 0))).astype(bf16),
        b_gk[0].reshape(1, DK),
        conv_w[0],
        gla_norm_w[0].reshape(1, DVH),
        w_out_conv[0].astype(bf16),
        w_out_gla[0].astype(bf16),
        w_o[0].astype(bf16),
        norm2_w[0].reshape(1, D),
        router_w[0].T.astype(bf16),
        ws_gate[0].astype(bf16),
        ws_up[0].astype(bf16),
        ws_down[0].astype(bf16),
    )

    mod = _adaln(jnp.concatenate([c_prompt, c_sample], axis=0), w_ada[0], b_ada[0].reshape(1, 6 * D))
    mod4 = mod.reshape(bp + bs, 6, 1, D).transpose(1, 0, 2, 3)
    mod_p, mod_s = mod4[:, :bp], mod4[:, bp:]

    lt_p = 512
    x2_p, h2_p, lg_p, conv_p, gla_p = _mixer(x_prompt, mod_p, None, weights, 1, lt_p, GLA_CHUNK)
    nb_s = 8
    x2_s, h2_s, lg_s, conv_s, gla_s = _mixer(x_sample, mod_s, (state_conv[0], state_gla[0]), weights,
                                             nb_s, ls, math.gcd(ls, GLA_CHUNK))
    lg = jnp.concatenate([lg_p.transpose(1, 0, 2).reshape(E, tp), lg_s.transpose(1, 0, 2).reshape(E, ts)], axis=1)

    eidx, wts, lrank, cnt = _route(lg, router_bias[0].reshape(E, 1))
    n_tiles = t_all // TILE
    max_rows = t_all * TOPK + n_tiles * E * (RUN_ALIGN - 1) + E * (MOE_BM - RUN_ALIGN)
    nblk = -(-max_rows // MOE_BM)
    slot, n8, loff, goff, blk_e = _plan(eidx, lrank, cnt, -(-nblk // 128) * 128)
    tables = (n8, loff, goff)

    xs = _dispatch(tables, slot, h2_p, h2_s, nblk * MOE_BM)
    ys = _experts(blk_e[0, :nblk], xs, we_gate[0], we_up[0], we_down[0])

    fw = final_norm_w.reshape(1, D)
    slot_t, wts_t = slot.T, wts.T
    y_prompt = _final(tables, slot_t, wts_t, x2_p, ys, mod_p, fw, bp, lp, 0, 1, TILE)
    y_sample = _final(tables, slot_t, wts_t, x2_s, ys, mod_s, fw, bs, ls, tp, TILE // ls, ls)
    return (y_prompt, y_sample, conv_p, gla_p, conv_s, gla_s)
```

```python
import functools
import math

import jax
import jax.numpy as jnp
from jax import lax
from jax.experimental import pallas as pl
from jax.experimental.pallas import tpu as pltpu

f32 = jnp.float32
bf16 = jnp.bfloat16
i32 = jnp.int32
i16 = jnp.int16

D = 1024
H = 4
DKH = 128
DVH = 256
DK = H * DKH
DV = H * DVH
RANK_PAD = 128
GATE_NORMALIZER = 16.0
GLA_CHUNK = 64
E = 64
TOPK = 8
NGROUPS = 8
TOPK_GROUPS = 4
DE = 256
ROUTED_SCALE = 2.5
EPS = 1e-6
NEG = float("-inf")

VMEM_LIMIT = 58 * 1024 * 1024
MOE_BM = 512
EXPERT_IN_SLOTS = 3
TILE = 256
MAX_TILES = 128
RUN_ALIGN = 16
SLOTS = TILE * TOPK + 1024
SLOTS_MAIN = TILE * TOPK + 512
RUN_PIECES = (256, 128, 64, 32, 16)
TAIL_PIECES = (256, 128, 64, 32, 16)
WAIT_PIECES = (2048, 1024, 512, 256, 128, 64, 32, 16)

def _dot(a, b):
    return jnp.dot(a, b, preferred_element_type=f32)


def _split3(x):
    hi = x.astype(bf16)
    r1 = x - hi.astype(f32)
    mid = r1.astype(bf16)
    lo = (r1 - mid.astype(f32)).astype(bf16)
    return hi, mid, lo


def _dot01(m01, x):
    hi, mid, lo = _split3(x)
    return _dot(m01, hi) + _dot(m01, mid) + _dot(m01, lo)


def _dotx01(x, m01):
    hi, mid, lo = _split3(x)
    return _dot(hi, m01) + _dot(mid, m01) + _dot(lo, m01)


def _rms(v, w):
    ms = jnp.mean(v * v, axis=-1, keepdims=True)
    return v * lax.rsqrt(ms + EPS) * w


def _const_spec(shape):
    n = len(shape)
    return pl.BlockSpec(shape, lambda *_: (0,) * n, pipeline_mode=pl.Buffered(1))


def _adaln_body(c_ref, w_ref, b_ref, o_ref):
    c = c_ref[...]
    a = (c * jax.nn.sigmoid(c)).astype(bf16)
    o_ref[...] = _dot(a, w_ref[...].astype(bf16)) + b_ref[...]


def _adaln(c_all, w_ada, b_ada):
    nrow = c_all.shape[0]
    tn = 1536
    return pl.pallas_call(
        _adaln_body,
        grid=(6 * D // tn,),
        in_specs=[pl.BlockSpec((nrow, D), lambda j: (0, 0)),
                  pl.BlockSpec((D, tn), lambda j: (0, j)),
                  pl.BlockSpec((1, tn), lambda j: (0, j))],
        out_specs=pl.BlockSpec((nrow, tn), lambda j: (0, j)),
        out_shape=jax.ShapeDtypeStruct((nrow, 6 * D), f32),
        compiler_params=pltpu.CompilerParams(dimension_semantics=("arbitrary",), vmem_limit_bytes=VMEM_LIMIT),
        name="adaln",
    )(c_all, w_ada, b_ada)


def _mixer_body(nb, lt, chunk, has_state, *refs):
    if has_state:
        (x_ref, mod_ref, cs_ref, gs_ref, *rest) = refs
    else:
        (x_ref, mod_ref, *rest) = refs
        cs_ref = gs_ref = None
    (n1_ref, wmain_ref, wz_ref, wu_ref, wgk_ref, bgk_ref, cw_ref, gnw_ref, woc_ref, wog_ref, wo_ref,
     n2_ref, rwt_ref, wsg_ref, wsu_ref, wsd_ref,
     x2_ref, h2_ref, lg_ref, nc_ref, ng_ref, carry_ref, st_ref, og_ref, sin_ref) = rest
    m = nb * lt
    i = pl.program_id(1)
    last = pl.num_programs(1) - 1

    def rows(v):
        w = v.shape[-1]
        if nb == 1:
            return v.reshape(1, w)
        return jnp.broadcast_to(v, (nb, lt, w)).reshape(m, w)

    @pl.when(i == 0)
    def _():
        if has_state:
            cs = cs_ref[...]
            carry_ref[0] = cs[:, 0:1, :]
            carry_ref[1] = cs[:, 1:2, :]
            for hh in range(H):
                st_ref[hh] = jnp.swapaxes(gs_ref[:, hh], 1, 2)
        else:
            carry_ref[...] = jnp.zeros_like(carry_ref)
            st_ref[...] = jnp.zeros_like(st_ref)

    x = x_ref[...].reshape(m, D)
    sh1, sc1, g1, sh2, sc2, g2 = [rows(mod_ref[j]) for j in range(6)]
    hb = (_rms(x, n1_ref[...]) * (1.0 + sc1) + sh1).astype(bf16)

    def proj(lo, hi):
        return _dot(hb, wmain_ref[:, lo:hi])

    pre = proj(D, 2 * D) * proj(2 * D, 3 * D)
    c0 = rows(carry_ref[0])
    c1 = rows(carry_ref[1])
    l_idx = lax.broadcasted_iota(i32, (m, D), 0) & (lt - 1)
    r1 = pltpu.roll(pre, 1, 0)
    r2 = pltpu.roll(pre, 2, 0)
    prev1 = jnp.where(l_idx == 0, c1, r1)
    prev2 = jnp.where(l_idx == 0, c0, jnp.where(l_idx == 1, c1, r2))
    cw = cw_ref[...]
    conv = cw[0:1] * prev2 + cw[1:2] * prev1 + cw[2:3] * pre
    y_a = _dot((proj(0, D) * conv).astype(bf16), woc_ref[...])
    pre3 = pre.reshape(nb, lt, D)
    tail = pre3[:, lt - 2:lt, :]
    carry_ref[0] = tail[:, 0:1, :]
    carry_ref[1] = tail[:, 1:2, :]
    nc_ref[0] = tail

    q = proj(3 * D, 3 * D + DK)
    k = proj(3 * D + DK, 3 * D + 2 * DK)
    v = proj(4 * D, 5 * D)
    g = proj(5 * D, 6 * D)
    z = _dot(hb, wz_ref[...]).astype(bf16)
    pa = _dot(z, wgk_ref[...]) + bgk_ref[...]
    la = (jnp.minimum(pa, 0.0) - jnp.log1p(jnp.exp(-jnp.abs(pa)))) / GATE_NORMALIZER
    rr = lax.broadcasted_iota(i32, (m, m), 0)
    cc = lax.broadcasted_iota(i32, (m, m), 1)
    same_chunk = (rr & ~(chunk - 1)) == (cc & ~(chunk - 1))
    tri = (same_chunk & (cc <= rr)).astype(bf16)
    bcum = _dot01(tri, la)
    ngrp = m // chunk
    bend = bcum.reshape(ngrp, chunk, DK)[:, chunk - 1:chunk, :]
    bend_rows = jnp.broadcast_to(bend, (ngrp, chunk, DK)).reshape(m, DK)
    nch = lt // chunk
    assert nb == 1 or nch == 1
    hs = H * nb
    nbat = nch * hs

    def per_head(xm, w):
        x3 = xm.reshape(nb * nch, chunk, H * w)
        parts = [x3[:, :, hh * w:(hh + 1) * w] for hh in range(H)]
        return jnp.stack(parts, axis=1 if nb == 1 else 0).reshape(nbat, chunk, w)

    qd = per_head((q * (DKH ** -0.5) * jnp.exp(bcum)).astype(bf16), DKH)
    kd = per_head((k * jnp.exp(-bcum)).astype(bf16), DKH)
    ke = per_head((k * jnp.exp(bend_rows - bcum)).astype(bf16), DKH)
    vv = per_head(v.astype(bf16), DVH)
    a_end = per_head(jnp.exp(bend_rows), DKH)[:, 0:1, :]
    causal = lax.broadcasted_iota(i32, (chunk, chunk), 1) <= lax.broadcasted_iota(i32, (chunk, chunk), 0)
    sc = jnp.einsum('bcd,bsd->bcs', qd, kd, preferred_element_type=f32)
    sc = jnp.where(causal, sc, 0.0).astype(bf16)
    o = jnp.einsum('bcs,bse->bce', sc, vv, preferred_element_type=f32)
    upd = jnp.einsum('bse,bsd->bed', vv, ke, preferred_element_type=f32)
    st = st_ref[...].reshape(hs, DVH, DKH)
    for j in range(nch):
        sin_ref[j * hs:(j + 1) * hs] = st.astype(bf16)
        st = st * a_end[j * hs:(j + 1) * hs] + upd[j * hs:(j + 1) * hs]
    st_ref[...] = st.reshape(H, nb, DVH, DKH)
    o = o + jnp.einsum('bcd,bed->bce', qd, sin_ref[...], preferred_element_type=f32)
    on = _rms(o, gnw_ref[...])
    g3 = g.reshape(nb, lt, DV)
    for j in range(nch):
        js = slice(j * chunk, (j + 1) * chunk)
        for hh in range(H):
            vs = slice(hh * DVH, (hh + 1) * DVH)
            gh = g3[:, js, vs]
            b0 = (j * H + hh) * nb
            og_ref[:, js, vs] = (on[b0:b0 + nb] * (gh * jax.nn.sigmoid(gh))).astype(bf16)

    @pl.when(i == last)
    def _():
        for hh in range(H):
            ng_ref[0, :, hh] = jnp.swapaxes(st_ref[hh], 1, 2)

    y_b = _dot(og_ref[...].reshape(m, DV), wog_ref[...])
    u_a = _dot(hb, wu_ref[:, 0:D])
    u_b = _dot(hb, wu_ref[:, D:2 * D])
    merged = (jax.nn.sigmoid(u_a) * y_a + jax.nn.sigmoid(u_b) * y_b).astype(bf16)
    x1 = x + g1 * _dot(merged, wo_ref[...])

    h2 = (_rms(x1, n2_ref[...]) * (1.0 + sc2) + sh2).astype(bf16)
    sg = _dot(h2, wsg_ref[...])
    act = (sg * jax.nn.sigmoid(sg) * _dot(h2, wsu_ref[...])).astype(bf16)
    x2_ref[...] = x1 + g2 * _dot(act, wsd_ref[...])
    h2_ref[...] = h2
    lg_ref[0] = lax.dot_general(rwt_ref[...], h2, (((1,), (1,)), ((), ())), preferred_element_type=f32)


def _mixer(x, mod4, states, weights, nb, lt, chunk):
    bsz, seqlen, _ = x.shape
    has_state = states is not None
    m = nb * lt
    n_lt = seqlen // lt
    nsteps = (bsz // nb) * n_lt
    t_all = bsz * seqlen

    def tok_blk(b, i):
        return b * n_lt + i

    in_specs = [pl.BlockSpec((nb, lt, D), lambda b, i: (b, i, 0)),
                pl.BlockSpec((6, nb, 1, D), lambda b, i: (0, b, 0, 0))]
    args = [x, mod4]
    if has_state:
        in_specs += [pl.BlockSpec((nb, 2, D), lambda b, i: (b, 0, 0)),
                     pl.BlockSpec((nb, H, DKH, DVH), lambda b, i: (b, 0, 0, 0))]
        args += list(states)
    in_specs += [_const_spec(w.shape) for w in weights]
    args += list(weights)
    out_specs = [pl.BlockSpec((m, D), lambda b, i: (tok_blk(b, i), 0)),
                 pl.BlockSpec((m, D), lambda b, i: (tok_blk(b, i), 0)),
                 pl.BlockSpec((1, E, m), lambda b, i: (b * n_lt + i, 0, 0)),
                 pl.BlockSpec((1, nb, 2, D), lambda b, i: (0, b, 0, 0)),
                 pl.BlockSpec((1, nb, H, DKH, DVH), lambda b, i: (0, b, 0, 0, 0))]
    out_shape = [jax.ShapeDtypeStruct((t_all, D), f32),
                 jax.ShapeDtypeStruct((t_all, D), bf16),
                 jax.ShapeDtypeStruct((nsteps, E, m), f32),
                 jax.ShapeDtypeStruct((1, bsz, 2, D), f32),
                 jax.ShapeDtypeStruct((1, bsz, H, DKH, DVH), f32)]
    return pl.pallas_call(
        functools.partial(_mixer_body, nb, lt, chunk, has_state),
        grid=(bsz // nb, n_lt),
        in_specs=in_specs,
        out_specs=out_specs,
        out_shape=out_shape,
        scratch_shapes=[pltpu.VMEM((2, nb, 1, D), f32),
                        pltpu.VMEM((H, nb, DVH, DKH), f32),
                        pltpu.VMEM((nb, lt, DV), bf16),
                        pltpu.VMEM(((lt // chunk) * H * nb, DVH, DKH), bf16)],
        compiler_params=pltpu.CompilerParams(dimension_semantics=("arbitrary", "arbitrary"),
                                             vmem_limit_bytes=VMEM_LIMIT),
        name="mixer_state" if has_state else "mixer_prompt",
    )(*args)


def _route_body(lg_ref, bias_ref, eidx_ref, w_ref, lrank_ref, cnt_ref):
    step = pl.program_id(0)
    tl = lg_ref.shape[1]

    scores = jax.nn.sigmoid(lg_ref[...])
    choice = scores + bias_ref[...]
    gsz = E // NGROUPS
    grp = choice.reshape(NGROUPS, gsz, tl)
    mi = lax.broadcasted_iota(i32, (NGROUPS, gsz, tl), 1)
    m1 = jnp.max(grp, axis=1, keepdims=True)
    first = jnp.min(jnp.where(grp == m1, mi, gsz), axis=1, keepdims=True)
    m2 = jnp.max(jnp.where(mi == first, NEG, grp), axis=1, keepdims=True)
    gscore = (m1 + m2).reshape(NGROUPS, tl)
    gi = lax.broadcasted_iota(i32, (NGROUPS, tl), 0)
    gsel = jnp.zeros((NGROUPS, tl), f32)
    work = gscore
    for _ in range(TOPK_GROUPS):
        mx = jnp.max(work, axis=0, keepdims=True)
        f = jnp.min(jnp.where(work == mx, gi, NGROUPS), axis=0, keepdims=True)
        hit = gi == f
        gsel = jnp.where(hit, 1.0, gsel)
        work = jnp.where(hit, NEG, work)
    emask = jnp.broadcast_to(gsel.reshape(NGROUPS, 1, tl), (NGROUPS, gsz, tl)).reshape(E, tl)
    masked = jnp.where(emask > 0.0, choice, NEG)
    ei = lax.broadcasted_iota(i32, (E, tl), 0)
    sel = jnp.zeros((E, tl), f32)
    hits, ws = [], []
    for kk in range(TOPK):
        mx = jnp.max(masked, axis=0, keepdims=True)
        f = jnp.min(jnp.where(masked == mx, ei, E), axis=0, keepdims=True)
        hit = ei == f
        ws.append(jnp.sum(jnp.where(hit, scores, 0.0), axis=0, keepdims=True))
        masked = jnp.where(hit, NEG, masked)
        sel = jnp.where(hit, 1.0, sel)
        hits.append(hit)
        eidx_ref[kk:kk + 1, :] = f
    wsum = ws[0]
    for t in ws[1:]:
        wsum = wsum + t
    for kk in range(TOPK):
        w_ref[kk:kk + 1, :] = ws[kk] / wsum * ROUTED_SCALE

    ui = lax.broadcasted_iota(i32, (tl, tl), 0)
    uj = lax.broadcasted_iota(i32, (tl, tl), 1)
    upper = (ui <= uj).astype(bf16)
    lrank = _dot(sel.astype(bf16), upper) - sel
    for kk in range(TOPK):
        lrank_ref[kk:kk + 1, :] = jnp.sum(jnp.where(hits[kk], lrank, 0.0), axis=0, keepdims=True).astype(i32)

    @pl.when(step == 0)
    def _():
        cnt_ref[...] = jnp.zeros_like(cnt_ref)
    lane = lax.broadcasted_iota(i32, cnt_ref.shape, 1)
    cnt_ref[...] = jnp.where(lane == step, jnp.sum(sel, axis=1, keepdims=True), cnt_ref[...])


def _route(logits_t, bias_col):
    t_all = logits_t.shape[1]
    tl = TILE
    assert t_all // tl <= MAX_TILES
    tok = lambda s: (0, s)
    return pl.pallas_call(
        _route_body,
        grid=(t_all // tl,),
        in_specs=[pl.BlockSpec((E, tl), tok), pl.BlockSpec((E, 1), lambda s: (0, 0))],
        out_specs=[pl.BlockSpec((TOPK, tl), tok), pl.BlockSpec((TOPK, tl), tok), pl.BlockSpec((TOPK, tl), tok),
                   pl.BlockSpec((E, MAX_TILES), lambda s: (0, 0))],
        out_shape=[jax.ShapeDtypeStruct((TOPK, t_all), i32), jax.ShapeDtypeStruct((TOPK, t_all), f32),
                   jax.ShapeDtypeStruct((TOPK, t_all), i32), jax.ShapeDtypeStruct((E, MAX_TILES), f32)],
        compiler_params=pltpu.CompilerParams(dimension_semantics=("arbitrary",), vmem_limit_bytes=VMEM_LIMIT),
        name="route",
    )(logits_t, bias_col)


def _plan_body(eidx_ref, lrank_ref, cnt_ref, slot_ref, n8_ref, l_ref, g_ref, blk_ref):
    i = pl.program_id(0)
    tl = eidx_ref.shape[1]

    @pl.when(i == 0)
    def _():
        cnt = cnt_ref[...]
        n8 = jnp.floor((cnt + (RUN_ALIGN - 1)) * (1.0 / RUN_ALIGN)) * RUN_ALIGN
        ti = lax.broadcasted_iota(i32, (MAX_TILES, MAX_TILES), 0)
        tj = lax.broadcasted_iota(i32, (MAX_TILES, MAX_TILES), 1)
        before = _dotx01(n8, (ti < tj).astype(bf16))
        region = jnp.sum(n8, axis=1, keepdims=True)
        nblocks = jnp.floor((region + (MOE_BM - 1)) * (1.0 / MOE_BM))
        ri = lax.broadcasted_iota(i32, (E, E), 0)
        ci = lax.broadcasted_iota(i32, (E, E), 1)
        below = (ci < ri).astype(bf16)
        bstart = _dot01(below, jnp.broadcast_to(nblocks, (E, MAX_TILES)))
        n8_ref[...] = n8.astype(i32)
        l_ref[...] = _dot01(below, n8).astype(i32)
        g_ref[...] = (bstart * MOE_BM + before).astype(i32)
        lane0 = lax.broadcasted_iota(i32, (E, MAX_TILES), 1) == 0
        blk_ref[...] = jnp.where(lane0, bstart, nblocks).astype(i32)

    lane = lax.broadcasted_iota(i32, (E, MAX_TILES), 1)
    lcol = jnp.sum(jnp.where(lane == i, l_ref[...].astype(f32), 0.0), axis=1, keepdims=True)
    ei = lax.broadcasted_iota(i32, (E, tl), 0)
    for kk in range(TOPK):
        hit = ei == eidx_ref[kk:kk + 1, :]
        start = jnp.sum(jnp.where(hit, lcol, 0.0), axis=0, keepdims=True)
        slot_ref[kk:kk + 1, :] = start.astype(i32) + lrank_ref[kk:kk + 1, :]


def _plan(eidx, lrank, cnt):
    t_all = eidx.shape[1]
    tl = TILE
    tok = lambda s: (0, s)
    table = pl.BlockSpec((E, MAX_TILES), lambda s: (0, 0))
    return pl.pallas_call(
        _plan_body,
        grid=(t_all // tl,),
        in_specs=[pl.BlockSpec((TOPK, tl), tok), pl.BlockSpec((TOPK, tl), tok), table],
        out_specs=[pl.BlockSpec((TOPK, tl), tok), table, table, table, table],
        out_shape=[jax.ShapeDtypeStruct((TOPK, t_all), i32)] + [jax.ShapeDtypeStruct((E, MAX_TILES), i32)] * 4,
        compiler_params=pltpu.CompilerParams(dimension_semantics=("arbitrary",), vmem_limit_bytes=VMEM_LIMIT),
        name="plan",
    )(eidx, lrank, cnt)


def _for_each_piece(total, pieces, fn):
    for j, b in enumerate(pieces):
        @pl.when((total & b) != 0)
        def _():
            fn(pl.multiple_of(total & ~(2 * b - 1), pieces[-1]), b, j)


def _for_each_run(n8_ref, l_ref, g_ref, tile, fn):
    def per_expert(e, c):
        lo = pl.multiple_of(l_ref[e, tile], RUN_ALIGN)
        go = pl.multiple_of(g_ref[e, tile], RUN_ALIGN)
        _for_each_piece(n8_ref[e, tile], RUN_PIECES, lambda off, b, j: fn(lo + off, go + off, b, j))
        return c
    lax.fori_loop(0, E, per_expert, 0)


def _start(copy, j):
    copy.start(priority=j % 2)


def _wait_rows(make_copy, rows):
    _for_each_piece(rows, WAIT_PIECES, lambda off, b, j: make_copy(b).wait())


def _dispatch_body(tiles_a, n8_ref, l_ref, g_ref, slot_ref, ha_ref, hb_ref, xs_ref, buf_ref, zero_ref, sem, zsem):
    tile = pl.program_id(0)
    i = tile
    nsteps = pl.num_programs(0)
    cur = i % 2

    slot = slot_ref[...].astype(i16)
    h = jnp.where(tile < tiles_a, ha_ref[...], hb_ref[...])
    filled = l_ref[E - 1, tile] + n8_ref[E - 1, tile]

    def order(row0, nrows):
        srow = lax.broadcasted_iota(i32, (nrows, TILE), 0).astype(i16) + jnp.int16(row0)
        onehot = jnp.zeros((nrows, TILE), bf16)
        for kk in range(TOPK):
            onehot = jnp.where(srow == slot[kk:kk + 1, :], jnp.ones((), bf16), onehot)
        buf_ref[cur, row0:row0 + nrows, :] = _dot(onehot, h).astype(bf16)

    order(0, SLOTS_MAIN)

    @pl.when(filled > SLOTS_MAIN)
    def _():
        order(SLOTS_MAIN, SLOTS - SLOTS_MAIN)

    def run_copy(sl):
        return lambda srow_, grow, b: pltpu.make_async_copy(
            buf_ref.at[sl, pl.ds(srow_, b)], xs_ref.at[pl.ds(grow, b)], sem.at[sl])

    _for_each_run(n8_ref, l_ref, g_ref, tile, lambda s, g, b, j: _start(run_copy(cur)(s, g, b), j))

    def wait_tile(t, sl):
        _wait_rows(lambda b: run_copy(sl)(0, 0, b), l_ref[E - 1, t] + n8_ref[E - 1, t])

    @pl.when(i > 0)
    def _():
        wait_tile(tile - 1, 1 - cur)

    last = nsteps - 1

    def zero_fill(fn):
        def per_expert(e, c):
            end = pl.multiple_of(g_ref[e, last] + n8_ref[e, last], RUN_ALIGN)
            _for_each_piece((-end) & (MOE_BM - 1), TAIL_PIECES, lambda off, b, j: fn(pltpu.make_async_copy(
                zero_ref.at[pl.ds(0, b)], xs_ref.at[pl.ds(end + off, b)], zsem)))
            return c
        lax.fori_loop(0, E, per_expert, 0)
        used = (g_ref[E - 1, last] + n8_ref[E - 1, last] + (MOE_BM - 1)) // MOE_BM

        def per_block(blk, c):
            fn(pltpu.make_async_copy(zero_ref, xs_ref.at[pl.ds(pl.multiple_of(blk * MOE_BM, MOE_BM), MOE_BM)], zsem))
            return c
        lax.fori_loop(used, xs_ref.shape[0] // MOE_BM, per_block, 0)

    @pl.when(i == 0)
    def _():
        zero_ref[...] = jnp.zeros_like(zero_ref)
        zero_fill(lambda cp: cp.start())

    @pl.when(i == last)
    def _():
        wait_tile(tile, cur)
        zero_fill(lambda cp: cp.wait())


def _dispatch(tables, slot, h2_a, h2_b, nrows):
    tiles_a, tiles_b = h2_a.shape[0] // TILE, h2_b.shape[0] // TILE
    return pl.pallas_call(
        functools.partial(_dispatch_body, tiles_a),
        grid_spec=pltpu.PrefetchScalarGridSpec(
            num_scalar_prefetch=len(tables),
            grid=(tiles_a + tiles_b,),
            in_specs=[pl.BlockSpec((TOPK, TILE), lambda i, *_: (0, i)),
                      pl.BlockSpec((TILE, D), lambda i, *_: (jnp.minimum(i, tiles_a - 1), 0)),
                      pl.BlockSpec((TILE, D), lambda i, *_: (jnp.maximum(i - tiles_a, 0), 0))],
            out_specs=pl.BlockSpec(memory_space=pl.ANY),
            scratch_shapes=[pltpu.VMEM((2, SLOTS, D), bf16), pltpu.VMEM((MOE_BM, D), bf16),
                            pltpu.SemaphoreType.DMA((2,)), pltpu.SemaphoreType.DMA]),
        out_shape=jax.ShapeDtypeStruct((nrows, D), bf16),
        compiler_params=pltpu.CompilerParams(dimension_semantics=("arbitrary",), vmem_limit_bytes=VMEM_LIMIT),
        name="dispatch",
    )(*tables, slot, h2_a, h2_b)


def _expert_body(blk_ref, wg_ref, wu_ref, wd_ref, xs_ref, ys_ref, xbuf_ref, ybuf_ref, wgu_ref, wdb_ref, isem, osem):
    e = pl.program_id(0)
    first, count = blk_ref[e, 0], blk_ref[e, 1]
    used = blk_ref[E - 1, 0] + blk_ref[E - 1, 1]
    wgu_ref[:, :DE] = wg_ref[0].astype(bf16)
    wgu_ref[:, DE:] = wu_ref[0].astype(bf16)
    wdb_ref[...] = wd_ref[0].astype(bf16)

    def rows(g):
        return pl.ds(pl.multiple_of(g * MOE_BM, MOE_BM), MOE_BM)

    def load(g):
        sl = g % EXPERT_IN_SLOTS
        return pltpu.make_async_copy(xs_ref.at[rows(g)], xbuf_ref.at[sl], isem.at[sl])

    def store(g):
        sl = g % 2
        return pltpu.make_async_copy(ybuf_ref.at[sl], ys_ref.at[rows(g)], osem.at[sl])

    @pl.when(e == 0)
    def _():
        for g in range(EXPERT_IN_SLOTS - 1):
            @pl.when(g < used)
            def _():
                load(g).start()

    def block(j, c):
        g = first + j

        @pl.when(g + (EXPERT_IN_SLOTS - 1) < used)
        def _():
            load(g + (EXPERT_IN_SLOTS - 1)).start()
        load(g).wait()

        @pl.when(g >= 2)
        def _():
            store(g - 2).wait()
        hgu = _dot(xbuf_ref[g % EXPERT_IN_SLOTS], wgu_ref[...])
        hg = hgu[:, :DE]
        act = (hg * jax.nn.sigmoid(hg) * hgu[:, DE:]).astype(bf16)
        ybuf_ref[g % 2] = _dot(act, wdb_ref[...]).astype(bf16)
        store(g).start()
        return c
    lax.fori_loop(0, count, block, 0)

    @pl.when(e == E - 1)
    def _():
        for back in (2, 1):
            @pl.when(used >= back)
            def _():
                store(used - back).wait()


def _experts(blk, xs, we_gate, we_up, we_down):
    ex = lambda e, blk_: (e, 0, 0)
    return pl.pallas_call(
        _expert_body,
        grid_spec=pltpu.PrefetchScalarGridSpec(
            num_scalar_prefetch=1,
            grid=(E,),
            in_specs=[pl.BlockSpec((1, D, DE), ex), pl.BlockSpec((1, D, DE), ex), pl.BlockSpec((1, DE, D), ex),
                      pl.BlockSpec(memory_space=pl.ANY)],
            out_specs=pl.BlockSpec(memory_space=pl.ANY),
            scratch_shapes=[pltpu.VMEM((EXPERT_IN_SLOTS, MOE_BM, D), bf16), pltpu.VMEM((2, MOE_BM, D), bf16),
                            pltpu.VMEM((D, 2 * DE), bf16), pltpu.VMEM((DE, D), bf16),
                            pltpu.SemaphoreType.DMA((EXPERT_IN_SLOTS,)), pltpu.SemaphoreType.DMA((2,))]),
        out_shape=jax.ShapeDtypeStruct(xs.shape, xs.dtype),
        input_output_aliases={4: 0},
        compiler_params=pltpu.CompilerParams(dimension_semantics=("arbitrary",), vmem_limit_bytes=VMEM_LIMIT),
        name="experts",
    )(blk, we_gate, we_up, we_down, xs)


def _final_body(nb, lt, tile0, n8_ref, l_ref, g_ref, slot_ref, wt_ref, x2_ref, g2_ref, fw_ref, ys_ref, o_ref,
                buf_ref, routed_ref, sem):
    m = nb * lt
    n_inner = pl.num_programs(1)
    step = pl.program_id(0) * n_inner + pl.program_id(1)
    nsteps = pl.num_programs(0) * n_inner
    tile = tile0 + step
    cur = step % 2

    def run_copy(sl):
        return lambda srow, grow, b: pltpu.make_async_copy(
            ys_ref.at[pl.ds(grow, b)], buf_ref.at[sl, pl.ds(srow, b)], sem.at[sl])

    @pl.when(step == 0)
    def _():
        buf_ref[...] = jnp.zeros_like(buf_ref)
        _for_each_run(n8_ref, l_ref, g_ref, tile, lambda s, g, b, j: _start(run_copy(0)(s, g, b), j))

    @pl.when(step + 1 < nsteps)
    def _():
        _for_each_run(n8_ref, l_ref, g_ref, tile + 1, lambda s, g, b, j: _start(run_copy(1 - cur)(s, g, b), j))

    _wait_rows(lambda b: run_copy(cur)(0, 0, b), l_ref[E - 1, tile] + n8_ref[E - 1, tile])

    slot = slot_ref[...].astype(i16)
    wt = wt_ref[...]
    wt_hi = wt.astype(bf16)
    wt_lo = (wt - wt_hi.astype(f32)).astype(bf16)
    def combine(col0, ncols):
        scol = lax.broadcasted_iota(i32, (m, ncols), 1).astype(i16) + jnp.int16(col0)
        w_hi = jnp.zeros((m, ncols), bf16)
        w_lo = jnp.zeros((m, ncols), bf16)
        for kk in range(TOPK):
            hit = scol == slot[:, kk:kk + 1]
            w_hi = jnp.where(hit, wt_hi[:, kk:kk + 1], w_hi)
            w_lo = jnp.where(hit, wt_lo[:, kk:kk + 1], w_lo)
        yb = buf_ref[cur, col0:col0 + ncols, :]
        return _dot(w_hi, yb) + _dot(w_lo, yb)

    routed_ref[...] = combine(0, SLOTS_MAIN)

    @pl.when(l_ref[E - 1, tile] + n8_ref[E - 1, tile] > SLOTS_MAIN)
    def _():
        routed_ref[...] += combine(SLOTS_MAIN, SLOTS - SLOTS_MAIN)
    routed = routed_ref[...]
    g2 = g2_ref[0]
    if nb == 1:
        g2 = g2.reshape(1, D)
    else:
        g2 = jnp.broadcast_to(g2, (nb, lt, D)).reshape(m, D)
    y = x2_ref[...] + g2 * routed
    o_ref[...] = _rms(y, fw_ref[...]).reshape(nb, lt, D)


def _final(tables, slot_t, wts_t, x2, ys, mod4, fw, bsz, seqlen, row0, nb, lt):
    m = nb * lt
    assert m == TILE and row0 % TILE == 0
    n_lt = seqlen // lt
    tile0 = row0 // TILE
    tile = lambda b, i: tile0 + b * n_lt + i
    return pl.pallas_call(
        functools.partial(_final_body, nb, lt, tile0),
        grid_spec=pltpu.PrefetchScalarGridSpec(
            num_scalar_prefetch=len(tables),
            grid=(bsz // nb, n_lt),
            in_specs=[pl.BlockSpec((m, TOPK), lambda b, i, *_: (tile(b, i), 0)),
                      pl.BlockSpec((m, TOPK), lambda b, i, *_: (tile(b, i), 0)),
                      pl.BlockSpec((m, D), lambda b, i, *_: (b * n_lt + i, 0)),
                      pl.BlockSpec((1, nb, 1, D), lambda b, i, *_: (5, b, 0, 0)),
                      pl.BlockSpec((1, D), lambda b, i, *_: (0, 0)),
                      pl.BlockSpec(memory_space=pl.ANY)],
            out_specs=pl.BlockSpec((nb, lt, D), lambda b, i, *_: (b, i, 0)),
            scratch_shapes=[pltpu.VMEM((2, SLOTS, D), bf16), pltpu.VMEM((m, D), f32),
                            pltpu.SemaphoreType.DMA((2,))]),
        out_shape=jax.ShapeDtypeStruct((bsz, seqlen, D), f32),
        compiler_params=pltpu.CompilerParams(dimension_semantics=("arbitrary", "arbitrary"),
                                             vmem_limit_bytes=VMEM_LIMIT),
        name="final",
    )(*tables, slot_t, wts_t, x2, mod4, fw, ys)


def kernel(x_prompt, x_sample, state_conv, state_gla, c_prompt, c_sample, w_ada, b_ada, norm1_w, w_in, conv_w,
           w_gk, b_gk, gla_norm_w, w_out_conv, w_out_gla, w_o, norm2_w, router_w, router_bias, we_gate, we_up,
           we_down, ws_gate, ws_up, ws_down, final_norm_w):
    assert w_ada.shape[0] == 1, "single-layer step"
    bp, lp, _ = x_prompt.shape
    bs, ls, _ = x_sample.shape
    tp, ts = bp * lp, bs * ls
    t_all = tp + ts

    w_in0 = w_in[0]
    n_main = 3 * D + 2 * DK + 2 * DV
    rank = w_gk.shape[1]
    weights = (
        norm1_w[0].reshape(1, D),
        w_in0[:, :n_main].astype(bf16),
        jnp.pad(w_in0[:, n_main:n_main + rank], ((0, 0), (0, RANK_PAD - rank))).astype(bf16),
        w_in0[:, n_main + rank:].astype(bf16),
        jnp.pad(w_gk[0], ((0, RANK_PAD - rank), (0, 0))).astype(bf16),
        b_gk[0].reshape(1, DK),
        conv_w[0],
        gla_norm_w[0].reshape(1, DVH),
        w_out_conv[0].astype(bf16),
        w_out_gla[0].astype(bf16),
        w_o[0].astype(bf16),
        norm2_w[0].reshape(1, D),
        router_w[0].T.astype(bf16),
        ws_gate[0].astype(bf16),
        ws_up[0].astype(bf16),
        ws_down[0].astype(bf16),
    )

    mod = _adaln(jnp.concatenate([c_prompt, c_sample], axis=0), w_ada[0], b_ada[0].reshape(1, 6 * D))
    mod4 = mod.reshape(bp + bs, 6, 1, D).transpose(1, 0, 2, 3)
    mod_p, mod_s = mod4[:, :bp], mod4[:, bp:]

    lt_p = 512
    x2_p, h2_p, lg_p, conv_p, gla_p = _mixer(x_prompt, mod_p, None, weights, 1, lt_p, GLA_CHUNK)
    nb_s = 8
    x2_s, h2_s, lg_s, conv_s, gla_s = _mixer(x_sample, mod_s, (state_conv[0], state_gla[0]), weights,
                                             nb_s, ls, math.gcd(ls, GLA_CHUNK))
    lg = jnp.concatenate([lg_p.transpose(1, 0, 2).reshape(E, tp), lg_s.transpose(1, 0, 2).reshape(E, ts)], axis=1)

    eidx, wts, lrank, cnt = _route(lg, router_bias[0].reshape(E, 1))
    n_tiles = t_all // TILE
    max_rows = t_all * TOPK + n_tiles * E * (RUN_ALIGN - 1) + E * (MOE_BM - RUN_ALIGN)
    nblk = -(-max_rows // MOE_BM)
    slot, n8, loff, goff, blk = _plan(eidx, lrank, cnt)
    tables = (n8, loff, goff)

    xs = _dispatch(tables, slot, h2_p, h2_s, nblk * MOE_BM)
    ys = _experts(blk, xs, we_gate[0], we_up[0], we_down[0])

    fw = final_norm_w.reshape(1, D)
    slot_t, wts_t = slot.T, wts.T
    y_prompt = _final(tables, slot_t, wts_t, x2_p, ys, mod_p, fw, bp, lp, 0, 1, TILE)
    y_sample = _final(tables, slot_t, wts_t, x2_s, ys, mod_s, fw, bs, ls, tp, TILE // ls, ls)
    return (y_prompt, y_sample, conv_p, gla_p, conv_s, gla_s)
```

```python
import functools
import math

import jax
import jax.numpy as jnp
from jax import lax
from jax.experimental import pallas as pl
from jax.experimental.pallas import tpu as pltpu

f32 = jnp.float32
bf16 = jnp.bfloat16
i32 = jnp.int32
i16 = jnp.int16

D = 1024
H = 4
DKH = 128
DVH = 256
DK = H * DKH
DV = H * DVH
RANK_PAD = 128
GATE_NORMALIZER = 16.0
GLA_CHUNK = 64
E = 64
TOPK = 8
NGROUPS = 8
TOPK_GROUPS = 4
DE = 256
ROUTED_SCALE = 2.5
EPS = 1e-6
NEG = float("-inf")

VMEM_LIMIT = 58 * 1024 * 1024
MOE_BM = 512
EXPERT_IN_SLOTS = 3
RING = 3
TILE = 256
MAX_TILES = 128
RUN_ALIGN = 16
SLOTS = TILE * TOPK + 1024
SLOTS_MAIN = TILE * TOPK + 512
RUN_PIECES = (256, 128, 64, 32, 16)
TAIL_PIECES = (256, 128, 64, 32, 16)
WAIT_PIECES = (2048, 1024, 512, 256, 128, 64, 32, 16)

def _dot(a, b):
    return jnp.dot(a, b, preferred_element_type=f32)


def _split3(x):
    hi = x.astype(bf16)
    r1 = x - hi.astype(f32)
    mid = r1.astype(bf16)
    lo = (r1 - mid.astype(f32)).astype(bf16)
    return hi, mid, lo


def _dot01(m01, x):
    hi, mid, lo = _split3(x)
    return _dot(m01, hi) + _dot(m01, mid) + _dot(m01, lo)


def _dotx01(x, m01):
    hi, mid, lo = _split3(x)
    return _dot(hi, m01) + _dot(mid, m01) + _dot(lo, m01)


def _rms(v, w):
    ms = jnp.mean(v * v, axis=-1, keepdims=True)
    return v * lax.rsqrt(ms + EPS) * w


def _const_spec(shape):
    n = len(shape)
    return pl.BlockSpec(shape, lambda *_: (0,) * n, pipeline_mode=pl.Buffered(1))


def _adaln_body(c_ref, w_ref, b_ref, o_ref):
    c = c_ref[...]
    a = (c * jax.nn.sigmoid(c)).astype(bf16)
    o_ref[...] = _dot(a, w_ref[...].astype(bf16)) + b_ref[...]


def _adaln(c_all, w_ada, b_ada):
    nrow = c_all.shape[0]
    tn = 1536
    return pl.pallas_call(
        _adaln_body,
        grid=(6 * D // tn,),
        in_specs=[pl.BlockSpec((nrow, D), lambda j: (0, 0)),
                  pl.BlockSpec((D, tn), lambda j: (0, j)),
                  pl.BlockSpec((1, tn), lambda j: (0, j))],
        out_specs=pl.BlockSpec((nrow, tn), lambda j: (0, j)),
        out_shape=jax.ShapeDtypeStruct((nrow, 6 * D), f32),
        compiler_params=pltpu.CompilerParams(dimension_semantics=("arbitrary",), vmem_limit_bytes=VMEM_LIMIT),
        name="adaln",
    )(c_all, w_ada, b_ada)


def _mixer_body(nb, lt, chunk, has_state, *refs):
    if has_state:
        (x_ref, mod_ref, cs_ref, gs_ref, *rest) = refs
    else:
        (x_ref, mod_ref, *rest) = refs
        cs_ref = gs_ref = None
    (n1_ref, wmain_ref, wz_ref, wu_ref, wgk_ref, bgk_ref, cw_ref, gnw_ref, woc_ref, wog_ref, wo_ref,
     n2_ref, rwt_ref, wsg_ref, wsu_ref, wsd_ref,
     x2_ref, h2_ref, lg_ref, nc_ref, ng_ref, carry_ref, st_ref, og_ref, sin_ref) = rest
    m = nb * lt
    i = pl.program_id(1)
    last = pl.num_programs(1) - 1

    def rows(v):
        w = v.shape[-1]
        if nb == 1:
            return v.reshape(1, w)
        return jnp.broadcast_to(v, (nb, lt, w)).reshape(m, w)

    @pl.when(i == 0)
    def _():
        if has_state:
            cs = cs_ref[...]
            carry_ref[0] = cs[:, 0:1, :]
            carry_ref[1] = cs[:, 1:2, :]
            for hh in range(H):
                st_ref[hh] = jnp.swapaxes(gs_ref[:, hh], 1, 2)
        else:
            carry_ref[...] = jnp.zeros_like(carry_ref)
            st_ref[...] = jnp.zeros_like(st_ref)

    x = x_ref[...].reshape(m, D)
    sh1, sc1, g1, sh2, sc2, g2 = [rows(mod_ref[j]) for j in range(6)]
    hb = (_rms(x, n1_ref[...]) * (1.0 + sc1) + sh1).astype(bf16)

    def proj(lo, hi):
        return _dot(hb, wmain_ref[:, lo:hi])

    pre = proj(D, 2 * D) * proj(2 * D, 3 * D)
    c0 = rows(carry_ref[0])
    c1 = rows(carry_ref[1])
    l_idx = lax.broadcasted_iota(i32, (m, D), 0) & (lt - 1)
    r1 = pltpu.roll(pre, 1, 0)
    r2 = pltpu.roll(pre, 2, 0)
    prev1 = jnp.where(l_idx == 0, c1, r1)
    prev2 = jnp.where(l_idx == 0, c0, jnp.where(l_idx == 1, c1, r2))
    cw = cw_ref[...]
    conv = cw[0:1] * prev2 + cw[1:2] * prev1 + cw[2:3] * pre
    y_a = _dot((proj(0, D) * conv).astype(bf16), woc_ref[...])
    pre3 = pre.reshape(nb, lt, D)
    tail = pre3[:, lt - 2:lt, :]
    carry_ref[0] = tail[:, 0:1, :]
    carry_ref[1] = tail[:, 1:2, :]
    nc_ref[0] = tail

    q = proj(3 * D, 3 * D + DK)
    k = proj(3 * D + DK, 3 * D + 2 * DK)
    v = proj(4 * D, 5 * D)
    g = proj(5 * D, 6 * D)
    z = _dot(hb, wz_ref[...]).astype(bf16)
    pa = _dot(z, wgk_ref[...]) + bgk_ref[...]
    la = (jnp.minimum(pa, 0.0) - jnp.log1p(jnp.exp(-jnp.abs(pa)))) / GATE_NORMALIZER
    rr = lax.broadcasted_iota(i32, (m, m), 0)
    cc = lax.broadcasted_iota(i32, (m, m), 1)
    same_chunk = (rr & ~(chunk - 1)) == (cc & ~(chunk - 1))
    tri = (same_chunk & (cc <= rr)).astype(bf16)
    bcum = _dot01(tri, la)
    ngrp = m // chunk
    bend = bcum.reshape(ngrp, chunk, DK)[:, chunk - 1:chunk, :]
    bend_rows = jnp.broadcast_to(bend, (ngrp, chunk, DK)).reshape(m, DK)
    nch = lt // chunk
    assert nb == 1 or nch == 1
    hs = H * nb
    nbat = nch * hs

    def per_head(xm, w):
        x3 = xm.reshape(nb * nch, chunk, H * w)
        parts = [x3[:, :, hh * w:(hh + 1) * w] for hh in range(H)]
        return jnp.stack(parts, axis=1 if nb == 1 else 0).reshape(nbat, chunk, w)

    qd = per_head((q * (DKH ** -0.5) * jnp.exp(bcum)).astype(bf16), DKH)
    kd = per_head((k * jnp.exp(-bcum)).astype(bf16), DKH)
    ke = per_head((k * jnp.exp(bend_rows - bcum)).astype(bf16), DKH)
    vv = per_head(v.astype(bf16), DVH)
    a_end = per_head(jnp.exp(bend_rows), DKH)[:, 0:1, :]
    causal = lax.broadcasted_iota(i32, (chunk, chunk), 1) <= lax.broadcasted_iota(i32, (chunk, chunk), 0)
    sc = jnp.einsum('bcd,bsd->bcs', qd, kd, preferred_element_type=f32)
    sc = jnp.where(causal, sc, 0.0).astype(bf16)
    o = jnp.einsum('bcs,bse->bce', sc, vv, preferred_element_type=f32)
    upd = jnp.einsum('bse,bsd->bed', vv, ke, preferred_element_type=f32)
    st = st_ref[...].reshape(hs, DVH, DKH)
    for j in range(nch):
        sin_ref[j * hs:(j + 1) * hs] = st.astype(bf16)
        st = st * a_end[j * hs:(j + 1) * hs] + upd[j * hs:(j + 1) * hs]
    st_ref[...] = st.reshape(H, nb, DVH, DKH)
    o = o + jnp.einsum('bcd,bed->bce', qd, sin_ref[...], preferred_element_type=f32)
    on = _rms(o, gnw_ref[...])
    g3 = g.reshape(nb, lt, DV)
    for j in range(nch):
        js = slice(j * chunk, (j + 1) * chunk)
        for hh in range(H):
            vs = slice(hh * DVH, (hh + 1) * DVH)
            gh = g3[:, js, vs]
            b0 = (j * H + hh) * nb
            og_ref[:, js, vs] = (on[b0:b0 + nb] * (gh * jax.nn.sigmoid(gh))).astype(bf16)

    @pl.when(i == last)
    def _():
        for hh in range(H):
            ng_ref[0, :, hh] = jnp.swapaxes(st_ref[hh], 1, 2)

    y_b = _dot(og_ref[...].reshape(m, DV), wog_ref[...])
    u_a = _dot(hb, wu_ref[:, 0:D])
    u_b = _dot(hb, wu_ref[:, D:2 * D])
    merged = (jax.nn.sigmoid(u_a) * y_a + jax.nn.sigmoid(u_b) * y_b).astype(bf16)
    x1 = x + g1 * _dot(merged, wo_ref[...])

    h2 = (_rms(x1, n2_ref[...]) * (1.0 + sc2) + sh2).astype(bf16)
    sg = _dot(h2, wsg_ref[...])
    act = (sg * jax.nn.sigmoid(sg) * _dot(h2, wsu_ref[...])).astype(bf16)
    x2_ref[...] = x1 + g2 * _dot(act, wsd_ref[...])
    h2_ref[...] = h2
    lg_ref[0] = lax.dot_general(rwt_ref[...], h2, (((1,), (1,)), ((), ())), preferred_element_type=f32)


def _mixer(x, mod4, states, weights, nb, lt, chunk):
    bsz, seqlen, _ = x.shape
    has_state = states is not None
    m = nb * lt
    n_lt = seqlen // lt
    nsteps = (bsz // nb) * n_lt
    t_all = bsz * seqlen

    def tok_blk(b, i):
        return b * n_lt + i

    in_specs = [pl.BlockSpec((nb, lt, D), lambda b, i: (b, i, 0)),
                pl.BlockSpec((6, nb, 1, D), lambda b, i: (0, b, 0, 0))]
    args = [x, mod4]
    if has_state:
        in_specs += [pl.BlockSpec((nb, 2, D), lambda b, i: (b, 0, 0)),
                     pl.BlockSpec((nb, H, DKH, DVH), lambda b, i: (b, 0, 0, 0))]
        args += list(states)
    in_specs += [_const_spec(w.shape) for w in weights]
    args += list(weights)
    out_specs = [pl.BlockSpec((m, D), lambda b, i: (tok_blk(b, i), 0)),
                 pl.BlockSpec((m, D), lambda b, i: (tok_blk(b, i), 0)),
                 pl.BlockSpec((1, E, m), lambda b, i: (b * n_lt + i, 0, 0)),
                 pl.BlockSpec((1, nb, 2, D), lambda b, i: (0, b, 0, 0)),
                 pl.BlockSpec((1, nb, H, DKH, DVH), lambda b, i: (0, b, 0, 0, 0))]
    out_shape = [jax.ShapeDtypeStruct((t_all, D), f32),
                 jax.ShapeDtypeStruct((t_all, D), bf16),
                 jax.ShapeDtypeStruct((nsteps, E, m), f32),
                 jax.ShapeDtypeStruct((1, bsz, 2, D), f32),
                 jax.ShapeDtypeStruct((1, bsz, H, DKH, DVH), f32)]
    return pl.pallas_call(
        functools.partial(_mixer_body, nb, lt, chunk, has_state),
        grid=(bsz // nb, n_lt),
        in_specs=in_specs,
        out_specs=out_specs,
        out_shape=out_shape,
        scratch_shapes=[pltpu.VMEM((2, nb, 1, D), f32),
                        pltpu.VMEM((H, nb, DVH, DKH), f32),
                        pltpu.VMEM((nb, lt, DV), bf16),
                        pltpu.VMEM(((lt // chunk) * H * nb, DVH, DKH), bf16)],
        compiler_params=pltpu.CompilerParams(dimension_semantics=("arbitrary", "arbitrary"),
                                             vmem_limit_bytes=VMEM_LIMIT),
        name="mixer_state" if has_state else "mixer_prompt",
    )(*args)


def _route_body(lg_ref, bias_ref, eidx_ref, w_ref, lrank_ref, cnt_ref):
    step = pl.program_id(0)
    tl = lg_ref.shape[1]

    scores = jax.nn.sigmoid(lg_ref[...])
    choice = scores + bias_ref[...]
    gsz = E // NGROUPS
    grp = choice.reshape(NGROUPS, gsz, tl)
    mi = lax.broadcasted_iota(i32, (NGROUPS, gsz, tl), 1)
    m1 = jnp.max(grp, axis=1, keepdims=True)
    first = jnp.min(jnp.where(grp == m1, mi, gsz), axis=1, keepdims=True)
    m2 = jnp.max(jnp.where(mi == first, NEG, grp), axis=1, keepdims=True)
    gscore = (m1 + m2).reshape(NGROUPS, tl)
    gi = lax.broadcasted_iota(i32, (NGROUPS, tl), 0)
    gsel = jnp.zeros((NGROUPS, tl), f32)
    work = gscore
    for _ in range(TOPK_GROUPS):
        mx = jnp.max(work, axis=0, keepdims=True)
        f = jnp.min(jnp.where(work == mx, gi, NGROUPS), axis=0, keepdims=True)
        hit = gi == f
        gsel = jnp.where(hit, 1.0, gsel)
        work = jnp.where(hit, NEG, work)
    emask = jnp.broadcast_to(gsel.reshape(NGROUPS, 1, tl), (NGROUPS, gsz, tl)).reshape(E, tl)
    masked = jnp.where(emask > 0.0, choice, NEG)
    ei = lax.broadcasted_iota(i32, (E, tl), 0)
    sel = jnp.zeros((E, tl), f32)
    hits, ws = [], []
    for kk in range(TOPK):
        mx = jnp.max(masked, axis=0, keepdims=True)
        f = jnp.min(jnp.where(masked == mx, ei, E), axis=0, keepdims=True)
        hit = ei == f
        ws.append(jnp.sum(jnp.where(hit, scores, 0.0), axis=0, keepdims=True))
        masked = jnp.where(hit, NEG, masked)
        sel = jnp.where(hit, 1.0, sel)
        hits.append(hit)
        eidx_ref[kk:kk + 1, :] = f
    wsum = ws[0]
    for t in ws[1:]:
        wsum = wsum + t
    for kk in range(TOPK):
        w_ref[kk:kk + 1, :] = ws[kk] / wsum * ROUTED_SCALE

    ui = lax.broadcasted_iota(i32, (tl, tl), 0)
    uj = lax.broadcasted_iota(i32, (tl, tl), 1)
    upper = (ui <= uj).astype(bf16)
    lrank = _dot(sel.astype(bf16), upper) - sel
    for kk in range(TOPK):
        lrank_ref[kk:kk + 1, :] = jnp.sum(jnp.where(hits[kk], lrank, 0.0), axis=0, keepdims=True).astype(i32)

    @pl.when(step == 0)
    def _():
        cnt_ref[...] = jnp.zeros_like(cnt_ref)
    lane = lax.broadcasted_iota(i32, cnt_ref.shape, 1)
    cnt_ref[...] = jnp.where(lane == step, jnp.sum(sel, axis=1, keepdims=True), cnt_ref[...])


def _route(logits_t, bias_col):
    t_all = logits_t.shape[1]
    tl = TILE
    assert t_all // tl <= MAX_TILES
    tok = lambda s: (0, s)
    return pl.pallas_call(
        _route_body,
        grid=(t_all // tl,),
        in_specs=[pl.BlockSpec((E, tl), tok), pl.BlockSpec((E, 1), lambda s: (0, 0))],
        out_specs=[pl.BlockSpec((TOPK, tl), tok), pl.BlockSpec((TOPK, tl), tok), pl.BlockSpec((TOPK, tl), tok),
                   pl.BlockSpec((E, MAX_TILES), lambda s: (0, 0))],
        out_shape=[jax.ShapeDtypeStruct((TOPK, t_all), i32), jax.ShapeDtypeStruct((TOPK, t_all), f32),
                   jax.ShapeDtypeStruct((TOPK, t_all), i32), jax.ShapeDtypeStruct((E, MAX_TILES), f32)],
        compiler_params=pltpu.CompilerParams(dimension_semantics=("arbitrary",), vmem_limit_bytes=VMEM_LIMIT),
        name="route",
    )(logits_t, bias_col)


def _plan_body(eidx_ref, lrank_ref, cnt_ref, slot_ref, n8_ref, l_ref, g_ref, blk_ref):
    i = pl.program_id(0)
    tl = eidx_ref.shape[1]

    @pl.when(i == 0)
    def _():
        cnt = cnt_ref[...]
        n8 = jnp.floor((cnt + (RUN_ALIGN - 1)) * (1.0 / RUN_ALIGN)) * RUN_ALIGN
        ti = lax.broadcasted_iota(i32, (MAX_TILES, MAX_TILES), 0)
        tj = lax.broadcasted_iota(i32, (MAX_TILES, MAX_TILES), 1)
        before = _dotx01(n8, (ti < tj).astype(bf16))
        region = jnp.sum(n8, axis=1, keepdims=True)
        nblocks = jnp.floor((region + (MOE_BM - 1)) * (1.0 / MOE_BM))
        ri = lax.broadcasted_iota(i32, (E, E), 0)
        ci = lax.broadcasted_iota(i32, (E, E), 1)
        below = (ci < ri).astype(bf16)
        bstart = _dot01(below, jnp.broadcast_to(nblocks, (E, MAX_TILES)))
        n8_ref[...] = n8.astype(i32)
        l_ref[...] = _dot01(below, n8).astype(i32)
        g_ref[...] = (bstart * MOE_BM + before).astype(i32)
        lane0 = lax.broadcasted_iota(i32, (E, MAX_TILES), 1) == 0
        blk_ref[...] = jnp.where(lane0, bstart, nblocks).astype(i32)

    lane = lax.broadcasted_iota(i32, (E, MAX_TILES), 1)
    lcol = jnp.sum(jnp.where(lane == i, l_ref[...].astype(f32), 0.0), axis=1, keepdims=True)
    ei = lax.broadcasted_iota(i32, (E, tl), 0)
    for kk in range(TOPK):
        hit = ei == eidx_ref[kk:kk + 1, :]
        start = jnp.sum(jnp.where(hit, lcol, 0.0), axis=0, keepdims=True)
        slot_ref[kk:kk + 1, :] = start.astype(i32) + lrank_ref[kk:kk + 1, :]


def _plan(eidx, lrank, cnt):
    t_all = eidx.shape[1]
    tl = TILE
    tok = lambda s: (0, s)
    table = pl.BlockSpec((E, MAX_TILES), lambda s: (0, 0))
    return pl.pallas_call(
        _plan_body,
        grid=(t_all // tl,),
        in_specs=[pl.BlockSpec((TOPK, tl), tok), pl.BlockSpec((TOPK, tl), tok), table],
        out_specs=[pl.BlockSpec((TOPK, tl), tok), table, table, table, table],
        out_shape=[jax.ShapeDtypeStruct((TOPK, t_all), i32)] + [jax.ShapeDtypeStruct((E, MAX_TILES), i32)] * 4,
        compiler_params=pltpu.CompilerParams(dimension_semantics=("arbitrary",), vmem_limit_bytes=VMEM_LIMIT),
        name="plan",
    )(eidx, lrank, cnt)


def _for_each_piece(total, pieces, fn):
    for j, b in enumerate(pieces):
        @pl.when((total & b) != 0)
        def _():
            fn(pl.multiple_of(total & ~(2 * b - 1), pieces[-1]), b, j)


def _for_each_run(n8_ref, l_ref, g_ref, tile, fn):
    def per_expert(e, c):
        lo = pl.multiple_of(l_ref[e, tile], RUN_ALIGN)
        go = pl.multiple_of(g_ref[e, tile], RUN_ALIGN)
        _for_each_piece(n8_ref[e, tile], RUN_PIECES, lambda off, b, j: fn(lo + off, go + off, b, j))
        return c
    lax.fori_loop(0, E, per_expert, 0)


def _start(copy, j):
    copy.start(priority=j % 2)


def _wait_rows(make_copy, rows):
    _for_each_piece(rows, WAIT_PIECES, lambda off, b, j: make_copy(b).wait())


def _dispatch_body(tiles_a, n8_ref, l_ref, g_ref, slot_ref, ha_ref, hb_ref, xs_ref, buf_ref, zero_ref, sem, zsem):
    tile = pl.program_id(0)
    i = tile
    nsteps = pl.num_programs(0)
    cur = i % RING

    slot = slot_ref[...].astype(i16)
    h = jnp.where(tile < tiles_a, ha_ref[...], hb_ref[...])
    filled = l_ref[E - 1, tile] + n8_ref[E - 1, tile]

    def order(row0, nrows):
        srow = lax.broadcasted_iota(i32, (nrows, TILE), 0).astype(i16) + jnp.int16(row0)
        onehot = jnp.zeros((nrows, TILE), bf16)
        for kk in range(TOPK):
            onehot = jnp.where(srow == slot[kk:kk + 1, :], jnp.ones((), bf16), onehot)
        buf_ref[cur, row0:row0 + nrows, :] = _dot(onehot, h).astype(bf16)

    order(0, SLOTS_MAIN)

    @pl.when(filled > SLOTS_MAIN)
    def _():
        order(SLOTS_MAIN, SLOTS - SLOTS_MAIN)

    def run_copy(sl):
        return lambda srow_, grow, b: pltpu.make_async_copy(
            buf_ref.at[sl, pl.ds(srow_, b)], xs_ref.at[pl.ds(grow, b)], sem.at[sl])

    _for_each_run(n8_ref, l_ref, g_ref, tile, lambda s, g, b, j: _start(run_copy(cur)(s, g, b), j))

    def wait_tile(t, sl):
        _wait_rows(lambda b: run_copy(sl)(0, 0, b), l_ref[E - 1, t] + n8_ref[E - 1, t])

    @pl.when(i >= RING - 1)
    def _():
        wait_tile(tile - (RING - 1), (i + 1) % RING)

    last = nsteps - 1

    def zero_fill(fn):
        def per_expert(e, c):
            end = pl.multiple_of(g_ref[e, last] + n8_ref[e, last], RUN_ALIGN)
            _for_each_piece((-end) & (MOE_BM - 1), TAIL_PIECES, lambda off, b, j: fn(pltpu.make_async_copy(
                zero_ref.at[pl.ds(0, b)], xs_ref.at[pl.ds(end + off, b)], zsem)))
            return c
        lax.fori_loop(0, E, per_expert, 0)
        used = (g_ref[E - 1, last] + n8_ref[E - 1, last] + (MOE_BM - 1)) // MOE_BM

        def per_block(blk, c):
            fn(pltpu.make_async_copy(zero_ref, xs_ref.at[pl.ds(pl.multiple_of(blk * MOE_BM, MOE_BM), MOE_BM)], zsem))
            return c
        lax.fori_loop(used, xs_ref.shape[0] // MOE_BM, per_block, 0)

    @pl.when(i == 0)
    def _():
        zero_ref[...] = jnp.zeros_like(zero_ref)
        zero_fill(lambda cp: cp.start())

    @pl.when(i == last)
    def _():
        for back in range(RING - 2, -1, -1):
            @pl.when(i >= back)
            def _():
                wait_tile(tile - back, (i - back) % RING)
        zero_fill(lambda cp: cp.wait())


def _dispatch(tables, slot, h2_a, h2_b, nrows):
    tiles_a, tiles_b = h2_a.shape[0] // TILE, h2_b.shape[0] // TILE
    return pl.pallas_call(
        functools.partial(_dispatch_body, tiles_a),
        grid_spec=pltpu.PrefetchScalarGridSpec(
            num_scalar_prefetch=len(tables),
            grid=(tiles_a + tiles_b,),
            in_specs=[pl.BlockSpec((TOPK, TILE), lambda i, *_: (0, i)),
                      pl.BlockSpec((TILE, D), lambda i, *_: (jnp.minimum(i, tiles_a - 1), 0)),
                      pl.BlockSpec((TILE, D), lambda i, *_: (jnp.maximum(i - tiles_a, 0), 0))],
            out_specs=pl.BlockSpec(memory_space=pl.ANY),
            scratch_shapes=[pltpu.VMEM((RING, SLOTS, D), bf16), pltpu.VMEM((MOE_BM, D), bf16),
                            pltpu.SemaphoreType.DMA((RING,)), pltpu.SemaphoreType.DMA]),
        out_shape=jax.ShapeDtypeStruct((nrows, D), bf16),
        compiler_params=pltpu.CompilerParams(dimension_semantics=("arbitrary",), vmem_limit_bytes=VMEM_LIMIT),
        name="dispatch",
    )(*tables, slot, h2_a, h2_b)


def _expert_body(blk_ref, wg_ref, wu_ref, wd_ref, xs_ref, ys_ref, xbuf_ref, ybuf_ref, wgu_ref, wdb_ref, isem, osem):
    e = pl.program_id(0)
    first, count = blk_ref[e, 0], blk_ref[e, 1]
    used = blk_ref[E - 1, 0] + blk_ref[E - 1, 1]
    wgu_ref[:, :DE] = wg_ref[0].astype(bf16)
    wgu_ref[:, DE:] = wu_ref[0].astype(bf16)
    wdb_ref[...] = wd_ref[0].astype(bf16)

    def rows(g):
        return pl.ds(pl.multiple_of(g * MOE_BM, MOE_BM), MOE_BM)

    def load(g):
        sl = g % EXPERT_IN_SLOTS
        return pltpu.make_async_copy(xs_ref.at[rows(g)], xbuf_ref.at[sl], isem.at[sl])

    def store(g):
        sl = g % 2
        return pltpu.make_async_copy(ybuf_ref.at[sl], ys_ref.at[rows(g)], osem.at[sl])

    @pl.when(e == 0)
    def _():
        for g in range(EXPERT_IN_SLOTS - 1):
            @pl.when(g < used)
            def _():
                load(g).start()

    def block(j, c):
        g = first + j

        @pl.when(g + (EXPERT_IN_SLOTS - 1) < used)
        def _():
            load(g + (EXPERT_IN_SLOTS - 1)).start()
        load(g).wait()

        @pl.when(g >= 2)
        def _():
            store(g - 2).wait()
        hgu = _dot(xbuf_ref[g % EXPERT_IN_SLOTS], wgu_ref[...])
        hg = hgu[:, :DE]
        act = (hg * jax.nn.sigmoid(hg) * hgu[:, DE:]).astype(bf16)
        ybuf_ref[g % 2] = _dot(act, wdb_ref[...]).astype(bf16)
        store(g).start()
        return c
    lax.fori_loop(0, count, block, 0)

    @pl.when(e == E - 1)
    def _():
        for back in (2, 1):
            @pl.when(used >= back)
            def _():
                store(used - back).wait()


def _experts(blk, xs, we_gate, we_up, we_down):
    ex = lambda e, blk_: (e, 0, 0)
    return pl.pallas_call(
        _expert_body,
        grid_spec=pltpu.PrefetchScalarGridSpec(
            num_scalar_prefetch=1,
            grid=(E,),
            in_specs=[pl.BlockSpec((1, D, DE), ex), pl.BlockSpec((1, D, DE), ex), pl.BlockSpec((1, DE, D), ex),
                      pl.BlockSpec(memory_space=pl.ANY)],
            out_specs=pl.BlockSpec(memory_space=pl.ANY),
            scratch_shapes=[pltpu.VMEM((EXPERT_IN_SLOTS, MOE_BM, D), bf16), pltpu.VMEM((2, MOE_BM, D), bf16),
                            pltpu.VMEM((D, 2 * DE), bf16), pltpu.VMEM((DE, D), bf16),
                            pltpu.SemaphoreType.DMA((EXPERT_IN_SLOTS,)), pltpu.SemaphoreType.DMA((2,))]),
        out_shape=jax.ShapeDtypeStruct(xs.shape, xs.dtype),
        input_output_aliases={4: 0},
        compiler_params=pltpu.CompilerParams(dimension_semantics=("arbitrary",), vmem_limit_bytes=VMEM_LIMIT),
        name="experts",
    )(blk, we_gate, we_up, we_down, xs)


def _final_body(nb, lt, tile0, n8_ref, l_ref, g_ref, slot_ref, wt_ref, x2_ref, g2_ref, fw_ref, ys_ref, o_ref,
                buf_ref, routed_ref, sem):
    m = nb * lt
    n_inner = pl.num_programs(1)
    step = pl.program_id(0) * n_inner + pl.program_id(1)
    nsteps = pl.num_programs(0) * n_inner
    tile = tile0 + step
    cur = step % RING

    def run_copy(sl):
        return lambda srow, grow, b: pltpu.make_async_copy(
            ys_ref.at[pl.ds(grow, b)], buf_ref.at[sl, pl.ds(srow, b)], sem.at[sl])

    def fetch(ahead):
        sl = (step + ahead) % RING
        _for_each_run(n8_ref, l_ref, g_ref, tile + ahead, lambda s, g, b, j: _start(run_copy(sl)(s, g, b), j))

    @pl.when(step == 0)
    def _():
        buf_ref[...] = jnp.zeros_like(buf_ref)
        for ahead in range(RING - 1):
            @pl.when(ahead < nsteps)
            def _():
                fetch(ahead)

    @pl.when(step + (RING - 1) < nsteps)
    def _():
        fetch(RING - 1)

    _wait_rows(lambda b: run_copy(cur)(0, 0, b), l_ref[E - 1, tile] + n8_ref[E - 1, tile])

    slot = slot_ref[...].astype(i16)
    wt = wt_ref[...]
    wt_hi = wt.astype(bf16)
    wt_lo = (wt - wt_hi.astype(f32)).astype(bf16)
    def combine(col0, ncols):
        scol = lax.broadcasted_iota(i32, (m, ncols), 1).astype(i16) + jnp.int16(col0)
        w_hi = jnp.zeros((m, ncols), bf16)
        w_lo = jnp.zeros((m, ncols), bf16)
        for kk in range(TOPK):
            hit = scol == slot[:, kk:kk + 1]
            w_hi = jnp.where(hit, wt_hi[:, kk:kk + 1], w_hi)
            w_lo = jnp.where(hit, wt_lo[:, kk:kk + 1], w_lo)
        yb = buf_ref[cur, col0:col0 + ncols, :]
        return _dot(w_hi, yb) + _dot(w_lo, yb)

    routed_ref[...] = combine(0, SLOTS_MAIN)

    @pl.when(l_ref[E - 1, tile] + n8_ref[E - 1, tile] > SLOTS_MAIN)
    def _():
        routed_ref[...] += combine(SLOTS_MAIN, SLOTS - SLOTS_MAIN)
    routed = routed_ref[...]
    g2 = g2_ref[0]
    if nb == 1:
        g2 = g2.reshape(1, D)
    else:
        g2 = jnp.broadcast_to(g2, (nb, lt, D)).reshape(m, D)
    y = x2_ref[...] + g2 * routed
    o_ref[...] = _rms(y, fw_ref[...]).reshape(nb, lt, D)


def _final(tables, slot_t, wts_t, x2, ys, mod4, fw, bsz, seqlen, row0, nb, lt):
    m = nb * lt
    assert m == TILE and row0 % TILE == 0
    n_lt = seqlen // lt
    tile0 = row0 // TILE
    tile = lambda b, i: tile0 + b * n_lt + i
    return pl.pallas_call(
        functools.partial(_final_body, nb, lt, tile0),
        grid_spec=pltpu.PrefetchScalarGridSpec(
            num_scalar_prefetch=len(tables),
            grid=(bsz // nb, n_lt),
            in_specs=[pl.BlockSpec((m, TOPK), lambda b, i, *_: (tile(b, i), 0)),
                      pl.BlockSpec((m, TOPK), lambda b, i, *_: (tile(b, i), 0)),
                      pl.BlockSpec((m, D), lambda b, i, *_: (b * n_lt + i, 0)),
                      pl.BlockSpec((1, nb, 1, D), lambda b, i, *_: (5, b, 0, 0)),
                      pl.BlockSpec((1, D), lambda b, i, *_: (0, 0)),
                      pl.BlockSpec(memory_space=pl.ANY)],
            out_specs=pl.BlockSpec((nb, lt, D), lambda b, i, *_: (b, i, 0)),
            scratch_shapes=[pltpu.VMEM((RING, SLOTS, D), bf16), pltpu.VMEM((m, D), f32),
                            pltpu.SemaphoreType.DMA((RING,))]),
        out_shape=jax.ShapeDtypeStruct((bsz, seqlen, D), f32),
        compiler_params=pltpu.CompilerParams(dimension_semantics=("arbitrary", "arbitrary"),
                                             vmem_limit_bytes=VMEM_LIMIT),
        name="final",
    )(*tables, slot_t, wts_t, x2, mod4, fw, ys)


def kernel(x_prompt, x_sample, state_conv, state_gla, c_prompt, c_sample, w_ada, b_ada, norm1_w, w_in, conv_w,
           w_gk, b_gk, gla_norm_w, w_out_conv, w_out_gla, w_o, norm2_w, router_w, router_bias, we_gate, we_up,
           we_down, ws_gate, ws_up, ws_down, final_norm_w):
    assert w_ada.shape[0] == 1, "single-layer step"
    bp, lp, _ = x_prompt.shape
    bs, ls, _ = x_sample.shape
    tp, ts = bp * lp, bs * ls
    t_all = tp + ts

    w_in0 = w_in[0]
    n_main = 3 * D + 2 * DK + 2 * DV
    rank = w_gk.shape[1]
    weights = (
        norm1_w[0].reshape(1, D),
        w_in0[:, :n_main].astype(bf16),
        jnp.pad(w_in0[:, n_main:n_main + rank], ((0, 0), (0, RANK_PAD - rank))).astype(bf16),
        w_in0[:, n_main + rank:].astype(bf16),
        jnp.pad(w_gk[0], ((0, RANK_PAD - rank), (0, 0))).astype(bf16),
        b_gk[0].reshape(1, DK),
        conv_w[0],
        gla_norm_w[0].reshape(1, DVH),
        w_out_conv[0].astype(bf16),
        w_out_gla[0].astype(bf16),
        w_o[0].astype(bf16),
        norm2_w[0].reshape(1, D),
        router_w[0].T.astype(bf16),
        ws_gate[0].astype(bf16),
        ws_up[0].astype(bf16),
        ws_down[0].astype(bf16),
    )

    mod = _adaln(jnp.concatenate([c_prompt, c_sample], axis=0), w_ada[0], b_ada[0].reshape(1, 6 * D))
    mod4 = mod.reshape(bp + bs, 6, 1, D).transpose(1, 0, 2, 3)
    mod_p, mod_s = mod4[:, :bp], mod4[:, bp:]

    lt_p = 512
    x2_p, h2_p, lg_p, conv_p, gla_p = _mixer(x_prompt, mod_p, None, weights, 1, lt_p, GLA_CHUNK)
    nb_s = 8
    x2_s, h2_s, lg_s, conv_s, gla_s = _mixer(x_sample, mod_s, (state_conv[0], state_gla[0]), weights,
                                             nb_s, ls, math.gcd(ls, GLA_CHUNK))
    lg = jnp.concatenate([lg_p.transpose(1, 0, 2).reshape(E, tp), lg_s.transpose(1, 0, 2).reshape(E, ts)], axis=1)

    eidx, wts, lrank, cnt = _route(lg, router_bias[0].reshape(E, 1))
    n_tiles = t_all // TILE
    max_rows = t_all * TOPK + n_tiles * E * (RUN_ALIGN - 1) + E * (MOE_BM - RUN_ALIGN)
    nblk = -(-max_rows // MOE_BM)
    slot, n8, loff, goff, blk = _plan(eidx, lrank, cnt)
    tables = (n8, loff, goff)

    xs = _dispatch(tables, slot, h2_p, h2_s, nblk * MOE_BM)
    ys = _experts(blk, xs, we_gate[0], we_up[0], we_down[0])

    fw = final_norm_w.reshape(1, D)
    slot_t, wts_t = slot.T, wts.T
    y_prompt = _final(tables, slot_t, wts_t, x2_p, ys, mod_p, fw, bp, lp, 0, 1, TILE)
    y_sample = _final(tables, slot_t, wts_t, x2_s, ys, mod_s, fw, bs, ls, tp, TILE // ls, ls)
    return (y_prompt, y_sample, conv_p, gla_p, conv_s, gla_s)
```

```python
import functools
import math

import jax
import jax.numpy as jnp
from jax import lax
from jax.experimental import pallas as pl
from jax.experimental.pallas import tpu as pltpu

f32 = jnp.float32
bf16 = jnp.bfloat16
i32 = jnp.int32
i16 = jnp.int16

D = 1024
H = 4
DKH = 128
DVH = 256
DK = H * DKH
DV = H * DVH
RANK_PAD = 128
GATE_NORMALIZER = 16.0
GLA_CHUNK = 64
E = 64
TOPK = 8
NGROUPS = 8
TOPK_GROUPS = 4
DE = 256
ROUTED_SCALE = 2.5
EPS = 1e-6
NEG = float("-inf")

VMEM_LIMIT = 58 * 1024 * 1024
MOE_BM = 512
EXPERT_IN_SLOTS = 3
RING = 3
TILE = 256
MAX_TILES = 128
RUN_ALIGN = 16
SLOTS = TILE * TOPK + 1024
SLOTS_MAIN = TILE * TOPK + 512
RUN_PIECES = (256, 128, 64, 32, 16)
TAIL_PIECES = (256, 128, 64, 32, 16)
WAIT_PIECES = (2048, 1024, 512, 256, 128, 64, 32, 16)

def _dot(a, b):
    return jnp.dot(a, b, preferred_element_type=f32)


def _split3(x):
    hi = x.astype(bf16)
    r1 = x - hi.astype(f32)
    mid = r1.astype(bf16)
    lo = (r1 - mid.astype(f32)).astype(bf16)
    return hi, mid, lo


def _dot01(m01, x):
    hi, mid, lo = _split3(x)
    return _dot(m01, hi) + _dot(m01, mid) + _dot(m01, lo)


def _dotx01(x, m01):
    hi, mid, lo = _split3(x)
    return _dot(hi, m01) + _dot(mid, m01) + _dot(lo, m01)


def _rms(v, w):
    ms = jnp.mean(v * v, axis=-1, keepdims=True)
    return v * lax.rsqrt(ms + EPS) * w


def _const_spec(shape):
    n = len(shape)
    return pl.BlockSpec(shape, lambda *_: (0,) * n, pipeline_mode=pl.Buffered(1))


def _adaln_body(c_ref, w_ref, b_ref, o_ref):
    c = c_ref[...]
    a = (c * jax.nn.sigmoid(c)).astype(bf16)
    o_ref[...] = _dot(a, w_ref[...].astype(bf16)) + b_ref[...]


def _adaln(c_all, w_ada, b_ada):
    nrow = c_all.shape[0]
    tn = 1536
    return pl.pallas_call(
        _adaln_body,
        grid=(6 * D // tn,),
        in_specs=[pl.BlockSpec((nrow, D), lambda j: (0, 0)),
                  pl.BlockSpec((D, tn), lambda j: (0, j)),
                  pl.BlockSpec((1, tn), lambda j: (0, j))],
        out_specs=pl.BlockSpec((nrow, tn), lambda j: (0, j)),
        out_shape=jax.ShapeDtypeStruct((nrow, 6 * D), f32),
        compiler_params=pltpu.CompilerParams(dimension_semantics=("arbitrary",), vmem_limit_bytes=VMEM_LIMIT),
        name="adaln",
    )(c_all, w_ada, b_ada)


def _mixer_body(nb, lt, chunk, has_state, *refs):
    if has_state:
        (x_ref, mod_ref, cs_ref, gs_ref, *rest) = refs
    else:
        (x_ref, mod_ref, *rest) = refs
        cs_ref = gs_ref = None
    (n1_ref, wmain_ref, wz_ref, wu_ref, wgk_ref, bgk_ref, cw_ref, gnw_ref, woc_ref, wog_ref, wo_ref,
     n2_ref, rwt_ref, wsg_ref, wsu_ref, wsd_ref,
     x2_ref, h2_ref, lg_ref, nc_ref, ng_ref, carry_ref, st_ref, og_ref, sin_ref) = rest
    m = nb * lt
    i = pl.program_id(1)
    last = pl.num_programs(1) - 1

    def rows(v):
        w = v.shape[-1]
        if nb == 1:
            return v.reshape(1, w)
        return jnp.broadcast_to(v, (nb, lt, w)).reshape(m, w)

    @pl.when(i == 0)
    def _():
        if has_state:
            cs = cs_ref[...]
            carry_ref[0] = cs[:, 0:1, :]
            carry_ref[1] = cs[:, 1:2, :]
            for hh in range(H):
                st_ref[hh] = jnp.swapaxes(gs_ref[:, hh], 1, 2)
        else:
            carry_ref[...] = jnp.zeros_like(carry_ref)
            st_ref[...] = jnp.zeros_like(st_ref)

    x = x_ref[...].reshape(m, D)
    sh1, sc1, g1, sh2, sc2, g2 = [rows(mod_ref[j]) for j in range(6)]
    hb = (_rms(x, n1_ref[...]) * (1.0 + sc1) + sh1).astype(bf16)

    def proj(lo, hi):
        return _dot(hb, wmain_ref[:, lo:hi])

    pre = proj(D, 2 * D) * proj(2 * D, 3 * D)
    c0 = rows(carry_ref[0])
    c1 = rows(carry_ref[1])
    l_idx = lax.broadcasted_iota(i32, (m, D), 0) & (lt - 1)
    r1 = pltpu.roll(pre, 1, 0)
    r2 = pltpu.roll(pre, 2, 0)
    prev1 = jnp.where(l_idx == 0, c1, r1)
    prev2 = jnp.where(l_idx == 0, c0, jnp.where(l_idx == 1, c1, r2))
    cw = cw_ref[...]
    conv = cw[0:1] * prev2 + cw[1:2] * prev1 + cw[2:3] * pre
    y_a = _dot((proj(0, D) * conv).astype(bf16), woc_ref[...])
    pre3 = pre.reshape(nb, lt, D)
    tail = pre3[:, lt - 2:lt, :]
    carry_ref[0] = tail[:, 0:1, :]
    carry_ref[1] = tail[:, 1:2, :]
    nc_ref[0] = tail

    q = proj(3 * D, 3 * D + DK)
    k = proj(3 * D + DK, 3 * D + 2 * DK)
    v = proj(4 * D, 5 * D)
    g = proj(5 * D, 6 * D)
    z = _dot(hb, wz_ref[...]).astype(bf16)
    pa = _dot(z, wgk_ref[...]) + bgk_ref[...]
    la = (jnp.minimum(pa, 0.0) - jnp.log1p(jnp.exp(-jnp.abs(pa)))) / GATE_NORMALIZER
    pos = lax.broadcasted_iota(i32, (m, DK), 0) & (chunk - 1)
    bcum = la
    shift = 1
    while shift < chunk:
        bcum = bcum + jnp.where(pos >= shift, pltpu.roll(bcum, shift, 0), 0.0)
        shift *= 2
    ngrp = m // chunk
    bend = bcum.reshape(ngrp, chunk, DK)[:, chunk - 1:chunk, :]
    bend_rows = jnp.broadcast_to(bend, (ngrp, chunk, DK)).reshape(m, DK)
    nch = lt // chunk
    assert nb == 1 or nch == 1
    hs = H * nb
    nbat = nch * hs

    def per_head(xm, w):
        x3 = xm.reshape(nb * nch, chunk, H * w)
        parts = [x3[:, :, hh * w:(hh + 1) * w] for hh in range(H)]
        return jnp.stack(parts, axis=1 if nb == 1 else 0).reshape(nbat, chunk, w)

    qd = per_head((q * (DKH ** -0.5) * jnp.exp(bcum)).astype(bf16), DKH)
    kd = per_head((k * jnp.exp(-bcum)).astype(bf16), DKH)
    ke = per_head((k * jnp.exp(bend_rows - bcum)).astype(bf16), DKH)
    vv = per_head(v.astype(bf16), DVH)
    a_end = per_head(jnp.exp(bend_rows), DKH)[:, 0:1, :]
    causal = lax.broadcasted_iota(i32, (chunk, chunk), 1) <= lax.broadcasted_iota(i32, (chunk, chunk), 0)
    sc = jnp.einsum('bcd,bsd->bcs', qd, kd, preferred_element_type=f32)
    sc = jnp.where(causal, sc, 0.0).astype(bf16)
    o = jnp.einsum('bcs,bse->bce', sc, vv, preferred_element_type=f32)
    upd = jnp.einsum('bse,bsd->bed', vv, ke, preferred_element_type=f32)
    st = st_ref[...].reshape(hs, DVH, DKH)
    for j in range(nch):
        sin_ref[j * hs:(j + 1) * hs] = st.astype(bf16)
        st = st * a_end[j * hs:(j + 1) * hs] + upd[j * hs:(j + 1) * hs]
    st_ref[...] = st.reshape(H, nb, DVH, DKH)
    o = o + jnp.einsum('bcd,bed->bce', qd, sin_ref[...], preferred_element_type=f32)
    on = _rms(o, gnw_ref[...])
    g3 = g.reshape(nb, lt, DV)
    for j in range(nch):
        js = slice(j * chunk, (j + 1) * chunk)
        for hh in range(H):
            vs = slice(hh * DVH, (hh + 1) * DVH)
            gh = g3[:, js, vs]
            b0 = (j * H + hh) * nb
            og_ref[:, js, vs] = (on[b0:b0 + nb] * (gh * jax.nn.sigmoid(gh))).astype(bf16)

    @pl.when(i == last)
    def _():
        for hh in range(H):
            ng_ref[0, :, hh] = jnp.swapaxes(st_ref[hh], 1, 2)

    y_b = _dot(og_ref[...].reshape(m, DV), wog_ref[...])
    u_a = _dot(hb, wu_ref[:, 0:D])
    u_b = _dot(hb, wu_ref[:, D:2 * D])
    merged = (jax.nn.sigmoid(u_a) * y_a + jax.nn.sigmoid(u_b) * y_b).astype(bf16)
    x1 = x + g1 * _dot(merged, wo_ref[...])

    h2 = (_rms(x1, n2_ref[...]) * (1.0 + sc2) + sh2).astype(bf16)
    sg = _dot(h2, wsg_ref[...])
    act = (sg * jax.nn.sigmoid(sg) * _dot(h2, wsu_ref[...])).astype(bf16)
    x2_ref[...] = x1 + g2 * _dot(act, wsd_ref[...])
    h2_ref[...] = h2
    lg_ref[0] = lax.dot_general(rwt_ref[...], h2, (((1,), (1,)), ((), ())), preferred_element_type=f32)


def _mixer(x, mod4, states, weights, nb, lt, chunk):
    bsz, seqlen, _ = x.shape
    has_state = states is not None
    m = nb * lt
    n_lt = seqlen // lt
    nsteps = (bsz // nb) * n_lt
    t_all = bsz * seqlen

    def tok_blk(b, i):
        return b * n_lt + i

    in_specs = [pl.BlockSpec((nb, lt, D), lambda b, i: (b, i, 0)),
                pl.BlockSpec((6, nb, 1, D), lambda b, i: (0, b, 0, 0))]
    args = [x, mod4]
    if has_state:
        in_specs += [pl.BlockSpec((nb, 2, D), lambda b, i: (b, 0, 0)),
                     pl.BlockSpec((nb, H, DKH, DVH), lambda b, i: (b, 0, 0, 0))]
        args += list(states)
    in_specs += [_const_spec(w.shape) for w in weights]
    args += list(weights)
    out_specs = [pl.BlockSpec((m, D), lambda b, i: (tok_blk(b, i), 0)),
                 pl.BlockSpec((m, D), lambda b, i: (tok_blk(b, i), 0)),
                 pl.BlockSpec((1, E, m), lambda b, i: (b * n_lt + i, 0, 0)),
                 pl.BlockSpec((1, nb, 2, D), lambda b, i: (0, b, 0, 0)),
                 pl.BlockSpec((1, nb, H, DKH, DVH), lambda b, i: (0, b, 0, 0, 0))]
    out_shape = [jax.ShapeDtypeStruct((t_all, D), f32),
                 jax.ShapeDtypeStruct((t_all, D), bf16),
                 jax.ShapeDtypeStruct((nsteps, E, m), f32),
                 jax.ShapeDtypeStruct((1, bsz, 2, D), f32),
                 jax.ShapeDtypeStruct((1, bsz, H, DKH, DVH), f32)]
    return pl.pallas_call(
        functools.partial(_mixer_body, nb, lt, chunk, has_state),
        grid=(bsz // nb, n_lt),
        in_specs=in_specs,
        out_specs=out_specs,
        out_shape=out_shape,
        scratch_shapes=[pltpu.VMEM((2, nb, 1, D), f32),
                        pltpu.VMEM((H, nb, DVH, DKH), f32),
                        pltpu.VMEM((nb, lt, DV), bf16),
                        pltpu.VMEM(((lt // chunk) * H * nb, DVH, DKH), bf16)],
        compiler_params=pltpu.CompilerParams(dimension_semantics=("arbitrary", "arbitrary"),
                                             vmem_limit_bytes=VMEM_LIMIT),
        name="mixer_state" if has_state else "mixer_prompt",
    )(*args)


def _route_body(lg_ref, bias_ref, eidx_ref, w_ref, lrank_ref, cnt_ref):
    step = pl.program_id(0)
    tl = lg_ref.shape[1]

    scores = jax.nn.sigmoid(lg_ref[...])
    choice = scores + bias_ref[...]
    gsz = E // NGROUPS
    grp = choice.reshape(NGROUPS, gsz, tl)
    mi = lax.broadcasted_iota(i32, (NGROUPS, gsz, tl), 1)
    m1 = jnp.max(grp, axis=1, keepdims=True)
    first = jnp.min(jnp.where(grp == m1, mi, gsz), axis=1, keepdims=True)
    m2 = jnp.max(jnp.where(mi == first, NEG, grp), axis=1, keepdims=True)
    gscore = (m1 + m2).reshape(NGROUPS, tl)
    gi = lax.broadcasted_iota(i32, (NGROUPS, tl), 0)
    gsel = jnp.zeros((NGROUPS, tl), f32)
    work = gscore
    for _ in range(TOPK_GROUPS):
        mx = jnp.max(work, axis=0, keepdims=True)
        f = jnp.min(jnp.where(work == mx, gi, NGROUPS), axis=0, keepdims=True)
        hit = gi == f
        gsel = jnp.where(hit, 1.0, gsel)
        work = jnp.where(hit, NEG, work)
    emask = jnp.broadcast_to(gsel.reshape(NGROUPS, 1, tl), (NGROUPS, gsz, tl)).reshape(E, tl)
    masked = jnp.where(emask > 0.0, choice, NEG)
    ei = lax.broadcasted_iota(i32, (E, tl), 0)
    sel = jnp.zeros((E, tl), f32)
    hits, ws = [], []
    for kk in range(TOPK):
        mx = jnp.max(masked, axis=0, keepdims=True)
        f = jnp.min(jnp.where(masked == mx, ei, E), axis=0, keepdims=True)
        hit = ei == f
        ws.append(jnp.sum(jnp.where(hit, scores, 0.0), axis=0, keepdims=True))
        masked = jnp.where(hit, NEG, masked)
        sel = jnp.where(hit, 1.0, sel)
        hits.append(hit)
        eidx_ref[kk:kk + 1, :] = f
    wsum = ws[0]
    for t in ws[1:]:
        wsum = wsum + t
    for kk in range(TOPK):
        w_ref[kk:kk + 1, :] = ws[kk] / wsum * ROUTED_SCALE

    ui = lax.broadcasted_iota(i32, (tl, tl), 0)
    uj = lax.broadcasted_iota(i32, (tl, tl), 1)
    upper = (ui <= uj).astype(bf16)
    lrank = _dot(sel.astype(bf16), upper) - sel
    for kk in range(TOPK):
        lrank_ref[kk:kk + 1, :] = jnp.sum(jnp.where(hits[kk], lrank, 0.0), axis=0, keepdims=True).astype(i32)

    @pl.when(step == 0)
    def _():
        cnt_ref[...] = jnp.zeros_like(cnt_ref)
    lane = lax.broadcasted_iota(i32, cnt_ref.shape, 1)
    cnt_ref[...] = jnp.where(lane == step, jnp.sum(sel, axis=1, keepdims=True), cnt_ref[...])


def _route(logits_t, bias_col):
    t_all = logits_t.shape[1]
    tl = TILE
    assert t_all // tl <= MAX_TILES
    tok = lambda s: (0, s)
    return pl.pallas_call(
        _route_body,
        grid=(t_all // tl,),
        in_specs=[pl.BlockSpec((E, tl), tok), pl.BlockSpec((E, 1), lambda s: (0, 0))],
        out_specs=[pl.BlockSpec((TOPK, tl), tok), pl.BlockSpec((TOPK, tl), tok), pl.BlockSpec((TOPK, tl), tok),
                   pl.BlockSpec((E, MAX_TILES), lambda s: (0, 0))],
        out_shape=[jax.ShapeDtypeStruct((TOPK, t_all), i32), jax.ShapeDtypeStruct((TOPK, t_all), f32),
                   jax.ShapeDtypeStruct((TOPK, t_all), i32), jax.ShapeDtypeStruct((E, MAX_TILES), f32)],
        compiler_params=pltpu.CompilerParams(dimension_semantics=("arbitrary",), vmem_limit_bytes=VMEM_LIMIT),
        name="route",
    )(logits_t, bias_col)


def _plan_body(eidx_ref, lrank_ref, cnt_ref, slot_ref, n8_ref, l_ref, g_ref, blk_ref):
    i = pl.program_id(0)
    tl = eidx_ref.shape[1]

    @pl.when(i == 0)
    def _():
        cnt = cnt_ref[...]
        n8 = jnp.floor((cnt + (RUN_ALIGN - 1)) * (1.0 / RUN_ALIGN)) * RUN_ALIGN
        ti = lax.broadcasted_iota(i32, (MAX_TILES, MAX_TILES), 0)
        tj = lax.broadcasted_iota(i32, (MAX_TILES, MAX_TILES), 1)
        before = _dotx01(n8, (ti < tj).astype(bf16))
        region = jnp.sum(n8, axis=1, keepdims=True)
        nblocks = jnp.floor((region + (MOE_BM - 1)) * (1.0 / MOE_BM))
        ri = lax.broadcasted_iota(i32, (E, E), 0)
        ci = lax.broadcasted_iota(i32, (E, E), 1)
        below = (ci < ri).astype(bf16)
        bstart = _dot01(below, jnp.broadcast_to(nblocks, (E, MAX_TILES)))
        n8_ref[...] = n8.astype(i32)
        l_ref[...] = _dot01(below, n8).astype(i32)
        g_ref[...] = (bstart * MOE_BM + before).astype(i32)
        lane0 = lax.broadcasted_iota(i32, (E, MAX_TILES), 1) == 0
        blk_ref[...] = jnp.where(lane0, bstart, nblocks).astype(i32)

    lane = lax.broadcasted_iota(i32, (E, MAX_TILES), 1)
    lcol = jnp.sum(jnp.where(lane == i, l_ref[...].astype(f32), 0.0), axis=1, keepdims=True)
    ei = lax.broadcasted_iota(i32, (E, tl), 0)
    for kk in range(TOPK):
        hit = ei == eidx_ref[kk:kk + 1, :]
        start = jnp.sum(jnp.where(hit, lcol, 0.0), axis=0, keepdims=True)
        slot_ref[kk:kk + 1, :] = start.astype(i32) + lrank_ref[kk:kk + 1, :]


def _plan(eidx, lrank, cnt):
    t_all = eidx.shape[1]
    tl = TILE
    tok = lambda s: (0, s)
    table = pl.BlockSpec((E, MAX_TILES), lambda s: (0, 0))
    return pl.pallas_call(
        _plan_body,
        grid=(t_all // tl,),
        in_specs=[pl.BlockSpec((TOPK, tl), tok), pl.BlockSpec((TOPK, tl), tok), table],
        out_specs=[pl.BlockSpec((TOPK, tl), tok), table, table, table, table],
        out_shape=[jax.ShapeDtypeStruct((TOPK, t_all), i32)] + [jax.ShapeDtypeStruct((E, MAX_TILES), i32)] * 4,
        compiler_params=pltpu.CompilerParams(dimension_semantics=("arbitrary",), vmem_limit_bytes=VMEM_LIMIT),
        name="plan",
    )(eidx, lrank, cnt)


def _for_each_piece(total, pieces, fn):
    for j, b in enumerate(pieces):
        @pl.when((total & b) != 0)
        def _():
            fn(pl.multiple_of(total & ~(2 * b - 1), pieces[-1]), b, j)


def _for_each_run(n8_ref, l_ref, g_ref, tile, fn):
    def per_expert(e, c):
        lo = pl.multiple_of(l_ref[e, tile], RUN_ALIGN)
        go = pl.multiple_of(g_ref[e, tile], RUN_ALIGN)
        _for_each_piece(n8_ref[e, tile], RUN_PIECES, lambda off, b, j: fn(lo + off, go + off, b, j))
        return c
    lax.fori_loop(0, E, per_expert, 0)


def _start(copy, j):
    copy.start(priority=j % 2)


def _wait_rows(make_copy, rows):
    _for_each_piece(rows, WAIT_PIECES, lambda off, b, j: make_copy(b).wait())


def _dispatch_body(tiles_a, n8_ref, l_ref, g_ref, slot_ref, ha_ref, hb_ref, xs_ref, buf_ref, zero_ref, sem, zsem):
    tile = pl.program_id(0)
    i = tile
    nsteps = pl.num_programs(0)
    cur = i % RING

    slot = slot_ref[...].astype(i16)
    h = jnp.where(tile < tiles_a, ha_ref[...], hb_ref[...])
    filled = l_ref[E - 1, tile] + n8_ref[E - 1, tile]

    def order(row0, nrows):
        srow = lax.broadcasted_iota(i32, (nrows, TILE), 0).astype(i16) + jnp.int16(row0)
        onehot = jnp.zeros((nrows, TILE), bf16)
        for kk in range(TOPK):
            onehot = jnp.where(srow == slot[kk:kk + 1, :], jnp.ones((), bf16), onehot)
        buf_ref[cur, row0:row0 + nrows, :] = _dot(onehot, h).astype(bf16)

    order(0, SLOTS_MAIN)

    @pl.when(filled > SLOTS_MAIN)
    def _():
        order(SLOTS_MAIN, SLOTS - SLOTS_MAIN)

    def run_copy(sl):
        return lambda srow_, grow, b: pltpu.make_async_copy(
            buf_ref.at[sl, pl.ds(srow_, b)], xs_ref.at[pl.ds(grow, b)], sem.at[sl])

    _for_each_run(n8_ref, l_ref, g_ref, tile, lambda s, g, b, j: _start(run_copy(cur)(s, g, b), j))

    def wait_tile(t, sl):
        _wait_rows(lambda b: run_copy(sl)(0, 0, b), l_ref[E - 1, t] + n8_ref[E - 1, t])

    @pl.when(i >= RING - 1)
    def _():
        wait_tile(tile - (RING - 1), (i + 1) % RING)

    last = nsteps - 1

    def zero_fill(fn):
        def per_expert(e, c):
            end = pl.multiple_of(g_ref[e, last] + n8_ref[e, last], RUN_ALIGN)
            _for_each_piece((-end) & (MOE_BM - 1), TAIL_PIECES, lambda off, b, j: fn(pltpu.make_async_copy(
                zero_ref.at[pl.ds(0, b)], xs_ref.at[pl.ds(end + off, b)], zsem)))
            return c
        lax.fori_loop(0, E, per_expert, 0)
        used = (g_ref[E - 1, last] + n8_ref[E - 1, last] + (MOE_BM - 1)) // MOE_BM

        def per_block(blk, c):
            fn(pltpu.make_async_copy(zero_ref, xs_ref.at[pl.ds(pl.multiple_of(blk * MOE_BM, MOE_BM), MOE_BM)], zsem))
            return c
        lax.fori_loop(used, xs_ref.shape[0] // MOE_BM, per_block, 0)

    @pl.when(i == 0)
    def _():
        zero_ref[...] = jnp.zeros_like(zero_ref)
        zero_fill(lambda cp: cp.start())

    @pl.when(i == last)
    def _():
        for back in range(RING - 2, -1, -1):
            @pl.when(i >= back)
            def _():
                wait_tile(tile - back, (i - back) % RING)
        zero_fill(lambda cp: cp.wait())


def _dispatch(tables, slot, h2_a, h2_b, nrows):
    tiles_a, tiles_b = h2_a.shape[0] // TILE, h2_b.shape[0] // TILE
    return pl.pallas_call(
        functools.partial(_dispatch_body, tiles_a),
        grid_spec=pltpu.PrefetchScalarGridSpec(
            num_scalar_prefetch=len(tables),
            grid=(tiles_a + tiles_b,),
            in_specs=[pl.BlockSpec((TOPK, TILE), lambda i, *_: (0, i)),
                      pl.BlockSpec((TILE, D), lambda i, *_: (jnp.minimum(i, tiles_a - 1), 0)),
                      pl.BlockSpec((TILE, D), lambda i, *_: (jnp.maximum(i - tiles_a, 0), 0))],
            out_specs=pl.BlockSpec(memory_space=pl.ANY),
            scratch_shapes=[pltpu.VMEM((RING, SLOTS, D), bf16), pltpu.VMEM((MOE_BM, D), bf16),
                            pltpu.SemaphoreType.DMA((RING,)), pltpu.SemaphoreType.DMA]),
        out_shape=jax.ShapeDtypeStruct((nrows, D), bf16),
        compiler_params=pltpu.CompilerParams(dimension_semantics=("arbitrary",), vmem_limit_bytes=VMEM_LIMIT),
        name="dispatch",
    )(*tables, slot, h2_a, h2_b)


def _expert_body(blk_ref, wg_ref, wu_ref, wd_ref, xs_ref, ys_ref, xbuf_ref, ybuf_ref, wgu_ref, wdb_ref, isem, osem):
    e = pl.program_id(0)
    first, count = blk_ref[e, 0], blk_ref[e, 1]
    used = blk_ref[E - 1, 0] + blk_ref[E - 1, 1]
    wgu_ref[:, :DE] = wg_ref[0].astype(bf16)
    wgu_ref[:, DE:] = wu_ref[0].astype(bf16)
    wdb_ref[...] = wd_ref[0].astype(bf16)

    def rows(g):
        return pl.ds(pl.multiple_of(g * MOE_BM, MOE_BM), MOE_BM)

    def load(g):
        sl = g % EXPERT_IN_SLOTS
        return pltpu.make_async_copy(xs_ref.at[rows(g)], xbuf_ref.at[sl], isem.at[sl])

    def store(g):
        sl = g % 2
        return pltpu.make_async_copy(ybuf_ref.at[sl], ys_ref.at[rows(g)], osem.at[sl])

    @pl.when(e == 0)
    def _():
        for g in range(EXPERT_IN_SLOTS - 1):
            @pl.when(g < used)
            def _():
                load(g).start()

    def block(j, c):
        g = first + j

        @pl.when(g + (EXPERT_IN_SLOTS - 1) < used)
        def _():
            load(g + (EXPERT_IN_SLOTS - 1)).start()
        load(g).wait()

        @pl.when(g >= 2)
        def _():
            store(g - 2).wait()
        hgu = _dot(xbuf_ref[g % EXPERT_IN_SLOTS], wgu_ref[...])
        hg = hgu[:, :DE]
        act = (hg * jax.nn.sigmoid(hg) * hgu[:, DE:]).astype(bf16)
        ybuf_ref[g % 2] = _dot(act, wdb_ref[...]).astype(bf16)
        store(g).start()
        return c
    lax.fori_loop(0, count, block, 0)

    @pl.when(e == E - 1)
    def _():
        for back in (2, 1):
            @pl.when(used >= back)
            def _():
                store(used - back).wait()


def _experts(blk, xs, we_gate, we_up, we_down):
    ex = lambda e, blk_: (e, 0, 0)
    return pl.pallas_call(
        _expert_body,
        grid_spec=pltpu.PrefetchScalarGridSpec(
            num_scalar_prefetch=1,
            grid=(E,),
            in_specs=[pl.BlockSpec((1, D, DE), ex), pl.BlockSpec((1, D, DE), ex), pl.BlockSpec((1, DE, D), ex),
                      pl.BlockSpec(memory_space=pl.ANY)],
            out_specs=pl.BlockSpec(memory_space=pl.ANY),
            scratch_shapes=[pltpu.VMEM((EXPERT_IN_SLOTS, MOE_BM, D), bf16), pltpu.VMEM((2, MOE_BM, D), bf16),
                            pltpu.VMEM((D, 2 * DE), bf16), pltpu.VMEM((DE, D), bf16),
                            pltpu.SemaphoreType.DMA((EXPERT_IN_SLOTS,)), pltpu.SemaphoreType.DMA((2,))]),
        out_shape=jax.ShapeDtypeStruct(xs.shape, xs.dtype),
        input_output_aliases={4: 0},
        compiler_params=pltpu.CompilerParams(dimension_semantics=("arbitrary",), vmem_limit_bytes=VMEM_LIMIT),
        name="experts",
    )(blk, we_gate, we_up, we_down, xs)


def _final_body(nb, lt, tile0, n8_ref, l_ref, g_ref, slot_ref, wt_ref, x2_ref, g2_ref, fw_ref, ys_ref, o_ref,
                buf_ref, routed_ref, sem):
    m = nb * lt
    n_inner = pl.num_programs(1)
    step = pl.program_id(0) * n_inner + pl.program_id(1)
    nsteps = pl.num_programs(0) * n_inner
    tile = tile0 + step
    cur = step % RING

    def run_copy(sl):
        return lambda srow, grow, b: pltpu.make_async_copy(
            ys_ref.at[pl.ds(grow, b)], buf_ref.at[sl, pl.ds(srow, b)], sem.at[sl])

    def fetch(ahead):
        sl = (step + ahead) % RING
        _for_each_run(n8_ref, l_ref, g_ref, tile + ahead, lambda s, g, b, j: _start(run_copy(sl)(s, g, b), j))

    @pl.when(step == 0)
    def _():
        buf_ref[...] = jnp.zeros_like(buf_ref)
        for ahead in range(RING - 1):
            @pl.when(ahead < nsteps)
            def _():
                fetch(ahead)

    @pl.when(step + (RING - 1) < nsteps)
    def _():
        fetch(RING - 1)

    _wait_rows(lambda b: run_copy(cur)(0, 0, b), l_ref[E - 1, tile] + n8_ref[E - 1, tile])

    slot = slot_ref[...].astype(i16)
    wt = wt_ref[...]
    wt_hi = wt.astype(bf16)
    wt_lo = (wt - wt_hi.astype(f32)).astype(bf16)
    def combine(col0, ncols):
        scol = lax.broadcasted_iota(i32, (m, ncols), 1).astype(i16) + jnp.int16(col0)
        w_hi = jnp.zeros((m, ncols), bf16)
        w_lo = jnp.zeros((m, ncols), bf16)
        for kk in range(TOPK):
            hit = scol == slot[:, kk:kk + 1]
            w_hi = jnp.where(hit, wt_hi[:, kk:kk + 1], w_hi)
            w_lo = jnp.where(hit, wt_lo[:, kk:kk + 1], w_lo)
        yb = buf_ref[cur, col0:col0 + ncols, :]
        return _dot(w_hi, yb) + _dot(w_lo, yb)

    routed_ref[...] = combine(0, SLOTS_MAIN)

    @pl.when(l_ref[E - 1, tile] + n8_ref[E - 1, tile] > SLOTS_MAIN)
    def _():
        routed_ref[...] += combine(SLOTS_MAIN, SLOTS - SLOTS_MAIN)
    routed = routed_ref[...]
    g2 = g2_ref[0]
    if nb == 1:
        g2 = g2.reshape(1, D)
    else:
        g2 = jnp.broadcast_to(g2, (nb, lt, D)).reshape(m, D)
    y = x2_ref[...] + g2 * routed
    o_ref[...] = _rms(y, fw_ref[...]).reshape(nb, lt, D)


def _final(tables, slot_t, wts_t, x2, ys, mod4, fw, bsz, seqlen, row0, nb, lt):
    m = nb * lt
    assert m == TILE and row0 % TILE == 0
    n_lt = seqlen // lt
    tile0 = row0 // TILE
    tile = lambda b, i: tile0 + b * n_lt + i
    return pl.pallas_call(
        functools.partial(_final_body, nb, lt, tile0),
        grid_spec=pltpu.PrefetchScalarGridSpec(
            num_scalar_prefetch=len(tables),
            grid=(bsz // nb, n_lt),
            in_specs=[pl.BlockSpec((m, TOPK), lambda b, i, *_: (tile(b, i), 0)),
                      pl.BlockSpec((m, TOPK), lambda b, i, *_: (tile(b, i), 0)),
                      pl.BlockSpec((m, D), lambda b, i, *_: (b * n_lt + i, 0)),
                      pl.BlockSpec((1, nb, 1, D), lambda b, i, *_: (5, b, 0, 0)),
                      pl.BlockSpec((1, D), lambda b, i, *_: (0, 0)),
                      pl.BlockSpec(memory_space=pl.ANY)],
            out_specs=pl.BlockSpec((nb, lt, D), lambda b, i, *_: (b, i, 0)),
            scratch_shapes=[pltpu.VMEM((RING, SLOTS, D), bf16), pltpu.VMEM((m, D), f32),
                            pltpu.SemaphoreType.DMA((RING,))]),
        out_shape=jax.ShapeDtypeStruct((bsz, seqlen, D), f32),
        compiler_params=pltpu.CompilerParams(dimension_semantics=("arbitrary", "arbitrary"),
                                             vmem_limit_bytes=VMEM_LIMIT),
        name="final",
    )(*tables, slot_t, wts_t, x2, mod4, fw, ys)


def kernel(x_prompt, x_sample, state_conv, state_gla, c_prompt, c_sample, w_ada, b_ada, norm1_w, w_in, conv_w,
           w_gk, b_gk, gla_norm_w, w_out_conv, w_out_gla, w_o, norm2_w, router_w, router_bias, we_gate, we_up,
           we_down, ws_gate, ws_up, ws_down, final_norm_w):
    assert w_ada.shape[0] == 1, "single-layer step"
    bp, lp, _ = x_prompt.shape
    bs, ls, _ = x_sample.shape
    tp, ts = bp * lp, bs * ls
    t_all = tp + ts

    w_in0 = w_in[0]
    n_main = 3 * D + 2 * DK + 2 * DV
    rank = w_gk.shape[1]
    weights = (
        norm1_w[0].reshape(1, D),
        w_in0[:, :n_main].astype(bf16),
        jnp.pad(w_in0[:, n_main:n_main + rank], ((0, 0), (0, RANK_PAD - rank))).astype(bf16),
        w_in0[:, n_main + rank:].astype(bf16),
        jnp.pad(w_gk[0], ((0, RANK_PAD - rank), (0, 0))).astype(bf16),
        b_gk[0].reshape(1, DK),
        conv_w[0],
        gla_norm_w[0].reshape(1, DVH),
        w_out_conv[0].astype(bf16),
        w_out_gla[0].astype(bf16),
        w_o[0].astype(bf16),
        norm2_w[0].reshape(1, D),
        router_w[0].T.astype(bf16),
        ws_gate[0].astype(bf16),
        ws_up[0].astype(bf16),
        ws_down[0].astype(bf16),
    )

    mod = _adaln(jnp.concatenate([c_prompt, c_sample], axis=0), w_ada[0], b_ada[0].reshape(1, 6 * D))
    mod4 = mod.reshape(bp + bs, 6, 1, D).transpose(1, 0, 2, 3)
    mod_p, mod_s = mod4[:, :bp], mod4[:, bp:]

    lt_p = 512
    x2_p, h2_p, lg_p, conv_p, gla_p = _mixer(x_prompt, mod_p, None, weights, 1, lt_p, GLA_CHUNK)
    nb_s = 8
    x2_s, h2_s, lg_s, conv_s, gla_s = _mixer(x_sample, mod_s, (state_conv[0], state_gla[0]), weights,
                                             nb_s, ls, math.gcd(ls, GLA_CHUNK))
    lg = jnp.concatenate([lg_p.transpose(1, 0, 2).reshape(E, tp), lg_s.transpose(1, 0, 2).reshape(E, ts)], axis=1)

    eidx, wts, lrank, cnt = _route(lg, router_bias[0].reshape(E, 1))
    n_tiles = t_all // TILE
    max_rows = t_all * TOPK + n_tiles * E * (RUN_ALIGN - 1) + E * (MOE_BM - RUN_ALIGN)
    nblk = -(-max_rows // MOE_BM)
    slot, n8, loff, goff, blk = _plan(eidx, lrank, cnt)
    tables = (n8, loff, goff)

    xs = _dispatch(tables, slot, h2_p, h2_s, nblk * MOE_BM)
    ys = _experts(blk, xs, we_gate[0], we_up[0], we_down[0])

    fw = final_norm_w.reshape(1, D)
    slot_t, wts_t = slot.T, wts.T
    y_prompt = _final(tables, slot_t, wts_t, x2_p, ys, mod_p, fw, bp, lp, 0, 1, TILE)
    y_sample = _final(tables, slot_t, wts_t, x2_s, ys, mod_s, fw, bs, ls, tp, TILE // ls, ls)
    return (y_prompt, y_sample, conv_p, gla_p, conv_s, gla_s)
```

```python
import functools
import math

import jax
import jax.numpy as jnp
from jax import lax
from jax.experimental import pallas as pl
from jax.experimental.pallas import tpu as pltpu

f32 = jnp.float32
bf16 = jnp.bfloat16
i32 = jnp.int32
i16 = jnp.int16

D = 1024
H = 4
DKH = 128
DVH = 256
DK = H * DKH
DV = H * DVH
RANK_PAD = 128
GATE_NORMALIZER = 16.0
GLA_CHUNK = 64
E = 64
TOPK = 8
NGROUPS = 8
TOPK_GROUPS = 4
DE = 256
ROUTED_SCALE = 2.5
EPS = 1e-6
NEG = float("-inf")

VMEM_LIMIT = 58 * 1024 * 1024
MOE_BM = 512
EXPERT_IN_SLOTS = 3
RING = 3
TILE = 256
MAX_TILES = 128
RUN_ALIGN = 16
SLOTS = TILE * TOPK + 1024
SLOTS_MAIN = TILE * TOPK + 512
RUN_PIECES = (256, 128, 64, 32, 16)
TAIL_PIECES = (256, 128, 64, 32, 16)
WAIT_PIECES = (2048, 1024, 512, 256, 128, 64, 32, 16)

def _dot(a, b):
    return jnp.dot(a, b, preferred_element_type=f32)


def _split3(x):
    hi = x.astype(bf16)
    r1 = x - hi.astype(f32)
    mid = r1.astype(bf16)
    lo = (r1 - mid.astype(f32)).astype(bf16)
    return hi, mid, lo


def _dot01(m01, x):
    hi, mid, lo = _split3(x)
    return _dot(m01, hi) + _dot(m01, mid) + _dot(m01, lo)


def _dotx01(x, m01):
    hi, mid, lo = _split3(x)
    return _dot(hi, m01) + _dot(mid, m01) + _dot(lo, m01)


def _rms(v, w):
    ms = jnp.mean(v * v, axis=-1, keepdims=True)
    return v * lax.rsqrt(ms + EPS) * w


def _const_spec(shape):
    n = len(shape)
    return pl.BlockSpec(shape, lambda *_: (0,) * n, pipeline_mode=pl.Buffered(1))


def _adaln_body(c_ref, w_ref, b_ref, o_ref):
    c = c_ref[...]
    a = (c * jax.nn.sigmoid(c)).astype(bf16)
    o_ref[...] = _dot(a, w_ref[...].astype(bf16)) + b_ref[...]


def _adaln(c_all, w_ada, b_ada):
    nrow = c_all.shape[0]
    tn = 1536
    return pl.pallas_call(
        _adaln_body,
        grid=(6 * D // tn,),
        in_specs=[pl.BlockSpec((nrow, D), lambda j: (0, 0)),
                  pl.BlockSpec((D, tn), lambda j: (0, j)),
                  pl.BlockSpec((1, tn), lambda j: (0, j))],
        out_specs=pl.BlockSpec((nrow, tn), lambda j: (0, j)),
        out_shape=jax.ShapeDtypeStruct((nrow, 6 * D), f32),
        compiler_params=pltpu.CompilerParams(dimension_semantics=("arbitrary",), vmem_limit_bytes=VMEM_LIMIT),
        name="adaln",
    )(c_all, w_ada, b_ada)


def _mixer_body(nb, lt, chunk, has_state, *refs):
    if has_state:
        (x_ref, mod_ref, cs_ref, gs_ref, *rest) = refs
    else:
        (x_ref, mod_ref, *rest) = refs
        cs_ref = gs_ref = None
    (n1_ref, wmain_ref, wz_ref, wu_ref, wgk_ref, bgk_ref, cw_ref, gnw_ref, woc_ref, wog_ref, wo_ref,
     n2_ref, rwt_ref, wsg_ref, wsu_ref, wsd_ref,
     x2_ref, h2_ref, lg_ref, nc_ref, ng_ref, carry_ref, st_ref, og_ref, sin_ref) = rest
    m = nb * lt
    i = pl.program_id(1)
    last = pl.num_programs(1) - 1

    def rows(v):
        w = v.shape[-1]
        if nb == 1:
            return v.reshape(1, w)
        return jnp.broadcast_to(v, (nb, lt, w)).reshape(m, w)

    @pl.when(i == 0)
    def _():
        if has_state:
            cs = cs_ref[...]
            carry_ref[0] = cs[:, 0:1, :]
            carry_ref[1] = cs[:, 1:2, :]
            for hh in range(H):
                st_ref[hh] = gs_ref[:, hh]
        else:
            carry_ref[...] = jnp.zeros_like(carry_ref)
            st_ref[...] = jnp.zeros_like(st_ref)

    x = x_ref[...].reshape(m, D)
    sh1, sc1, g1, sh2, sc2, g2 = [rows(mod_ref[j]) for j in range(6)]
    hb = (_rms(x, n1_ref[...]) * (1.0 + sc1) + sh1).astype(bf16)

    def proj(lo, hi):
        return _dot(hb, wmain_ref[:, lo:hi])

    pre = proj(D, 2 * D) * proj(2 * D, 3 * D)
    c0 = rows(carry_ref[0])
    c1 = rows(carry_ref[1])
    l_idx = lax.broadcasted_iota(i32, (m, D), 0) & (lt - 1)
    r1 = pltpu.roll(pre, 1, 0)
    r2 = pltpu.roll(pre, 2, 0)
    prev1 = jnp.where(l_idx == 0, c1, r1)
    prev2 = jnp.where(l_idx == 0, c0, jnp.where(l_idx == 1, c1, r2))
    cw = cw_ref[...]
    conv = cw[0:1] * prev2 + cw[1:2] * prev1 + cw[2:3] * pre
    y_a = _dot((proj(0, D) * conv).astype(bf16), woc_ref[...])
    pre3 = pre.reshape(nb, lt, D)
    tail = pre3[:, lt - 2:lt, :]
    carry_ref[0] = tail[:, 0:1, :]
    carry_ref[1] = tail[:, 1:2, :]
    nc_ref[0] = tail

    q = proj(3 * D, 3 * D + DK)
    k = proj(3 * D + DK, 3 * D + 2 * DK)
    v = proj(4 * D, 5 * D)
    g = proj(5 * D, 6 * D)
    z = _dot(hb, wz_ref[...]).astype(bf16)
    pa = _dot(z, wgk_ref[...]) + bgk_ref[...]
    la = (jnp.minimum(pa, 0.0) - jnp.log1p(jnp.exp(-jnp.abs(pa)))) / GATE_NORMALIZER
    pos = lax.broadcasted_iota(i32, (m, DK), 0) & (chunk - 1)
    bcum = la
    shift = 1
    while shift < chunk:
        bcum = bcum + jnp.where(pos >= shift, pltpu.roll(bcum, shift, 0), 0.0)
        shift *= 2
    ngrp = m // chunk
    bend = bcum.reshape(ngrp, chunk, DK)[:, chunk - 1:chunk, :]
    bend_rows = jnp.broadcast_to(bend, (ngrp, chunk, DK)).reshape(m, DK)
    nch = lt // chunk
    assert nb == 1 or nch == 1
    hs = H * nb
    nbat = nch * hs

    def per_head(xm, w):
        x3 = xm.reshape(nb * nch, chunk, H * w)
        parts = [x3[:, :, hh * w:(hh + 1) * w] for hh in range(H)]
        return jnp.stack(parts, axis=1 if nb == 1 else 0).reshape(nbat, chunk, w)

    qd = per_head((q * (DKH ** -0.5) * jnp.exp(bcum)).astype(bf16), DKH)
    kd = per_head((k * jnp.exp(-bcum)).astype(bf16), DKH)
    ke = per_head((k * jnp.exp(bend_rows - bcum)).astype(bf16), DKH)
    vv = per_head(v.astype(bf16), DVH)
    assert nbat <= 128
    a_end = per_head(jnp.exp(bend_rows), DKH)[:, 0, :]
    a_col = jnp.concatenate([a_end, jnp.zeros((128 - nbat, DKH), f32)], axis=0).T
    causal = lax.broadcasted_iota(i32, (chunk, chunk), 1) <= lax.broadcasted_iota(i32, (chunk, chunk), 0)
    sc = jnp.einsum('bcd,bsd->bcs', qd, kd, preferred_element_type=f32)
    sc = jnp.where(causal, sc, 0.0).astype(bf16)
    o = jnp.einsum('bcs,bse->bce', sc, vv, preferred_element_type=f32)
    upd = jnp.einsum('bsd,bse->bde', ke, vv, preferred_element_type=f32)
    st = [st_ref[s // nb, s % nb] for s in range(hs)]
    for j in range(nch):
        for s in range(hs):
            b = j * hs + s
            sin_ref[b] = st[s].astype(bf16)
            st[s] = st[s] * a_col[:, b:b + 1] + upd[b]
    for s in range(hs):
        st_ref[s // nb, s % nb] = st[s]
    o = o + jnp.einsum('bcd,bde->bce', qd, sin_ref[...], preferred_element_type=f32)
    on = _rms(o, gnw_ref[...])
    g3 = g.reshape(nb, lt, DV)
    for j in range(nch):
        js = slice(j * chunk, (j + 1) * chunk)
        for hh in range(H):
            vs = slice(hh * DVH, (hh + 1) * DVH)
            gh = g3[:, js, vs]
            b0 = (j * H + hh) * nb
            og_ref[:, js, vs] = (on[b0:b0 + nb] * (gh * jax.nn.sigmoid(gh))).astype(bf16)

    @pl.when(i == last)
    def _():
        for hh in range(H):
            ng_ref[0, :, hh] = st_ref[hh]

    y_b = _dot(og_ref[...].reshape(m, DV), wog_ref[...])
    u_a = _dot(hb, wu_ref[:, 0:D])
    u_b = _dot(hb, wu_ref[:, D:2 * D])
    merged = (jax.nn.sigmoid(u_a) * y_a + jax.nn.sigmoid(u_b) * y_b).astype(bf16)
    x1 = x + g1 * _dot(merged, wo_ref[...])

    h2 = (_rms(x1, n2_ref[...]) * (1.0 + sc2) + sh2).astype(bf16)
    sg = _dot(h2, wsg_ref[...])
    act = (sg * jax.nn.sigmoid(sg) * _dot(h2, wsu_ref[...])).astype(bf16)
    x2_ref[...] = x1 + g2 * _dot(act, wsd_ref[...])
    h2_ref[...] = h2
    lg_ref[0] = lax.dot_general(rwt_ref[...], h2, (((1,), (1,)), ((), ())), preferred_element_type=f32)


def _mixer(x, mod4, states, weights, nb, lt, chunk):
    bsz, seqlen, _ = x.shape
    has_state = states is not None
    m = nb * lt
    n_lt = seqlen // lt
    nsteps = (bsz // nb) * n_lt
    t_all = bsz * seqlen

    def tok_blk(b, i):
        return b * n_lt + i

    in_specs = [pl.BlockSpec((nb, lt, D), lambda b, i: (b, i, 0)),
                pl.BlockSpec((6, nb, 1, D), lambda b, i: (0, b, 0, 0))]
    args = [x, mod4]
    if has_state:
        in_specs += [pl.BlockSpec((nb, 2, D), lambda b, i: (b, 0, 0)),
                     pl.BlockSpec((nb, H, DKH, DVH), lambda b, i: (b, 0, 0, 0))]
        args += list(states)
    in_specs += [_const_spec(w.shape) for w in weights]
    args += list(weights)
    out_specs = [pl.BlockSpec((m, D), lambda b, i: (tok_blk(b, i), 0)),
                 pl.BlockSpec((m, D), lambda b, i: (tok_blk(b, i), 0)),
                 pl.BlockSpec((1, E, m), lambda b, i: (b * n_lt + i, 0, 0)),
                 pl.BlockSpec((1, nb, 2, D), lambda b, i: (0, b, 0, 0)),
                 pl.BlockSpec((1, nb, H, DKH, DVH), lambda b, i: (0, b, 0, 0, 0))]
    out_shape = [jax.ShapeDtypeStruct((t_all, D), f32),
                 jax.ShapeDtypeStruct((t_all, D), bf16),
                 jax.ShapeDtypeStruct((nsteps, E, m), f32),
                 jax.ShapeDtypeStruct((1, bsz, 2, D), f32),
                 jax.ShapeDtypeStruct((1, bsz, H, DKH, DVH), f32)]
    return pl.pallas_call(
        functools.partial(_mixer_body, nb, lt, chunk, has_state),
        grid=(bsz // nb, n_lt),
        in_specs=in_specs,
        out_specs=out_specs,
        out_shape=out_shape,
        scratch_shapes=[pltpu.VMEM((2, nb, 1, D), f32),
                        pltpu.VMEM((H, nb, DKH, DVH), f32),
                        pltpu.VMEM((nb, lt, DV), bf16),
                        pltpu.VMEM(((lt // chunk) * H * nb, DKH, DVH), bf16)],
        compiler_params=pltpu.CompilerParams(dimension_semantics=("arbitrary", "arbitrary"),
                                             vmem_limit_bytes=VMEM_LIMIT),
        name="mixer_state" if has_state else "mixer_prompt",
    )(*args)


def _route_body(lg_ref, bias_ref, eidx_ref, w_ref, lrank_ref, cnt_ref):
    step = pl.program_id(0)
    tl = lg_ref.shape[1]

    scores = jax.nn.sigmoid(lg_ref[...])
    choice = scores + bias_ref[...]
    gsz = E // NGROUPS
    grp = choice.reshape(NGROUPS, gsz, tl)
    mi = lax.broadcasted_iota(i32, (NGROUPS, gsz, tl), 1)
    m1 = jnp.max(grp, axis=1, keepdims=True)
    first = jnp.min(jnp.where(grp == m1, mi, gsz), axis=1, keepdims=True)
    m2 = jnp.max(jnp.where(mi == first, NEG, grp), axis=1, keepdims=True)
    gscore = (m1 + m2).reshape(NGROUPS, tl)
    gi = lax.broadcasted_iota(i32, (NGROUPS, tl), 0)
    gsel = jnp.zeros((NGROUPS, tl), f32)
    work = gscore
    for _ in range(TOPK_GROUPS):
        mx = jnp.max(work, axis=0, keepdims=True)
        f = jnp.min(jnp.where(work == mx, gi, NGROUPS), axis=0, keepdims=True)
        hit = gi == f
        gsel = jnp.where(hit, 1.0, gsel)
        work = jnp.where(hit, NEG, work)
    emask = jnp.broadcast_to(gsel.reshape(NGROUPS, 1, tl), (NGROUPS, gsz, tl)).reshape(E, tl)
    masked = jnp.where(emask > 0.0, choice, NEG)
    ei = lax.broadcasted_iota(i32, (E, tl), 0)
    sel = jnp.zeros((E, tl), f32)
    hits, ws = [], []
    for kk in range(TOPK):
        mx = jnp.max(masked, axis=0, keepdims=True)
        f = jnp.min(jnp.where(masked == mx, ei, E), axis=0, keepdims=True)
        hit = ei == f
        ws.append(jnp.sum(jnp.where(hit, scores, 0.0), axis=0, keepdims=True))
        masked = jnp.where(hit, NEG, masked)
        sel = jnp.where(hit, 1.0, sel)
        hits.append(hit)
        eidx_ref[kk:kk + 1, :] = f
    wsum = ws[0]
    for t in ws[1:]:
        wsum = wsum + t
    for kk in range(TOPK):
        w_ref[kk:kk + 1, :] = ws[kk] / wsum * ROUTED_SCALE

    ui = lax.broadcasted_iota(i32, (tl, tl), 0)
    uj = lax.broadcasted_iota(i32, (tl, tl), 1)
    upper = (ui <= uj).astype(bf16)
    lrank = _dot(sel.astype(bf16), upper) - sel
    for kk in range(TOPK):
        lrank_ref[kk:kk + 1, :] = jnp.sum(jnp.where(hits[kk], lrank, 0.0), axis=0, keepdims=True).astype(i32)

    @pl.when(step == 0)
    def _():
        cnt_ref[...] = jnp.zeros_like(cnt_ref)
    lane = lax.broadcasted_iota(i32, cnt_ref.shape, 1)
    cnt_ref[...] = jnp.where(lane == step, jnp.sum(sel, axis=1, keepdims=True), cnt_ref[...])


def _route(logits_t, bias_col):
    t_all = logits_t.shape[1]
    tl = TILE
    assert t_all // tl <= MAX_TILES
    tok = lambda s: (0, s)
    return pl.pallas_call(
        _route_body,
        grid=(t_all // tl,),
        in_specs=[pl.BlockSpec((E, tl), tok), pl.BlockSpec((E, 1), lambda s: (0, 0))],
        out_specs=[pl.BlockSpec((TOPK, tl), tok), pl.BlockSpec((TOPK, tl), tok), pl.BlockSpec((TOPK, tl), tok),
                   pl.BlockSpec((E, MAX_TILES), lambda s: (0, 0))],
        out_shape=[jax.ShapeDtypeStruct((TOPK, t_all), i32), jax.ShapeDtypeStruct((TOPK, t_all), f32),
                   jax.ShapeDtypeStruct((TOPK, t_all), i32), jax.ShapeDtypeStruct((E, MAX_TILES), f32)],
        compiler_params=pltpu.CompilerParams(dimension_semantics=("arbitrary",), vmem_limit_bytes=VMEM_LIMIT),
        name="route",
    )(logits_t, bias_col)


def _plan_body(eidx_ref, lrank_ref, cnt_ref, slot_ref, n8_ref, l_ref, g_ref, blk_ref):
    t_all = eidx_ref.shape[1]
    cnt = cnt_ref[...]
    n8 = jnp.floor((cnt + (RUN_ALIGN - 1)) * (1.0 / RUN_ALIGN)) * RUN_ALIGN
    ti = lax.broadcasted_iota(i32, (MAX_TILES, MAX_TILES), 0)
    tj = lax.broadcasted_iota(i32, (MAX_TILES, MAX_TILES), 1)
    before = _dotx01(n8, (ti < tj).astype(bf16))
    region = jnp.sum(n8, axis=1, keepdims=True)
    nblocks = jnp.floor((region + (MOE_BM - 1)) * (1.0 / MOE_BM))
    ri = lax.broadcasted_iota(i32, (E, E), 0)
    ci = lax.broadcasted_iota(i32, (E, E), 1)
    below = (ci < ri).astype(bf16)
    bstart = _dot01(below, jnp.broadcast_to(nblocks, (E, MAX_TILES)))
    loff = _dot01(below, n8)
    n8_ref[...] = n8.astype(i32)
    l_ref[...] = loff.astype(i32)
    g_ref[...] = (bstart * MOE_BM + before).astype(i32)
    lane0 = lax.broadcasted_iota(i32, (E, MAX_TILES), 1) == 0
    blk_ref[...] = jnp.where(lane0, bstart, nblocks).astype(i32)

    tile_of = lax.shift_right_logical(lax.broadcasted_iota(i32, (MAX_TILES, t_all), 1), TILE.bit_length() - 1)
    spread = (lax.broadcasted_iota(i32, (MAX_TILES, t_all), 0) == tile_of).astype(bf16)
    ltok = _dotx01(loff, spread)
    ei = lax.broadcasted_iota(i32, (E, t_all), 0)
    for kk in range(TOPK):
        hit = ei == eidx_ref[kk:kk + 1, :]
        start = jnp.sum(jnp.where(hit, ltok, 0.0), axis=0, keepdims=True)
        slot_ref[kk:kk + 1, :] = start.astype(i32) + lrank_ref[kk:kk + 1, :]


def _plan(eidx, lrank, cnt):
    t_all = eidx.shape[1]
    assert TILE & (TILE - 1) == 0
    tl = t_all
    tok = lambda s: (0, s)
    table = pl.BlockSpec((E, MAX_TILES), lambda s: (0, 0))
    return pl.pallas_call(
        _plan_body,
        grid=(1,),
        in_specs=[pl.BlockSpec((TOPK, tl), tok), pl.BlockSpec((TOPK, tl), tok), table],
        out_specs=[pl.BlockSpec((TOPK, tl), tok), table, table, table, table],
        out_shape=[jax.ShapeDtypeStruct((TOPK, t_all), i32)] + [jax.ShapeDtypeStruct((E, MAX_TILES), i32)] * 4,
        compiler_params=pltpu.CompilerParams(dimension_semantics=("arbitrary",), vmem_limit_bytes=VMEM_LIMIT),
        name="plan",
    )(eidx, lrank, cnt)


def _for_each_piece(total, pieces, fn):
    def piece(j, b):
        @pl.when((total & b) != 0)
        def _():
            fn(pl.multiple_of(total & ~(2 * b - 1), pieces[-1]), b, j)

    nbig = max(len(pieces) - 2, 0)

    @pl.when(total >= 4 * pieces[-1])
    def _():
        for j in range(nbig):
            piece(j, pieces[j])
    for j in range(nbig, len(pieces)):
        piece(j, pieces[j])


def _for_each_run(n8_ref, l_ref, g_ref, tile, fn):
    def per_expert(e, c):
        lo = pl.multiple_of(l_ref[e, tile], RUN_ALIGN)
        go = pl.multiple_of(g_ref[e, tile], RUN_ALIGN)
        _for_each_piece(n8_ref[e, tile], RUN_PIECES, lambda off, b, j: fn(lo + off, go + off, b, j))
        return c
    lax.fori_loop(0, E, per_expert, 0)


def _start(copy, j):
    copy.start(priority=j % 2)


def _wait_rows(make_copy, rows):
    _for_each_piece(rows, WAIT_PIECES, lambda off, b, j: make_copy(b).wait())


def _dispatch_body(tiles_a, n8_ref, l_ref, g_ref, slot_ref, ha_ref, hb_ref, xs_ref, buf_ref, zero_ref, sem, zsem):
    tile = pl.program_id(0)
    i = tile
    nsteps = pl.num_programs(0)
    cur = i % RING

    slot = slot_ref[...].astype(i16)
    h = jnp.where(tile < tiles_a, ha_ref[...], hb_ref[...])
    filled = l_ref[E - 1, tile] + n8_ref[E - 1, tile]

    def order(row0, nrows):
        srow = lax.broadcasted_iota(i32, (nrows, TILE), 0).astype(i16) + jnp.int16(row0)
        onehot = jnp.zeros((nrows, TILE), bf16)
        for kk in range(TOPK):
            onehot = jnp.where(srow == slot[kk:kk + 1, :], jnp.ones((), bf16), onehot)
        buf_ref[cur, row0:row0 + nrows, :] = _dot(onehot, h).astype(bf16)

    order(0, SLOTS_MAIN)

    @pl.when(filled > SLOTS_MAIN)
    def _():
        order(SLOTS_MAIN, SLOTS - SLOTS_MAIN)

    def run_copy(sl):
        return lambda srow_, grow, b: pltpu.make_async_copy(
            buf_ref.at[sl, pl.ds(srow_, b)], xs_ref.at[pl.ds(grow, b)], sem.at[sl])

    _for_each_run(n8_ref, l_ref, g_ref, tile, lambda s, g, b, j: _start(run_copy(cur)(s, g, b), j))

    def wait_tile(t, sl):
        _wait_rows(lambda b: run_copy(sl)(0, 0, b), l_ref[E - 1, t] + n8_ref[E - 1, t])

    @pl.when(i >= RING - 1)
    def _():
        wait_tile(tile - (RING - 1), (i + 1) % RING)

    last = nsteps - 1

    def zero_fill(fn):
        def per_expert(e, c):
            end = pl.multiple_of(g_ref[e, last] + n8_ref[e, last], RUN_ALIGN)
            _for_each_piece((-end) & (MOE_BM - 1), TAIL_PIECES, lambda off, b, j: fn(pltpu.make_async_copy(
                zero_ref.at[pl.ds(0, b)], xs_ref.at[pl.ds(end + off, b)], zsem)))
            return c
        lax.fori_loop(0, E, per_expert, 0)
        used = (g_ref[E - 1, last] + n8_ref[E - 1, last] + (MOE_BM - 1)) // MOE_BM

        def per_block(blk, c):
            fn(pltpu.make_async_copy(zero_ref, xs_ref.at[pl.ds(pl.multiple_of(blk * MOE_BM, MOE_BM), MOE_BM)], zsem))
            return c
        lax.fori_loop(used, xs_ref.shape[0] // MOE_BM, per_block, 0)

    @pl.when(i == 0)
    def _():
        zero_ref[...] = jnp.zeros_like(zero_ref)
        zero_fill(lambda cp: cp.start())

    @pl.when(i == last)
    def _():
        for back in range(RING - 2, -1, -1):
            @pl.when(i >= back)
            def _():
                wait_tile(tile - back, (i - back) % RING)
        zero_fill(lambda cp: cp.wait())


def _dispatch(tables, slot, h2_a, h2_b, nrows):
    tiles_a, tiles_b = h2_a.shape[0] // TILE, h2_b.shape[0] // TILE
    return pl.pallas_call(
        functools.partial(_dispatch_body, tiles_a),
        grid_spec=pltpu.PrefetchScalarGridSpec(
            num_scalar_prefetch=len(tables),
            grid=(tiles_a + tiles_b,),
            in_specs=[pl.BlockSpec((TOPK, TILE), lambda i, *_: (0, i)),
                      pl.BlockSpec((TILE, D), lambda i, *_: (jnp.minimum(i, tiles_a - 1), 0)),
                      pl.BlockSpec((TILE, D), lambda i, *_: (jnp.maximum(i - tiles_a, 0), 0))],
            out_specs=pl.BlockSpec(memory_space=pl.ANY),
            scratch_shapes=[pltpu.VMEM((RING, SLOTS, D), bf16), pltpu.VMEM((MOE_BM, D), bf16),
                            pltpu.SemaphoreType.DMA((RING,)), pltpu.SemaphoreType.DMA]),
        out_shape=jax.ShapeDtypeStruct((nrows, D), bf16),
        compiler_params=pltpu.CompilerParams(dimension_semantics=("arbitrary",), vmem_limit_bytes=VMEM_LIMIT),
        name="dispatch",
    )(*tables, slot, h2_a, h2_b)


def _expert_body(blk_ref, wg_ref, wu_ref, wd_ref, xs_ref, ys_ref, xbuf_ref, ybuf_ref, wgu_ref, wdb_ref, isem, osem):
    e = pl.program_id(0)
    first, count = blk_ref[e, 0], blk_ref[e, 1]
    used = blk_ref[E - 1, 0] + blk_ref[E - 1, 1]
    wgu_ref[:, :DE] = wg_ref[0].astype(bf16)
    wgu_ref[:, DE:] = wu_ref[0].astype(bf16)
    wdb_ref[...] = wd_ref[0].astype(bf16)

    def rows(g):
        return pl.ds(pl.multiple_of(g * MOE_BM, MOE_BM), MOE_BM)

    def load(g):
        sl = g % EXPERT_IN_SLOTS
        return pltpu.make_async_copy(xs_ref.at[rows(g)], xbuf_ref.at[sl], isem.at[sl])

    def store(g):
        sl = g % 2
        return pltpu.make_async_copy(ybuf_ref.at[sl], ys_ref.at[rows(g)], osem.at[sl])

    @pl.when(e == 0)
    def _():
        for g in range(EXPERT_IN_SLOTS - 1):
            @pl.when(g < used)
            def _():
                load(g).start()

    def block(j, c):
        g = first + j

        @pl.when(g + (EXPERT_IN_SLOTS - 1) < used)
        def _():
            load(g + (EXPERT_IN_SLOTS - 1)).start()
        load(g).wait()

        @pl.when(g >= 2)
        def _():
            store(g - 2).wait()
        hgu = _dot(xbuf_ref[g % EXPERT_IN_SLOTS], wgu_ref[...])
        hg = hgu[:, :DE]
        act = (hg * jax.nn.sigmoid(hg) * hgu[:, DE:]).astype(bf16)
        ybuf_ref[g % 2] = _dot(act, wdb_ref[...]).astype(bf16)
        store(g).start()
        return c
    lax.fori_loop(0, count, block, 0)

    @pl.when(e == E - 1)
    def _():
        for back in (2, 1):
            @pl.when(used >= back)
            def _():
                store(used - back).wait()


def _experts(blk, xs, we_gate, we_up, we_down):
    ex = lambda e, blk_: (e, 0, 0)
    return pl.pallas_call(
        _expert_body,
        grid_spec=pltpu.PrefetchScalarGridSpec(
            num_scalar_prefetch=1,
            grid=(E,),
            in_specs=[pl.BlockSpec((1, D, DE), ex), pl.BlockSpec((1, D, DE), ex), pl.BlockSpec((1, DE, D), ex),
                      pl.BlockSpec(memory_space=pl.ANY)],
            out_specs=pl.BlockSpec(memory_space=pl.ANY),
            scratch_shapes=[pltpu.VMEM((EXPERT_IN_SLOTS, MOE_BM, D), bf16), pltpu.VMEM((2, MOE_BM, D), bf16),
                            pltpu.VMEM((D, 2 * DE), bf16), pltpu.VMEM((DE, D), bf16),
                            pltpu.SemaphoreType.DMA((EXPERT_IN_SLOTS,)), pltpu.SemaphoreType.DMA((2,))]),
        out_shape=jax.ShapeDtypeStruct(xs.shape, xs.dtype),
        input_output_aliases={4: 0},
        compiler_params=pltpu.CompilerParams(dimension_semantics=("arbitrary",), vmem_limit_bytes=VMEM_LIMIT),
        name="experts",
    )(blk, we_gate, we_up, we_down, xs)


def _final_body(nb, lt, tile0, n8_ref, l_ref, g_ref, slot_ref, wt_ref, x2_ref, g2_ref, fw_ref, ys_ref, o_ref,
                buf_ref, routed_ref, sem):
    m = nb * lt
    n_inner = pl.num_programs(1)
    step = pl.program_id(0) * n_inner + pl.program_id(1)
    nsteps = pl.num_programs(0) * n_inner
    tile = tile0 + step
    cur = step % RING

    def run_copy(sl):
        return lambda srow, grow, b: pltpu.make_async_copy(
            ys_ref.at[pl.ds(grow, b)], buf_ref.at[sl, pl.ds(srow, b)], sem.at[sl])

    def fetch(ahead):
        sl = (step + ahead) % RING
        _for_each_run(n8_ref, l_ref, g_ref, tile + ahead, lambda s, g, b, j: _start(run_copy(sl)(s, g, b), j))

    @pl.when(step == 0)
    def _():
        buf_ref[...] = jnp.zeros_like(buf_ref)
        for ahead in range(RING - 1):
            @pl.when(ahead < nsteps)
            def _():
                fetch(ahead)

    @pl.when(step + (RING - 1) < nsteps)
    def _():
        fetch(RING - 1)

    _wait_rows(lambda b: run_copy(cur)(0, 0, b), l_ref[E - 1, tile] + n8_ref[E - 1, tile])

    slot = slot_ref[...].astype(i16)
    wt = wt_ref[...]
    wt_hi = wt.astype(bf16)
    wt_lo = (wt - wt_hi.astype(f32)).astype(bf16)
    def combine(col0, ncols):
        scol = lax.broadcasted_iota(i32, (m, ncols), 1).astype(i16) + jnp.int16(col0)
        w_hi = jnp.zeros((m, ncols), bf16)
        w_lo = jnp.zeros((m, ncols), bf16)
        for kk in range(TOPK):
            hit = scol == slot[:, kk:kk + 1]
            w_hi = jnp.where(hit, wt_hi[:, kk:kk + 1], w_hi)
            w_lo = jnp.where(hit, wt_lo[:, kk:kk + 1], w_lo)
        yb = buf_ref[cur, col0:col0 + ncols, :]
        return _dot(w_hi, yb) + _dot(w_lo, yb)

    routed_ref[...] = combine(0, SLOTS_MAIN)

    @pl.when(l_ref[E - 1, tile] + n8_ref[E - 1, tile] > SLOTS_MAIN)
    def _():
        routed_ref[...] += combine(SLOTS_MAIN, SLOTS - SLOTS_MAIN)
    routed = routed_ref[...]
    g2 = g2_ref[0]
    if nb == 1:
        g2 = g2.reshape(1, D)
    else:
        g2 = jnp.broadcast_to(g2, (nb, lt, D)).reshape(m, D)
    y = x2_ref[...] + g2 * routed
    o_ref[...] = _rms(y, fw_ref[...]).reshape(nb, lt, D)


def _final(tables, slot_t, wts_t, x2, ys, mod4, fw, bsz, seqlen, row0, nb, lt):
    m = nb * lt
    assert m == TILE and row0 % TILE == 0
    n_lt = seqlen // lt
    tile0 = row0 // TILE
    tile = lambda b, i: tile0 + b * n_lt + i
    return pl.pallas_call(
        functools.partial(_final_body, nb, lt, tile0),
        grid_spec=pltpu.PrefetchScalarGridSpec(
            num_scalar_prefetch=len(tables),
            grid=(bsz // nb, n_lt),
            in_specs=[pl.BlockSpec((m, TOPK), lambda b, i, *_: (tile(b, i), 0)),
                      pl.BlockSpec((m, TOPK), lambda b, i, *_: (tile(b, i), 0)),
                      pl.BlockSpec((m, D), lambda b, i, *_: (b * n_lt + i, 0)),
                      pl.BlockSpec((1, nb, 1, D), lambda b, i, *_: (5, b, 0, 0)),
                      pl.BlockSpec((1, D), lambda b, i, *_: (0, 0)),
                      pl.BlockSpec(memory_space=pl.ANY)],
            out_specs=pl.BlockSpec((nb, lt, D), lambda b, i, *_: (b, i, 0)),
            scratch_shapes=[pltpu.VMEM((RING, SLOTS, D), bf16), pltpu.VMEM((m, D), f32),
                            pltpu.SemaphoreType.DMA((RING,))]),
        out_shape=jax.ShapeDtypeStruct((bsz, seqlen, D), f32),
        compiler_params=pltpu.CompilerParams(dimension_semantics=("arbitrary", "arbitrary"),
                                             vmem_limit_bytes=VMEM_LIMIT),
        name="final",
    )(*tables, slot_t, wts_t, x2, mod4, fw, ys)


def kernel(x_prompt, x_sample, state_conv, state_gla, c_prompt, c_sample, w_ada, b_ada, norm1_w, w_in, conv_w,
           w_gk, b_gk, gla_norm_w, w_out_conv, w_out_gla, w_o, norm2_w, router_w, router_bias, we_gate, we_up,
           we_down, ws_gate, ws_up, ws_down, final_norm_w):
    assert w_ada.shape[0] == 1, "single-layer step"
    bp, lp, _ = x_prompt.shape
    bs, ls, _ = x_sample.shape
    tp, ts = bp * lp, bs * ls
    t_all = tp + ts

    w_in0 = w_in[0]
    n_main = 3 * D + 2 * DK + 2 * DV
    rank = w_gk.shape[1]
    weights = (
        norm1_w[0].reshape(1, D),
        w_in0[:, :n_main].astype(bf16),
        jnp.pad(w_in0[:, n_main:n_main + rank], ((0, 0), (0, RANK_PAD - rank))).astype(bf16),
        w_in0[:, n_main + rank:].astype(bf16),
        jnp.pad(w_gk[0], ((0, RANK_PAD - rank), (0, 0))).astype(bf16),
        b_gk[0].reshape(1, DK),
        conv_w[0],
        gla_norm_w[0].reshape(1, DVH),
        w_out_conv[0].astype(bf16),
        w_out_gla[0].astype(bf16),
        w_o[0].astype(bf16),
        norm2_w[0].reshape(1, D),
        router_w[0].T.astype(bf16),
        ws_gate[0].astype(bf16),
        ws_up[0].astype(bf16),
        ws_down[0].astype(bf16),
    )

    mod = _adaln(jnp.concatenate([c_prompt, c_sample], axis=0), w_ada[0], b_ada[0].reshape(1, 6 * D))
    mod4 = mod.reshape(bp + bs, 6, 1, D).transpose(1, 0, 2, 3)
    mod_p, mod_s = mod4[:, :bp], mod4[:, bp:]

    lt_p = 512
    x2_p, h2_p, lg_p, conv_p, gla_p = _mixer(x_prompt, mod_p, None, weights, 1, lt_p, GLA_CHUNK)
    nb_s = 8
    x2_s, h2_s, lg_s, conv_s, gla_s = _mixer(x_sample, mod_s, (state_conv[0], state_gla[0]), weights,
                                             nb_s, ls, math.gcd(ls, GLA_CHUNK))
    lg = jnp.concatenate([lg_p.transpose(1, 0, 2).reshape(E, tp), lg_s.transpose(1, 0, 2).reshape(E, ts)], axis=1)

    eidx, wts, lrank, cnt = _route(lg, router_bias[0].reshape(E, 1))
    n_tiles = t_all // TILE
    max_rows = t_all * TOPK + n_tiles * E * (RUN_ALIGN - 1) + E * (MOE_BM - RUN_ALIGN)
    nblk = -(-max_rows // MOE_BM)
    slot, n8, loff, goff, blk = _plan(eidx, lrank, cnt)
    tables = (n8, loff, goff)

    xs = _dispatch(tables, slot, h2_p, h2_s, nblk * MOE_BM)
    ys = _experts(blk, xs, we_gate[0], we_up[0], we_down[0])

    fw = final_norm_w.reshape(1, D)
    slot_t, wts_t = slot.T, wts.T
    y_prompt = _final(tables, slot_t, wts_t, x2_p, ys, mod_p, fw, bp, lp, 0, 1, TILE)
    y_sample = _final(tables, slot_t, wts_t, x2_s, ys, mod_s, fw, bs, ls, tp, TILE // ls, ls)
    return (y_prompt, y_sample, conv_p, gla_p, conv_s, gla_s)
```

```python
import functools
import math

import jax
import jax.numpy as jnp
from jax import lax
from jax.experimental import pallas as pl
from jax.experimental.pallas import tpu as pltpu

f32 = jnp.float32
bf16 = jnp.bfloat16
i32 = jnp.int32
i16 = jnp.int16

D = 1024
H = 4
DKH = 128
DVH = 256
DK = H * DKH
DV = H * DVH
RANK_PAD = 128
GATE_NORMALIZER = 16.0
GLA_CHUNK = 64
E = 64
TOPK = 8
NGROUPS = 8
TOPK_GROUPS = 4
DE = 256
ROUTED_SCALE = 2.5
EPS = 1e-6
NEG = float("-inf")

VMEM_LIMIT = 58 * 1024 * 1024
MOE_BM = 512
EXPERT_IN_SLOTS = 3
RING = 3
TILE = 256
MAX_TILES = 128
RUN_ALIGN = 16
SLOTS = TILE * TOPK + 1024
SLOTS_MAIN = TILE * TOPK + 512
RUN_PIECES = (256, 128, 64, 32, 16)
TAIL_PIECES = (256, 128, 64, 32, 16)
WAIT_PIECES = (2048, 1024, 512, 256, 128, 64, 32, 16)

def _dot(a, b):
    return jnp.dot(a, b, preferred_element_type=f32)


def _split3(x):
    hi = x.astype(bf16)
    r1 = x - hi.astype(f32)
    mid = r1.astype(bf16)
    lo = (r1 - mid.astype(f32)).astype(bf16)
    return hi, mid, lo


def _dot01(m01, x):
    hi, mid, lo = _split3(x)
    return _dot(m01, hi) + _dot(m01, mid) + _dot(m01, lo)


def _dotx01(x, m01):
    hi, mid, lo = _split3(x)
    return _dot(hi, m01) + _dot(mid, m01) + _dot(lo, m01)


def _rms(v, w):
    ms = jnp.mean(v * v, axis=-1, keepdims=True)
    return v * lax.rsqrt(ms + EPS) * w


def _const_spec(shape):
    n = len(shape)
    return pl.BlockSpec(shape, lambda *_: (0,) * n, pipeline_mode=pl.Buffered(1))


def _adaln_body(c_ref, w_ref, b_ref, o_ref):
    c = c_ref[...]
    a = (c * jax.nn.sigmoid(c)).astype(bf16)
    o_ref[...] = _dot(a, w_ref[...].astype(bf16)) + b_ref[...]


def _adaln(c_all, w_ada, b_ada):
    nrow = c_all.shape[0]
    tn = 1536
    return pl.pallas_call(
        _adaln_body,
        grid=(6 * D // tn,),
        in_specs=[pl.BlockSpec((nrow, D), lambda j: (0, 0)),
                  pl.BlockSpec((D, tn), lambda j: (0, j)),
                  pl.BlockSpec((1, tn), lambda j: (0, j))],
        out_specs=pl.BlockSpec((nrow, tn), lambda j: (0, j)),
        out_shape=jax.ShapeDtypeStruct((nrow, 6 * D), f32),
        compiler_params=pltpu.CompilerParams(dimension_semantics=("arbitrary",), vmem_limit_bytes=VMEM_LIMIT),
        name="adaln",
    )(c_all, w_ada, b_ada)


def _mixer_body(nb, lt, chunk, has_state, *refs):
    if has_state:
        (x_ref, mod_ref, cs_ref, gs_ref, *rest) = refs
    else:
        (x_ref, mod_ref, *rest) = refs
        cs_ref = gs_ref = None
    (n1_ref, wmain_ref, wz_ref, wu_ref, wgk_ref, bgk_ref, cw_ref, gnw_ref, woc_ref, wog_ref, wo_ref,
     n2_ref, rwt_ref, wsg_ref, wsu_ref, wsd_ref,
     x2_ref, h2_ref, lg_ref, nc_ref, ng_ref, carry_ref, st_ref, og_ref, sin_ref) = rest
    m = nb * lt
    i = pl.program_id(1)
    last = pl.num_programs(1) - 1

    def rows(v):
        w = v.shape[-1]
        if nb == 1:
            return v.reshape(1, w)
        return jnp.broadcast_to(v, (nb, lt, w)).reshape(m, w)

    @pl.when(i == 0)
    def _():
        if has_state:
            cs = cs_ref[...]
            carry_ref[0] = cs[:, 0:1, :]
            carry_ref[1] = cs[:, 1:2, :]
            for hh in range(H):
                st_ref[hh] = gs_ref[:, hh]
        else:
            carry_ref[...] = jnp.zeros_like(carry_ref)
            st_ref[...] = jnp.zeros_like(st_ref)

    x = x_ref[...].reshape(m, D)
    sh1, sc1, g1, sh2, sc2, g2 = [rows(mod_ref[j]) for j in range(6)]
    hb = (_rms(x, n1_ref[...]) * (1.0 + sc1) + sh1).astype(bf16)

    def proj(lo, hi):
        return _dot(hb, wmain_ref[:, lo:hi])

    pre = proj(D, 2 * D) * proj(2 * D, 3 * D)
    c0 = rows(carry_ref[0])
    c1 = rows(carry_ref[1])
    l_idx = lax.broadcasted_iota(i32, (m, D), 0) & (lt - 1)
    r1 = pltpu.roll(pre, 1, 0)
    r2 = pltpu.roll(pre, 2, 0)
    prev1 = jnp.where(l_idx == 0, c1, r1)
    prev2 = jnp.where(l_idx == 0, c0, jnp.where(l_idx == 1, c1, r2))
    cw = cw_ref[...]
    conv = cw[0:1] * prev2 + cw[1:2] * prev1 + cw[2:3] * pre
    y_a = _dot((proj(0, D) * conv).astype(bf16), woc_ref[...])
    pre3 = pre.reshape(nb, lt, D)
    tail = pre3[:, lt - 2:lt, :]
    carry_ref[0] = tail[:, 0:1, :]
    carry_ref[1] = tail[:, 1:2, :]
    nc_ref[0] = tail

    q = proj(3 * D, 3 * D + DK)
    k = proj(3 * D + DK, 3 * D + 2 * DK)
    v = proj(4 * D, 5 * D)
    g = proj(5 * D, 6 * D)
    z = _dot(hb, wz_ref[...]).astype(bf16)
    pa = _dot(z, wgk_ref[...]) + bgk_ref[...]
    la = (jnp.minimum(pa, 0.0) - jnp.log1p(jnp.exp(-jnp.abs(pa)))) / GATE_NORMALIZER
    pos = lax.broadcasted_iota(i32, (m, DK), 0) & (chunk - 1)
    bcum = la
    shift = 1
    while shift < chunk:
        bcum = bcum + jnp.where(pos >= shift, pltpu.roll(bcum, shift, 0), 0.0)
        shift *= 2
    ngrp = m // chunk
    bend = bcum.reshape(ngrp, chunk, DK)[:, chunk - 1:chunk, :]
    bend_rows = jnp.broadcast_to(bend, (ngrp, chunk, DK)).reshape(m, DK)
    nch = lt // chunk
    assert nb == 1 or nch == 1
    hs = H * nb
    nbat = nch * hs

    def per_head(xm, w):
        x3 = xm.reshape(nb * nch, chunk, H * w)
        parts = [x3[:, :, hh * w:(hh + 1) * w] for hh in range(H)]
        return jnp.stack(parts, axis=1 if nb == 1 else 0).reshape(nbat, chunk, w)

    qd = per_head((q * (DKH ** -0.5) * jnp.exp(bcum)).astype(bf16), DKH)
    kd = per_head((k * jnp.exp(-bcum)).astype(bf16), DKH)
    ke = per_head((k * jnp.exp(bend_rows - bcum)).astype(bf16), DKH)
    vv = per_head(v.astype(bf16), DVH)
    assert nbat <= 128
    a_end = per_head(jnp.exp(bend_rows), DKH)[:, 0, :]
    a_col = jnp.concatenate([a_end, jnp.zeros((128 - nbat, DKH), f32)], axis=0).T
    causal = lax.broadcasted_iota(i32, (chunk, chunk), 1) <= lax.broadcasted_iota(i32, (chunk, chunk), 0)
    sc = jnp.einsum('bcd,bsd->bcs', qd, kd, preferred_element_type=f32)
    sc = jnp.where(causal, sc, 0.0).astype(bf16)
    o = jnp.einsum('bcs,bse->bce', sc, vv, preferred_element_type=f32)
    upd = jnp.einsum('bsd,bse->bde', ke, vv, preferred_element_type=f32)
    st = [st_ref[s // nb, s % nb] for s in range(hs)]
    for j in range(nch):
        for s in range(hs):
            b = j * hs + s
            sin_ref[b] = st[s].astype(bf16)
            st[s] = st[s] * a_col[:, b:b + 1] + upd[b]
    for s in range(hs):
        st_ref[s // nb, s % nb] = st[s]
    o = o + jnp.einsum('bcd,bde->bce', qd, sin_ref[...], preferred_element_type=f32)
    on = _rms(o, gnw_ref[...])
    g3 = g.reshape(nb, lt, DV)
    for j in range(nch):
        js = slice(j * chunk, (j + 1) * chunk)
        for hh in range(H):
            vs = slice(hh * DVH, (hh + 1) * DVH)
            gh = g3[:, js, vs]
            b0 = (j * H + hh) * nb
            og_ref[:, js, vs] = (on[b0:b0 + nb] * (gh * jax.nn.sigmoid(gh))).astype(bf16)

    @pl.when(i == last)
    def _():
        for hh in range(H):
            ng_ref[0, :, hh] = st_ref[hh]

    y_b = _dot(og_ref[...].reshape(m, DV), wog_ref[...])
    u_a = _dot(hb, wu_ref[:, 0:D])
    u_b = _dot(hb, wu_ref[:, D:2 * D])
    merged = (jax.nn.sigmoid(u_a) * y_a + jax.nn.sigmoid(u_b) * y_b).astype(bf16)
    x1 = x + g1 * _dot(merged, wo_ref[...])

    h2 = (_rms(x1, n2_ref[...]) * (1.0 + sc2) + sh2).astype(bf16)
    sg = _dot(h2, wsg_ref[...])
    act = (sg * jax.nn.sigmoid(sg) * _dot(h2, wsu_ref[...])).astype(bf16)
    x2_ref[...] = x1 + g2 * _dot(act, wsd_ref[...])
    h2_ref[...] = h2
    lg_ref[0] = lax.dot_general(rwt_ref[...], h2, (((1,), (1,)), ((), ())), preferred_element_type=f32)


def _mixer(x, mod4, states, weights, nb, lt, chunk):
    bsz, seqlen, _ = x.shape
    has_state = states is not None
    m = nb * lt
    n_lt = seqlen // lt
    nsteps = (bsz // nb) * n_lt
    t_all = bsz * seqlen

    def tok_blk(b, i):
        return b * n_lt + i

    in_specs = [pl.BlockSpec((nb, lt, D), lambda b, i: (b, i, 0)),
                pl.BlockSpec((6, nb, 1, D), lambda b, i: (0, b, 0, 0))]
    args = [x, mod4]
    if has_state:
        in_specs += [pl.BlockSpec((nb, 2, D), lambda b, i: (b, 0, 0)),
                     pl.BlockSpec((nb, H, DKH, DVH), lambda b, i: (b, 0, 0, 0))]
        args += list(states)
    in_specs += [_const_spec(w.shape) for w in weights]
    args += list(weights)
    out_specs = [pl.BlockSpec((m, D), lambda b, i: (tok_blk(b, i), 0)),
                 pl.BlockSpec((m, D), lambda b, i: (tok_blk(b, i), 0)),
                 pl.BlockSpec((1, E, m), lambda b, i: (b * n_lt + i, 0, 0)),
                 pl.BlockSpec((1, nb, 2, D), lambda b, i: (0, b, 0, 0)),
                 pl.BlockSpec((1, nb, H, DKH, DVH), lambda b, i: (0, b, 0, 0, 0))]
    out_shape = [jax.ShapeDtypeStruct((t_all, D), f32),
                 jax.ShapeDtypeStruct((t_all, D), bf16),
                 jax.ShapeDtypeStruct((nsteps, E, m), f32),
                 jax.ShapeDtypeStruct((1, bsz, 2, D), f32),
                 jax.ShapeDtypeStruct((1, bsz, H, DKH, DVH), f32)]
    return pl.pallas_call(
        functools.partial(_mixer_body, nb, lt, chunk, has_state),
        grid=(bsz // nb, n_lt),
        in_specs=in_specs,
        out_specs=out_specs,
        out_shape=out_shape,
        scratch_shapes=[pltpu.VMEM((2, nb, 1, D), f32),
                        pltpu.VMEM((H, nb, DKH, DVH), f32),
                        pltpu.VMEM((nb, lt, DV), bf16),
                        pltpu.VMEM(((lt // chunk) * H * nb, DKH, DVH), bf16)],
        compiler_params=pltpu.CompilerParams(dimension_semantics=("arbitrary", "arbitrary"),
                                             vmem_limit_bytes=VMEM_LIMIT),
        name="mixer_state" if has_state else "mixer_prompt",
    )(*args)


def _route_body(lg_ref, bias_ref, eidx_ref, w_ref, lrank_ref, cnt_ref):
    step = pl.program_id(0)
    tl = lg_ref.shape[1]

    scores = jax.nn.sigmoid(lg_ref[...])
    choice = scores + bias_ref[...]
    gsz = E // NGROUPS
    grp = choice.reshape(NGROUPS, gsz, tl)
    mi = lax.broadcasted_iota(i32, (NGROUPS, gsz, tl), 1)
    m1 = jnp.max(grp, axis=1, keepdims=True)
    first = jnp.min(jnp.where(grp == m1, mi, gsz), axis=1, keepdims=True)
    m2 = jnp.max(jnp.where(mi == first, NEG, grp), axis=1, keepdims=True)
    gscore = (m1 + m2).reshape(NGROUPS, tl)
    gi = lax.broadcasted_iota(i32, (NGROUPS, tl), 0)
    gsel = jnp.zeros((NGROUPS, tl), f32)
    work = gscore
    for _ in range(TOPK_GROUPS):
        mx = jnp.max(work, axis=0, keepdims=True)
        f = jnp.min(jnp.where(work == mx, gi, NGROUPS), axis=0, keepdims=True)
        hit = gi == f
        gsel = jnp.where(hit, 1.0, gsel)
        work = jnp.where(hit, NEG, work)
    emask = jnp.broadcast_to(gsel.reshape(NGROUPS, 1, tl), (NGROUPS, gsz, tl)).reshape(E, tl)
    masked = jnp.where(emask > 0.0, choice, NEG)
    ei = lax.broadcasted_iota(i32, (E, tl), 0)
    sel = jnp.zeros((E, tl), f32)
    hits, ws = [], []
    for kk in range(TOPK):
        mx = jnp.max(masked, axis=0, keepdims=True)
        f = jnp.min(jnp.where(masked == mx, ei, E), axis=0, keepdims=True)
        hit = ei == f
        ws.append(jnp.sum(jnp.where(hit, scores, 0.0), axis=0, keepdims=True))
        masked = jnp.where(hit, NEG, masked)
        sel = jnp.where(hit, 1.0, sel)
        hits.append(hit)
        eidx_ref[kk:kk + 1, :] = f
    wsum = ws[0]
    for t in ws[1:]:
        wsum = wsum + t
    for kk in range(TOPK):
        w_ref[kk:kk + 1, :] = ws[kk] / wsum * ROUTED_SCALE

    ui = lax.broadcasted_iota(i32, (tl, tl), 0)
    uj = lax.broadcasted_iota(i32, (tl, tl), 1)
    upper = (ui <= uj).astype(bf16)
    lrank = _dot(sel.astype(bf16), upper) - sel
    for kk in range(TOPK):
        lrank_ref[kk:kk + 1, :] = jnp.sum(jnp.where(hits[kk], lrank, 0.0), axis=0, keepdims=True).astype(i32)

    @pl.when(step == 0)
    def _():
        cnt_ref[...] = jnp.zeros_like(cnt_ref)
    lane = lax.broadcasted_iota(i32, cnt_ref.shape, 1)
    cnt_ref[...] = jnp.where(lane == step, jnp.sum(sel, axis=1, keepdims=True), cnt_ref[...])


def _route(logits_t, bias_col):
    t_all = logits_t.shape[1]
    tl = TILE
    assert t_all // tl <= MAX_TILES
    tok = lambda s: (0, s)
    return pl.pallas_call(
        _route_body,
        grid=(t_all // tl,),
        in_specs=[pl.BlockSpec((E, tl), tok), pl.BlockSpec((E, 1), lambda s: (0, 0))],
        out_specs=[pl.BlockSpec((TOPK, tl), tok), pl.BlockSpec((TOPK, tl), tok), pl.BlockSpec((TOPK, tl), tok),
                   pl.BlockSpec((E, MAX_TILES), lambda s: (0, 0))],
        out_shape=[jax.ShapeDtypeStruct((TOPK, t_all), i32), jax.ShapeDtypeStruct((TOPK, t_all), f32),
                   jax.ShapeDtypeStruct((TOPK, t_all), i32), jax.ShapeDtypeStruct((E, MAX_TILES), f32)],
        compiler_params=pltpu.CompilerParams(dimension_semantics=("arbitrary",), vmem_limit_bytes=VMEM_LIMIT),
        name="route",
    )(logits_t, bias_col)


def _plan_body(eidx_ref, lrank_ref, cnt_ref, slot_ref, n8_ref, l_ref, g_ref, blk_ref):
    t_all = eidx_ref.shape[1]
    cnt = cnt_ref[...]
    n8 = jnp.floor((cnt + (RUN_ALIGN - 1)) * (1.0 / RUN_ALIGN)) * RUN_ALIGN
    ti = lax.broadcasted_iota(i32, (MAX_TILES, MAX_TILES), 0)
    tj = lax.broadcasted_iota(i32, (MAX_TILES, MAX_TILES), 1)
    before = _dotx01(n8, (ti < tj).astype(bf16))
    region = jnp.sum(n8, axis=1, keepdims=True)
    nblocks = jnp.floor((region + (MOE_BM - 1)) * (1.0 / MOE_BM))
    ri = lax.broadcasted_iota(i32, (E, E), 0)
    ci = lax.broadcasted_iota(i32, (E, E), 1)
    below = (ci < ri).astype(bf16)
    bstart = _dot01(below, jnp.broadcast_to(nblocks, (E, MAX_TILES)))
    loff = _dot01(below, n8)
    n8_ref[...] = n8.astype(i32)
    l_ref[...] = loff.astype(i32)
    g_ref[...] = (bstart * MOE_BM + before).astype(i32)
    lane0 = lax.broadcasted_iota(i32, (E, MAX_TILES), 1) == 0
    blk_ref[...] = jnp.where(lane0, bstart, nblocks).astype(i32)

    tile_of = lax.shift_right_logical(lax.broadcasted_iota(i32, (MAX_TILES, t_all), 1), TILE.bit_length() - 1)
    spread = (lax.broadcasted_iota(i32, (MAX_TILES, t_all), 0) == tile_of).astype(bf16)
    ltok = _dotx01(loff, spread)
    ei = lax.broadcasted_iota(i32, (E, t_all), 0)
    for kk in range(TOPK):
        hit = ei == eidx_ref[kk:kk + 1, :]
        start = jnp.sum(jnp.where(hit, ltok, 0.0), axis=0, keepdims=True)
        slot_ref[kk:kk + 1, :] = start.astype(i32) + lrank_ref[kk:kk + 1, :]


def _plan(eidx, lrank, cnt):
    t_all = eidx.shape[1]
    assert TILE & (TILE - 1) == 0
    tl = t_all
    tok = lambda s: (0, s)
    table = pl.BlockSpec((E, MAX_TILES), lambda s: (0, 0))
    return pl.pallas_call(
        _plan_body,
        grid=(1,),
        in_specs=[pl.BlockSpec((TOPK, tl), tok), pl.BlockSpec((TOPK, tl), tok), table],
        out_specs=[pl.BlockSpec((TOPK, tl), tok), table, table, table, table],
        out_shape=[jax.ShapeDtypeStruct((TOPK, t_all), i32)] + [jax.ShapeDtypeStruct((E, MAX_TILES), i32)] * 4,
        compiler_params=pltpu.CompilerParams(dimension_semantics=("arbitrary",), vmem_limit_bytes=VMEM_LIMIT),
        name="plan",
    )(eidx, lrank, cnt)


def _for_each_piece(total, pieces, fn):
    def piece(j, b):
        @pl.when((total & b) != 0)
        def _():
            fn(pl.multiple_of(total & ~(2 * b - 1), pieces[-1]), b, j)

    nbig = max(len(pieces) - 2, 0)

    @pl.when(total >= 4 * pieces[-1])
    def _():
        for j in range(nbig):
            piece(j, pieces[j])
    for j in range(nbig, len(pieces)):
        piece(j, pieces[j])


def _for_each_run(n8_ref, l_ref, g_ref, tile, fn):
    def per_expert(e, c):
        lo = pl.multiple_of(l_ref[e, tile], RUN_ALIGN)
        go = pl.multiple_of(g_ref[e, tile], RUN_ALIGN)
        _for_each_piece(n8_ref[e, tile], RUN_PIECES, lambda off, b, j: fn(lo + off, go + off, b, j))
        return c
    lax.fori_loop(0, E, per_expert, 0)


def _start(copy, j):
    copy.start(priority=j % 2)


def _wait_rows(make_copy, rows):
    _for_each_piece(rows, WAIT_PIECES, lambda off, b, j: make_copy(b).wait())


def _dispatch_body(tiles_a, n8_ref, l_ref, g_ref, slot_ref, ha_ref, hb_ref, xs_ref, buf_ref, zero_ref, sem, zsem):
    tile = pl.program_id(0)
    i = tile
    nsteps = pl.num_programs(0)
    cur = i % RING

    slot = slot_ref[...].astype(i16)
    h = jnp.where(tile < tiles_a, ha_ref[...], hb_ref[...])
    filled = l_ref[E - 1, tile] + n8_ref[E - 1, tile]

    def order(row0, nrows):
        srow = lax.broadcasted_iota(i32, (nrows, TILE), 0).astype(i16) + jnp.int16(row0)
        onehot = jnp.zeros((nrows, TILE), bf16)
        for kk in range(TOPK):
            onehot = jnp.where(srow == slot[kk:kk + 1, :], jnp.ones((), bf16), onehot)
        buf_ref[cur, row0:row0 + nrows, :] = _dot(onehot, h).astype(bf16)

    order(0, SLOTS_MAIN)

    @pl.when(filled > SLOTS_MAIN)
    def _():
        order(SLOTS_MAIN, SLOTS - SLOTS_MAIN)

    def run_copy(sl):
        return lambda srow_, grow, b: pltpu.make_async_copy(
            buf_ref.at[sl, pl.ds(srow_, b)], xs_ref.at[pl.ds(grow, b)], sem.at[sl])

    _for_each_run(n8_ref, l_ref, g_ref, tile, lambda s, g, b, j: _start(run_copy(cur)(s, g, b), j))

    def wait_tile(t, sl):
        _wait_rows(lambda b: run_copy(sl)(0, 0, b), l_ref[E - 1, t] + n8_ref[E - 1, t])

    @pl.when(i >= RING - 1)
    def _():
        wait_tile(tile - (RING - 1), (i + 1) % RING)

    last = nsteps - 1

    def zero_fill(fn):
        def per_expert(e, c):
            end = pl.multiple_of(g_ref[e, last] + n8_ref[e, last], RUN_ALIGN)
            _for_each_piece((-end) & (MOE_BM - 1), TAIL_PIECES, lambda off, b, j: fn(pltpu.make_async_copy(
                zero_ref.at[pl.ds(0, b)], xs_ref.at[pl.ds(end + off, b)], zsem)))
            return c
        lax.fori_loop(0, E, per_expert, 0)
        used = (g_ref[E - 1, last] + n8_ref[E - 1, last] + (MOE_BM - 1)) // MOE_BM

        def per_block(blk, c):
            fn(pltpu.make_async_copy(zero_ref, xs_ref.at[pl.ds(pl.multiple_of(blk * MOE_BM, MOE_BM), MOE_BM)], zsem))
            return c
        lax.fori_loop(used, xs_ref.shape[0] // MOE_BM, per_block, 0)

    @pl.when(i == 0)
    def _():
        zero_ref[...] = jnp.zeros_like(zero_ref)
        zero_fill(lambda cp: cp.start())

    @pl.when(i == last)
    def _():
        for back in range(RING - 2, -1, -1):
            @pl.when(i >= back)
            def _():
                wait_tile(tile - back, (i - back) % RING)
        zero_fill(lambda cp: cp.wait())


def _dispatch(tables, slot, h2_a, h2_b, nrows):
    tiles_a, tiles_b = h2_a.shape[0] // TILE, h2_b.shape[0] // TILE
    return pl.pallas_call(
        functools.partial(_dispatch_body, tiles_a),
        grid_spec=pltpu.PrefetchScalarGridSpec(
            num_scalar_prefetch=len(tables),
            grid=(tiles_a + tiles_b,),
            in_specs=[pl.BlockSpec((TOPK, TILE), lambda i, *_: (0, i)),
                      pl.BlockSpec((TILE, D), lambda i, *_: (jnp.minimum(i, tiles_a - 1), 0)),
                      pl.BlockSpec((TILE, D), lambda i, *_: (jnp.maximum(i - tiles_a, 0), 0))],
            out_specs=pl.BlockSpec(memory_space=pl.ANY),
            scratch_shapes=[pltpu.VMEM((RING, SLOTS, D), bf16), pltpu.VMEM((MOE_BM, D), bf16),
                            pltpu.SemaphoreType.DMA((RING,)), pltpu.SemaphoreType.DMA]),
        out_shape=jax.ShapeDtypeStruct((nrows, D), bf16),
        compiler_params=pltpu.CompilerParams(dimension_semantics=("arbitrary",), vmem_limit_bytes=VMEM_LIMIT),
        name="dispatch",
    )(*tables, slot, h2_a, h2_b)


def _expert_body(blk_ref, wg_ref, wu_ref, wd_ref, xs_ref, ys_ref, xbuf_ref, ybuf_ref, wgu_ref, wdb_ref, isem, osem):
    e = pl.program_id(0)
    first, count = blk_ref[e, 0], blk_ref[e, 1]
    used = blk_ref[E - 1, 0] + blk_ref[E - 1, 1]
    wgu_ref[:, :DE] = wg_ref[0].astype(bf16)
    wgu_ref[:, DE:] = wu_ref[0].astype(bf16)
    wdb_ref[...] = wd_ref[0].astype(bf16)

    def rows(g):
        return pl.ds(pl.multiple_of(g * MOE_BM, MOE_BM), MOE_BM)

    def load(g):
        sl = g % EXPERT_IN_SLOTS
        return pltpu.make_async_copy(xs_ref.at[rows(g)], xbuf_ref.at[sl], isem.at[sl])

    def store(g):
        sl = g % 2
        return pltpu.make_async_copy(ybuf_ref.at[sl], ys_ref.at[rows(g)], osem.at[sl])

    @pl.when(e == 0)
    def _():
        for g in range(EXPERT_IN_SLOTS - 1):
            @pl.when(g < used)
            def _():
                load(g).start()

    def block(j, c):
        g = first + j

        @pl.when(g + (EXPERT_IN_SLOTS - 1) < used)
        def _():
            load(g + (EXPERT_IN_SLOTS - 1)).start()
        load(g).wait()

        @pl.when(g >= 2)
        def _():
            store(g - 2).wait()
        hgu = _dot(xbuf_ref[g % EXPERT_IN_SLOTS], wgu_ref[...])
        hg = hgu[:, :DE]
        act = (hg * jax.nn.sigmoid(hg) * hgu[:, DE:]).astype(bf16)
        ybuf_ref[g % 2] = _dot(act, wdb_ref[...]).astype(bf16)
        store(g).start()
        return c
    lax.fori_loop(0, count, block, 0)

    @pl.when(e == E - 1)
    def _():
        for back in (2, 1):
            @pl.when(used >= back)
            def _():
                store(used - back).wait()


def _experts(blk, xs, we_gate, we_up, we_down):
    ex = lambda e, blk_: (e, 0, 0)
    return pl.pallas_call(
        _expert_body,
        grid_spec=pltpu.PrefetchScalarGridSpec(
            num_scalar_prefetch=1,
            grid=(E,),
            in_specs=[pl.BlockSpec((1, D, DE), ex), pl.BlockSpec((1, D, DE), ex), pl.BlockSpec((1, DE, D), ex),
                      pl.BlockSpec(memory_space=pl.ANY)],
            out_specs=pl.BlockSpec(memory_space=pl.ANY),
            scratch_shapes=[pltpu.VMEM((EXPERT_IN_SLOTS, MOE_BM, D), bf16), pltpu.VMEM((2, MOE_BM, D), bf16),
                            pltpu.VMEM((D, 2 * DE), bf16), pltpu.VMEM((DE, D), bf16),
                            pltpu.SemaphoreType.DMA((EXPERT_IN_SLOTS,)), pltpu.SemaphoreType.DMA((2,))]),
        out_shape=jax.ShapeDtypeStruct(xs.shape, xs.dtype),
        input_output_aliases={4: 0},
        compiler_params=pltpu.CompilerParams(dimension_semantics=("arbitrary",), vmem_limit_bytes=VMEM_LIMIT),
        name="experts",
    )(blk, we_gate, we_up, we_down, xs)


def _final_body(nb, lt, tile0, n8_ref, l_ref, g_ref, slot_ref, wt_ref, x2_ref, g2_ref, fw_ref, ys_ref, o_ref,
                buf_ref, routed_ref, sem):
    m = nb * lt
    n_inner = pl.num_programs(1)
    step = pl.program_id(0) * n_inner + pl.program_id(1)
    nsteps = pl.num_programs(0) * n_inner
    tile = tile0 + step
    cur = step % RING

    def run_copy(sl):
        return lambda srow, grow, b: pltpu.make_async_copy(
            ys_ref.at[pl.ds(grow, b)], buf_ref.at[sl, pl.ds(srow, b)], sem.at[sl])

    def fetch(ahead):
        sl = (step + ahead) % RING
        _for_each_run(n8_ref, l_ref, g_ref, tile + ahead, lambda s, g, b, j: _start(run_copy(sl)(s, g, b), j))

    @pl.when(step == 0)
    def _():
        buf_ref[...] = jnp.zeros_like(buf_ref)
        for ahead in range(RING - 1):
            @pl.when(ahead < nsteps)
            def _():
                fetch(ahead)

    @pl.when(step + (RING - 1) < nsteps)
    def _():
        fetch(RING - 1)

    _wait_rows(lambda b: run_copy(cur)(0, 0, b), l_ref[E - 1, tile] + n8_ref[E - 1, tile])

    slot = slot_ref[...].astype(i16)
    wt = wt_ref[...].astype(bf16)

    def combine(col0, ncols):
        scol = lax.broadcasted_iota(i32, (m, ncols), 1).astype(i16) + jnp.int16(col0)
        wsel = jnp.zeros((m, ncols), bf16)
        for kk in range(TOPK):
            wsel = jnp.where(scol == slot[:, kk:kk + 1], wt[:, kk:kk + 1], wsel)
        return _dot(wsel, buf_ref[cur, col0:col0 + ncols, :])

    routed_ref[...] = combine(0, SLOTS_MAIN)

    @pl.when(l_ref[E - 1, tile] + n8_ref[E - 1, tile] > SLOTS_MAIN)
    def _():
        routed_ref[...] += combine(SLOTS_MAIN, SLOTS - SLOTS_MAIN)
    routed = routed_ref[...]
    g2 = g2_ref[0]
    if nb == 1:
        g2 = g2.reshape(1, D)
    else:
        g2 = jnp.broadcast_to(g2, (nb, lt, D)).reshape(m, D)
    y = x2_ref[...] + g2 * routed
    o_ref[...] = _rms(y, fw_ref[...]).reshape(nb, lt, D)


def _final(tables, slot_t, wts_t, x2, ys, mod4, fw, bsz, seqlen, row0, nb, lt):
    m = nb * lt
    assert m == TILE and row0 % TILE == 0
    n_lt = seqlen // lt
    tile0 = row0 // TILE
    tile = lambda b, i: tile0 + b * n_lt + i
    return pl.pallas_call(
        functools.partial(_final_body, nb, lt, tile0),
        grid_spec=pltpu.PrefetchScalarGridSpec(
            num_scalar_prefetch=len(tables),
            grid=(bsz // nb, n_lt),
            in_specs=[pl.BlockSpec((m, TOPK), lambda b, i, *_: (tile(b, i), 0)),
                      pl.BlockSpec((m, TOPK), lambda b, i, *_: (tile(b, i), 0)),
                      pl.BlockSpec((m, D), lambda b, i, *_: (b * n_lt + i, 0)),
                      pl.BlockSpec((1, nb, 1, D), lambda b, i, *_: (5, b, 0, 0)),
                      pl.BlockSpec((1, D), lambda b, i, *_: (0, 0)),
                      pl.BlockSpec(memory_space=pl.ANY)],
            out_specs=pl.BlockSpec((nb, lt, D), lambda b, i, *_: (b, i, 0)),
            scratch_shapes=[pltpu.VMEM((RING, SLOTS, D), bf16), pltpu.VMEM((m, D), f32),
                            pltpu.SemaphoreType.DMA((RING,))]),
        out_shape=jax.ShapeDtypeStruct((bsz, seqlen, D), f32),
        compiler_params=pltpu.CompilerParams(dimension_semantics=("arbitrary", "arbitrary"),
                                             vmem_limit_bytes=VMEM_LIMIT),
        name="final",
    )(*tables, slot_t, wts_t, x2, mod4, fw, ys)


def kernel(x_prompt, x_sample, state_conv, state_gla, c_prompt, c_sample, w_ada, b_ada, norm1_w, w_in, conv_w,
           w_gk, b_gk, gla_norm_w, w_out_conv, w_out_gla, w_o, norm2_w, router_w, router_bias, we_gate, we_up,
           we_down, ws_gate, ws_up, ws_down, final_norm_w):
    assert w_ada.shape[0] == 1, "single-layer step"
    bp, lp, _ = x_prompt.shape
    bs, ls, _ = x_sample.shape
    tp, ts = bp * lp, bs * ls
    t_all = tp + ts

    w_in0 = w_in[0]
    n_main = 3 * D + 2 * DK + 2 * DV
    rank = w_gk.shape[1]
    weights = (
        norm1_w[0].reshape(1, D),
        w_in0[:, :n_main].astype(bf16),
        jnp.pad(w_in0[:, n_main:n_main + rank], ((0, 0), (0, RANK_PAD - rank))).astype(bf16),
        w_in0[:, n_main + rank:].astype(bf16),
        jnp.pad(w_gk[0], ((0, RANK_PAD - rank), (0, 0))).astype(bf16),
        b_gk[0].reshape(1, DK),
        conv_w[0],
        gla_norm_w[0].reshape(1, DVH),
        w_out_conv[0].astype(bf16),
        w_out_gla[0].astype(bf16),
        w_o[0].astype(bf16),
        norm2_w[0].reshape(1, D),
        router_w[0].T.astype(bf16),
        ws_gate[0].astype(bf16),
        ws_up[0].astype(bf16),
        ws_down[0].astype(bf16),
    )

    mod = _adaln(jnp.concatenate([c_prompt, c_sample], axis=0), w_ada[0], b_ada[0].reshape(1, 6 * D))
    mod4 = mod.reshape(bp + bs, 6, 1, D).transpose(1, 0, 2, 3)
    mod_p, mod_s = mod4[:, :bp], mod4[:, bp:]

    lt_p = 512
    x2_p, h2_p, lg_p, conv_p, gla_p = _mixer(x_prompt, mod_p, None, weights, 1, lt_p, GLA_CHUNK)
    nb_s = 8
    x2_s, h2_s, lg_s, conv_s, gla_s = _mixer(x_sample, mod_s, (state_conv[0], state_gla[0]), weights,
                                             nb_s, ls, math.gcd(ls, GLA_CHUNK))
    lg = jnp.concatenate([lg_p.transpose(1, 0, 2).reshape(E, tp), lg_s.transpose(1, 0, 2).reshape(E, ts)], axis=1)

    eidx, wts, lrank, cnt = _route(lg, router_bias[0].reshape(E, 1))
    n_tiles = t_all // TILE
    max_rows = t_all * TOPK + n_tiles * E * (RUN_ALIGN - 1) + E * (MOE_BM - RUN_ALIGN)
    nblk = -(-max_rows // MOE_BM)
    slot, n8, loff, goff, blk = _plan(eidx, lrank, cnt)
    tables = (n8, loff, goff)

    xs = _dispatch(tables, slot, h2_p, h2_s, nblk * MOE_BM)
    ys = _experts(blk, xs, we_gate[0], we_up[0], we_down[0])

    fw = final_norm_w.reshape(1, D)
    slot_t, wts_t = slot.T, wts.T
    y_prompt = _final(tables, slot_t, wts_t, x2_p, ys, mod_p, fw, bp, lp, 0, 1, TILE)
    y_sample = _final(tables, slot_t, wts_t, x2_s, ys, mod_s, fw, bs, ls, tp, TILE // ls, ls)
    return (y_prompt, y_sample, conv_p, gla_p, conv_s, gla_s)
```

```python
import functools
import math

import jax
import jax.numpy as jnp
from jax import lax
from jax.experimental import pallas as pl
from jax.experimental.pallas import tpu as pltpu

f32 = jnp.float32
bf16 = jnp.bfloat16
i32 = jnp.int32
i16 = jnp.int16

D = 1024
H = 4
DKH = 128
DVH = 256
DK = H * DKH
DV = H * DVH
RANK_PAD = 128
GATE_NORMALIZER = 16.0
GLA_CHUNK = 64
E = 64
TOPK = 8
NGROUPS = 8
TOPK_GROUPS = 4
DE = 256
ROUTED_SCALE = 2.5
EPS = 1e-6
NEG = float("-inf")

V7X_VMEM_BYTES = 64 * 1024 * 1024
VMEM_LIMIT = V7X_VMEM_BYTES - 6 * 1024 * 1024
MOE_BM = 512
EXPERT_IN_SLOTS = 3
RING = 2
TILE = 256
MAX_TILES = 128
RUN_ALIGN = 16
SLOTS = TILE * TOPK + 1024
SLOTS_MAIN = TILE * TOPK + 512
RUN_PIECES = (256, 128, 64, 32, 16)
TAIL_PIECES = (256, 128, 64, 32, 16)
WAIT_PIECES = (2048, 1024, 512, 256, 128, 64, 32, 16)

def _dot(a, b):
    return jnp.dot(a, b, preferred_element_type=f32)


def _split3(x):
    hi = x.astype(bf16)
    r1 = x - hi.astype(f32)
    mid = r1.astype(bf16)
    lo = (r1 - mid.astype(f32)).astype(bf16)
    return hi, mid, lo


def _dot01(m01, x):
    hi, mid, lo = _split3(x)
    return _dot(m01, hi) + _dot(m01, mid) + _dot(m01, lo)


def _dotx01(x, m01):
    hi, mid, lo = _split3(x)
    return _dot(hi, m01) + _dot(mid, m01) + _dot(lo, m01)


def _rms(v, w):
    ms = jnp.mean(v * v, axis=-1, keepdims=True)
    return v * lax.rsqrt(ms + EPS) * w


def _const_spec(shape):
    n = len(shape)
    return pl.BlockSpec(shape, lambda *_: (0,) * n, pipeline_mode=pl.Buffered(1))


def _adaln_body(c_ref, w_ref, b_ref, o_ref):
    c = c_ref[...]
    a = (c * jax.nn.sigmoid(c)).astype(bf16)
    o_ref[...] = _dot(a, w_ref[...].astype(bf16)) + b_ref[...]


def _adaln(c_all, w_ada, b_ada):
    nrow = c_all.shape[0]
    tn = 1536
    return pl.pallas_call(
        _adaln_body,
        grid=(6 * D // tn,),
        in_specs=[pl.BlockSpec((nrow, D), lambda j: (0, 0)),
                  pl.BlockSpec((D, tn), lambda j: (0, j)),
                  pl.BlockSpec((1, tn), lambda j: (0, j))],
        out_specs=pl.BlockSpec((nrow, tn), lambda j: (0, j)),
        out_shape=jax.ShapeDtypeStruct((nrow, 6 * D), f32),
        compiler_params=pltpu.CompilerParams(dimension_semantics=("arbitrary",), vmem_limit_bytes=VMEM_LIMIT),
        name="adaln",
    )(c_all, w_ada, b_ada)


def _mixer_body(nb, lt, chunk, has_state, *refs):
    if has_state:
        (x_ref, mod_ref, cs_ref, gs_ref, *rest) = refs
    else:
        (x_ref, mod_ref, *rest) = refs
        cs_ref = gs_ref = None
    (n1_ref, wmain_ref, wz_ref, wu_ref, wgk_ref, bgk_ref, cw_ref, gnw_ref, woc_ref, wog_ref, wo_ref,
     n2_ref, rwt_ref, wsg_ref, wsu_ref, wsd_ref,
     x2_ref, h2_ref, lg_ref, nc_ref, ng_ref, carry_ref, st_ref, og_ref, sin_ref) = rest
    m = nb * lt
    i = pl.program_id(1)
    last = pl.num_programs(1) - 1

    def rows(v):
        w = v.shape[-1]
        if nb == 1:
            return v.reshape(1, w)
        return jnp.broadcast_to(v, (nb, lt, w)).reshape(m, w)

    @pl.when(i == 0)
    def _():
        if has_state:
            cs = cs_ref[...]
            carry_ref[0] = cs[:, 0:1, :]
            carry_ref[1] = cs[:, 1:2, :]
            for hh in range(H):
                st_ref[hh] = gs_ref[:, hh]
        else:
            carry_ref[...] = jnp.zeros_like(carry_ref)
            st_ref[...] = jnp.zeros_like(st_ref)

    x = x_ref[...].reshape(m, D)
    sh1, sc1, g1, sh2, sc2, g2 = [rows(mod_ref[j]) for j in range(6)]
    hb = (_rms(x, n1_ref[...]) * (1.0 + sc1) + sh1).astype(bf16)

    def proj(lo, hi):
        return _dot(hb, wmain_ref[:, lo:hi])

    pre = proj(D, 2 * D) * proj(2 * D, 3 * D)
    c0 = rows(carry_ref[0])
    c1 = rows(carry_ref[1])
    l_idx = lax.broadcasted_iota(i32, (m, D), 0) & (lt - 1)
    r1 = pltpu.roll(pre, 1, 0)
    r2 = pltpu.roll(pre, 2, 0)
    prev1 = jnp.where(l_idx == 0, c1, r1)
    prev2 = jnp.where(l_idx == 0, c0, jnp.where(l_idx == 1, c1, r2))
    cw = cw_ref[...]
    conv = cw[0:1] * prev2 + cw[1:2] * prev1 + cw[2:3] * pre
    y_a = _dot((proj(0, D) * conv).astype(bf16), woc_ref[...])
    pre3 = pre.reshape(nb, lt, D)
    tail = pre3[:, lt - 2:lt, :]
    carry_ref[0] = tail[:, 0:1, :]
    carry_ref[1] = tail[:, 1:2, :]
    nc_ref[0] = tail

    q = proj(3 * D, 3 * D + DK)
    k = proj(3 * D + DK, 3 * D + 2 * DK)
    v = proj(4 * D, 5 * D)
    g = proj(5 * D, 6 * D)
    z = _dot(hb, wz_ref[...]).astype(bf16)
    pa = _dot(z, wgk_ref[...]) + bgk_ref[...]
    la = (jnp.minimum(pa, 0.0) - jnp.log1p(jnp.exp(-jnp.abs(pa)))) / GATE_NORMALIZER
    pos = lax.broadcasted_iota(i32, (m, DK), 0) & (chunk - 1)
    bcum = la
    shift = 1
    while shift < chunk:
        bcum = bcum + jnp.where(pos >= shift, pltpu.roll(bcum, shift, 0), 0.0)
        shift *= 2
    ngrp = m // chunk
    bend = bcum.reshape(ngrp, chunk, DK)[:, chunk - 1:chunk, :]
    bend_rows = jnp.broadcast_to(bend, (ngrp, chunk, DK)).reshape(m, DK)
    nch = lt // chunk
    assert nb == 1 or nch == 1
    hs = H * nb
    nbat = nch * hs

    def per_head(xm, w):
        x3 = xm.reshape(nb * nch, chunk, H * w)
        parts = [x3[:, :, hh * w:(hh + 1) * w] for hh in range(H)]
        return jnp.stack(parts, axis=1 if nb == 1 else 0).reshape(nbat, chunk, w)

    qd = per_head((q * (DKH ** -0.5) * jnp.exp(bcum)).astype(bf16), DKH)
    kd = per_head((k * jnp.exp(-bcum)).astype(bf16), DKH)
    ke = per_head((k * jnp.exp(bend_rows - bcum)).astype(bf16), DKH)
    vv = per_head(v.astype(bf16), DVH)
    assert nbat <= 128
    a_end = per_head(jnp.exp(bend_rows), DKH)[:, 0, :]
    a_col = jnp.concatenate([a_end, jnp.zeros((128 - nbat, DKH), f32)], axis=0).T
    causal = lax.broadcasted_iota(i32, (chunk, chunk), 1) <= lax.broadcasted_iota(i32, (chunk, chunk), 0)
    sc = jnp.einsum('bcd,bsd->bcs', qd, kd, preferred_element_type=f32)
    sc = jnp.where(causal, sc, 0.0).astype(bf16)
    o = jnp.einsum('bcs,bse->bce', sc, vv, preferred_element_type=f32)
    upd = jnp.einsum('bsd,bse->bde', ke, vv, preferred_element_type=f32)
    st = [st_ref[s // nb, s % nb] for s in range(hs)]
    for j in range(nch):
        for s in range(hs):
            b = j * hs + s
            sin_ref[b] = st[s].astype(bf16)
            st[s] = st[s] * a_col[:, b:b + 1] + upd[b]
    for s in range(hs):
        st_ref[s // nb, s % nb] = st[s]
    o = o + jnp.einsum('bcd,bde->bce', qd, sin_ref[...], preferred_element_type=f32)
    on = _rms(o, gnw_ref[...])
    g3 = g.reshape(nb, lt, DV)
    for j in range(nch):
        js = slice(j * chunk, (j + 1) * chunk)
        for hh in range(H):
            vs = slice(hh * DVH, (hh + 1) * DVH)
            gh = g3[:, js, vs]
            b0 = (j * H + hh) * nb
            og_ref[:, js, vs] = (on[b0:b0 + nb] * (gh * jax.nn.sigmoid(gh))).astype(bf16)

    @pl.when(i == last)
    def _():
        for hh in range(H):
            ng_ref[0, :, hh] = st_ref[hh]

    y_b = _dot(og_ref[...].reshape(m, DV), wog_ref[...])
    u_a = _dot(hb, wu_ref[:, 0:D])
    u_b = _dot(hb, wu_ref[:, D:2 * D])
    merged = (jax.nn.sigmoid(u_a) * y_a + jax.nn.sigmoid(u_b) * y_b).astype(bf16)
    x1 = x + g1 * _dot(merged, wo_ref[...])

    h2 = (_rms(x1, n2_ref[...]) * (1.0 + sc2) + sh2).astype(bf16)
    sg = _dot(h2, wsg_ref[...])
    act = (sg * jax.nn.sigmoid(sg) * _dot(h2, wsu_ref[...])).astype(bf16)
    x2_ref[...] = x1 + g2 * _dot(act, wsd_ref[...])
    h2_ref[...] = h2
    lg_ref[0] = lax.dot_general(rwt_ref[...], h2, (((1,), (1,)), ((), ())), preferred_element_type=f32)


def _mixer(x, mod4, states, weights, nb, lt, chunk):
    bsz, seqlen, _ = x.shape
    has_state = states is not None
    m = nb * lt
    n_lt = seqlen // lt
    nsteps = (bsz // nb) * n_lt
    t_all = bsz * seqlen

    def tok_blk(b, i):
        return b * n_lt + i

    in_specs = [pl.BlockSpec((nb, lt, D), lambda b, i: (b, i, 0)),
                pl.BlockSpec((6, nb, 1, D), lambda b, i: (0, b, 0, 0))]
    args = [x, mod4]
    if has_state:
        in_specs += [pl.BlockSpec((nb, 2, D), lambda b, i: (b, 0, 0)),
                     pl.BlockSpec((nb, H, DKH, DVH), lambda b, i: (b, 0, 0, 0))]
        args += list(states)
    in_specs += [_const_spec(w.shape) for w in weights]
    args += list(weights)
    out_specs = [pl.BlockSpec((m, D), lambda b, i: (tok_blk(b, i), 0)),
                 pl.BlockSpec((m, D), lambda b, i: (tok_blk(b, i), 0)),
                 pl.BlockSpec((1, E, m), lambda b, i: (b * n_lt + i, 0, 0)),
                 pl.BlockSpec((1, nb, 2, D), lambda b, i: (0, b, 0, 0)),
                 pl.BlockSpec((1, nb, H, DKH, DVH), lambda b, i: (0, b, 0, 0, 0))]
    out_shape = [jax.ShapeDtypeStruct((t_all, D), f32),
                 jax.ShapeDtypeStruct((t_all, D), bf16),
                 jax.ShapeDtypeStruct((nsteps, E, m), f32),
                 jax.ShapeDtypeStruct((1, bsz, 2, D), f32),
                 jax.ShapeDtypeStruct((1, bsz, H, DKH, DVH), f32)]
    return pl.pallas_call(
        functools.partial(_mixer_body, nb, lt, chunk, has_state),
        grid=(bsz // nb, n_lt),
        in_specs=in_specs,
        out_specs=out_specs,
        out_shape=out_shape,
        scratch_shapes=[pltpu.VMEM((2, nb, 1, D), f32),
                        pltpu.VMEM((H, nb, DKH, DVH), f32),
                        pltpu.VMEM((nb, lt, DV), bf16),
                        pltpu.VMEM(((lt // chunk) * H * nb, DKH, DVH), bf16)],
        compiler_params=pltpu.CompilerParams(dimension_semantics=("arbitrary", "arbitrary"),
                                             vmem_limit_bytes=VMEM_LIMIT),
        name="mixer_state" if has_state else "mixer_prompt",
    )(*args)


def _route_body(lg_ref, bias_ref, eidx_ref, w_ref, lrank_ref, cnt_ref):
    step = pl.program_id(0)
    tl = lg_ref.shape[1]

    scores = jax.nn.sigmoid(lg_ref[...])
    choice = scores + bias_ref[...]
    gsz = E // NGROUPS
    grp = choice.reshape(NGROUPS, gsz, tl)
    mi = lax.broadcasted_iota(i32, (NGROUPS, gsz, tl), 1)
    m1 = jnp.max(grp, axis=1, keepdims=True)
    first = jnp.min(jnp.where(grp == m1, mi, gsz), axis=1, keepdims=True)
    m2 = jnp.max(jnp.where(mi == first, NEG, grp), axis=1, keepdims=True)
    gscore = (m1 + m2).reshape(NGROUPS, tl)
    gi = lax.broadcasted_iota(i32, (NGROUPS, tl), 0)
    gsel = jnp.zeros((NGROUPS, tl), f32)
    work = gscore
    for _ in range(TOPK_GROUPS):
        mx = jnp.max(work, axis=0, keepdims=True)
        f = jnp.min(jnp.where(work == mx, gi, NGROUPS), axis=0, keepdims=True)
        hit = gi == f
        gsel = jnp.where(hit, 1.0, gsel)
        work = jnp.where(hit, NEG, work)
    emask = jnp.broadcast_to(gsel.reshape(NGROUPS, 1, tl), (NGROUPS, gsz, tl)).reshape(E, tl)
    masked = jnp.where(emask > 0.0, choice, NEG)
    ei = lax.broadcasted_iota(i32, (E, tl), 0)
    sel = jnp.zeros((E, tl), f32)
    hits, ws = [], []
    for kk in range(TOPK):
        mx = jnp.max(masked, axis=0, keepdims=True)
        f = jnp.min(jnp.where(masked == mx, ei, E), axis=0, keepdims=True)
        hit = ei == f
        ws.append(jnp.sum(jnp.where(hit, scores, 0.0), axis=0, keepdims=True))
        masked = jnp.where(hit, NEG, masked)
        sel = jnp.where(hit, 1.0, sel)
        hits.append(hit)
        eidx_ref[kk:kk + 1, :] = f
    wsum = ws[0]
    for t in ws[1:]:
        wsum = wsum + t
    for kk in range(TOPK):
        w_ref[kk:kk + 1, :] = ws[kk] / wsum * ROUTED_SCALE

    ui = lax.broadcasted_iota(i32, (tl, tl), 0)
    uj = lax.broadcasted_iota(i32, (tl, tl), 1)
    upper = (ui <= uj).astype(bf16)
    lrank = _dot(sel.astype(bf16), upper) - sel
    for kk in range(TOPK):
        lrank_ref[kk:kk + 1, :] = jnp.sum(jnp.where(hits[kk], lrank, 0.0), axis=0, keepdims=True).astype(i32)

    @pl.when(step == 0)
    def _():
        cnt_ref[...] = jnp.zeros_like(cnt_ref)
    lane = lax.broadcasted_iota(i32, cnt_ref.shape, 1)
    cnt_ref[...] = jnp.where(lane == step, jnp.sum(sel, axis=1, keepdims=True), cnt_ref[...])


def _route(logits_t, bias_col):
    t_all = logits_t.shape[1]
    tl = TILE
    assert t_all // tl <= MAX_TILES
    tok = lambda s: (0, s)
    return pl.pallas_call(
        _route_body,
        grid=(t_all // tl,),
        in_specs=[pl.BlockSpec((E, tl), tok), pl.BlockSpec((E, 1), lambda s: (0, 0))],
        out_specs=[pl.BlockSpec((TOPK, tl), tok), pl.BlockSpec((TOPK, tl), tok), pl.BlockSpec((TOPK, tl), tok),
                   pl.BlockSpec((E, MAX_TILES), lambda s: (0, 0))],
        out_shape=[jax.ShapeDtypeStruct((TOPK, t_all), i32), jax.ShapeDtypeStruct((TOPK, t_all), f32),
                   jax.ShapeDtypeStruct((TOPK, t_all), i32), jax.ShapeDtypeStruct((E, MAX_TILES), f32)],
        compiler_params=pltpu.CompilerParams(dimension_semantics=("arbitrary",), vmem_limit_bytes=VMEM_LIMIT),
        name="route",
    )(logits_t, bias_col)


def _plan_body(eidx_ref, lrank_ref, cnt_ref, slot_ref, n_ref, l_ref, g_ref, blk_ref):
    t_all = eidx_ref.shape[1]
    cnt = cnt_ref[...]
    nrun = jnp.floor((cnt + (RUN_ALIGN - 1)) * (1.0 / RUN_ALIGN)) * RUN_ALIGN
    ti = lax.broadcasted_iota(i32, (MAX_TILES, MAX_TILES), 0)
    tj = lax.broadcasted_iota(i32, (MAX_TILES, MAX_TILES), 1)
    before = _dotx01(nrun, (ti < tj).astype(bf16))
    region = jnp.sum(nrun, axis=1, keepdims=True)
    nblocks = jnp.floor((region + (MOE_BM - 1)) * (1.0 / MOE_BM))
    ri = lax.broadcasted_iota(i32, (E, E), 0)
    ci = lax.broadcasted_iota(i32, (E, E), 1)
    below = (ci < ri).astype(bf16)
    bstart = _dot01(below, jnp.broadcast_to(nblocks, (E, MAX_TILES)))
    loff = _dot01(below, nrun)
    n_ref[...] = nrun.astype(i32)
    l_ref[...] = loff.astype(i32)
    g_ref[...] = (bstart * MOE_BM + before).astype(i32)
    lane0 = lax.broadcasted_iota(i32, (E, MAX_TILES), 1) == 0
    blk_ref[...] = jnp.where(lane0, bstart, nblocks).astype(i32)

    tile_of = lax.shift_right_logical(lax.broadcasted_iota(i32, (MAX_TILES, t_all), 1), TILE.bit_length() - 1)
    spread = (lax.broadcasted_iota(i32, (MAX_TILES, t_all), 0) == tile_of).astype(bf16)
    ltok = _dotx01(loff, spread)
    ei = lax.broadcasted_iota(i32, (E, t_all), 0)
    for kk in range(TOPK):
        hit = ei == eidx_ref[kk:kk + 1, :]
        start = jnp.sum(jnp.where(hit, ltok, 0.0), axis=0, keepdims=True)
        slot_ref[kk:kk + 1, :] = start.astype(i32) + lrank_ref[kk:kk + 1, :]


def _plan(eidx, lrank, cnt):
    t_all = eidx.shape[1]
    assert TILE & (TILE - 1) == 0
    tl = t_all
    tok = lambda s: (0, s)
    table = pl.BlockSpec((E, MAX_TILES), lambda s: (0, 0))
    return pl.pallas_call(
        _plan_body,
        grid=(1,),
        in_specs=[pl.BlockSpec((TOPK, tl), tok), pl.BlockSpec((TOPK, tl), tok), table],
        out_specs=[pl.BlockSpec((TOPK, tl), tok), table, table, table, table],
        out_shape=[jax.ShapeDtypeStruct((TOPK, t_all), i32)] + [jax.ShapeDtypeStruct((E, MAX_TILES), i32)] * 4,
        compiler_params=pltpu.CompilerParams(dimension_semantics=("arbitrary",), vmem_limit_bytes=VMEM_LIMIT),
        name="plan",
    )(eidx, lrank, cnt)


def _for_each_piece(total, pieces, fn):
    for j, b in enumerate(pieces):
        @pl.when((total & b) != 0)
        def _():
            fn(pl.multiple_of(total & ~(2 * b - 1), pieces[-1]), b, j)


def _for_each_run(n_ref, l_ref, g_ref, tile, fn):
    def per_expert(e, c):
        lo = pl.multiple_of(l_ref[e, tile], RUN_ALIGN)
        go = pl.multiple_of(g_ref[e, tile], RUN_ALIGN)
        _for_each_piece(n_ref[e, tile], RUN_PIECES, lambda off, b, j: fn(lo + off, go + off, b, j))
        return c
    lax.fori_loop(0, E, per_expert, 0)


def _start(copy, j):
    copy.start(priority=j % 2)


def _wait_rows(make_copy, rows):
    _for_each_piece(rows, WAIT_PIECES, lambda off, b, j: make_copy(b).wait())


def _dispatch_body(tiles_a, n_ref, l_ref, g_ref, slot_ref, ha_ref, hb_ref, xs_ref, buf_ref, zero_ref, sem, zsem):
    tile = pl.program_id(0)
    nsteps = pl.num_programs(0)
    cur = tile % RING

    slot = slot_ref[...].astype(i16)
    h = jnp.where(tile < tiles_a, ha_ref[...], hb_ref[...])
    filled = l_ref[E - 1, tile] + n_ref[E - 1, tile]

    def order(row0, nrows):
        srow = lax.broadcasted_iota(i32, (nrows, TILE), 0).astype(i16) + jnp.int16(row0)
        onehot = jnp.zeros((nrows, TILE), bf16)
        for kk in range(TOPK):
            onehot = jnp.where(srow == slot[kk:kk + 1, :], jnp.ones((), bf16), onehot)
        buf_ref[cur, row0:row0 + nrows, :] = _dot(onehot, h).astype(bf16)

    order(0, SLOTS_MAIN)

    @pl.when(filled > SLOTS_MAIN)
    def _():
        order(SLOTS_MAIN, SLOTS - SLOTS_MAIN)

    def run_copy(sl):
        return lambda srow_, grow, b: pltpu.make_async_copy(
            buf_ref.at[sl, pl.ds(srow_, b)], xs_ref.at[pl.ds(grow, b)], sem.at[sl])

    _for_each_run(n_ref, l_ref, g_ref, tile, lambda s, g, b, j: _start(run_copy(cur)(s, g, b), j))

    def wait_tile(t, sl):
        _wait_rows(lambda b: run_copy(sl)(0, 0, b), l_ref[E - 1, t] + n_ref[E - 1, t])

    @pl.when(tile >= RING - 1)
    def _():
        wait_tile(tile - (RING - 1), (tile + 1) % RING)

    last = nsteps - 1

    def zero_fill(fn):
        def per_expert(e, c):
            end = pl.multiple_of(g_ref[e, last] + n_ref[e, last], RUN_ALIGN)
            _for_each_piece((-end) & (MOE_BM - 1), TAIL_PIECES, lambda off, b, j: fn(pltpu.make_async_copy(
                zero_ref.at[pl.ds(0, b)], xs_ref.at[pl.ds(end + off, b)], zsem)))
            return c
        lax.fori_loop(0, E, per_expert, 0)
        used = (g_ref[E - 1, last] + n_ref[E - 1, last] + (MOE_BM - 1)) // MOE_BM

        def per_block(blk, c):
            fn(pltpu.make_async_copy(zero_ref, xs_ref.at[pl.ds(pl.multiple_of(blk * MOE_BM, MOE_BM), MOE_BM)], zsem))
            return c
        lax.fori_loop(used, xs_ref.shape[0] // MOE_BM, per_block, 0)

    @pl.when(tile == 0)
    def _():
        zero_ref[...] = jnp.zeros_like(zero_ref)
        zero_fill(lambda cp: cp.start())

    @pl.when(tile == last)
    def _():
        for back in range(RING - 2, -1, -1):
            @pl.when(tile >= back)
            def _():
                wait_tile(tile - back, (tile - back) % RING)
        zero_fill(lambda cp: cp.wait())


def _dispatch(tables, slot, h2_a, h2_b, nrows):
    tiles_a, tiles_b = h2_a.shape[0] // TILE, h2_b.shape[0] // TILE
    return pl.pallas_call(
        functools.partial(_dispatch_body, tiles_a),
        grid_spec=pltpu.PrefetchScalarGridSpec(
            num_scalar_prefetch=len(tables),
            grid=(tiles_a + tiles_b,),
            in_specs=[pl.BlockSpec((TOPK, TILE), lambda i, *_: (0, i)),
                      pl.BlockSpec((TILE, D), lambda i, *_: (jnp.minimum(i, tiles_a - 1), 0)),
                      pl.BlockSpec((TILE, D), lambda i, *_: (jnp.maximum(i - tiles_a, 0), 0))],
            out_specs=pl.BlockSpec(memory_space=pl.ANY),
            scratch_shapes=[pltpu.VMEM((RING, SLOTS, D), bf16), pltpu.VMEM((MOE_BM, D), bf16),
                            pltpu.SemaphoreType.DMA((RING,)), pltpu.SemaphoreType.DMA]),
        out_shape=jax.ShapeDtypeStruct((nrows, D), bf16),
        compiler_params=pltpu.CompilerParams(dimension_semantics=("arbitrary",), vmem_limit_bytes=VMEM_LIMIT),
        name="dispatch",
    )(*tables, slot, h2_a, h2_b)


def _expert_body(blk_ref, wg_ref, wu_ref, wd_ref, xs_ref, ys_ref, xbuf_ref, ybuf_ref, wgu_ref, wdb_ref, isem, osem):
    e = pl.program_id(0)
    first, count = blk_ref[e, 0], blk_ref[e, 1]
    used = blk_ref[E - 1, 0] + blk_ref[E - 1, 1]
    wgu_ref[:, :DE] = wg_ref[0].astype(bf16)
    wgu_ref[:, DE:] = wu_ref[0].astype(bf16)
    wdb_ref[...] = wd_ref[0].astype(bf16)

    def rows(g):
        return pl.ds(pl.multiple_of(g * MOE_BM, MOE_BM), MOE_BM)

    def load(g):
        sl = g % EXPERT_IN_SLOTS
        return pltpu.make_async_copy(xs_ref.at[rows(g)], xbuf_ref.at[sl], isem.at[sl])

    def store(g):
        sl = g % 2
        return pltpu.make_async_copy(ybuf_ref.at[sl], ys_ref.at[rows(g)], osem.at[sl])

    @pl.when(e == 0)
    def _():
        for g in range(EXPERT_IN_SLOTS - 1):
            @pl.when(g < used)
            def _():
                load(g).start()

    def block(j, c):
        g = first + j

        @pl.when(g + (EXPERT_IN_SLOTS - 1) < used)
        def _():
            load(g + (EXPERT_IN_SLOTS - 1)).start()
        load(g).wait()

        @pl.when(g >= 2)
        def _():
            store(g - 2).wait()
        hgu = _dot(xbuf_ref[g % EXPERT_IN_SLOTS], wgu_ref[...])
        hg = hgu[:, :DE]
        act = (hg * jax.nn.sigmoid(hg) * hgu[:, DE:]).astype(bf16)
        ybuf_ref[g % 2] = _dot(act, wdb_ref[...]).astype(bf16)
        store(g).start()
        return c
    lax.fori_loop(0, count, block, 0)

    @pl.when(e == E - 1)
    def _():
        for back in (2, 1):
            @pl.when(used >= back)
            def _():
                store(used - back).wait()


def _experts(blk, xs, we_gate, we_up, we_down):
    ex = lambda e, blk_: (e, 0, 0)
    return pl.pallas_call(
        _expert_body,
        grid_spec=pltpu.PrefetchScalarGridSpec(
            num_scalar_prefetch=1,
            grid=(E,),
            in_specs=[pl.BlockSpec((1, D, DE), ex), pl.BlockSpec((1, D, DE), ex), pl.BlockSpec((1, DE, D), ex),
                      pl.BlockSpec(memory_space=pl.ANY)],
            out_specs=pl.BlockSpec(memory_space=pl.ANY),
            scratch_shapes=[pltpu.VMEM((EXPERT_IN_SLOTS, MOE_BM, D), bf16), pltpu.VMEM((2, MOE_BM, D), bf16),
                            pltpu.VMEM((D, 2 * DE), bf16), pltpu.VMEM((DE, D), bf16),
                            pltpu.SemaphoreType.DMA((EXPERT_IN_SLOTS,)), pltpu.SemaphoreType.DMA((2,))]),
        out_shape=jax.ShapeDtypeStruct(xs.shape, xs.dtype),
        input_output_aliases={4: 0},
        compiler_params=pltpu.CompilerParams(dimension_semantics=("arbitrary",), vmem_limit_bytes=VMEM_LIMIT),
        name="experts",
    )(blk, we_gate, we_up, we_down, xs)


def _final_body(nb, lt, tile0, n_ref, l_ref, g_ref, slot_ref, wt_ref, x2_ref, g2_ref, fw_ref, ys_ref, o_ref,
                buf_ref, routed_ref, sem):
    m = nb * lt
    n_inner = pl.num_programs(1)
    step = pl.program_id(0) * n_inner + pl.program_id(1)
    nsteps = pl.num_programs(0) * n_inner
    tile = tile0 + step
    cur = step % RING

    def run_copy(sl):
        return lambda srow, grow, b: pltpu.make_async_copy(
            ys_ref.at[pl.ds(grow, b)], buf_ref.at[sl, pl.ds(srow, b)], sem.at[sl])

    def fetch(ahead):
        sl = (step + ahead) % RING
        _for_each_run(n_ref, l_ref, g_ref, tile + ahead, lambda s, g, b, j: _start(run_copy(sl)(s, g, b), j))

    @pl.when(step == 0)
    def _():
        buf_ref[...] = jnp.zeros_like(buf_ref)
        for ahead in range(RING - 1):
            @pl.when(ahead < nsteps)
            def _():
                fetch(ahead)

    @pl.when(step + (RING - 1) < nsteps)
    def _():
        fetch(RING - 1)

    _wait_rows(lambda b: run_copy(cur)(0, 0, b), l_ref[E - 1, tile] + n_ref[E - 1, tile])

    slot = slot_ref[...].astype(i16)
    wt = wt_ref[...].astype(bf16)

    def combine(col0, ncols):
        scol = lax.broadcasted_iota(i32, (m, ncols), 1).astype(i16) + jnp.int16(col0)
        wsel = jnp.zeros((m, ncols), bf16)
        for kk in range(TOPK):
            wsel = jnp.where(scol == slot[:, kk:kk + 1], wt[:, kk:kk + 1], wsel)
        return _dot(wsel, buf_ref[cur, col0:col0 + ncols, :])

    routed_ref[...] = combine(0, SLOTS_MAIN)

    @pl.when(l_ref[E - 1, tile] + n_ref[E - 1, tile] > SLOTS_MAIN)
    def _():
        routed_ref[...] += combine(SLOTS_MAIN, SLOTS - SLOTS_MAIN)
    routed = routed_ref[...]
    g2 = g2_ref[0]
    if nb == 1:
        g2 = g2.reshape(1, D)
    else:
        g2 = jnp.broadcast_to(g2, (nb, lt, D)).reshape(m, D)
    y = x2_ref[...] + g2 * routed
    o_ref[...] = _rms(y, fw_ref[...]).reshape(nb, lt, D)


def _final(tables, slot_t, wts_t, x2, ys, mod4, fw, bsz, seqlen, row0, nb, lt):
    m = nb * lt
    assert m == TILE and row0 % TILE == 0
    n_lt = seqlen // lt
    tile0 = row0 // TILE
    tile = lambda b, i: tile0 + b * n_lt + i
    return pl.pallas_call(
        functools.partial(_final_body, nb, lt, tile0),
        grid_spec=pltpu.PrefetchScalarGridSpec(
            num_scalar_prefetch=len(tables),
            grid=(bsz // nb, n_lt),
            in_specs=[pl.BlockSpec((m, TOPK), lambda b, i, *_: (tile(b, i), 0)),
                      pl.BlockSpec((m, TOPK), lambda b, i, *_: (tile(b, i), 0)),
                      pl.BlockSpec((m, D), lambda b, i, *_: (b * n_lt + i, 0)),
                      pl.BlockSpec((1, nb, 1, D), lambda b, i, *_: (5, b, 0, 0)),
                      pl.BlockSpec((1, D), lambda b, i, *_: (0, 0)),
                      pl.BlockSpec(memory_space=pl.ANY)],
            out_specs=pl.BlockSpec((nb, lt, D), lambda b, i, *_: (b, i, 0)),
            scratch_shapes=[pltpu.VMEM((RING, SLOTS, D), bf16), pltpu.VMEM((m, D), f32),
                            pltpu.SemaphoreType.DMA((RING,))]),
        out_shape=jax.ShapeDtypeStruct((bsz, seqlen, D), f32),
        compiler_params=pltpu.CompilerParams(dimension_semantics=("arbitrary", "arbitrary"),
                                             vmem_limit_bytes=VMEM_LIMIT),
        name="final",
    )(*tables, slot_t, wts_t, x2, mod4, fw, ys)


def kernel(x_prompt, x_sample, state_conv, state_gla, c_prompt, c_sample, w_ada, b_ada, norm1_w, w_in, conv_w,
           w_gk, b_gk, gla_norm_w, w_out_conv, w_out_gla, w_o, norm2_w, router_w, router_bias, we_gate, we_up,
           we_down, ws_gate, ws_up, ws_down, final_norm_w):
    assert w_ada.shape[0] == 1, "single-layer step"
    bp, lp, _ = x_prompt.shape
    bs, ls, _ = x_sample.shape
    tp, ts = bp * lp, bs * ls
    t_all = tp + ts

    w_in0 = w_in[0]
    n_main = 3 * D + 2 * DK + 2 * DV
    rank = w_gk.shape[1]
    weights = (
        norm1_w[0].reshape(1, D),
        w_in0[:, :n_main].astype(bf16),
        jnp.pad(w_in0[:, n_main:n_main + rank], ((0, 0), (0, RANK_PAD - rank))).astype(bf16),
        w_in0[:, n_main + rank:].astype(bf16),
        jnp.pad(w_gk[0], ((0, RANK_PAD - rank), (0, 0))).astype(bf16),
        b_gk[0].reshape(1, DK),
        conv_w[0],
        gla_norm_w[0].reshape(1, DVH),
        w_out_conv[0].astype(bf16),
        w_out_gla[0].astype(bf16),
        w_o[0].astype(bf16),
        norm2_w[0].reshape(1, D),
        router_w[0].T.astype(bf16),
        ws_gate[0].astype(bf16),
        ws_up[0].astype(bf16),
        ws_down[0].astype(bf16),
    )

    mod = _adaln(jnp.concatenate([c_prompt, c_sample], axis=0), w_ada[0], b_ada[0].reshape(1, 6 * D))
    mod4 = mod.reshape(bp + bs, 6, 1, D).transpose(1, 0, 2, 3)
    mod_p, mod_s = mod4[:, :bp], mod4[:, bp:]

    lt_p = 512
    x2_p, h2_p, lg_p, conv_p, gla_p = _mixer(x_prompt, mod_p, None, weights, 1, lt_p, GLA_CHUNK)
    nb_s = 8
    x2_s, h2_s, lg_s, conv_s, gla_s = _mixer(x_sample, mod_s, (state_conv[0], state_gla[0]), weights,
                                             nb_s, ls, math.gcd(ls, GLA_CHUNK))
    lg = jnp.concatenate([lg_p.transpose(1, 0, 2).reshape(E, tp), lg_s.transpose(1, 0, 2).reshape(E, ts)], axis=1)

    eidx, wts, lrank, cnt = _route(lg, router_bias[0].reshape(E, 1))
    n_tiles = t_all // TILE
    max_rows = t_all * TOPK + n_tiles * E * (RUN_ALIGN - 1) + E * (MOE_BM - RUN_ALIGN)
    nblk = -(-max_rows // MOE_BM)
    slot, nrun, loff, goff, blk = _plan(eidx, lrank, cnt)
    tables = (nrun, loff, goff)

    xs = _dispatch(tables, slot, h2_p, h2_s, nblk * MOE_BM)
    ys = _experts(blk, xs, we_gate[0], we_up[0], we_down[0])

    fw = final_norm_w.reshape(1, D)
    slot_t, wts_t = slot.T, wts.T
    y_prompt = _final(tables, slot_t, wts_t, x2_p, ys, mod_p, fw, bp, lp, 0, 1, TILE)
    y_sample = _final(tables, slot_t, wts_t, x2_s, ys, mod_s, fw, bs, ls, tp, TILE // ls, ls)
    return (y_prompt, y_sample, conv_p, gla_p, conv_s, gla_s)
```

```python
import functools
import math

import jax
import jax.numpy as jnp
from jax import lax
from jax.experimental import pallas as pl
from jax.experimental.pallas import tpu as pltpu

f32 = jnp.float32
bf16 = jnp.bfloat16
i32 = jnp.int32
i16 = jnp.int16

D = 1024
H = 4
DKH = 128
DVH = 256
DK = H * DKH
DV = H * DVH
RANK_PAD = 128
GATE_NORMALIZER = 16.0
GLA_CHUNK = 64
E = 64
TOPK = 8
NGROUPS = 8
TOPK_GROUPS = 4
DE = 256
ROUTED_SCALE = 2.5
EPS = 1e-6
NEG = float("-inf")

V7X_VMEM_BYTES = 64 * 1024 * 1024
VMEM_LIMIT = V7X_VMEM_BYTES - 6 * 1024 * 1024
MOE_BM = 512
EXPERT_IN_SLOTS = 3
RING = 2
TILE = 256
MAX_TILES = 128
ROUTE_TILES = 2
RUN_ALIGN = 16
SLOTS = TILE * TOPK + 1024
SLOTS_MAIN = TILE * TOPK + 512
RUN_PIECES = (256, 128, 64, 32, 16)
TAIL_PIECES = (256, 128, 64, 32, 16)
WAIT_PIECES = (2048, 1024, 512, 256, 128, 64, 32, 16)

def _dot(a, b):
    return jnp.dot(a, b, preferred_element_type=f32)


def _split3(x):
    hi = x.astype(bf16)
    r1 = x - hi.astype(f32)
    mid = r1.astype(bf16)
    lo = (r1 - mid.astype(f32)).astype(bf16)
    return hi, mid, lo


def _dot01(m01, x):
    hi, mid, lo = _split3(x)
    return _dot(m01, hi) + _dot(m01, mid) + _dot(m01, lo)


def _dotx01(x, m01):
    hi, mid, lo = _split3(x)
    return _dot(hi, m01) + _dot(mid, m01) + _dot(lo, m01)


def _rms(v, w):
    ms = jnp.mean(v * v, axis=-1, keepdims=True)
    return v * lax.rsqrt(ms + EPS) * w


def _const_spec(shape):
    n = len(shape)
    return pl.BlockSpec(shape, lambda *_: (0,) * n, pipeline_mode=pl.Buffered(1))


def _adaln_body(c_ref, w_ref, b_ref, o_ref):
    c = c_ref[...]
    a = (c * jax.nn.sigmoid(c)).astype(bf16)
    o_ref[...] = _dot(a, w_ref[...].astype(bf16)) + b_ref[...]


def _adaln(c_all, w_ada, b_ada):
    nrow = c_all.shape[0]
    tn = 1536
    return pl.pallas_call(
        _adaln_body,
        grid=(6 * D // tn,),
        in_specs=[pl.BlockSpec((nrow, D), lambda j: (0, 0)),
                  pl.BlockSpec((D, tn), lambda j: (0, j)),
                  pl.BlockSpec((1, tn), lambda j: (0, j))],
        out_specs=pl.BlockSpec((nrow, tn), lambda j: (0, j)),
        out_shape=jax.ShapeDtypeStruct((nrow, 6 * D), f32),
        compiler_params=pltpu.CompilerParams(dimension_semantics=("arbitrary",), vmem_limit_bytes=VMEM_LIMIT),
        name="adaln",
    )(c_all, w_ada, b_ada)


def _mixer_body(nb, lt, chunk, has_state, *refs):
    if has_state:
        (x_ref, mod_ref, cs_ref, gs_ref, *rest) = refs
    else:
        (x_ref, mod_ref, *rest) = refs
        cs_ref = gs_ref = None
    (n1_ref, wmain_ref, wz_ref, wu_ref, wgk_ref, bgk_ref, cw_ref, gnw_ref, woc_ref, wog_ref, wo_ref,
     n2_ref, rwt_ref, wsg_ref, wsu_ref, wsd_ref,
     x2_ref, h2_ref, lg_ref, nc_ref, ng_ref, carry_ref, st_ref, og_ref, sin_ref) = rest
    m = nb * lt
    i = pl.program_id(1)
    last = pl.num_programs(1) - 1

    def rows(v):
        w = v.shape[-1]
        if nb == 1:
            return v.reshape(1, w)
        return jnp.broadcast_to(v, (nb, lt, w)).reshape(m, w)

    @pl.when(i == 0)
    def _():
        if has_state:
            cs = cs_ref[...]
            carry_ref[0] = cs[:, 0:1, :]
            carry_ref[1] = cs[:, 1:2, :]
            for hh in range(H):
                st_ref[hh] = gs_ref[:, hh]
        else:
            carry_ref[...] = jnp.zeros_like(carry_ref)
            st_ref[...] = jnp.zeros_like(st_ref)

    x = x_ref[...].reshape(m, D)
    sh1, sc1, g1, sh2, sc2, g2 = [rows(mod_ref[j]) for j in range(6)]
    hb = (_rms(x, n1_ref[...]) * (1.0 + sc1) + sh1).astype(bf16)

    def proj(lo, hi):
        return _dot(hb, wmain_ref[:, lo:hi])

    pre = proj(D, 2 * D) * proj(2 * D, 3 * D)
    c0 = rows(carry_ref[0])
    c1 = rows(carry_ref[1])
    l_idx = lax.broadcasted_iota(i32, (m, D), 0) & (lt - 1)
    r1 = pltpu.roll(pre, 1, 0)
    r2 = pltpu.roll(pre, 2, 0)
    prev1 = jnp.where(l_idx == 0, c1, r1)
    prev2 = jnp.where(l_idx == 0, c0, jnp.where(l_idx == 1, c1, r2))
    cw = cw_ref[...]
    conv = cw[0:1] * prev2 + cw[1:2] * prev1 + cw[2:3] * pre
    y_a = _dot((proj(0, D) * conv).astype(bf16), woc_ref[...])
    pre3 = pre.reshape(nb, lt, D)
    tail = pre3[:, lt - 2:lt, :]
    carry_ref[0] = tail[:, 0:1, :]
    carry_ref[1] = tail[:, 1:2, :]
    nc_ref[0] = tail

    q = proj(3 * D, 3 * D + DK)
    k = proj(3 * D + DK, 3 * D + 2 * DK)
    v = proj(4 * D, 5 * D)
    g = proj(5 * D, 6 * D)
    z = _dot(hb, wz_ref[...]).astype(bf16)
    pa = _dot(z, wgk_ref[...]) + bgk_ref[...]
    la = (jnp.minimum(pa, 0.0) - jnp.log1p(jnp.exp(-jnp.abs(pa)))) / GATE_NORMALIZER
    pos = lax.broadcasted_iota(i32, (m, DK), 0) & (chunk - 1)
    bcum = la
    shift = 1
    while shift < chunk:
        bcum = bcum + jnp.where(pos >= shift, pltpu.roll(bcum, shift, 0), 0.0)
        shift *= 2
    ngrp = m // chunk
    bend = bcum.reshape(ngrp, chunk, DK)[:, chunk - 1:chunk, :]
    bend_rows = jnp.broadcast_to(bend, (ngrp, chunk, DK)).reshape(m, DK)
    nch = lt // chunk
    assert nb == 1 or nch == 1
    hs = H * nb
    nbat = nch * hs

    def per_head(xm, w):
        x3 = xm.reshape(nb * nch, chunk, H * w)
        parts = [x3[:, :, hh * w:(hh + 1) * w] for hh in range(H)]
        return jnp.stack(parts, axis=1 if nb == 1 else 0).reshape(nbat, chunk, w)

    qd = per_head((q * (DKH ** -0.5) * jnp.exp(bcum)).astype(bf16), DKH)
    kd = per_head((k * jnp.exp(-bcum)).astype(bf16), DKH)
    ke = per_head((k * jnp.exp(bend_rows - bcum)).astype(bf16), DKH)
    vv = per_head(v.astype(bf16), DVH)
    assert nbat <= 128
    a_end = per_head(jnp.exp(bend_rows), DKH)[:, 0, :]
    a_col = jnp.concatenate([a_end, jnp.zeros((128 - nbat, DKH), f32)], axis=0).T
    causal = lax.broadcasted_iota(i32, (chunk, chunk), 1) <= lax.broadcasted_iota(i32, (chunk, chunk), 0)
    sc = jnp.einsum('bcd,bsd->bcs', qd, kd, preferred_element_type=f32)
    sc = jnp.where(causal, sc, 0.0).astype(bf16)
    o = jnp.einsum('bcs,bse->bce', sc, vv, preferred_element_type=f32)
    upd = jnp.einsum('bsd,bse->bde', ke, vv, preferred_element_type=f32)
    st = [st_ref[s // nb, s % nb] for s in range(hs)]
    for j in range(nch):
        for s in range(hs):
            b = j * hs + s
            sin_ref[b] = st[s].astype(bf16)
            st[s] = st[s] * a_col[:, b:b + 1] + upd[b]
    for s in range(hs):
        st_ref[s // nb, s % nb] = st[s]
    o = o + jnp.einsum('bcd,bde->bce', qd, sin_ref[...], preferred_element_type=f32)
    on = _rms(o, gnw_ref[...])
    g3 = g.reshape(nb, lt, DV)
    for j in range(nch):
        js = slice(j * chunk, (j + 1) * chunk)
        for hh in range(H):
            vs = slice(hh * DVH, (hh + 1) * DVH)
            gh = g3[:, js, vs]
            b0 = (j * H + hh) * nb
            og_ref[:, js, vs] = (on[b0:b0 + nb] * (gh * jax.nn.sigmoid(gh))).astype(bf16)

    @pl.when(i == last)
    def _():
        for hh in range(H):
            ng_ref[0, :, hh] = st_ref[hh]

    y_b = _dot(og_ref[...].reshape(m, DV), wog_ref[...])
    u_a = _dot(hb, wu_ref[:, 0:D])
    u_b = _dot(hb, wu_ref[:, D:2 * D])
    merged = (jax.nn.sigmoid(u_a) * y_a + jax.nn.sigmoid(u_b) * y_b).astype(bf16)
    x1 = x + g1 * _dot(merged, wo_ref[...])

    h2 = (_rms(x1, n2_ref[...]) * (1.0 + sc2) + sh2).astype(bf16)
    sg = _dot(h2, wsg_ref[...])
    act = (sg * jax.nn.sigmoid(sg) * _dot(h2, wsu_ref[...])).astype(bf16)
    x2_ref[...] = x1 + g2 * _dot(act, wsd_ref[...])
    h2_ref[...] = h2
    lg_ref[0] = lax.dot_general(rwt_ref[...], h2, (((1,), (1,)), ((), ())), preferred_element_type=f32)


def _mixer(x, mod4, states, weights, nb, lt, chunk):
    bsz, seqlen, _ = x.shape
    has_state = states is not None
    m = nb * lt
    n_lt = seqlen // lt
    nsteps = (bsz // nb) * n_lt
    t_all = bsz * seqlen

    def tok_blk(b, i):
        return b * n_lt + i

    in_specs = [pl.BlockSpec((nb, lt, D), lambda b, i: (b, i, 0)),
                pl.BlockSpec((6, nb, 1, D), lambda b, i: (0, b, 0, 0))]
    args = [x, mod4]
    if has_state:
        in_specs += [pl.BlockSpec((nb, 2, D), lambda b, i: (b, 0, 0)),
                     pl.BlockSpec((nb, H, DKH, DVH), lambda b, i: (b, 0, 0, 0))]
        args += list(states)
    in_specs += [_const_spec(w.shape) for w in weights]
    args += list(weights)
    out_specs = [pl.BlockSpec((m, D), lambda b, i: (tok_blk(b, i), 0)),
                 pl.BlockSpec((m, D), lambda b, i: (tok_blk(b, i), 0)),
                 pl.BlockSpec((1, E, m), lambda b, i: (b * n_lt + i, 0, 0)),
                 pl.BlockSpec((1, nb, 2, D), lambda b, i: (0, b, 0, 0)),
                 pl.BlockSpec((1, nb, H, DKH, DVH), lambda b, i: (0, b, 0, 0, 0))]
    out_shape = [jax.ShapeDtypeStruct((t_all, D), f32),
                 jax.ShapeDtypeStruct((t_all, D), bf16),
                 jax.ShapeDtypeStruct((nsteps, E, m), f32),
                 jax.ShapeDtypeStruct((1, bsz, 2, D), f32),
                 jax.ShapeDtypeStruct((1, bsz, H, DKH, DVH), f32)]
    return pl.pallas_call(
        functools.partial(_mixer_body, nb, lt, chunk, has_state),
        grid=(bsz // nb, n_lt),
        in_specs=in_specs,
        out_specs=out_specs,
        out_shape=out_shape,
        scratch_shapes=[pltpu.VMEM((2, nb, 1, D), f32),
                        pltpu.VMEM((H, nb, DKH, DVH), f32),
                        pltpu.VMEM((nb, lt, DV), bf16),
                        pltpu.VMEM(((lt // chunk) * H * nb, DKH, DVH), bf16)],
        compiler_params=pltpu.CompilerParams(dimension_semantics=("arbitrary", "arbitrary"),
                                             vmem_limit_bytes=VMEM_LIMIT),
        name="mixer_state" if has_state else "mixer_prompt",
    )(*args)


def _route_body(lg_ref, bias_ref, eidx_ref, w_ref, lrank_ref, cnt_ref):
    step = pl.program_id(0)
    tl = lg_ref.shape[1]

    scores = jax.nn.sigmoid(lg_ref[...])
    choice = scores + bias_ref[...]
    gsz = E // NGROUPS
    grp = choice.reshape(NGROUPS, gsz, tl)
    mi = lax.broadcasted_iota(i32, (NGROUPS, gsz, tl), 1)
    m1 = jnp.max(grp, axis=1, keepdims=True)
    first = jnp.min(jnp.where(grp == m1, mi, gsz), axis=1, keepdims=True)
    m2 = jnp.max(jnp.where(mi == first, NEG, grp), axis=1, keepdims=True)
    gscore = (m1 + m2).reshape(NGROUPS, tl)
    gi = lax.broadcasted_iota(i32, (NGROUPS, tl), 0)
    gsel = jnp.zeros((NGROUPS, tl), f32)
    work = gscore
    for _ in range(TOPK_GROUPS):
        mx = jnp.max(work, axis=0, keepdims=True)
        f = jnp.min(jnp.where(work == mx, gi, NGROUPS), axis=0, keepdims=True)
        hit = gi == f
        gsel = jnp.where(hit, 1.0, gsel)
        work = jnp.where(hit, NEG, work)
    emask = jnp.broadcast_to(gsel.reshape(NGROUPS, 1, tl), (NGROUPS, gsz, tl)).reshape(E, tl)
    masked = jnp.where(emask > 0.0, choice, NEG)
    ei = lax.broadcasted_iota(i32, (E, tl), 0)
    sel = jnp.zeros((E, tl), f32)
    hits, ws = [], []
    for kk in range(TOPK):
        mx = jnp.max(masked, axis=0, keepdims=True)
        f = jnp.min(jnp.where(masked == mx, ei, E), axis=0, keepdims=True)
        hit = ei == f
        ws.append(jnp.sum(jnp.where(hit, scores, 0.0), axis=0, keepdims=True))
        masked = jnp.where(hit, NEG, masked)
        sel = jnp.where(hit, 1.0, sel)
        hits.append(hit)
        eidx_ref[kk:kk + 1, :] = f
    wsum = ws[0]
    for t in ws[1:]:
        wsum = wsum + t
    for kk in range(TOPK):
        w_ref[kk:kk + 1, :] = ws[kk] / wsum * ROUTED_SCALE

    ui = lax.broadcasted_iota(i32, (tl, tl), 0)
    uj = lax.broadcasted_iota(i32, (tl, tl), 1)
    upper = ((ui <= uj) & ((ui & -TILE) == (uj & -TILE))).astype(bf16)
    lrank = _dot(sel.astype(bf16), upper) - sel
    for kk in range(TOPK):
        lrank_ref[kk:kk + 1, :] = jnp.sum(jnp.where(hits[kk], lrank, 0.0), axis=0, keepdims=True).astype(i32)

    @pl.when(step == 0)
    def _():
        cnt_ref[...] = jnp.zeros_like(cnt_ref)
    lane = lax.broadcasted_iota(i32, cnt_ref.shape, 1)
    tiles_per_step = tl // TILE
    for sub in range(tiles_per_step):
        col = jnp.sum(sel[:, sub * TILE:(sub + 1) * TILE], axis=1, keepdims=True)
        cnt_ref[...] = jnp.where(lane == step * tiles_per_step + sub, col, cnt_ref[...])


def _route(logits_t, bias_col):
    t_all = logits_t.shape[1]
    tl = ROUTE_TILES * TILE
    assert t_all % tl == 0 and t_all // TILE <= MAX_TILES
    tok = lambda s: (0, s)
    return pl.pallas_call(
        _route_body,
        grid=(t_all // tl,),
        in_specs=[pl.BlockSpec((E, tl), tok), pl.BlockSpec((E, 1), lambda s: (0, 0))],
        out_specs=[pl.BlockSpec((TOPK, tl), tok), pl.BlockSpec((TOPK, tl), tok), pl.BlockSpec((TOPK, tl), tok),
                   pl.BlockSpec((E, MAX_TILES), lambda s: (0, 0))],
        out_shape=[jax.ShapeDtypeStruct((TOPK, t_all), i32), jax.ShapeDtypeStruct((TOPK, t_all), f32),
                   jax.ShapeDtypeStruct((TOPK, t_all), i32), jax.ShapeDtypeStruct((E, MAX_TILES), f32)],
        compiler_params=pltpu.CompilerParams(dimension_semantics=("arbitrary",), vmem_limit_bytes=VMEM_LIMIT),
        name="route",
    )(logits_t, bias_col)


def _plan_body(eidx_ref, lrank_ref, cnt_ref, slot_ref, n_ref, l_ref, g_ref, blk_ref):
    t_all = eidx_ref.shape[1]
    cnt = cnt_ref[...]
    nrun = jnp.floor((cnt + (RUN_ALIGN - 1)) * (1.0 / RUN_ALIGN)) * RUN_ALIGN
    ti = lax.broadcasted_iota(i32, (MAX_TILES, MAX_TILES), 0)
    tj = lax.broadcasted_iota(i32, (MAX_TILES, MAX_TILES), 1)
    before = _dotx01(nrun, (ti < tj).astype(bf16))
    region = jnp.sum(nrun, axis=1, keepdims=True)
    nblocks = jnp.floor((region + (MOE_BM - 1)) * (1.0 / MOE_BM))
    ri = lax.broadcasted_iota(i32, (E, E), 0)
    ci = lax.broadcasted_iota(i32, (E, E), 1)
    below = (ci < ri).astype(bf16)
    bstart = _dot01(below, jnp.broadcast_to(nblocks, (E, MAX_TILES)))
    loff = _dot01(below, nrun)
    n_ref[...] = nrun.astype(i32)
    l_ref[...] = loff.astype(i32)
    g_ref[...] = (bstart * MOE_BM + before).astype(i32)
    lane0 = lax.broadcasted_iota(i32, (E, MAX_TILES), 1) == 0
    blk_ref[...] = jnp.where(lane0, bstart, nblocks).astype(i32)

    tile_of = lax.shift_right_logical(lax.broadcasted_iota(i32, (MAX_TILES, t_all), 1), TILE.bit_length() - 1)
    spread = (lax.broadcasted_iota(i32, (MAX_TILES, t_all), 0) == tile_of).astype(bf16)
    ltok = _dotx01(loff, spread)
    ei = lax.broadcasted_iota(i32, (E, t_all), 0)
    for kk in range(TOPK):
        hit = ei == eidx_ref[kk:kk + 1, :]
        start = jnp.sum(jnp.where(hit, ltok, 0.0), axis=0, keepdims=True)
        slot_ref[kk:kk + 1, :] = start.astype(i32) + lrank_ref[kk:kk + 1, :]


def _plan(eidx, lrank, cnt):
    t_all = eidx.shape[1]
    assert TILE & (TILE - 1) == 0
    tl = t_all
    tok = lambda s: (0, s)
    table = pl.BlockSpec((E, MAX_TILES), lambda s: (0, 0))
    return pl.pallas_call(
        _plan_body,
        grid=(1,),
        in_specs=[pl.BlockSpec((TOPK, tl), tok), pl.BlockSpec((TOPK, tl), tok), table],
        out_specs=[pl.BlockSpec((TOPK, tl), tok), table, table, table, table],
        out_shape=[jax.ShapeDtypeStruct((TOPK, t_all), i32)] + [jax.ShapeDtypeStruct((E, MAX_TILES), i32)] * 4,
        compiler_params=pltpu.CompilerParams(dimension_semantics=("arbitrary",), vmem_limit_bytes=VMEM_LIMIT),
        name="plan",
    )(eidx, lrank, cnt)


def _for_each_piece(total, pieces, fn):
    for j, b in enumerate(pieces):
        @pl.when((total & b) != 0)
        def _():
            fn(pl.multiple_of(total & ~(2 * b - 1), pieces[-1]), b, j)


def _for_each_run(n_ref, l_ref, g_ref, tile, fn):
    def per_expert(e, c):
        lo = pl.multiple_of(l_ref[e, tile], RUN_ALIGN)
        go = pl.multiple_of(g_ref[e, tile], RUN_ALIGN)
        _for_each_piece(n_ref[e, tile], RUN_PIECES, lambda off, b, j: fn(lo + off, go + off, b, j))
        return c
    lax.fori_loop(0, E, per_expert, 0)


def _start(copy, j):
    copy.start(priority=j % 2)


def _wait_rows(make_copy, rows):
    _for_each_piece(rows, WAIT_PIECES, lambda off, b, j: make_copy(b).wait())


def _dispatch_body(tiles_a, n_ref, l_ref, g_ref, slot_ref, ha_ref, hb_ref, xs_ref, buf_ref, zero_ref, sem, zsem):
    tile = pl.program_id(0)
    nsteps = pl.num_programs(0)
    cur = tile % RING

    slot = slot_ref[...].astype(i16)
    h = jnp.where(tile < tiles_a, ha_ref[...], hb_ref[...])
    filled = l_ref[E - 1, tile] + n_ref[E - 1, tile]

    def order(row0, nrows):
        srow = lax.broadcasted_iota(i32, (nrows, TILE), 0).astype(i16) + jnp.int16(row0)
        onehot = jnp.zeros((nrows, TILE), bf16)
        for kk in range(TOPK):
            onehot = jnp.where(srow == slot[kk:kk + 1, :], jnp.ones((), bf16), onehot)
        buf_ref[cur, row0:row0 + nrows, :] = _dot(onehot, h).astype(bf16)

    order(0, SLOTS_MAIN)

    @pl.when(filled > SLOTS_MAIN)
    def _():
        order(SLOTS_MAIN, SLOTS - SLOTS_MAIN)

    def run_copy(sl):
        return lambda srow_, grow, b: pltpu.make_async_copy(
            buf_ref.at[sl, pl.ds(srow_, b)], xs_ref.at[pl.ds(grow, b)], sem.at[sl])

    _for_each_run(n_ref, l_ref, g_ref, tile, lambda s, g, b, j: _start(run_copy(cur)(s, g, b), j))

    def wait_tile(t, sl):
        _wait_rows(lambda b: run_copy(sl)(0, 0, b), l_ref[E - 1, t] + n_ref[E - 1, t])

    @pl.when(tile >= RING - 1)
    def _():
        wait_tile(tile - (RING - 1), (tile + 1) % RING)

    last = nsteps - 1

    def zero_fill(fn):
        def per_expert(e, c):
            end = pl.multiple_of(g_ref[e, last] + n_ref[e, last], RUN_ALIGN)
            _for_each_piece((-end) & (MOE_BM - 1), TAIL_PIECES, lambda off, b, j: fn(pltpu.make_async_copy(
                zero_ref.at[pl.ds(0, b)], xs_ref.at[pl.ds(end + off, b)], zsem)))
            return c
        lax.fori_loop(0, E, per_expert, 0)
        used = (g_ref[E - 1, last] + n_ref[E - 1, last] + (MOE_BM - 1)) // MOE_BM

        def per_block(blk, c):
            fn(pltpu.make_async_copy(zero_ref, xs_ref.at[pl.ds(pl.multiple_of(blk * MOE_BM, MOE_BM), MOE_BM)], zsem))
            return c
        lax.fori_loop(used, xs_ref.shape[0] // MOE_BM, per_block, 0)

    @pl.when(tile == 0)
    def _():
        zero_ref[...] = jnp.zeros_like(zero_ref)
        zero_fill(lambda cp: cp.start())

    @pl.when(tile == last)
    def _():
        for back in range(RING - 2, -1, -1):
            @pl.when(tile >= back)
            def _():
                wait_tile(tile - back, (tile - back) % RING)
        zero_fill(lambda cp: cp.wait())


def _dispatch(tables, slot, h2_a, h2_b, nrows):
    tiles_a, tiles_b = h2_a.shape[0] // TILE, h2_b.shape[0] // TILE
    return pl.pallas_call(
        functools.partial(_dispatch_body, tiles_a),
        grid_spec=pltpu.PrefetchScalarGridSpec(
            num_scalar_prefetch=len(tables),
            grid=(tiles_a + tiles_b,),
            in_specs=[pl.BlockSpec((TOPK, TILE), lambda i, *_: (0, i)),
                      pl.BlockSpec((TILE, D), lambda i, *_: (jnp.minimum(i, tiles_a - 1), 0)),
                      pl.BlockSpec((TILE, D), lambda i, *_: (jnp.maximum(i - tiles_a, 0), 0))],
            out_specs=pl.BlockSpec(memory_space=pl.ANY),
            scratch_shapes=[pltpu.VMEM((RING, SLOTS, D), bf16), pltpu.VMEM((MOE_BM, D), bf16),
                            pltpu.SemaphoreType.DMA((RING,)), pltpu.SemaphoreType.DMA]),
        out_shape=jax.ShapeDtypeStruct((nrows, D), bf16),
        compiler_params=pltpu.CompilerParams(dimension_semantics=("arbitrary",), vmem_limit_bytes=VMEM_LIMIT),
        name="dispatch",
    )(*tables, slot, h2_a, h2_b)


def _expert_body(blk_ref, wg_ref, wu_ref, wd_ref, xs_ref, ys_ref, xbuf_ref, ybuf_ref, wgu_ref, wdb_ref, isem, osem):
    e = pl.program_id(0)
    first, count = blk_ref[e, 0], blk_ref[e, 1]
    used = blk_ref[E - 1, 0] + blk_ref[E - 1, 1]
    wgu_ref[:, :DE] = wg_ref[0].astype(bf16)
    wgu_ref[:, DE:] = wu_ref[0].astype(bf16)
    wdb_ref[...] = wd_ref[0].astype(bf16)

    def rows(g):
        return pl.ds(pl.multiple_of(g * MOE_BM, MOE_BM), MOE_BM)

    def load(g):
        sl = g % EXPERT_IN_SLOTS
        return pltpu.make_async_copy(xs_ref.at[rows(g)], xbuf_ref.at[sl], isem.at[sl])

    def store(g):
        sl = g % 2
        return pltpu.make_async_copy(ybuf_ref.at[sl], ys_ref.at[rows(g)], osem.at[sl])

    @pl.when(e == 0)
    def _():
        for g in range(EXPERT_IN_SLOTS - 1):
            @pl.when(g < used)
            def _():
                load(g).start()

    def block(j, c):
        g = first + j

        @pl.when(g + (EXPERT_IN_SLOTS - 1) < used)
        def _():
            load(g + (EXPERT_IN_SLOTS - 1)).start()
        load(g).wait()

        @pl.when(g >= 2)
        def _():
            store(g - 2).wait()
        hgu = _dot(xbuf_ref[g % EXPERT_IN_SLOTS], wgu_ref[...])
        hg = hgu[:, :DE]
        act = (hg * jax.nn.sigmoid(hg) * hgu[:, DE:]).astype(bf16)
        ybuf_ref[g % 2] = _dot(act, wdb_ref[...]).astype(bf16)
        store(g).start()
        return c
    lax.fori_loop(0, count, block, 0)

    @pl.when(e == E - 1)
    def _():
        for back in (2, 1):
            @pl.when(used >= back)
            def _():
                store(used - back).wait()


def _experts(blk, xs, we_gate, we_up, we_down):
    ex = lambda e, blk_: (e, 0, 0)
    return pl.pallas_call(
        _expert_body,
        grid_spec=pltpu.PrefetchScalarGridSpec(
            num_scalar_prefetch=1,
            grid=(E,),
            in_specs=[pl.BlockSpec((1, D, DE), ex), pl.BlockSpec((1, D, DE), ex), pl.BlockSpec((1, DE, D), ex),
                      pl.BlockSpec(memory_space=pl.ANY)],
            out_specs=pl.BlockSpec(memory_space=pl.ANY),
            scratch_shapes=[pltpu.VMEM((EXPERT_IN_SLOTS, MOE_BM, D), bf16), pltpu.VMEM((2, MOE_BM, D), bf16),
                            pltpu.VMEM((D, 2 * DE), bf16), pltpu.VMEM((DE, D), bf16),
                            pltpu.SemaphoreType.DMA((EXPERT_IN_SLOTS,)), pltpu.SemaphoreType.DMA((2,))]),
        out_shape=jax.ShapeDtypeStruct(xs.shape, xs.dtype),
        input_output_aliases={4: 0},
        compiler_params=pltpu.CompilerParams(dimension_semantics=("arbitrary",), vmem_limit_bytes=VMEM_LIMIT),
        name="experts",
    )(blk, we_gate, we_up, we_down, xs)


def _final_body(nb, lt, tile0, n_ref, l_ref, g_ref, slot_ref, wt_ref, x2_ref, g2_ref, fw_ref, ys_ref, o_ref,
                buf_ref, routed_ref, sem):
    m = nb * lt
    n_inner = pl.num_programs(1)
    step = pl.program_id(0) * n_inner + pl.program_id(1)
    nsteps = pl.num_programs(0) * n_inner
    tile = tile0 + step
    cur = step % RING

    def run_copy(sl):
        return lambda srow, grow, b: pltpu.make_async_copy(
            ys_ref.at[pl.ds(grow, b)], buf_ref.at[sl, pl.ds(srow, b)], sem.at[sl])

    def fetch(ahead):
        sl = (step + ahead) % RING
        _for_each_run(n_ref, l_ref, g_ref, tile + ahead, lambda s, g, b, j: _start(run_copy(sl)(s, g, b), j))

    @pl.when(step == 0)
    def _():
        buf_ref[...] = jnp.zeros_like(buf_ref)
        for ahead in range(RING - 1):
            @pl.when(ahead < nsteps)
            def _():
                fetch(ahead)

    @pl.when(step + (RING - 1) < nsteps)
    def _():
        fetch(RING - 1)

    _wait_rows(lambda b: run_copy(cur)(0, 0, b), l_ref[E - 1, tile] + n_ref[E - 1, tile])

    slot = slot_ref[...].astype(i16)
    wt = wt_ref[...].astype(bf16)

    def combine(col0, ncols):
        scol = lax.broadcasted_iota(i32, (m, ncols), 1).astype(i16) + jnp.int16(col0)
        wsel = jnp.zeros((m, ncols), bf16)
        for kk in range(TOPK):
            wsel = jnp.where(scol == slot[:, kk:kk + 1], wt[:, kk:kk + 1], wsel)
        return _dot(wsel, buf_ref[cur, col0:col0 + ncols, :])

    routed_ref[...] = combine(0, SLOTS_MAIN)

    @pl.when(l_ref[E - 1, tile] + n_ref[E - 1, tile] > SLOTS_MAIN)
    def _():
        routed_ref[...] += combine(SLOTS_MAIN, SLOTS - SLOTS_MAIN)
    routed = routed_ref[...]
    g2 = g2_ref[0]
    if nb == 1:
        g2 = g2.reshape(1, D)
    else:
        g2 = jnp.broadcast_to(g2, (nb, lt, D)).reshape(m, D)
    y = x2_ref[...] + g2 * routed
    o_ref[...] = _rms(y, fw_ref[...]).reshape(nb, lt, D)


def _final(tables, slot_t, wts_t, x2, ys, mod4, fw, bsz, seqlen, row0, nb, lt):
    m = nb * lt
    assert m == TILE and row0 % TILE == 0
    n_lt = seqlen // lt
    tile0 = row0 // TILE
    tile = lambda b, i: tile0 + b * n_lt + i
    return pl.pallas_call(
        functools.partial(_final_body, nb, lt, tile0),
        grid_spec=pltpu.PrefetchScalarGridSpec(
            num_scalar_prefetch=len(tables),
            grid=(bsz // nb, n_lt),
            in_specs=[pl.BlockSpec((m, TOPK), lambda b, i, *_: (tile(b, i), 0)),
                      pl.BlockSpec((m, TOPK), lambda b, i, *_: (tile(b, i), 0)),
                      pl.BlockSpec((m, D), lambda b, i, *_: (b * n_lt + i, 0)),
                      pl.BlockSpec((1, nb, 1, D), lambda b, i, *_: (5, b, 0, 0)),
                      pl.BlockSpec((1, D), lambda b, i, *_: (0, 0)),
                      pl.BlockSpec(memory_space=pl.ANY)],
            out_specs=pl.BlockSpec((nb, lt, D), lambda b, i, *_: (b, i, 0)),
            scratch_shapes=[pltpu.VMEM((RING, SLOTS, D), bf16), pltpu.VMEM((m, D), f32),
                            pltpu.SemaphoreType.DMA((RING,))]),
        out_shape=jax.ShapeDtypeStruct((bsz, seqlen, D), f32),
        compiler_params=pltpu.CompilerParams(dimension_semantics=("arbitrary", "arbitrary"),
                                             vmem_limit_bytes=VMEM_LIMIT),
        name="final",
    )(*tables, slot_t, wts_t, x2, mod4, fw, ys)


def kernel(x_prompt, x_sample, state_conv, state_gla, c_prompt, c_sample, w_ada, b_ada, norm1_w, w_in, conv_w,
           w_gk, b_gk, gla_norm_w, w_out_conv, w_out_gla, w_o, norm2_w, router_w, router_bias, we_gate, we_up,
           we_down, ws_gate, ws_up, ws_down, final_norm_w):
    assert w_ada.shape[0] == 1, "single-layer step"
    bp, lp, _ = x_prompt.shape
    bs, ls, _ = x_sample.shape
    tp, ts = bp * lp, bs * ls
    t_all = tp + ts

    w_in0 = w_in[0]
    n_main = 3 * D + 2 * DK + 2 * DV
    rank = w_gk.shape[1]
    weights = (
        norm1_w[0].reshape(1, D),
        w_in0[:, :n_main].astype(bf16),
        jnp.pad(w_in0[:, n_main:n_main + rank], ((0, 0), (0, RANK_PAD - rank))).astype(bf16),
        w_in0[:, n_main + rank:].astype(bf16),
        jnp.pad(w_gk[0], ((0, RANK_PAD - rank), (0, 0))).astype(bf16),
        b_gk[0].reshape(1, DK),
        conv_w[0],
        gla_norm_w[0].reshape(1, DVH),
        w_out_conv[0].astype(bf16),
        w_out_gla[0].astype(bf16),
        w_o[0].astype(bf16),
        norm2_w[0].reshape(1, D),
        router_w[0].T.astype(bf16),
        ws_gate[0].astype(bf16),
        ws_up[0].astype(bf16),
        ws_down[0].astype(bf16),
    )

    mod = _adaln(jnp.concatenate([c_prompt, c_sample], axis=0), w_ada[0], b_ada[0].reshape(1, 6 * D))
    mod4 = mod.reshape(bp + bs, 6, 1, D).transpose(1, 0, 2, 3)
    mod_p, mod_s = mod4[:, :bp], mod4[:, bp:]

    lt_p = 512
    x2_p, h2_p, lg_p, conv_p, gla_p = _mixer(x_prompt, mod_p, None, weights, 1, lt_p, GLA_CHUNK)
    nb_s = 8
    x2_s, h2_s, lg_s, conv_s, gla_s = _mixer(x_sample, mod_s, (state_conv[0], state_gla[0]), weights,
                                             nb_s, ls, math.gcd(ls, GLA_CHUNK))
    lg = jnp.concatenate([lg_p.transpose(1, 0, 2).reshape(E, tp), lg_s.transpose(1, 0, 2).reshape(E, ts)], axis=1)

    eidx, wts, lrank, cnt = _route(lg, router_bias[0].reshape(E, 1))
    n_tiles = t_all // TILE
    max_rows = t_all * TOPK + n_tiles * E * (RUN_ALIGN - 1) + E * (MOE_BM - RUN_ALIGN)
    nblk = -(-max_rows // MOE_BM)
    slot, nrun, loff, goff, blk = _plan(eidx, lrank, cnt)
    tables = (nrun, loff, goff)

    xs = _dispatch(tables, slot, h2_p, h2_s, nblk * MOE_BM)
    ys = _experts(blk, xs, we_gate[0], we_up[0], we_down[0])

    fw = final_norm_w.reshape(1, D)
    slot_t, wts_t = slot.T, wts.T
    y_prompt = _final(tables, slot_t, wts_t, x2_p, ys, mod_p, fw, bp, lp, 0, 1, TILE)
    y_sample = _final(tables, slot_t, wts_t, x2_s, ys, mod_s, fw, bs, ls, tp, TILE // ls, ls)
    return (y_prompt, y_sample, conv_p, gla_p, conv_s, gla_s)
```

```python
import functools
import math

import jax
import jax.numpy as jnp
from jax import lax
from jax.experimental import pallas as pl
from jax.experimental.pallas import tpu as pltpu

f32 = jnp.float32
bf16 = jnp.bfloat16
i32 = jnp.int32
i16 = jnp.int16

D = 1024
H = 4
DKH = 128
DVH = 256
DK = H * DKH
DV = H * DVH
RANK_PAD = 128
GATE_NORMALIZER = 16.0
GLA_CHUNK = 64
E = 64
TOPK = 8
NGROUPS = 8
TOPK_GROUPS = 4
DE = 256
ROUTED_SCALE = 2.5
EPS = 1e-6
NEG = float("-inf")

V7X_VMEM_BYTES = 64 * 1024 * 1024
VMEM_LIMIT = V7X_VMEM_BYTES - 6 * 1024 * 1024
MOE_BM = 512
EXPERT_IN_SLOTS = 4
RING = 2
TILE = 256
MAX_TILES = 128
ROUTE_TILES = 2
RUN_ALIGN = 16
SLOTS = TILE * TOPK + 1024
SLOTS_MAIN = TILE * TOPK + 512
RUN_PIECES = (256, 128, 64, 32, 16)
TAIL_PIECES = (256, 128, 64, 32, 16)
WAIT_PIECES = (2048, 1024, 512, 256, 128, 64, 32, 16)

def _dot(a, b):
    return jnp.dot(a, b, preferred_element_type=f32)


def _split3(x):
    hi = x.astype(bf16)
    r1 = x - hi.astype(f32)
    mid = r1.astype(bf16)
    lo = (r1 - mid.astype(f32)).astype(bf16)
    return hi, mid, lo


def _dot01(m01, x):
    hi, mid, lo = _split3(x)
    return _dot(m01, hi) + _dot(m01, mid) + _dot(m01, lo)


def _dotx01(x, m01):
    hi, mid, lo = _split3(x)
    return _dot(hi, m01) + _dot(mid, m01) + _dot(lo, m01)


def _rms(v, w):
    ms = jnp.mean(v * v, axis=-1, keepdims=True)
    return v * lax.rsqrt(ms + EPS) * w


def _const_spec(shape):
    n = len(shape)
    return pl.BlockSpec(shape, lambda *_: (0,) * n, pipeline_mode=pl.Buffered(1))


def _adaln_body(c_ref, w_ref, b_ref, o_ref):
    c = c_ref[...]
    a = (c * jax.nn.sigmoid(c)).astype(bf16)
    o_ref[...] = _dot(a, w_ref[...].astype(bf16)) + b_ref[...]


def _adaln(c_all, w_ada, b_ada):
    nrow = c_all.shape[0]
    tn = 1536
    return pl.pallas_call(
        _adaln_body,
        grid=(6 * D // tn,),
        in_specs=[pl.BlockSpec((nrow, D), lambda j: (0, 0)),
                  pl.BlockSpec((D, tn), lambda j: (0, j)),
                  pl.BlockSpec((1, tn), lambda j: (0, j))],
        out_specs=pl.BlockSpec((nrow, tn), lambda j: (0, j)),
        out_shape=jax.ShapeDtypeStruct((nrow, 6 * D), f32),
        compiler_params=pltpu.CompilerParams(dimension_semantics=("arbitrary",), vmem_limit_bytes=VMEM_LIMIT),
        name="adaln",
    )(c_all, w_ada, b_ada)


def _mixer_body(nb, lt, chunk, has_state, *refs):
    if has_state:
        (x_ref, mod_ref, cs_ref, gs_ref, *rest) = refs
    else:
        (x_ref, mod_ref, *rest) = refs
        cs_ref = gs_ref = None
    (n1_ref, wmain_ref, wz_ref, wu_ref, wgk_ref, bgk_ref, cw_ref, gnw_ref, woc_ref, wog_ref, wo_ref,
     n2_ref, rwt_ref, wsg_ref, wsu_ref, wsd_ref,
     x2_ref, h2_ref, lg_ref, nc_ref, ng_ref, carry_ref, st_ref, og_ref, sin_ref) = rest
    m = nb * lt
    i = pl.program_id(1)
    last = pl.num_programs(1) - 1

    def rows(v):
        w = v.shape[-1]
        if nb == 1:
            return v.reshape(1, w)
        return jnp.broadcast_to(v, (nb, lt, w)).reshape(m, w)

    @pl.when(i == 0)
    def _():
        if has_state:
            cs = cs_ref[...]
            carry_ref[0] = cs[:, 0:1, :]
            carry_ref[1] = cs[:, 1:2, :]
        else:
            carry_ref[...] = jnp.zeros_like(carry_ref)
            st_ref[...] = jnp.zeros_like(st_ref)

    x = x_ref[...].reshape(m, D)
    sh1, sc1, g1, sh2, sc2, g2 = [rows(mod_ref[j]) for j in range(6)]
    hb = (_rms(x, n1_ref[...]) * (1.0 + sc1) + sh1).astype(bf16)

    def proj(lo, hi):
        return _dot(hb, wmain_ref[:, lo:hi])

    pre = proj(D, 2 * D) * proj(2 * D, 3 * D)
    c0 = rows(carry_ref[0])
    c1 = rows(carry_ref[1])
    l_idx = lax.broadcasted_iota(i32, (m, D), 0) & (lt - 1)
    r1 = pltpu.roll(pre, 1, 0)
    r2 = pltpu.roll(pre, 2, 0)
    prev1 = jnp.where(l_idx == 0, c1, r1)
    prev2 = jnp.where(l_idx == 0, c0, jnp.where(l_idx == 1, c1, r2))
    cw = cw_ref[...]
    conv = cw[0:1] * prev2 + cw[1:2] * prev1 + cw[2:3] * pre
    y_a = _dot((proj(0, D) * conv).astype(bf16), woc_ref[...])
    pre3 = pre.reshape(nb, lt, D)
    tail = pre3[:, lt - 2:lt, :]
    carry_ref[0] = tail[:, 0:1, :]
    carry_ref[1] = tail[:, 1:2, :]
    nc_ref[0] = tail

    q = proj(3 * D, 3 * D + DK)
    k = proj(3 * D + DK, 3 * D + 2 * DK)
    v = proj(4 * D, 5 * D)
    g = proj(5 * D, 6 * D)
    z = _dot(hb, wz_ref[...]).astype(bf16)
    pa = _dot(z, wgk_ref[...]) + bgk_ref[...]
    la = (jnp.minimum(pa, 0.0) - jnp.log1p(jnp.exp(-jnp.abs(pa)))) / GATE_NORMALIZER
    pos = lax.broadcasted_iota(i32, (m, DK), 0) & (chunk - 1)
    bcum = la
    shift = 1
    while shift < chunk:
        bcum = bcum + jnp.where(pos >= shift, pltpu.roll(bcum, shift, 0), 0.0)
        shift *= 2
    ngrp = m // chunk
    bend = bcum.reshape(ngrp, chunk, DK)[:, chunk - 1:chunk, :]
    bend_rows = jnp.broadcast_to(bend, (ngrp, chunk, DK)).reshape(m, DK)
    nch = lt // chunk
    assert nb == 1 or nch == 1
    hs = H * nb
    nbat = nch * hs

    def per_head(xm, w):
        x3 = xm.reshape(nb * nch, chunk, H * w)
        parts = [x3[:, :, hh * w:(hh + 1) * w] for hh in range(H)]
        return jnp.stack(parts, axis=1 if nb == 1 else 0).reshape(nbat, chunk, w)

    qd = per_head((q * (DKH ** -0.5) * jnp.exp(bcum)).astype(bf16), DKH)
    kd = per_head((k * jnp.exp(-bcum)).astype(bf16), DKH)
    ke = per_head((k * jnp.exp(bend_rows - bcum)).astype(bf16), DKH)
    vv = per_head(v.astype(bf16), DVH)
    assert nbat <= 128
    a_end = per_head(jnp.exp(bend_rows), DKH)[:, 0, :]
    a_col = jnp.concatenate([a_end, jnp.zeros((128 - nbat, DKH), f32)], axis=0).T
    causal = lax.broadcasted_iota(i32, (chunk, chunk), 1) <= lax.broadcasted_iota(i32, (chunk, chunk), 0)
    sc = jnp.einsum('bcd,bsd->bcs', qd, kd, preferred_element_type=f32)
    sc = jnp.where(causal, sc, 0.0).astype(bf16)
    o = jnp.einsum('bcs,bse->bce', sc, vv, preferred_element_type=f32)
    upd = jnp.einsum('bsd,bse->bde', ke, vv, preferred_element_type=f32)
    st = [gs_ref[s % nb, s // nb] if has_state else st_ref[s // nb, s % nb] for s in range(hs)]
    for j in range(nch):
        for s in range(hs):
            b = j * hs + s
            sin_ref[b] = st[s].astype(bf16)
            st[s] = st[s] * a_col[:, b:b + 1] + upd[b]
    for s in range(hs):
        if has_state:
            ng_ref[0, s % nb, s // nb] = st[s]
        else:
            st_ref[s // nb, s % nb] = st[s]
    o = o + jnp.einsum('bcd,bde->bce', qd, sin_ref[...], preferred_element_type=f32)
    on = _rms(o, gnw_ref[...])
    g3 = g.reshape(nb, lt, DV)
    for j in range(nch):
        js = slice(j * chunk, (j + 1) * chunk)
        for hh in range(H):
            vs = slice(hh * DVH, (hh + 1) * DVH)
            gh = g3[:, js, vs]
            b0 = (j * H + hh) * nb
            og_ref[:, js, vs] = (on[b0:b0 + nb] * (gh * jax.nn.sigmoid(gh))).astype(bf16)

    if not has_state:
        @pl.when(i == last)
        def _():
            for hh in range(H):
                ng_ref[0, :, hh] = st_ref[hh]

    y_b = _dot(og_ref[...].reshape(m, DV), wog_ref[...])
    u_a = _dot(hb, wu_ref[:, 0:D])
    u_b = _dot(hb, wu_ref[:, D:2 * D])
    merged = (jax.nn.sigmoid(u_a) * y_a + jax.nn.sigmoid(u_b) * y_b).astype(bf16)
    x1 = x + g1 * _dot(merged, wo_ref[...])

    h2 = (_rms(x1, n2_ref[...]) * (1.0 + sc2) + sh2).astype(bf16)
    sg = _dot(h2, wsg_ref[...])
    act = (sg * jax.nn.sigmoid(sg) * _dot(h2, wsu_ref[...])).astype(bf16)
    x2_ref[...] = x1 + g2 * _dot(act, wsd_ref[...])
    h2_ref[...] = h2
    lg_ref[0] = lax.dot_general(rwt_ref[...], h2, (((1,), (1,)), ((), ())), preferred_element_type=f32)


def _mixer(x, mod4, states, weights, nb, lt, chunk):
    bsz, seqlen, _ = x.shape
    has_state = states is not None
    m = nb * lt
    n_lt = seqlen // lt
    assert not has_state or n_lt == 1
    nsteps = (bsz // nb) * n_lt
    t_all = bsz * seqlen

    def tok_blk(b, i):
        return b * n_lt + i

    in_specs = [pl.BlockSpec((nb, lt, D), lambda b, i: (b, i, 0)),
                pl.BlockSpec((6, nb, 1, D), lambda b, i: (0, b, 0, 0))]
    args = [x, mod4]
    if has_state:
        in_specs += [pl.BlockSpec((nb, 2, D), lambda b, i: (b, 0, 0)),
                     pl.BlockSpec((nb, H, DKH, DVH), lambda b, i: (b, 0, 0, 0))]
        args += list(states)
    in_specs += [_const_spec(w.shape) for w in weights]
    args += list(weights)
    out_specs = [pl.BlockSpec((m, D), lambda b, i: (tok_blk(b, i), 0)),
                 pl.BlockSpec((m, D), lambda b, i: (tok_blk(b, i), 0)),
                 pl.BlockSpec((1, E, m), lambda b, i: (b * n_lt + i, 0, 0)),
                 pl.BlockSpec((1, nb, 2, D), lambda b, i: (0, b, 0, 0)),
                 pl.BlockSpec((1, nb, H, DKH, DVH), lambda b, i: (0, b, 0, 0, 0))]
    out_shape = [jax.ShapeDtypeStruct((t_all, D), f32),
                 jax.ShapeDtypeStruct((t_all, D), bf16),
                 jax.ShapeDtypeStruct((nsteps, E, m), f32),
                 jax.ShapeDtypeStruct((1, bsz, 2, D), f32),
                 jax.ShapeDtypeStruct((1, bsz, H, DKH, DVH), f32)]
    return pl.pallas_call(
        functools.partial(_mixer_body, nb, lt, chunk, has_state),
        grid=(bsz // nb, n_lt),
        in_specs=in_specs,
        out_specs=out_specs,
        out_shape=out_shape,
        scratch_shapes=[pltpu.VMEM((2, nb, 1, D), f32),
                        pltpu.VMEM((H, nb, DKH, DVH), f32),
                        pltpu.VMEM((nb, lt, DV), bf16),
                        pltpu.VMEM(((lt // chunk) * H * nb, DKH, DVH), bf16)],
        compiler_params=pltpu.CompilerParams(dimension_semantics=("arbitrary", "arbitrary"),
                                             vmem_limit_bytes=VMEM_LIMIT),
        name="mixer_state" if has_state else "mixer_prompt",
    )(*args)


def _route_body(lg_ref, bias_ref, eidx_ref, w_ref, lrank_ref, cnt_ref):
    step = pl.program_id(0)
    tl = lg_ref.shape[1]

    scores = jax.nn.sigmoid(lg_ref[...])
    choice = scores + bias_ref[...]
    gsz = E // NGROUPS
    grp = choice.reshape(NGROUPS, gsz, tl)
    mi = lax.broadcasted_iota(i32, (NGROUPS, gsz, tl), 1)
    m1 = jnp.max(grp, axis=1, keepdims=True)
    first = jnp.min(jnp.where(grp == m1, mi, gsz), axis=1, keepdims=True)
    m2 = jnp.max(jnp.where(mi == first, NEG, grp), axis=1, keepdims=True)
    gscore = (m1 + m2).reshape(NGROUPS, tl)
    gi = lax.broadcasted_iota(i32, (NGROUPS, tl), 0)
    gsel = jnp.zeros((NGROUPS, tl), f32)
    work = gscore
    for _ in range(TOPK_GROUPS):
        mx = jnp.max(work, axis=0, keepdims=True)
        f = jnp.min(jnp.where(work == mx, gi, NGROUPS), axis=0, keepdims=True)
        hit = gi == f
        gsel = jnp.where(hit, 1.0, gsel)
        work = jnp.where(hit, NEG, work)
    emask = jnp.broadcast_to(gsel.reshape(NGROUPS, 1, tl), (NGROUPS, gsz, tl)).reshape(E, tl)
    masked = jnp.where(emask > 0.0, choice, NEG)
    ei = lax.broadcasted_iota(i32, (E, tl), 0)
    sel = jnp.zeros((E, tl), f32)
    hits, ws = [], []
    for kk in range(TOPK):
        mx = jnp.max(masked, axis=0, keepdims=True)
        f = jnp.min(jnp.where(masked == mx, ei, E), axis=0, keepdims=True)
        hit = ei == f
        ws.append(jnp.sum(jnp.where(hit, scores, 0.0), axis=0, keepdims=True))
        masked = jnp.where(hit, NEG, masked)
        sel = jnp.where(hit, 1.0, sel)
        hits.append(hit)
        eidx_ref[kk:kk + 1, :] = f
    wsum = ws[0]
    for t in ws[1:]:
        wsum = wsum + t
    for kk in range(TOPK):
        w_ref[kk:kk + 1, :] = ws[kk] / wsum * ROUTED_SCALE

    ui = lax.broadcasted_iota(i32, (tl, tl), 0)
    uj = lax.broadcasted_iota(i32, (tl, tl), 1)
    upper = ((ui <= uj) & ((ui & -TILE) == (uj & -TILE))).astype(bf16)
    lrank = _dot(sel.astype(bf16), upper) - sel
    for kk in range(TOPK):
        lrank_ref[kk:kk + 1, :] = jnp.sum(jnp.where(hits[kk], lrank, 0.0), axis=0, keepdims=True).astype(i32)

    @pl.when(step == 0)
    def _():
        cnt_ref[...] = jnp.zeros_like(cnt_ref)
    lane = lax.broadcasted_iota(i32, cnt_ref.shape, 1)
    tiles_per_step = tl // TILE
    for sub in range(tiles_per_step):
        col = jnp.sum(sel[:, sub * TILE:(sub + 1) * TILE], axis=1, keepdims=True)
        cnt_ref[...] = jnp.where(lane == step * tiles_per_step + sub, col, cnt_ref[...])


def _route(logits_t, bias_col):
    t_all = logits_t.shape[1]
    tl = ROUTE_TILES * TILE
    assert t_all % tl == 0 and t_all // TILE <= MAX_TILES
    tok = lambda s: (0, s)
    return pl.pallas_call(
        _route_body,
        grid=(t_all // tl,),
        in_specs=[pl.BlockSpec((E, tl), tok), pl.BlockSpec((E, 1), lambda s: (0, 0))],
        out_specs=[pl.BlockSpec((TOPK, tl), tok), pl.BlockSpec((TOPK, tl), tok), pl.BlockSpec((TOPK, tl), tok),
                   pl.BlockSpec((E, MAX_TILES), lambda s: (0, 0))],
        out_shape=[jax.ShapeDtypeStruct((TOPK, t_all), i32), jax.ShapeDtypeStruct((TOPK, t_all), f32),
                   jax.ShapeDtypeStruct((TOPK, t_all), i32), jax.ShapeDtypeStruct((E, MAX_TILES), f32)],
        compiler_params=pltpu.CompilerParams(dimension_semantics=("arbitrary",), vmem_limit_bytes=VMEM_LIMIT),
        name="route",
    )(logits_t, bias_col)


def _plan_body(eidx_ref, lrank_ref, cnt_ref, slot_ref, n_ref, l_ref, g_ref, blk_ref):
    t_all = eidx_ref.shape[1]
    cnt = cnt_ref[...]
    nrun = jnp.floor((cnt + (RUN_ALIGN - 1)) * (1.0 / RUN_ALIGN)) * RUN_ALIGN
    ti = lax.broadcasted_iota(i32, (MAX_TILES, MAX_TILES), 0)
    tj = lax.broadcasted_iota(i32, (MAX_TILES, MAX_TILES), 1)
    before = _dotx01(nrun, (ti < tj).astype(bf16))
    region = jnp.sum(nrun, axis=1, keepdims=True)
    nblocks = jnp.floor((region + (MOE_BM - 1)) * (1.0 / MOE_BM))
    ri = lax.broadcasted_iota(i32, (E, E), 0)
    ci = lax.broadcasted_iota(i32, (E, E), 1)
    below = (ci < ri).astype(bf16)
    bstart = _dot01(below, jnp.broadcast_to(nblocks, (E, MAX_TILES)))
    loff = _dot01(below, nrun)
    n_ref[...] = nrun.astype(i32)
    l_ref[...] = loff.astype(i32)
    g_ref[...] = (bstart * MOE_BM + before).astype(i32)
    lane0 = lax.broadcasted_iota(i32, (E, MAX_TILES), 1) == 0
    blk_ref[...] = jnp.where(lane0, bstart, nblocks).astype(i32)

    tile_of = lax.shift_right_logical(lax.broadcasted_iota(i32, (MAX_TILES, t_all), 1), TILE.bit_length() - 1)
    spread = (lax.broadcasted_iota(i32, (MAX_TILES, t_all), 0) == tile_of).astype(bf16)
    ltok = _dotx01(loff, spread)
    ei = lax.broadcasted_iota(i32, (E, t_all), 0)
    for kk in range(TOPK):
        hit = ei == eidx_ref[kk:kk + 1, :]
        start = jnp.sum(jnp.where(hit, ltok, 0.0), axis=0, keepdims=True)
        slot_ref[kk:kk + 1, :] = start.astype(i32) + lrank_ref[kk:kk + 1, :]


def _plan(eidx, lrank, cnt):
    t_all = eidx.shape[1]
    assert TILE & (TILE - 1) == 0
    tl = t_all
    tok = lambda s: (0, s)
    table = pl.BlockSpec((E, MAX_TILES), lambda s: (0, 0))
    return pl.pallas_call(
        _plan_body,
        grid=(1,),
        in_specs=[pl.BlockSpec((TOPK, tl), tok), pl.BlockSpec((TOPK, tl), tok), table],
        out_specs=[pl.BlockSpec((TOPK, tl), tok), table, table, table, table],
        out_shape=[jax.ShapeDtypeStruct((TOPK, t_all), i32)] + [jax.ShapeDtypeStruct((E, MAX_TILES), i32)] * 4,
        compiler_params=pltpu.CompilerParams(dimension_semantics=("arbitrary",), vmem_limit_bytes=VMEM_LIMIT),
        name="plan",
    )(eidx, lrank, cnt)


def _for_each_piece(total, pieces, fn):
    for j, b in enumerate(pieces):
        @pl.when((total & b) != 0)
        def _():
            fn(pl.multiple_of(total & ~(2 * b - 1), pieces[-1]), b, j)


def _for_each_run(n_ref, l_ref, g_ref, tile, fn):
    def per_expert(e, c):
        lo = pl.multiple_of(l_ref[e, tile], RUN_ALIGN)
        go = pl.multiple_of(g_ref[e, tile], RUN_ALIGN)
        _for_each_piece(n_ref[e, tile], RUN_PIECES, lambda off, b, j: fn(lo + off, go + off, b, j))
        return c
    lax.fori_loop(0, E, per_expert, 0)


def _start(copy, j):
    copy.start(priority=j % 2)


def _wait_rows(make_copy, rows):
    _for_each_piece(rows, WAIT_PIECES, lambda off, b, j: make_copy(b).wait())


def _dispatch_body(tiles_a, n_ref, l_ref, g_ref, slot_ref, ha_ref, hb_ref, xs_ref, buf_ref, zero_ref, sem, zsem):
    tile = pl.program_id(0)
    nsteps = pl.num_programs(0)
    cur = tile % RING

    slot = slot_ref[...].astype(i16)
    h = jnp.where(tile < tiles_a, ha_ref[...], hb_ref[...])
    filled = l_ref[E - 1, tile] + n_ref[E - 1, tile]

    def order(row0, nrows):
        srow = lax.broadcasted_iota(i32, (nrows, TILE), 0).astype(i16) + jnp.int16(row0)
        onehot = jnp.zeros((nrows, TILE), bf16)
        for kk in range(TOPK):
            onehot = jnp.where(srow == slot[kk:kk + 1, :], jnp.ones((), bf16), onehot)
        buf_ref[cur, row0:row0 + nrows, :] = _dot(onehot, h).astype(bf16)

    order(0, SLOTS_MAIN)

    @pl.when(filled > SLOTS_MAIN)
    def _():
        order(SLOTS_MAIN, SLOTS - SLOTS_MAIN)

    def run_copy(sl):
        return lambda srow_, grow, b: pltpu.make_async_copy(
            buf_ref.at[sl, pl.ds(srow_, b)], xs_ref.at[pl.ds(grow, b)], sem.at[sl])

    _for_each_run(n_ref, l_ref, g_ref, tile, lambda s, g, b, j: _start(run_copy(cur)(s, g, b), j))

    def wait_tile(t, sl):
        _wait_rows(lambda b: run_copy(sl)(0, 0, b), l_ref[E - 1, t] + n_ref[E - 1, t])

    @pl.when(tile >= RING - 1)
    def _():
        wait_tile(tile - (RING - 1), (tile + 1) % RING)

    last = nsteps - 1

    def zero_fill(fn):
        def per_expert(e, c):
            end = pl.multiple_of(g_ref[e, last] + n_ref[e, last], RUN_ALIGN)
            _for_each_piece((-end) & (MOE_BM - 1), TAIL_PIECES, lambda off, b, j: fn(pltpu.make_async_copy(
                zero_ref.at[pl.ds(0, b)], xs_ref.at[pl.ds(end + off, b)], zsem)))
            return c
        lax.fori_loop(0, E, per_expert, 0)
        used = (g_ref[E - 1, last] + n_ref[E - 1, last] + (MOE_BM - 1)) // MOE_BM

        def per_block(blk, c):
            fn(pltpu.make_async_copy(zero_ref, xs_ref.at[pl.ds(pl.multiple_of(blk * MOE_BM, MOE_BM), MOE_BM)], zsem))
            return c
        lax.fori_loop(used, xs_ref.shape[0] // MOE_BM, per_block, 0)

    @pl.when(tile == 0)
    def _():
        zero_ref[...] = jnp.zeros_like(zero_ref)
        zero_fill(lambda cp: cp.start())

    @pl.when(tile == last)
    def _():
        for back in range(RING - 2, -1, -1):
            @pl.when(tile >= back)
            def _():
                wait_tile(tile - back, (tile - back) % RING)
        zero_fill(lambda cp: cp.wait())


def _dispatch(tables, slot, h2_a, h2_b, nrows):
    tiles_a, tiles_b = h2_a.shape[0] // TILE, h2_b.shape[0] // TILE
    return pl.pallas_call(
        functools.partial(_dispatch_body, tiles_a),
        grid_spec=pltpu.PrefetchScalarGridSpec(
            num_scalar_prefetch=len(tables),
            grid=(tiles_a + tiles_b,),
            in_specs=[pl.BlockSpec((TOPK, TILE), lambda i, *_: (0, i)),
                      pl.BlockSpec((TILE, D), lambda i, *_: (jnp.minimum(i, tiles_a - 1), 0)),
                      pl.BlockSpec((TILE, D), lambda i, *_: (jnp.maximum(i - tiles_a, 0), 0))],
            out_specs=pl.BlockSpec(memory_space=pl.ANY),
            scratch_shapes=[pltpu.VMEM((RING, SLOTS, D), bf16), pltpu.VMEM((MOE_BM, D), bf16),
                            pltpu.SemaphoreType.DMA((RING,)), pltpu.SemaphoreType.DMA]),
        out_shape=jax.ShapeDtypeStruct((nrows, D), bf16),
        compiler_params=pltpu.CompilerParams(dimension_semantics=("arbitrary",), vmem_limit_bytes=VMEM_LIMIT),
        name="dispatch",
    )(*tables, slot, h2_a, h2_b)


def _expert_body(blk_ref, wg_ref, wu_ref, wd_ref, xs_ref, ys_ref, xbuf_ref, ybuf_ref, wgu_ref, wdb_ref, isem, osem):
    e = pl.program_id(0)
    first, count = blk_ref[e, 0], blk_ref[e, 1]
    used = blk_ref[E - 1, 0] + blk_ref[E - 1, 1]
    wgu_ref[:, :DE] = wg_ref[0].astype(bf16)
    wgu_ref[:, DE:] = wu_ref[0].astype(bf16)
    wdb_ref[...] = wd_ref[0].astype(bf16)

    def rows(g):
        return pl.ds(pl.multiple_of(g * MOE_BM, MOE_BM), MOE_BM)

    def load(g):
        sl = g % EXPERT_IN_SLOTS
        return pltpu.make_async_copy(xs_ref.at[rows(g)], xbuf_ref.at[sl], isem.at[sl])

    def store(g):
        sl = g % 2
        return pltpu.make_async_copy(ybuf_ref.at[sl], ys_ref.at[rows(g)], osem.at[sl])

    @pl.when(e == 0)
    def _():
        for g in range(EXPERT_IN_SLOTS - 1):
            @pl.when(g < used)
            def _():
                load(g).start()

    def block(j, c):
        g = first + j

        @pl.when(g + (EXPERT_IN_SLOTS - 1) < used)
        def _():
            load(g + (EXPERT_IN_SLOTS - 1)).start()
        load(g).wait()

        @pl.when(g >= 2)
        def _():
            store(g - 2).wait()
        hgu = _dot(xbuf_ref[g % EXPERT_IN_SLOTS], wgu_ref[...])
        hg = hgu[:, :DE]
        act = (hg * jax.nn.sigmoid(hg) * hgu[:, DE:]).astype(bf16)
        ybuf_ref[g % 2] = _dot(act, wdb_ref[...]).astype(bf16)
        store(g).start()
        return c
    lax.fori_loop(0, count, block, 0)

    @pl.when(e == E - 1)
    def _():
        for back in (2, 1):
            @pl.when(used >= back)
            def _():
                store(used - back).wait()


def _experts(blk, xs, we_gate, we_up, we_down):
    ex = lambda e, blk_: (e, 0, 0)
    return pl.pallas_call(
        _expert_body,
        grid_spec=pltpu.PrefetchScalarGridSpec(
            num_scalar_prefetch=1,
            grid=(E,),
            in_specs=[pl.BlockSpec((1, D, DE), ex), pl.BlockSpec((1, D, DE), ex), pl.BlockSpec((1, DE, D), ex),
                      pl.BlockSpec(memory_space=pl.ANY)],
            out_specs=pl.BlockSpec(memory_space=pl.ANY),
            scratch_shapes=[pltpu.VMEM((EXPERT_IN_SLOTS, MOE_BM, D), bf16), pltpu.VMEM((2, MOE_BM, D), bf16),
                            pltpu.VMEM((D, 2 * DE), bf16), pltpu.VMEM((DE, D), bf16),
                            pltpu.SemaphoreType.DMA((EXPERT_IN_SLOTS,)), pltpu.SemaphoreType.DMA((2,))]),
        out_shape=jax.ShapeDtypeStruct(xs.shape, xs.dtype),
        input_output_aliases={4: 0},
        compiler_params=pltpu.CompilerParams(dimension_semantics=("arbitrary",), vmem_limit_bytes=VMEM_LIMIT),
        name="experts",
    )(blk, we_gate, we_up, we_down, xs)


def _final_body(nb, lt, tile0, n_ref, l_ref, g_ref, slot_ref, wt_ref, x2_ref, g2_ref, fw_ref, ys_ref, o_ref,
                buf_ref, routed_ref, sem):
    m = nb * lt
    n_inner = pl.num_programs(1)
    step = pl.program_id(0) * n_inner + pl.program_id(1)
    nsteps = pl.num_programs(0) * n_inner
    tile = tile0 + step
    cur = step % RING

    def run_copy(sl):
        return lambda srow, grow, b: pltpu.make_async_copy(
            ys_ref.at[pl.ds(grow, b)], buf_ref.at[sl, pl.ds(srow, b)], sem.at[sl])

    def fetch(ahead):
        sl = (step + ahead) % RING
        _for_each_run(n_ref, l_ref, g_ref, tile + ahead, lambda s, g, b, j: _start(run_copy(sl)(s, g, b), j))

    @pl.when(step == 0)
    def _():
        buf_ref[...] = jnp.zeros_like(buf_ref)
        for ahead in range(RING - 1):
            @pl.when(ahead < nsteps)
            def _():
                fetch(ahead)

    @pl.when(step + (RING - 1) < nsteps)
    def _():
        fetch(RING - 1)

    _wait_rows(lambda b: run_copy(cur)(0, 0, b), l_ref[E - 1, tile] + n_ref[E - 1, tile])

    slot = slot_ref[...].astype(i16)
    wt = wt_ref[...].astype(bf16)

    def combine(col0, ncols):
        scol = lax.broadcasted_iota(i32, (m, ncols), 1).astype(i16) + jnp.int16(col0)
        wsel = jnp.zeros((m, ncols), bf16)
        for kk in range(TOPK):
            wsel = jnp.where(scol == slot[:, kk:kk + 1], wt[:, kk:kk + 1], wsel)
        return _dot(wsel, buf_ref[cur, col0:col0 + ncols, :])

    routed_ref[...] = combine(0, SLOTS_MAIN)

    @pl.when(l_ref[E - 1, tile] + n_ref[E - 1, tile] > SLOTS_MAIN)
    def _():
        routed_ref[...] += combine(SLOTS_MAIN, SLOTS - SLOTS_MAIN)
    routed = routed_ref[...]
    g2 = g2_ref[0]
    if nb == 1:
        g2 = g2.reshape(1, D)
    else:
        g2 = jnp.broadcast_to(g2, (nb, lt, D)).reshape(m, D)
    y = x2_ref[...] + g2 * routed
    o_ref[...] = _rms(y, fw_ref[...]).reshape(nb, lt, D)


def _final(tables, slot_t, wts_t, x2, ys, mod4, fw, bsz, seqlen, row0, nb, lt):
    m = nb * lt
    assert m == TILE and row0 % TILE == 0
    n_lt = seqlen // lt
    tile0 = row0 // TILE
    tile = lambda b, i: tile0 + b * n_lt + i
    return pl.pallas_call(
        functools.partial(_final_body, nb, lt, tile0),
        grid_spec=pltpu.PrefetchScalarGridSpec(
            num_scalar_prefetch=len(tables),
            grid=(bsz // nb, n_lt),
            in_specs=[pl.BlockSpec((m, TOPK), lambda b, i, *_: (tile(b, i), 0)),
                      pl.BlockSpec((m, TOPK), lambda b, i, *_: (tile(b, i), 0)),
                      pl.BlockSpec((m, D), lambda b, i, *_: (b * n_lt + i, 0)),
                      pl.BlockSpec((1, nb, 1, D), lambda b, i, *_: (5, b, 0, 0)),
                      pl.BlockSpec((1, D), lambda b, i, *_: (0, 0)),
                      pl.BlockSpec(memory_space=pl.ANY)],
            out_specs=pl.BlockSpec((nb, lt, D), lambda b, i, *_: (b, i, 0)),
            scratch_shapes=[pltpu.VMEM((RING, SLOTS, D), bf16), pltpu.VMEM((m, D), f32),
                            pltpu.SemaphoreType.DMA((RING,))]),
        out_shape=jax.ShapeDtypeStruct((bsz, seqlen, D), f32),
        compiler_params=pltpu.CompilerParams(dimension_semantics=("arbitrary", "arbitrary"),
                                             vmem_limit_bytes=VMEM_LIMIT),
        name="final",
    )(*tables, slot_t, wts_t, x2, mod4, fw, ys)


def kernel(x_prompt, x_sample, state_conv, state_gla, c_prompt, c_sample, w_ada, b_ada, norm1_w, w_in, conv_w,
           w_gk, b_gk, gla_norm_w, w_out_conv, w_out_gla, w_o, norm2_w, router_w, router_bias, we_gate, we_up,
           we_down, ws_gate, ws_up, ws_down, final_norm_w):
    assert w_ada.shape[0] == 1, "single-layer step"
    bp, lp, _ = x_prompt.shape
    bs, ls, _ = x_sample.shape
    tp, ts = bp * lp, bs * ls
    t_all = tp + ts

    w_in0 = w_in[0]
    n_main = 3 * D + 2 * DK + 2 * DV
    rank = w_gk.shape[1]
    weights = (
        norm1_w[0].reshape(1, D),
        w_in0[:, :n_main].astype(bf16),
        jnp.pad(w_in0[:, n_main:n_main + rank], ((0, 0), (0, RANK_PAD - rank))).astype(bf16),
        w_in0[:, n_main + rank:].astype(bf16),
        jnp.pad(w_gk[0], ((0, RANK_PAD - rank), (0, 0))).astype(bf16),
        b_gk[0].reshape(1, DK),
        conv_w[0],
        gla_norm_w[0].reshape(1, DVH),
        w_out_conv[0].astype(bf16),
        w_out_gla[0].astype(bf16),
        w_o[0].astype(bf16),
        norm2_w[0].reshape(1, D),
        router_w[0].T.astype(bf16),
        ws_gate[0].astype(bf16),
        ws_up[0].astype(bf16),
        ws_down[0].astype(bf16),
    )

    mod = _adaln(jnp.concatenate([c_prompt, c_sample], axis=0), w_ada[0], b_ada[0].reshape(1, 6 * D))
    mod4 = mod.reshape(bp + bs, 6, 1, D).transpose(1, 0, 2, 3)
    mod_p, mod_s = mod4[:, :bp], mod4[:, bp:]

    lt_p = 512
    x2_p, h2_p, lg_p, conv_p, gla_p = _mixer(x_prompt, mod_p, None, weights, 1, lt_p, GLA_CHUNK)
    nb_s = 8
    x2_s, h2_s, lg_s, conv_s, gla_s = _mixer(x_sample, mod_s, (state_conv[0], state_gla[0]), weights,
                                             nb_s, ls, math.gcd(ls, GLA_CHUNK))
    lg = jnp.concatenate([lg_p.transpose(1, 0, 2).reshape(E, tp), lg_s.transpose(1, 0, 2).reshape(E, ts)], axis=1)

    eidx, wts, lrank, cnt = _route(lg, router_bias[0].reshape(E, 1))
    n_tiles = t_all // TILE
    max_rows = t_all * TOPK + n_tiles * E * (RUN_ALIGN - 1) + E * (MOE_BM - RUN_ALIGN)
    nblk = -(-max_rows // MOE_BM)
    slot, nrun, loff, goff, blk = _plan(eidx, lrank, cnt)
    tables = (nrun, loff, goff)

    xs = _dispatch(tables, slot, h2_p, h2_s, nblk * MOE_BM)
    ys = _experts(blk, xs, we_gate[0], we_up[0], we_down[0])

    fw = final_norm_w.reshape(1, D)
    slot_t, wts_t = slot.T, wts.T
    y_prompt = _final(tables, slot_t, wts_t, x2_p, ys, mod_p, fw, bp, lp, 0, 1, TILE)
    y_sample = _final(tables, slot_t, wts_t, x2_s, ys, mod_s, fw, bs, ls, tp, TILE // ls, ls)
    return (y_prompt, y_sample, conv_p, gla_p, conv_s, gla_s)
```

```python
import functools
import math

import jax
import jax.numpy as jnp
from jax import lax
from jax.experimental import pallas as pl
from jax.experimental.pallas import tpu as pltpu

f32 = jnp.float32
bf16 = jnp.bfloat16
i32 = jnp.int32
i16 = jnp.int16

D = 1024
H = 4
DKH = 128
DVH = 256
DK = H * DKH
DV = H * DVH
RANK_PAD = 128
GATE_NORMALIZER = 16.0
GLA_CHUNK = 64
E = 64
TOPK = 8
NGROUPS = 8
TOPK_GROUPS = 4
DE = 256
ROUTED_SCALE = 2.5
EPS = 1e-6
NEG = float("-inf")

V7X_VMEM_BYTES = 64 * 1024 * 1024
VMEM_LIMIT = V7X_VMEM_BYTES - 6 * 1024 * 1024
MOE_BM = 512
EXPERT_IN_SLOTS = 6
RING = 2
TILE = 256
MAX_TILES = 128
ROUTE_TILES = 2
RUN_ALIGN = 16
SLOTS = TILE * TOPK + 1024
SLOTS_MAIN = TILE * TOPK + 512
RUN_PIECES = (256, 128, 64, 32, 16)
TAIL_PIECES = (256, 128, 64, 32, 16)
WAIT_PIECES = (2048, 1024, 512, 256, 128, 64, 32, 16)

def _dot(a, b):
    return jnp.dot(a, b, preferred_element_type=f32)


def _split3(x):
    hi = x.astype(bf16)
    r1 = x - hi.astype(f32)
    mid = r1.astype(bf16)
    lo = (r1 - mid.astype(f32)).astype(bf16)
    return hi, mid, lo


def _dot01(m01, x):
    hi, mid, lo = _split3(x)
    return _dot(m01, hi) + _dot(m01, mid) + _dot(m01, lo)


def _dotx01(x, m01):
    hi, mid, lo = _split3(x)
    return _dot(hi, m01) + _dot(mid, m01) + _dot(lo, m01)


def _rms(v, w):
    ms = jnp.mean(v * v, axis=-1, keepdims=True)
    return v * lax.rsqrt(ms + EPS) * w


def _const_spec(shape):
    n = len(shape)
    return pl.BlockSpec(shape, lambda *_: (0,) * n, pipeline_mode=pl.Buffered(1))


def _adaln_body(c_ref, w_ref, b_ref, o_ref):
    c = c_ref[...]
    a = (c * jax.nn.sigmoid(c)).astype(bf16)
    o_ref[...] = _dot(a, w_ref[...].astype(bf16)) + b_ref[...]


def _adaln(c_all, w_ada, b_ada):
    nrow = c_all.shape[0]
    tn = 1536
    return pl.pallas_call(
        _adaln_body,
        grid=(6 * D // tn,),
        in_specs=[pl.BlockSpec((nrow, D), lambda j: (0, 0)),
                  pl.BlockSpec((D, tn), lambda j: (0, j)),
                  pl.BlockSpec((1, tn), lambda j: (0, j))],
        out_specs=pl.BlockSpec((nrow, tn), lambda j: (0, j)),
        out_shape=jax.ShapeDtypeStruct((nrow, 6 * D), f32),
        compiler_params=pltpu.CompilerParams(dimension_semantics=("arbitrary",), vmem_limit_bytes=VMEM_LIMIT),
        name="adaln",
    )(c_all, w_ada, b_ada)


def _mixer_body(nb, lt, chunk, has_state, *refs):
    if has_state:
        (x_ref, mod_ref, cs_ref, gs_ref, *rest) = refs
    else:
        (x_ref, mod_ref, *rest) = refs
        cs_ref = gs_ref = None
    (n1_ref, wmain_ref, wz_ref, wu_ref, wgk_ref, bgk_ref, cw_ref, gnw_ref, woc_ref, wog_ref, wo_ref,
     n2_ref, rwt_ref, wsg_ref, wsu_ref, wsd_ref,
     x2_ref, h2_ref, lg_ref, nc_ref, ng_ref, carry_ref, st_ref, og_ref, sin_ref) = rest
    m = nb * lt
    i = pl.program_id(1)
    last = pl.num_programs(1) - 1

    def rows(v):
        w = v.shape[-1]
        if nb == 1:
            return v.reshape(1, w)
        return jnp.broadcast_to(v, (nb, lt, w)).reshape(m, w)

    @pl.when(i == 0)
    def _():
        if has_state:
            cs = cs_ref[...]
            carry_ref[0] = cs[:, 0:1, :]
            carry_ref[1] = cs[:, 1:2, :]
        else:
            carry_ref[...] = jnp.zeros_like(carry_ref)
            st_ref[...] = jnp.zeros_like(st_ref)

    x = x_ref[...].reshape(m, D)
    sh1, sc1, g1, sh2, sc2, g2 = [rows(mod_ref[j]) for j in range(6)]
    hb = (_rms(x, n1_ref[...]) * (1.0 + sc1) + sh1).astype(bf16)

    def proj(lo, hi):
        return _dot(hb, wmain_ref[:, lo:hi])

    pre = proj(D, 2 * D) * proj(2 * D, 3 * D)
    c0 = rows(carry_ref[0])
    c1 = rows(carry_ref[1])
    l_idx = lax.broadcasted_iota(i32, (m, D), 0) & (lt - 1)
    r1 = pltpu.roll(pre, 1, 0)
    r2 = pltpu.roll(pre, 2, 0)
    prev1 = jnp.where(l_idx == 0, c1, r1)
    prev2 = jnp.where(l_idx == 0, c0, jnp.where(l_idx == 1, c1, r2))
    cw = cw_ref[...]
    conv = cw[0:1] * prev2 + cw[1:2] * prev1 + cw[2:3] * pre
    y_a = _dot((proj(0, D) * conv).astype(bf16), woc_ref[...])
    pre3 = pre.reshape(nb, lt, D)
    tail = pre3[:, lt - 2:lt, :]
    carry_ref[0] = tail[:, 0:1, :]
    carry_ref[1] = tail[:, 1:2, :]
    nc_ref[0] = tail

    q = proj(3 * D, 3 * D + DK)
    k = proj(3 * D + DK, 3 * D + 2 * DK)
    v = proj(4 * D, 5 * D)
    g = proj(5 * D, 6 * D)
    z = _dot(hb, wz_ref[...]).astype(bf16)
    pa = _dot(z, wgk_ref[...]) + bgk_ref[...]
    la = (jnp.minimum(pa, 0.0) - jnp.log1p(jnp.exp(-jnp.abs(pa)))) / GATE_NORMALIZER
    pos = lax.broadcasted_iota(i32, (m, DK), 0) & (chunk - 1)
    bcum = la
    shift = 1
    while shift < chunk:
        bcum = bcum + jnp.where(pos >= shift, pltpu.roll(bcum, shift, 0), 0.0)
        shift *= 2
    ngrp = m // chunk
    bend = bcum.reshape(ngrp, chunk, DK)[:, chunk - 1:chunk, :]
    bend_rows = jnp.broadcast_to(bend, (ngrp, chunk, DK)).reshape(m, DK)
    nch = lt // chunk
    assert nb == 1 or nch == 1
    hs = H * nb
    nbat = nch * hs

    def per_head(xm, w):
        x3 = xm.reshape(nb * nch, chunk, H * w)
        parts = [x3[:, :, hh * w:(hh + 1) * w] for hh in range(H)]
        return jnp.stack(parts, axis=1 if nb == 1 else 0).reshape(nbat, chunk, w)

    qd = per_head((q * (DKH ** -0.5) * jnp.exp(bcum)).astype(bf16), DKH)
    kd = per_head((k * jnp.exp(-bcum)).astype(bf16), DKH)
    ke = per_head((k * jnp.exp(bend_rows - bcum)).astype(bf16), DKH)
    vv = per_head(v.astype(bf16), DVH)
    assert nbat <= 128
    a_end = per_head(jnp.exp(bend_rows), DKH)[:, 0, :]
    a_col = jnp.concatenate([a_end, jnp.zeros((128 - nbat, DKH), f32)], axis=0).T
    causal = lax.broadcasted_iota(i32, (chunk, chunk), 1) <= lax.broadcasted_iota(i32, (chunk, chunk), 0)
    sc = jnp.einsum('bcd,bsd->bcs', qd, kd, preferred_element_type=f32)
    sc = jnp.where(causal, sc, 0.0).astype(bf16)
    o = jnp.einsum('bcs,bse->bce', sc, vv, preferred_element_type=f32)
    upd = jnp.einsum('bsd,bse->bde', ke, vv, preferred_element_type=f32)
    st = [gs_ref[s % nb, s // nb] if has_state else st_ref[s // nb, s % nb] for s in range(hs)]
    for j in range(nch):
        for s in range(hs):
            b = j * hs + s
            sin_ref[b] = st[s].astype(bf16)
            st[s] = st[s] * a_col[:, b:b + 1] + upd[b]
    for s in range(hs):
        if has_state:
            ng_ref[0, s % nb, s // nb] = st[s]
        else:
            st_ref[s // nb, s % nb] = st[s]
    o = o + jnp.einsum('bcd,bde->bce', qd, sin_ref[...], preferred_element_type=f32)
    on = _rms(o, gnw_ref[...])
    g3 = g.reshape(nb, lt, DV)
    for j in range(nch):
        js = slice(j * chunk, (j + 1) * chunk)
        for hh in range(H):
            vs = slice(hh * DVH, (hh + 1) * DVH)
            gh = g3[:, js, vs]
            b0 = (j * H + hh) * nb
            og_ref[:, js, vs] = (on[b0:b0 + nb] * (gh * jax.nn.sigmoid(gh))).astype(bf16)

    if not has_state:
        @pl.when(i == last)
        def _():
            for hh in range(H):
                ng_ref[0, :, hh] = st_ref[hh]

    y_b = _dot(og_ref[...].reshape(m, DV), wog_ref[...])
    u_a = _dot(hb, wu_ref[:, 0:D])
    u_b = _dot(hb, wu_ref[:, D:2 * D])
    merged = (jax.nn.sigmoid(u_a) * y_a + jax.nn.sigmoid(u_b) * y_b).astype(bf16)
    x1 = x + g1 * _dot(merged, wo_ref[...])

    h2 = (_rms(x1, n2_ref[...]) * (1.0 + sc2) + sh2).astype(bf16)
    sg = _dot(h2, wsg_ref[...])
    act = (sg * jax.nn.sigmoid(sg) * _dot(h2, wsu_ref[...])).astype(bf16)
    x2_ref[...] = x1 + g2 * _dot(act, wsd_ref[...])
    h2_ref[...] = h2
    lg_ref[0] = lax.dot_general(rwt_ref[...], h2, (((1,), (1,)), ((), ())), preferred_element_type=f32)


def _mixer(x, mod4, states, weights, nb, lt, chunk):
    bsz, seqlen, _ = x.shape
    has_state = states is not None
    m = nb * lt
    n_lt = seqlen // lt
    assert not has_state or n_lt == 1
    nsteps = (bsz // nb) * n_lt
    t_all = bsz * seqlen

    def tok_blk(b, i):
        return b * n_lt + i

    in_specs = [pl.BlockSpec((nb, lt, D), lambda b, i: (b, i, 0)),
                pl.BlockSpec((6, nb, 1, D), lambda b, i: (0, b, 0, 0))]
    args = [x, mod4]
    if has_state:
        in_specs += [pl.BlockSpec((nb, 2, D), lambda b, i: (b, 0, 0)),
                     pl.BlockSpec((nb, H, DKH, DVH), lambda b, i: (b, 0, 0, 0))]
        args += list(states)
    in_specs += [_const_spec(w.shape) for w in weights]
    args += list(weights)
    out_specs = [pl.BlockSpec((m, D), lambda b, i: (tok_blk(b, i), 0)),
                 pl.BlockSpec((m, D), lambda b, i: (tok_blk(b, i), 0)),
                 pl.BlockSpec((1, E, m), lambda b, i: (b * n_lt + i, 0, 0)),
                 pl.BlockSpec((1, nb, 2, D), lambda b, i: (0, b, 0, 0)),
                 pl.BlockSpec((1, nb, H, DKH, DVH), lambda b, i: (0, b, 0, 0, 0))]
    out_shape = [jax.ShapeDtypeStruct((t_all, D), f32),
                 jax.ShapeDtypeStruct((t_all, D), bf16),
                 jax.ShapeDtypeStruct((nsteps, E, m), f32),
                 jax.ShapeDtypeStruct((1, bsz, 2, D), f32),
                 jax.ShapeDtypeStruct((1, bsz, H, DKH, DVH), f32)]
    return pl.pallas_call(
        functools.partial(_mixer_body, nb, lt, chunk, has_state),
        grid=(bsz // nb, n_lt),
        in_specs=in_specs,
        out_specs=out_specs,
        out_shape=out_shape,
        scratch_shapes=[pltpu.VMEM((2, nb, 1, D), f32),
                        pltpu.VMEM((H, nb, DKH, DVH), f32),
                        pltpu.VMEM((nb, lt, DV), bf16),
                        pltpu.VMEM(((lt // chunk) * H * nb, DKH, DVH), bf16)],
        compiler_params=pltpu.CompilerParams(dimension_semantics=("arbitrary", "arbitrary"),
                                             vmem_limit_bytes=VMEM_LIMIT),
        name="mixer_state" if has_state else "mixer_prompt",
    )(*args)


def _route_body(lg_ref, bias_ref, eidx_ref, w_ref, lrank_ref, cnt_ref):
    step = pl.program_id(0)
    tl = lg_ref.shape[1]

    scores = jax.nn.sigmoid(lg_ref[...])
    choice = scores + bias_ref[...]
    gsz = E // NGROUPS
    grp = choice.reshape(NGROUPS, gsz, tl)
    mi = lax.broadcasted_iota(i32, (NGROUPS, gsz, tl), 1)
    m1 = jnp.max(grp, axis=1, keepdims=True)
    first = jnp.min(jnp.where(grp == m1, mi, gsz), axis=1, keepdims=True)
    m2 = jnp.max(jnp.where(mi == first, NEG, grp), axis=1, keepdims=True)
    gscore = (m1 + m2).reshape(NGROUPS, tl)
    gi = lax.broadcasted_iota(i32, (NGROUPS, tl), 0)
    gsel = jnp.zeros((NGROUPS, tl), f32)
    work = gscore
    for _ in range(TOPK_GROUPS):
        mx = jnp.max(work, axis=0, keepdims=True)
        f = jnp.min(jnp.where(work == mx, gi, NGROUPS), axis=0, keepdims=True)
        hit = gi == f
        gsel = jnp.where(hit, 1.0, gsel)
        work = jnp.where(hit, NEG, work)
    emask = jnp.broadcast_to(gsel.reshape(NGROUPS, 1, tl), (NGROUPS, gsz, tl)).reshape(E, tl)
    masked = jnp.where(emask > 0.0, choice, NEG)
    ei = lax.broadcasted_iota(i32, (E, tl), 0)
    sel = jnp.zeros((E, tl), f32)
    hits, ws = [], []
    for kk in range(TOPK):
        mx = jnp.max(masked, axis=0, keepdims=True)
        f = jnp.min(jnp.where(masked == mx, ei, E), axis=0, keepdims=True)
        hit = ei == f
        ws.append(jnp.sum(jnp.where(hit, scores, 0.0), axis=0, keepdims=True))
        masked = jnp.where(hit, NEG, masked)
        sel = jnp.where(hit, 1.0, sel)
        hits.append(hit)
        eidx_ref[kk:kk + 1, :] = f
    wsum = ws[0]
    for t in ws[1:]:
        wsum = wsum + t
    for kk in range(TOPK):
        w_ref[kk:kk + 1, :] = ws[kk] / wsum * ROUTED_SCALE

    ui = lax.broadcasted_iota(i32, (tl, tl), 0)
    uj = lax.broadcasted_iota(i32, (tl, tl), 1)
    upper = ((ui <= uj) & ((ui & -TILE) == (uj & -TILE))).astype(bf16)
    lrank = _dot(sel.astype(bf16), upper) - sel
    for kk in range(TOPK):
        lrank_ref[kk:kk + 1, :] = jnp.sum(jnp.where(hits[kk], lrank, 0.0), axis=0, keepdims=True).astype(i32)

    @pl.when(step == 0)
    def _():
        cnt_ref[...] = jnp.zeros_like(cnt_ref)
    lane = lax.broadcasted_iota(i32, cnt_ref.shape, 1)
    tiles_per_step = tl // TILE
    for sub in range(tiles_per_step):
        col = jnp.sum(sel[:, sub * TILE:(sub + 1) * TILE], axis=1, keepdims=True)
        cnt_ref[...] = jnp.where(lane == step * tiles_per_step + sub, col, cnt_ref[...])


def _route(logits_t, bias_col):
    t_all = logits_t.shape[1]
    tl = ROUTE_TILES * TILE
    assert t_all % tl == 0 and t_all // TILE <= MAX_TILES
    tok = lambda s: (0, s)
    return pl.pallas_call(
        _route_body,
        grid=(t_all // tl,),
        in_specs=[pl.BlockSpec((E, tl), tok), pl.BlockSpec((E, 1), lambda s: (0, 0))],
        out_specs=[pl.BlockSpec((TOPK, tl), tok), pl.BlockSpec((TOPK, tl), tok), pl.BlockSpec((TOPK, tl), tok),
                   pl.BlockSpec((E, MAX_TILES), lambda s: (0, 0))],
        out_shape=[jax.ShapeDtypeStruct((TOPK, t_all), i32), jax.ShapeDtypeStruct((TOPK, t_all), f32),
                   jax.ShapeDtypeStruct((TOPK, t_all), i32), jax.ShapeDtypeStruct((E, MAX_TILES), f32)],
        compiler_params=pltpu.CompilerParams(dimension_semantics=("arbitrary",), vmem_limit_bytes=VMEM_LIMIT),
        name="route",
    )(logits_t, bias_col)


def _plan_body(eidx_ref, lrank_ref, cnt_ref, slot_ref, n_ref, l_ref, g_ref, blk_ref):
    t_all = eidx_ref.shape[1]
    cnt = cnt_ref[...]
    nrun = jnp.floor((cnt + (RUN_ALIGN - 1)) * (1.0 / RUN_ALIGN)) * RUN_ALIGN
    ti = lax.broadcasted_iota(i32, (MAX_TILES, MAX_TILES), 0)
    tj = lax.broadcasted_iota(i32, (MAX_TILES, MAX_TILES), 1)
    before = _dotx01(nrun, (ti < tj).astype(bf16))
    region = jnp.sum(nrun, axis=1, keepdims=True)
    nblocks = jnp.floor((region + (MOE_BM - 1)) * (1.0 / MOE_BM))
    ri = lax.broadcasted_iota(i32, (E, E), 0)
    ci = lax.broadcasted_iota(i32, (E, E), 1)
    below = (ci < ri).astype(bf16)
    bstart = _dot01(below, jnp.broadcast_to(nblocks, (E, MAX_TILES)))
    loff = _dot01(below, nrun)
    n_ref[...] = nrun.astype(i32)
    l_ref[...] = loff.astype(i32)
    g_ref[...] = (bstart * MOE_BM + before).astype(i32)
    lane0 = lax.broadcasted_iota(i32, (E, MAX_TILES), 1) == 0
    blk_ref[...] = jnp.where(lane0, bstart, nblocks).astype(i32)

    tile_of = lax.shift_right_logical(lax.broadcasted_iota(i32, (MAX_TILES, t_all), 1), TILE.bit_length() - 1)
    spread = (lax.broadcasted_iota(i32, (MAX_TILES, t_all), 0) == tile_of).astype(bf16)
    ltok = _dotx01(loff, spread)
    ei = lax.broadcasted_iota(i32, (E, t_all), 0)
    for kk in range(TOPK):
        hit = ei == eidx_ref[kk:kk + 1, :]
        start = jnp.sum(jnp.where(hit, ltok, 0.0), axis=0, keepdims=True)
        slot_ref[kk:kk + 1, :] = start.astype(i32) + lrank_ref[kk:kk + 1, :]


def _plan(eidx, lrank, cnt):
    t_all = eidx.shape[1]
    assert TILE & (TILE - 1) == 0
    tl = t_all
    tok = lambda s: (0, s)
    table = pl.BlockSpec((E, MAX_TILES), lambda s: (0, 0))
    return pl.pallas_call(
        _plan_body,
        grid=(1,),
        in_specs=[pl.BlockSpec((TOPK, tl), tok), pl.BlockSpec((TOPK, tl), tok), table],
        out_specs=[pl.BlockSpec((TOPK, tl), tok), table, table, table, table],
        out_shape=[jax.ShapeDtypeStruct((TOPK, t_all), i32)] + [jax.ShapeDtypeStruct((E, MAX_TILES), i32)] * 4,
        compiler_params=pltpu.CompilerParams(dimension_semantics=("arbitrary",), vmem_limit_bytes=VMEM_LIMIT),
        name="plan",
    )(eidx, lrank, cnt)


def _for_each_piece(total, pieces, fn):
    for j, b in enumerate(pieces):
        @pl.when((total & b) != 0)
        def _():
            fn(pl.multiple_of(total & ~(2 * b - 1), pieces[-1]), b, j)


def _for_each_run(n_ref, l_ref, g_ref, tile, fn):
    def per_expert(e, c):
        lo = pl.multiple_of(l_ref[e, tile], RUN_ALIGN)
        go = pl.multiple_of(g_ref[e, tile], RUN_ALIGN)
        _for_each_piece(n_ref[e, tile], RUN_PIECES, lambda off, b, j: fn(lo + off, go + off, b, j))
        return c
    lax.fori_loop(0, E, per_expert, 0)


def _start(copy, j):
    copy.start(priority=j % 2)


def _wait_rows(make_copy, rows):
    _for_each_piece(rows, WAIT_PIECES, lambda off, b, j: make_copy(b).wait())


def _dispatch_body(tiles_a, n_ref, l_ref, g_ref, slot_ref, ha_ref, hb_ref, xs_ref, buf_ref, zero_ref, sem, zsem):
    tile = pl.program_id(0)
    nsteps = pl.num_programs(0)
    cur = tile % RING

    slot = slot_ref[...].astype(i16)
    h = jnp.where(tile < tiles_a, ha_ref[...], hb_ref[...])
    filled = l_ref[E - 1, tile] + n_ref[E - 1, tile]

    def order(row0, nrows):
        srow = lax.broadcasted_iota(i32, (nrows, TILE), 0).astype(i16) + jnp.int16(row0)
        onehot = jnp.zeros((nrows, TILE), bf16)
        for kk in range(TOPK):
            onehot = jnp.where(srow == slot[kk:kk + 1, :], jnp.ones((), bf16), onehot)
        buf_ref[cur, row0:row0 + nrows, :] = _dot(onehot, h).astype(bf16)

    order(0, SLOTS_MAIN)

    @pl.when(filled > SLOTS_MAIN)
    def _():
        order(SLOTS_MAIN, SLOTS - SLOTS_MAIN)

    def run_copy(sl):
        return lambda srow_, grow, b: pltpu.make_async_copy(
            buf_ref.at[sl, pl.ds(srow_, b)], xs_ref.at[pl.ds(grow, b)], sem.at[sl])

    _for_each_run(n_ref, l_ref, g_ref, tile, lambda s, g, b, j: _start(run_copy(cur)(s, g, b), j))

    def wait_tile(t, sl):
        _wait_rows(lambda b: run_copy(sl)(0, 0, b), l_ref[E - 1, t] + n_ref[E - 1, t])

    @pl.when(tile >= RING - 1)
    def _():
        wait_tile(tile - (RING - 1), (tile + 1) % RING)

    last = nsteps - 1

    def zero_fill(fn):
        def per_expert(e, c):
            end = pl.multiple_of(g_ref[e, last] + n_ref[e, last], RUN_ALIGN)
            _for_each_piece((-end) & (MOE_BM - 1), TAIL_PIECES, lambda off, b, j: fn(pltpu.make_async_copy(
                zero_ref.at[pl.ds(0, b)], xs_ref.at[pl.ds(end + off, b)], zsem)))
            return c
        lax.fori_loop(0, E, per_expert, 0)
        used = (g_ref[E - 1, last] + n_ref[E - 1, last] + (MOE_BM - 1)) // MOE_BM

        def per_block(blk, c):
            fn(pltpu.make_async_copy(zero_ref, xs_ref.at[pl.ds(pl.multiple_of(blk * MOE_BM, MOE_BM), MOE_BM)], zsem))
            return c
        lax.fori_loop(used, xs_ref.shape[0] // MOE_BM, per_block, 0)

    @pl.when(tile == 0)
    def _():
        zero_ref[...] = jnp.zeros_like(zero_ref)
        zero_fill(lambda cp: cp.start())

    @pl.when(tile == last)
    def _():
        for back in range(RING - 2, -1, -1):
            @pl.when(tile >= back)
            def _():
                wait_tile(tile - back, (tile - back) % RING)
        zero_fill(lambda cp: cp.wait())


def _dispatch(tables, slot, h2_a, h2_b, nrows):
    tiles_a, tiles_b = h2_a.shape[0] // TILE, h2_b.shape[0] // TILE
    return pl.pallas_call(
        functools.partial(_dispatch_body, tiles_a),
        grid_spec=pltpu.PrefetchScalarGridSpec(
            num_scalar_prefetch=len(tables),
            grid=(tiles_a + tiles_b,),
            in_specs=[pl.BlockSpec((TOPK, TILE), lambda i, *_: (0, i)),
                      pl.BlockSpec((TILE, D), lambda i, *_: (jnp.minimum(i, tiles_a - 1), 0)),
                      pl.BlockSpec((TILE, D), lambda i, *_: (jnp.maximum(i - tiles_a, 0), 0))],
            out_specs=pl.BlockSpec(memory_space=pl.ANY),
            scratch_shapes=[pltpu.VMEM((RING, SLOTS, D), bf16), pltpu.VMEM((MOE_BM, D), bf16),
                            pltpu.SemaphoreType.DMA((RING,)), pltpu.SemaphoreType.DMA]),
        out_shape=jax.ShapeDtypeStruct((nrows, D), bf16),
        compiler_params=pltpu.CompilerParams(dimension_semantics=("arbitrary",), vmem_limit_bytes=VMEM_LIMIT),
        name="dispatch",
    )(*tables, slot, h2_a, h2_b)


def _expert_body(blk_ref, wg_ref, wu_ref, wd_ref, xs_ref, ys_ref, xbuf_ref, ybuf_ref, wgu_ref, wdb_ref, isem, osem):
    e = pl.program_id(0)
    first, count = blk_ref[e, 0], blk_ref[e, 1]
    used = blk_ref[E - 1, 0] + blk_ref[E - 1, 1]
    wgu_ref[:, :DE] = wg_ref[0].astype(bf16)
    wgu_ref[:, DE:] = wu_ref[0].astype(bf16)
    wdb_ref[...] = wd_ref[0].astype(bf16)

    def rows(g):
        return pl.ds(pl.multiple_of(g * MOE_BM, MOE_BM), MOE_BM)

    def load(g):
        sl = g % EXPERT_IN_SLOTS
        return pltpu.make_async_copy(xs_ref.at[rows(g)], xbuf_ref.at[sl], isem.at[sl])

    def store(g):
        sl = g % 2
        return pltpu.make_async_copy(ybuf_ref.at[sl], ys_ref.at[rows(g)], osem.at[sl])

    @pl.when(e == 0)
    def _():
        for g in range(EXPERT_IN_SLOTS - 1):
            @pl.when(g < used)
            def _():
                load(g).start()

    def block(j, c):
        g = first + j

        @pl.when(g + (EXPERT_IN_SLOTS - 1) < used)
        def _():
            load(g + (EXPERT_IN_SLOTS - 1)).start()
        load(g).wait()

        @pl.when(g >= 2)
        def _():
            store(g - 2).wait()
        hgu = _dot(xbuf_ref[g % EXPERT_IN_SLOTS], wgu_ref[...])
        hg = hgu[:, :DE]
        act = (hg * jax.nn.sigmoid(hg) * hgu[:, DE:]).astype(bf16)
        ybuf_ref[g % 2] = _dot(act, wdb_ref[...]).astype(bf16)
        store(g).start()
        return c
    lax.fori_loop(0, count, block, 0)

    @pl.when(e == E - 1)
    def _():
        for back in (2, 1):
            @pl.when(used >= back)
            def _():
                store(used - back).wait()


def _experts(blk, xs, we_gate, we_up, we_down):
    ex = lambda e, blk_: (e, 0, 0)
    return pl.pallas_call(
        _expert_body,
        grid_spec=pltpu.PrefetchScalarGridSpec(
            num_scalar_prefetch=1,
            grid=(E,),
            in_specs=[pl.BlockSpec((1, D, DE), ex), pl.BlockSpec((1, D, DE), ex), pl.BlockSpec((1, DE, D), ex),
                      pl.BlockSpec(memory_space=pl.ANY)],
            out_specs=pl.BlockSpec(memory_space=pl.ANY),
            scratch_shapes=[pltpu.VMEM((EXPERT_IN_SLOTS, MOE_BM, D), bf16), pltpu.VMEM((2, MOE_BM, D), bf16),
                            pltpu.VMEM((D, 2 * DE), bf16), pltpu.VMEM((DE, D), bf16),
                            pltpu.SemaphoreType.DMA((EXPERT_IN_SLOTS,)), pltpu.SemaphoreType.DMA((2,))]),
        out_shape=jax.ShapeDtypeStruct(xs.shape, xs.dtype),
        input_output_aliases={4: 0},
        compiler_params=pltpu.CompilerParams(dimension_semantics=("arbitrary",), vmem_limit_bytes=VMEM_LIMIT),
        name="experts",
    )(blk, we_gate, we_up, we_down, xs)


def _final_body(nb, lt, tile0, n_ref, l_ref, g_ref, slot_ref, wt_ref, x2_ref, g2_ref, fw_ref, ys_ref, o_ref,
                buf_ref, routed_ref, sem):
    m = nb * lt
    n_inner = pl.num_programs(1)
    step = pl.program_id(0) * n_inner + pl.program_id(1)
    nsteps = pl.num_programs(0) * n_inner
    tile = tile0 + step
    cur = step % RING

    def run_copy(sl):
        return lambda srow, grow, b: pltpu.make_async_copy(
            ys_ref.at[pl.ds(grow, b)], buf_ref.at[sl, pl.ds(srow, b)], sem.at[sl])

    def fetch(ahead):
        sl = (step + ahead) % RING
        _for_each_run(n_ref, l_ref, g_ref, tile + ahead, lambda s, g, b, j: _start(run_copy(sl)(s, g, b), j))

    @pl.when(step == 0)
    def _():
        buf_ref[...] = jnp.zeros_like(buf_ref)
        for ahead in range(RING - 1):
            @pl.when(ahead < nsteps)
            def _():
                fetch(ahead)

    @pl.when(step + (RING - 1) < nsteps)
    def _():
        fetch(RING - 1)

    _wait_rows(lambda b: run_copy(cur)(0, 0, b), l_ref[E - 1, tile] + n_ref[E - 1, tile])

    slot = slot_ref[...].astype(i16)
    wt = wt_ref[...].astype(bf16)

    def combine(col0, ncols):
        scol = lax.broadcasted_iota(i32, (m, ncols), 1).astype(i16) + jnp.int16(col0)
        wsel = jnp.zeros((m, ncols), bf16)
        for kk in range(TOPK):
            wsel = jnp.where(scol == slot[:, kk:kk + 1], wt[:, kk:kk + 1], wsel)
        return _dot(wsel, buf_ref[cur, col0:col0 + ncols, :])

    routed_ref[...] = combine(0, SLOTS_MAIN)

    @pl.when(l_ref[E - 1, tile] + n_ref[E - 1, tile] > SLOTS_MAIN)
    def _():
        routed_ref[...] += combine(SLOTS_MAIN, SLOTS - SLOTS_MAIN)
    routed = routed_ref[...]
    g2 = g2_ref[0]
    if nb == 1:
        g2 = g2.reshape(1, D)
    else:
        g2 = jnp.broadcast_to(g2, (nb, lt, D)).reshape(m, D)
    y = x2_ref[...] + g2 * routed
    o_ref[...] = _rms(y, fw_ref[...]).reshape(nb, lt, D)


def _final(tables, slot_t, wts_t, x2, ys, mod4, fw, bsz, seqlen, row0, nb, lt):
    m = nb * lt
    assert m == TILE and row0 % TILE == 0
    n_lt = seqlen // lt
    tile0 = row0 // TILE
    tile = lambda b, i: tile0 + b * n_lt + i
    return pl.pallas_call(
        functools.partial(_final_body, nb, lt, tile0),
        grid_spec=pltpu.PrefetchScalarGridSpec(
            num_scalar_prefetch=len(tables),
            grid=(bsz // nb, n_lt),
            in_specs=[pl.BlockSpec((m, TOPK), lambda b, i, *_: (tile(b, i), 0)),
                      pl.BlockSpec((m, TOPK), lambda b, i, *_: (tile(b, i), 0)),
                      pl.BlockSpec((m, D), lambda b, i, *_: (b * n_lt + i, 0)),
                      pl.BlockSpec((1, nb, 1, D), lambda b, i, *_: (5, b, 0, 0)),
                      pl.BlockSpec((1, D), lambda b, i, *_: (0, 0)),
                      pl.BlockSpec(memory_space=pl.ANY)],
            out_specs=pl.BlockSpec((nb, lt, D), lambda b, i, *_: (b, i, 0)),
            scratch_shapes=[pltpu.VMEM((RING, SLOTS, D), bf16), pltpu.VMEM((m, D), f32),
                            pltpu.SemaphoreType.DMA((RING,))]),
        out_shape=jax.ShapeDtypeStruct((bsz, seqlen, D), f32),
        compiler_params=pltpu.CompilerParams(dimension_semantics=("arbitrary", "arbitrary"),
                                             vmem_limit_bytes=VMEM_LIMIT),
        name="final",
    )(*tables, slot_t, wts_t, x2, mod4, fw, ys)


def kernel(x_prompt, x_sample, state_conv, state_gla, c_prompt, c_sample, w_ada, b_ada, norm1_w, w_in, conv_w,
           w_gk, b_gk, gla_norm_w, w_out_conv, w_out_gla, w_o, norm2_w, router_w, router_bias, we_gate, we_up,
           we_down, ws_gate, ws_up, ws_down, final_norm_w):
    assert w_ada.shape[0] == 1, "single-layer step"
    bp, lp, _ = x_prompt.shape
    bs, ls, _ = x_sample.shape
    tp, ts = bp * lp, bs * ls
    t_all = tp + ts

    w_in0 = w_in[0]
    n_main = 3 * D + 2 * DK + 2 * DV
    rank = w_gk.shape[1]
    weights = (
        norm1_w[0].reshape(1, D),
        w_in0[:, :n_main].astype(bf16),
        jnp.pad(w_in0[:, n_main:n_main + rank], ((0, 0), (0, RANK_PAD - rank))).astype(bf16),
        w_in0[:, n_main + rank:].astype(bf16),
        jnp.pad(w_gk[0], ((0, RANK_PAD - rank), (0, 0))).astype(bf16),
        b_gk[0].reshape(1, DK),
        conv_w[0],
        gla_norm_w[0].reshape(1, DVH),
        w_out_conv[0].astype(bf16),
        w_out_gla[0].astype(bf16),
        w_o[0].astype(bf16),
        norm2_w[0].reshape(1, D),
        router_w[0].T.astype(bf16),
        ws_gate[0].astype(bf16),
        ws_up[0].astype(bf16),
        ws_down[0].astype(bf16),
    )

    mod = _adaln(jnp.concatenate([c_prompt, c_sample], axis=0), w_ada[0], b_ada[0].reshape(1, 6 * D))
    mod4 = mod.reshape(bp + bs, 6, 1, D).transpose(1, 0, 2, 3)
    mod_p, mod_s = mod4[:, :bp], mod4[:, bp:]

    lt_p = 512
    x2_p, h2_p, lg_p, conv_p, gla_p = _mixer(x_prompt, mod_p, None, weights, 1, lt_p, GLA_CHUNK)
    nb_s = 8
    x2_s, h2_s, lg_s, conv_s, gla_s = _mixer(x_sample, mod_s, (state_conv[0], state_gla[0]), weights,
                                             nb_s, ls, math.gcd(ls, GLA_CHUNK))
    lg = jnp.concatenate([lg_p.transpose(1, 0, 2).reshape(E, tp), lg_s.transpose(1, 0, 2).reshape(E, ts)], axis=1)

    eidx, wts, lrank, cnt = _route(lg, router_bias[0].reshape(E, 1))
    n_tiles = t_all // TILE
    max_rows = t_all * TOPK + n_tiles * E * (RUN_ALIGN - 1) + E * (MOE_BM - RUN_ALIGN)
    nblk = -(-max_rows // MOE_BM)
    slot, nrun, loff, goff, blk = _plan(eidx, lrank, cnt)
    tables = (nrun, loff, goff)

    xs = _dispatch(tables, slot, h2_p, h2_s, nblk * MOE_BM)
    ys = _experts(blk, xs, we_gate[0], we_up[0], we_down[0])

    fw = final_norm_w.reshape(1, D)
    slot_t, wts_t = slot.T, wts.T
    y_prompt = _final(tables, slot_t, wts_t, x2_p, ys, mod_p, fw, bp, lp, 0, 1, TILE)
    y_sample = _final(tables, slot_t, wts_t, x2_s, ys, mod_s, fw, bs, ls, tp, TILE // ls, ls)
    return (y_prompt, y_sample, conv_p, gla_p, conv_s, gla_s)
```

```python
import functools
import math

import jax
import jax.numpy as jnp
from jax import lax
from jax.experimental import pallas as pl
from jax.experimental.pallas import tpu as pltpu

f32 = jnp.float32
bf16 = jnp.bfloat16
i32 = jnp.int32
i16 = jnp.int16

D = 1024
H = 4
DKH = 128
DVH = 256
DK = H * DKH
DV = H * DVH
RANK_PAD = 128
GATE_NORMALIZER = 16.0
GLA_CHUNK = 64
E = 64
TOPK = 8
NGROUPS = 8
TOPK_GROUPS = 4
DE = 256
ROUTED_SCALE = 2.5
EPS = 1e-6
NEG = float("-inf")

V7X_VMEM_BYTES = 64 * 1024 * 1024
VMEM_LIMIT = V7X_VMEM_BYTES - 6 * 1024 * 1024
MOE_BM = 512
EXPERT_IN_SLOTS = 4
RING = 2
TILE = 256
MAX_TILES = 128
ROUTE_TILES = 2
RUN_ALIGN = 16
SLOTS = TILE * TOPK + 1024
SLOTS_MAIN = TILE * TOPK + 512
RUN_PIECES = (256, 128, 64, 32, 16)
TAIL_PIECES = (256, 128, 64, 32, 16)
WAIT_PIECES = (2048, 1024, 512, 256, 128, 64, 32, 16)

def _dot(a, b):
    return jnp.dot(a, b, preferred_element_type=f32)


def _split3(x):
    hi = x.astype(bf16)
    r1 = x - hi.astype(f32)
    mid = r1.astype(bf16)
    lo = (r1 - mid.astype(f32)).astype(bf16)
    return hi, mid, lo


def _dot01(m01, x):
    hi, mid, lo = _split3(x)
    return _dot(m01, hi) + _dot(m01, mid) + _dot(m01, lo)


def _dotx01(x, m01):
    hi, mid, lo = _split3(x)
    return _dot(hi, m01) + _dot(mid, m01) + _dot(lo, m01)


def _rms(v, w):
    ms = jnp.mean(v * v, axis=-1, keepdims=True)
    return v * lax.rsqrt(ms + EPS) * w


def _const_spec(shape):
    n = len(shape)
    return pl.BlockSpec(shape, lambda *_: (0,) * n, pipeline_mode=pl.Buffered(1))


def _adaln_body(c_ref, w_ref, b_ref, o_ref):
    c = c_ref[...]
    a = (c * jax.nn.sigmoid(c)).astype(bf16)
    o_ref[...] = _dot(a, w_ref[...].astype(bf16)) + b_ref[...]


def _adaln(c_all, w_ada, b_ada):
    nrow = c_all.shape[0]
    tn = 1536
    return pl.pallas_call(
        _adaln_body,
        grid=(6 * D // tn,),
        in_specs=[pl.BlockSpec((nrow, D), lambda j: (0, 0)),
                  pl.BlockSpec((D, tn), lambda j: (0, j)),
                  pl.BlockSpec((1, tn), lambda j: (0, j))],
        out_specs=pl.BlockSpec((nrow, tn), lambda j: (0, j)),
        out_shape=jax.ShapeDtypeStruct((nrow, 6 * D), f32),
        compiler_params=pltpu.CompilerParams(dimension_semantics=("arbitrary",), vmem_limit_bytes=VMEM_LIMIT),
        name="adaln",
    )(c_all, w_ada, b_ada)


def _mixer_body(nb, lt, chunk, has_state, *refs):
    if has_state:
        (x_ref, mod_ref, cs_ref, gs_ref, *rest) = refs
    else:
        (x_ref, mod_ref, *rest) = refs
        cs_ref = gs_ref = None
    (n1_ref, wmain_ref, wz_ref, wu_ref, wgk_ref, bgk_ref, cw_ref, gnw_ref, woc_ref, wog_ref, wo_ref,
     n2_ref, rwt_ref, wsg_ref, wsu_ref, wsd_ref,
     x2_ref, h2_ref, lg_ref, nc_ref, ng_ref, carry_ref, st_ref, og_ref, sin_ref) = rest
    m = nb * lt
    i = pl.program_id(1)
    last = pl.num_programs(1) - 1

    def rows(v):
        w = v.shape[-1]
        if nb == 1:
            return v.reshape(1, w)
        return jnp.broadcast_to(v, (nb, lt, w)).reshape(m, w)

    @pl.when(i == 0)
    def _():
        if has_state:
            cs = cs_ref[...]
            carry_ref[0] = cs[:, 0:1, :]
            carry_ref[1] = cs[:, 1:2, :]
        else:
            carry_ref[...] = jnp.zeros_like(carry_ref)
            st_ref[...] = jnp.zeros_like(st_ref)

    x = x_ref[...].reshape(m, D)
    sh1, sc1, g1, sh2, sc2, g2 = [rows(mod_ref[j]) for j in range(6)]
    hb = (_rms(x, n1_ref[...]) * (1.0 + sc1) + sh1).astype(bf16)

    def proj(lo, hi):
        return _dot(hb, wmain_ref[:, lo:hi])

    pre = proj(D, 2 * D) * proj(2 * D, 3 * D)
    c0 = rows(carry_ref[0])
    c1 = rows(carry_ref[1])
    l_idx = lax.broadcasted_iota(i32, (m, D), 0) & (lt - 1)
    r1 = pltpu.roll(pre, 1, 0)
    r2 = pltpu.roll(pre, 2, 0)
    prev1 = jnp.where(l_idx == 0, c1, r1)
    prev2 = jnp.where(l_idx == 0, c0, jnp.where(l_idx == 1, c1, r2))
    cw = cw_ref[...]
    conv = cw[0:1] * prev2 + cw[1:2] * prev1 + cw[2:3] * pre
    y_a = _dot((proj(0, D) * conv).astype(bf16), woc_ref[...])
    pre3 = pre.reshape(nb, lt, D)
    tail = pre3[:, lt - 2:lt, :]
    carry_ref[0] = tail[:, 0:1, :]
    carry_ref[1] = tail[:, 1:2, :]
    nc_ref[0] = tail

    q = proj(3 * D, 3 * D + DK)
    k = proj(3 * D + DK, 3 * D + 2 * DK)
    v = proj(4 * D, 5 * D)
    g = proj(5 * D, 6 * D)
    z = _dot(hb, wz_ref[...]).astype(bf16)
    pa = _dot(z, wgk_ref[...]) + bgk_ref[...]
    la = (jnp.minimum(pa, 0.0) - jnp.log1p(jnp.exp(-jnp.abs(pa)))) / GATE_NORMALIZER
    pos = lax.broadcasted_iota(i32, (m, DK), 0) & (chunk - 1)
    bcum = la
    shift = 1
    while shift < chunk:
        bcum = bcum + jnp.where(pos >= shift, pltpu.roll(bcum, shift, 0), 0.0)
        shift *= 2
    ngrp = m // chunk
    bend = bcum.reshape(ngrp, chunk, DK)[:, chunk - 1:chunk, :]
    bend_rows = jnp.broadcast_to(bend, (ngrp, chunk, DK)).reshape(m, DK)
    nch = lt // chunk
    assert nb == 1 or nch == 1
    hs = H * nb
    nbat = nch * hs

    def per_head(xm, w):
        x3 = xm.reshape(nb * nch, chunk, H * w)
        parts = [x3[:, :, hh * w:(hh + 1) * w] for hh in range(H)]
        return jnp.stack(parts, axis=1 if nb == 1 else 0).reshape(nbat, chunk, w)

    qd = per_head((q * (DKH ** -0.5) * jnp.exp(bcum)).astype(bf16), DKH)
    kd = per_head((k * jnp.exp(-bcum)).astype(bf16), DKH)
    ke = per_head((k * jnp.exp(bend_rows - bcum)).astype(bf16), DKH)
    vv = per_head(v.astype(bf16), DVH)
    assert nbat <= 128
    a_end = per_head(jnp.exp(bend_rows), DKH)[:, 0, :]
    a_col = jnp.concatenate([a_end, jnp.zeros((128 - nbat, DKH), f32)], axis=0).T
    causal = lax.broadcasted_iota(i32, (chunk, chunk), 1) <= lax.broadcasted_iota(i32, (chunk, chunk), 0)
    sc = jnp.einsum('bcd,bsd->bcs', qd, kd, preferred_element_type=f32)
    sc = jnp.where(causal, sc, 0.0).astype(bf16)
    o = jnp.einsum('bcs,bse->bce', sc, vv, preferred_element_type=f32)
    upd = jnp.einsum('bsd,bse->bde', ke, vv, preferred_element_type=f32)
    st = [gs_ref[s % nb, s // nb] if has_state else st_ref[s // nb, s % nb] for s in range(hs)]
    for j in range(nch):
        for s in range(hs):
            b = j * hs + s
            sin_ref[b] = st[s].astype(bf16)
            st[s] = st[s] * a_col[:, b:b + 1] + upd[b]
    for s in range(hs):
        if has_state:
            ng_ref[0, s % nb, s // nb] = st[s]
        else:
            st_ref[s // nb, s % nb] = st[s]
    o = o + jnp.einsum('bcd,bde->bce', qd, sin_ref[...], preferred_element_type=f32)
    on = _rms(o, gnw_ref[...])
    g3 = g.reshape(nb, lt, DV)
    for j in range(nch):
        js = slice(j * chunk, (j + 1) * chunk)
        for hh in range(H):
            vs = slice(hh * DVH, (hh + 1) * DVH)
            gh = g3[:, js, vs]
            b0 = (j * H + hh) * nb
            og_ref[:, js, vs] = (on[b0:b0 + nb] * (gh * jax.nn.sigmoid(gh))).astype(bf16)

    if not has_state:
        @pl.when(i == last)
        def _():
            for hh in range(H):
                ng_ref[0, :, hh] = st_ref[hh]

    y_b = _dot(og_ref[...].reshape(m, DV), wog_ref[...])
    u_a = _dot(hb, wu_ref[:, 0:D])
    u_b = _dot(hb, wu_ref[:, D:2 * D])
    merged = (jax.nn.sigmoid(u_a) * y_a + jax.nn.sigmoid(u_b) * y_b).astype(bf16)
    x1 = x + g1 * _dot(merged, wo_ref[...])

    h2 = (_rms(x1, n2_ref[...]) * (1.0 + sc2) + sh2).astype(bf16)
    sg = _dot(h2, wsg_ref[...])
    act = (sg * jax.nn.sigmoid(sg) * _dot(h2, wsu_ref[...])).astype(bf16)
    x2_ref[...] = x1 + g2 * _dot(act, wsd_ref[...])
    h2_ref[...] = h2
    lg_ref[0] = lax.dot_general(rwt_ref[...], h2, (((1,), (1,)), ((), ())), preferred_element_type=f32)


def _mixer(x, mod4, states, weights, nb, lt, chunk):
    bsz, seqlen, _ = x.shape
    has_state = states is not None
    m = nb * lt
    n_lt = seqlen // lt
    assert not has_state or n_lt == 1
    nsteps = (bsz // nb) * n_lt
    t_all = bsz * seqlen

    def tok_blk(b, i):
        return b * n_lt + i

    in_specs = [pl.BlockSpec((nb, lt, D), lambda b, i: (b, i, 0)),
                pl.BlockSpec((6, nb, 1, D), lambda b, i: (0, b, 0, 0))]
    args = [x, mod4]
    if has_state:
        in_specs += [pl.BlockSpec((nb, 2, D), lambda b, i: (b, 0, 0)),
                     pl.BlockSpec((nb, H, DKH, DVH), lambda b, i: (b, 0, 0, 0))]
        args += list(states)
    in_specs += [_const_spec(w.shape) for w in weights]
    args += list(weights)
    out_specs = [pl.BlockSpec((m, D), lambda b, i: (tok_blk(b, i), 0)),
                 pl.BlockSpec((m, D), lambda b, i: (tok_blk(b, i), 0)),
                 pl.BlockSpec((1, E, m), lambda b, i: (b * n_lt + i, 0, 0)),
                 pl.BlockSpec((1, nb, 2, D), lambda b, i: (0, b, 0, 0)),
                 pl.BlockSpec((1, nb, H, DKH, DVH), lambda b, i: (0, b, 0, 0, 0))]
    out_shape = [jax.ShapeDtypeStruct((t_all, D), f32),
                 jax.ShapeDtypeStruct((t_all, D), bf16),
                 jax.ShapeDtypeStruct((nsteps, E, m), f32),
                 jax.ShapeDtypeStruct((1, bsz, 2, D), f32),
                 jax.ShapeDtypeStruct((1, bsz, H, DKH, DVH), f32)]
    return pl.pallas_call(
        functools.partial(_mixer_body, nb, lt, chunk, has_state),
        grid=(bsz // nb, n_lt),
        in_specs=in_specs,
        out_specs=out_specs,
        out_shape=out_shape,
        scratch_shapes=[pltpu.VMEM((2, nb, 1, D), f32),
                        pltpu.VMEM((H, nb, DKH, DVH), f32),
                        pltpu.VMEM((nb, lt, DV), bf16),
                        pltpu.VMEM(((lt // chunk) * H * nb, DKH, DVH), bf16)],
        compiler_params=pltpu.CompilerParams(dimension_semantics=("arbitrary", "arbitrary"),
                                             vmem_limit_bytes=VMEM_LIMIT),
        name="mixer_state" if has_state else "mixer_prompt",
    )(*args)


def _route_body(lg_ref, bias_ref, eidx_ref, w_ref, lrank_ref, cnt_ref):
    step = pl.program_id(0)
    tl = lg_ref.shape[1]

    scores = jax.nn.sigmoid(lg_ref[...])
    choice = scores + bias_ref[...]
    gsz = E // NGROUPS
    grp = choice.reshape(NGROUPS, gsz, tl)
    mi = lax.broadcasted_iota(i32, (NGROUPS, gsz, tl), 1)
    m1 = jnp.max(grp, axis=1, keepdims=True)
    first = jnp.min(jnp.where(grp == m1, mi, gsz), axis=1, keepdims=True)
    m2 = jnp.max(jnp.where(mi == first, NEG, grp), axis=1, keepdims=True)
    gscore = (m1 + m2).reshape(NGROUPS, tl)
    gi = lax.broadcasted_iota(i32, (NGROUPS, tl), 0)
    gsel = jnp.zeros((NGROUPS, tl), f32)
    work = gscore
    for _ in range(TOPK_GROUPS):
        mx = jnp.max(work, axis=0, keepdims=True)
        f = jnp.min(jnp.where(work == mx, gi, NGROUPS), axis=0, keepdims=True)
        hit = gi == f
        gsel = jnp.where(hit, 1.0, gsel)
        work = jnp.where(hit, NEG, work)
    emask = jnp.broadcast_to(gsel.reshape(NGROUPS, 1, tl), (NGROUPS, gsz, tl)).reshape(E, tl)
    masked = jnp.where(emask > 0.0, choice, NEG)
    ei = lax.broadcasted_iota(i32, (E, tl), 0)
    sel = jnp.zeros((E, tl), f32)
    hits, ws = [], []
    for kk in range(TOPK):
        mx = jnp.max(masked, axis=0, keepdims=True)
        f = jnp.min(jnp.where(masked == mx, ei, E), axis=0, keepdims=True)
        hit = ei == f
        ws.append(jnp.sum(jnp.where(hit, scores, 0.0), axis=0, keepdims=True))
        masked = jnp.where(hit, NEG, masked)
        sel = jnp.where(hit, 1.0, sel)
        hits.append(hit)
        eidx_ref[kk:kk + 1, :] = f
    wsum = ws[0]
    for t in ws[1:]:
        wsum = wsum + t
    for kk in range(TOPK):
        w_ref[kk:kk + 1, :] = ws[kk] / wsum * ROUTED_SCALE

    ui = lax.broadcasted_iota(i32, (tl, tl), 0)
    uj = lax.broadcasted_iota(i32, (tl, tl), 1)
    upper = ((ui <= uj) & ((ui & -TILE) == (uj & -TILE))).astype(bf16)
    lrank = _dot(sel.astype(bf16), upper) - sel
    for kk in range(TOPK):
        lrank_ref[kk:kk + 1, :] = jnp.sum(jnp.where(hits[kk], lrank, 0.0), axis=0, keepdims=True).astype(i32)

    @pl.when(step == 0)
    def _():
        cnt_ref[...] = jnp.zeros_like(cnt_ref)
    lane = lax.broadcasted_iota(i32, cnt_ref.shape, 1)
    tiles_per_step = tl // TILE
    for sub in range(tiles_per_step):
        col = jnp.sum(sel[:, sub * TILE:(sub + 1) * TILE], axis=1, keepdims=True)
        cnt_ref[...] = jnp.where(lane == step * tiles_per_step + sub, col, cnt_ref[...])


def _route(logits_t, bias_col):
    t_all = logits_t.shape[1]
    tl = ROUTE_TILES * TILE
    assert t_all % tl == 0 and t_all // TILE <= MAX_TILES
    tok = lambda s: (0, s)
    return pl.pallas_call(
        _route_body,
        grid=(t_all // tl,),
        in_specs=[pl.BlockSpec((E, tl), tok), pl.BlockSpec((E, 1), lambda s: (0, 0))],
        out_specs=[pl.BlockSpec((TOPK, tl), tok), pl.BlockSpec((TOPK, tl), tok), pl.BlockSpec((TOPK, tl), tok),
                   pl.BlockSpec((E, MAX_TILES), lambda s: (0, 0))],
        out_shape=[jax.ShapeDtypeStruct((TOPK, t_all), i32), jax.ShapeDtypeStruct((TOPK, t_all), f32),
                   jax.ShapeDtypeStruct((TOPK, t_all), i32), jax.ShapeDtypeStruct((E, MAX_TILES), f32)],
        compiler_params=pltpu.CompilerParams(dimension_semantics=("arbitrary",), vmem_limit_bytes=VMEM_LIMIT),
        name="route",
    )(logits_t, bias_col)


def _plan_body(eidx_ref, lrank_ref, cnt_ref, slot_ref, n_ref, l_ref, g_ref, blk_ref):
    t_all = eidx_ref.shape[1]
    cnt = cnt_ref[...]
    nrun = jnp.floor((cnt + (RUN_ALIGN - 1)) * (1.0 / RUN_ALIGN)) * RUN_ALIGN
    ti = lax.broadcasted_iota(i32, (MAX_TILES, MAX_TILES), 0)
    tj = lax.broadcasted_iota(i32, (MAX_TILES, MAX_TILES), 1)
    before = _dotx01(nrun, (ti < tj).astype(bf16))
    region = jnp.sum(nrun, axis=1, keepdims=True)
    nblocks = jnp.floor((region + (MOE_BM - 1)) * (1.0 / MOE_BM))
    ri = lax.broadcasted_iota(i32, (E, E), 0)
    ci = lax.broadcasted_iota(i32, (E, E), 1)
    below = (ci < ri).astype(bf16)
    bstart = _dot01(below, jnp.broadcast_to(nblocks, (E, MAX_TILES)))
    loff = _dot01(below, nrun)
    n_ref[...] = nrun.astype(i32)
    l_ref[...] = loff.astype(i32)
    g_ref[...] = (bstart * MOE_BM + before).astype(i32)
    lane0 = lax.broadcasted_iota(i32, (E, MAX_TILES), 1) == 0
    blk_ref[...] = jnp.where(lane0, bstart, nblocks).astype(i32)

    tile_of = lax.shift_right_logical(lax.broadcasted_iota(i32, (MAX_TILES, t_all), 1), TILE.bit_length() - 1)
    spread = (lax.broadcasted_iota(i32, (MAX_TILES, t_all), 0) == tile_of).astype(bf16)
    ltok = _dotx01(loff, spread)
    ei = lax.broadcasted_iota(i32, (E, t_all), 0)
    for kk in range(TOPK):
        hit = ei == eidx_ref[kk:kk + 1, :]
        start = jnp.sum(jnp.where(hit, ltok, 0.0), axis=0, keepdims=True)
        slot_ref[kk:kk + 1, :] = start.astype(i32) + lrank_ref[kk:kk + 1, :]


def _plan(eidx, lrank, cnt):
    t_all = eidx.shape[1]
    assert TILE & (TILE - 1) == 0
    tl = t_all
    tok = lambda s: (0, s)
    table = pl.BlockSpec((E, MAX_TILES), lambda s: (0, 0))
    return pl.pallas_call(
        _plan_body,
        grid=(1,),
        in_specs=[pl.BlockSpec((TOPK, tl), tok), pl.BlockSpec((TOPK, tl), tok), table],
        out_specs=[pl.BlockSpec((TOPK, tl), tok), table, table, table, table],
        out_shape=[jax.ShapeDtypeStruct((TOPK, t_all), i32)] + [jax.ShapeDtypeStruct((E, MAX_TILES), i32)] * 4,
        compiler_params=pltpu.CompilerParams(dimension_semantics=("arbitrary",), vmem_limit_bytes=VMEM_LIMIT),
        name="plan",
    )(eidx, lrank, cnt)


def _for_each_piece(total, pieces, fn):
    for j, b in enumerate(pieces):
        @pl.when((total & b) != 0)
        def _():
            fn(pl.multiple_of(total & ~(2 * b - 1), pieces[-1]), b, j)


def _for_each_run(n_ref, l_ref, g_ref, tile, fn):
    def per_expert(e, c):
        lo = pl.multiple_of(l_ref[e, tile], RUN_ALIGN)
        go = pl.multiple_of(g_ref[e, tile], RUN_ALIGN)
        _for_each_piece(n_ref[e, tile], RUN_PIECES, lambda off, b, j: fn(lo + off, go + off, b, j))
        return c
    lax.fori_loop(0, E, per_expert, 0)


def _start(copy, j):
    copy.start(priority=j % 2)


def _wait_rows(make_copy, rows):
    _for_each_piece(rows, WAIT_PIECES, lambda off, b, j: make_copy(b).wait())


def _dispatch_body(tiles_a, n_ref, l_ref, g_ref, slot_ref, ha_ref, hb_ref, xs_ref, buf_ref, zero_ref, sem, zsem):
    tile = pl.program_id(0)
    nsteps = pl.num_programs(0)
    cur = tile % RING

    slot = slot_ref[...].astype(i16)
    h = jnp.where(tile < tiles_a, ha_ref[...], hb_ref[...])
    filled = l_ref[E - 1, tile] + n_ref[E - 1, tile]

    def order(row0, nrows):
        srow = lax.broadcasted_iota(i32, (nrows, TILE), 0).astype(i16) + jnp.int16(row0)
        onehot = jnp.zeros((nrows, TILE), bf16)
        for kk in range(TOPK):
            onehot = jnp.where(srow == slot[kk:kk + 1, :], jnp.ones((), bf16), onehot)
        buf_ref[cur, row0:row0 + nrows, :] = _dot(onehot, h).astype(bf16)

    order(0, SLOTS_MAIN)

    @pl.when(filled > SLOTS_MAIN)
    def _():
        order(SLOTS_MAIN, SLOTS - SLOTS_MAIN)

    def run_copy(sl):
        return lambda srow_, grow, b: pltpu.make_async_copy(
            buf_ref.at[sl, pl.ds(srow_, b)], xs_ref.at[pl.ds(grow, b)], sem.at[sl])

    _for_each_run(n_ref, l_ref, g_ref, tile, lambda s, g, b, j: _start(run_copy(cur)(s, g, b), j))

    def wait_tile(t, sl):
        _wait_rows(lambda b: run_copy(sl)(0, 0, b), l_ref[E - 1, t] + n_ref[E - 1, t])

    @pl.when(tile >= RING - 1)
    def _():
        wait_tile(tile - (RING - 1), (tile + 1) % RING)

    last = nsteps - 1

    def zero_fill(fn):
        def per_expert(e, c):
            end = pl.multiple_of(g_ref[e, last] + n_ref[e, last], RUN_ALIGN)
            _for_each_piece((-end) & (MOE_BM - 1), TAIL_PIECES, lambda off, b, j: fn(pltpu.make_async_copy(
                zero_ref.at[pl.ds(0, b)], xs_ref.at[pl.ds(end + off, b)], zsem)))
            return c
        lax.fori_loop(0, E, per_expert, 0)
        used = (g_ref[E - 1, last] + n_ref[E - 1, last] + (MOE_BM - 1)) // MOE_BM

        def per_block(blk, c):
            fn(pltpu.make_async_copy(zero_ref, xs_ref.at[pl.ds(pl.multiple_of(blk * MOE_BM, MOE_BM), MOE_BM)], zsem))
            return c
        lax.fori_loop(used, xs_ref.shape[0] // MOE_BM, per_block, 0)

    @pl.when(tile == 0)
    def _():
        zero_ref[...] = jnp.zeros_like(zero_ref)
        zero_fill(lambda cp: cp.start())

    @pl.when(tile == last)
    def _():
        for back in range(RING - 2, -1, -1):
            @pl.when(tile >= back)
            def _():
                wait_tile(tile - back, (tile - back) % RING)
        zero_fill(lambda cp: cp.wait())


def _dispatch(tables, slot, h2_a, h2_b, nrows):
    tiles_a, tiles_b = h2_a.shape[0] // TILE, h2_b.shape[0] // TILE
    return pl.pallas_call(
        functools.partial(_dispatch_body, tiles_a),
        grid_spec=pltpu.PrefetchScalarGridSpec(
            num_scalar_prefetch=len(tables),
            grid=(tiles_a + tiles_b,),
            in_specs=[pl.BlockSpec((TOPK, TILE), lambda i, *_: (0, i)),
                      pl.BlockSpec((TILE, D), lambda i, *_: (jnp.minimum(i, tiles_a - 1), 0)),
                      pl.BlockSpec((TILE, D), lambda i, *_: (jnp.maximum(i - tiles_a, 0), 0))],
            out_specs=pl.BlockSpec(memory_space=pl.ANY),
            scratch_shapes=[pltpu.VMEM((RING, SLOTS, D), bf16), pltpu.VMEM((MOE_BM, D), bf16),
                            pltpu.SemaphoreType.DMA((RING,)), pltpu.SemaphoreType.DMA]),
        out_shape=jax.ShapeDtypeStruct((nrows, D), bf16),
        compiler_params=pltpu.CompilerParams(dimension_semantics=("arbitrary",), vmem_limit_bytes=VMEM_LIMIT),
        name="dispatch",
    )(*tables, slot, h2_a, h2_b)


def _expert_body(blk_ref, wg_ref, wu_ref, wd_ref, xs_ref, ys_ref, xbuf_ref, ybuf_ref, wgu_ref, wdb_ref, isem, osem):
    e = pl.program_id(0)
    first, count = blk_ref[e, 0], blk_ref[e, 1]
    used = blk_ref[E - 1, 0] + blk_ref[E - 1, 1]
    wgu_ref[:, :DE] = wg_ref[0].astype(bf16)
    wgu_ref[:, DE:] = wu_ref[0].astype(bf16)
    wdb_ref[...] = wd_ref[0].astype(bf16)

    def rows(g):
        return pl.ds(pl.multiple_of(g * MOE_BM, MOE_BM), MOE_BM)

    def load(g):
        sl = g % EXPERT_IN_SLOTS
        return pltpu.make_async_copy(xs_ref.at[rows(g)], xbuf_ref.at[sl], isem.at[sl])

    def store(g):
        sl = g % 2
        return pltpu.make_async_copy(ybuf_ref.at[sl], ys_ref.at[rows(g)], osem.at[sl])

    @pl.when(e == 0)
    def _():
        for g in range(EXPERT_IN_SLOTS - 1):
            @pl.when(g < used)
            def _():
                load(g).start()

    def block(j, c):
        g = first + j

        @pl.when(g + (EXPERT_IN_SLOTS - 1) < used)
        def _():
            load(g + (EXPERT_IN_SLOTS - 1)).start()
        load(g).wait()

        @pl.when(g >= 2)
        def _():
            store(g - 2).wait()
        hgu = _dot(xbuf_ref[g % EXPERT_IN_SLOTS], wgu_ref[...])
        hg = hgu[:, :DE]
        act = (hg * jax.nn.sigmoid(hg) * hgu[:, DE:]).astype(bf16)
        ybuf_ref[g % 2] = _dot(act, wdb_ref[...]).astype(bf16)
        store(g).start()
        return c
    lax.fori_loop(0, count, block, 0)

    @pl.when(e == E - 1)
    def _():
        for back in (2, 1):
            @pl.when(used >= back)
            def _():
                store(used - back).wait()


def _experts(blk, xs, we_gate, we_up, we_down):
    ex = lambda e, blk_: (e, 0, 0)
    return pl.pallas_call(
        _expert_body,
        grid_spec=pltpu.PrefetchScalarGridSpec(
            num_scalar_prefetch=1,
            grid=(E,),
            in_specs=[pl.BlockSpec((1, D, DE), ex), pl.BlockSpec((1, D, DE), ex), pl.BlockSpec((1, DE, D), ex),
                      pl.BlockSpec(memory_space=pl.ANY)],
            out_specs=pl.BlockSpec(memory_space=pl.ANY),
            scratch_shapes=[pltpu.VMEM((EXPERT_IN_SLOTS, MOE_BM, D), bf16), pltpu.VMEM((2, MOE_BM, D), bf16),
                            pltpu.VMEM((D, 2 * DE), bf16), pltpu.VMEM((DE, D), bf16),
                            pltpu.SemaphoreType.DMA((EXPERT_IN_SLOTS,)), pltpu.SemaphoreType.DMA((2,))]),
        out_shape=jax.ShapeDtypeStruct(xs.shape, xs.dtype),
        input_output_aliases={4: 0},
        compiler_params=pltpu.CompilerParams(dimension_semantics=("arbitrary",), vmem_limit_bytes=VMEM_LIMIT),
        name="experts",
    )(blk, we_gate, we_up, we_down, xs)


def _final_body(nb, lt, tile0, n_ref, l_ref, g_ref, slot_ref, wt_ref, x2_ref, g2_ref, fw_ref, ys_ref, o_ref,
                buf_ref, routed_ref, sem):
    m = nb * lt
    n_inner = pl.num_programs(1)
    step = pl.program_id(0) * n_inner + pl.program_id(1)
    nsteps = pl.num_programs(0) * n_inner
    tile = tile0 + step
    cur = step % RING

    def run_copy(sl):
        return lambda srow, grow, b: pltpu.make_async_copy(
            ys_ref.at[pl.ds(grow, b)], buf_ref.at[sl, pl.ds(srow, b)], sem.at[sl])

    def fetch(ahead):
        sl = (step + ahead) % RING
        _for_each_run(n_ref, l_ref, g_ref, tile + ahead, lambda s, g, b, j: _start(run_copy(sl)(s, g, b), j))

    @pl.when(step == 0)
    def _():
        buf_ref[...] = jnp.zeros_like(buf_ref)
        for ahead in range(RING - 1):
            @pl.when(ahead < nsteps)
            def _():
                fetch(ahead)

    @pl.when(step + (RING - 1) < nsteps)
    def _():
        fetch(RING - 1)

    _wait_rows(lambda b: run_copy(cur)(0, 0, b), l_ref[E - 1, tile] + n_ref[E - 1, tile])

    slot = slot_ref[...].astype(i16)
    wt = wt_ref[...].astype(bf16)

    def combine(col0, ncols):
        scol = lax.broadcasted_iota(i32, (m, ncols), 1).astype(i16) + jnp.int16(col0)
        wsel = jnp.zeros((m, ncols), bf16)
        for kk in range(TOPK):
            wsel = jnp.where(scol == slot[:, kk:kk + 1], wt[:, kk:kk + 1], wsel)
        return _dot(wsel, buf_ref[cur, col0:col0 + ncols, :])

    routed_ref[...] = combine(0, SLOTS_MAIN)

    @pl.when(l_ref[E - 1, tile] + n_ref[E - 1, tile] > SLOTS_MAIN)
    def _():
        routed_ref[...] += combine(SLOTS_MAIN, SLOTS - SLOTS_MAIN)
    routed = routed_ref[...]
    g2 = g2_ref[0]
    if nb == 1:
        g2 = g2.reshape(1, D)
    else:
        g2 = jnp.broadcast_to(g2, (nb, lt, D)).reshape(m, D)
    y = x2_ref[...] + g2 * routed
    o_ref[...] = _rms(y, fw_ref[...]).reshape(nb, lt, D)


def _final(tables, slot_t, wts_t, x2, ys, mod4, fw, bsz, seqlen, row0, nb, lt):
    m = nb * lt
    assert m == TILE and row0 % TILE == 0
    n_lt = seqlen // lt
    tile0 = row0 // TILE
    tile = lambda b, i: tile0 + b * n_lt + i
    return pl.pallas_call(
        functools.partial(_final_body, nb, lt, tile0),
        grid_spec=pltpu.PrefetchScalarGridSpec(
            num_scalar_prefetch=len(tables),
            grid=(bsz // nb, n_lt),
            in_specs=[pl.BlockSpec((m, TOPK), lambda b, i, *_: (tile(b, i), 0)),
                      pl.BlockSpec((m, TOPK), lambda b, i, *_: (tile(b, i), 0)),
                      pl.BlockSpec((m, D), lambda b, i, *_: (b * n_lt + i, 0)),
                      pl.BlockSpec((1, nb, 1, D), lambda b, i, *_: (5, b, 0, 0)),
                      pl.BlockSpec((1, D), lambda b, i, *_: (0, 0)),
                      pl.BlockSpec(memory_space=pl.ANY)],
            out_specs=pl.BlockSpec((nb, lt, D), lambda b, i, *_: (b, i, 0)),
            scratch_shapes=[pltpu.VMEM((RING, SLOTS, D), bf16), pltpu.VMEM((m, D), f32),
                            pltpu.SemaphoreType.DMA((RING,))]),
        out_shape=jax.ShapeDtypeStruct((bsz, seqlen, D), f32),
        compiler_params=pltpu.CompilerParams(dimension_semantics=("arbitrary", "arbitrary"),
                                             vmem_limit_bytes=VMEM_LIMIT),
        name="final",
    )(*tables, slot_t, wts_t, x2, mod4, fw, ys)


def kernel(x_prompt, x_sample, state_conv, state_gla, c_prompt, c_sample, w_ada, b_ada, norm1_w, w_in, conv_w,
           w_gk, b_gk, gla_norm_w, w_out_conv, w_out_gla, w_o, norm2_w, router_w, router_bias, we_gate, we_up,
           we_down, ws_gate, ws_up, ws_down, final_norm_w):
    assert w_ada.shape[0] == 1, "single-layer step"
    bp, lp, _ = x_prompt.shape
    bs, ls, _ = x_sample.shape
    tp, ts = bp * lp, bs * ls
    t_all = tp + ts

    w_in0 = w_in[0]
    n_main = 3 * D + 2 * DK + 2 * DV
    rank = w_gk.shape[1]
    weights = (
        norm1_w[0].reshape(1, D),
        w_in0[:, :n_main].astype(bf16),
        jnp.pad(w_in0[:, n_main:n_main + rank], ((0, 0), (0, RANK_PAD - rank))).astype(bf16),
        w_in0[:, n_main + rank:].astype(bf16),
        jnp.pad(w_gk[0], ((0, RANK_PAD - rank), (0, 0))).astype(bf16),
        b_gk[0].reshape(1, DK),
        conv_w[0],
        gla_norm_w[0].reshape(1, DVH),
        w_out_conv[0].astype(bf16),
        w_out_gla[0].astype(bf16),
        w_o[0].astype(bf16),
        norm2_w[0].reshape(1, D),
        router_w[0].T.astype(bf16),
        ws_gate[0].astype(bf16),
        ws_up[0].astype(bf16),
        ws_down[0].astype(bf16),
    )

    mod = _adaln(jnp.concatenate([c_prompt, c_sample], axis=0), w_ada[0], b_ada[0].reshape(1, 6 * D))
    mod4 = mod.reshape(bp + bs, 6, 1, D).transpose(1, 0, 2, 3)
    mod_p, mod_s = mod4[:, :bp], mod4[:, bp:]

    lt_p = 512
    x2_p, h2_p, lg_p, conv_p, gla_p = _mixer(x_prompt, mod_p, None, weights, 1, lt_p, GLA_CHUNK)
    nb_s = 8
    x2_s, h2_s, lg_s, conv_s, gla_s = _mixer(x_sample, mod_s, (state_conv[0], state_gla[0]), weights,
                                             nb_s, ls, math.gcd(ls, GLA_CHUNK))
    lg = jnp.concatenate([lg_p.transpose(1, 0, 2).reshape(E, tp), lg_s.transpose(1, 0, 2).reshape(E, ts)], axis=1)

    eidx, wts, lrank, cnt = _route(lg, router_bias[0].reshape(E, 1))
    n_tiles = t_all // TILE
    max_rows = t_all * TOPK + n_tiles * E * (RUN_ALIGN - 1) + E * (MOE_BM - RUN_ALIGN)
    nblk = -(-max_rows // MOE_BM)
    slot, nrun, loff, goff, blk = _plan(eidx, lrank, cnt)
    tables = (nrun, loff, goff)

    xs = _dispatch(tables, slot, h2_p, h2_s, nblk * MOE_BM)
    ys = _experts(blk, xs, we_gate[0], we_up[0], we_down[0])

    fw = final_norm_w.reshape(1, D)
    slot_t, wts_t = slot.T, wts.T
    y_prompt = _final(tables, slot_t, wts_t, x2_p, ys, mod_p, fw, bp, lp, 0, 1, TILE)
    y_sample = _final(tables, slot_t, wts_t, x2_s, ys, mod_s, fw, bs, ls, tp, TILE // ls, ls)
    return (y_prompt, y_sample, conv_p, gla_p, conv_s, gla_s)
```

```python
import functools
import math

import jax
import jax.numpy as jnp
from jax import lax
from jax.experimental import pallas as pl
from jax.experimental.pallas import tpu as pltpu

f32 = jnp.float32
bf16 = jnp.bfloat16
i32 = jnp.int32
i16 = jnp.int16

D = 1024
H = 4
DKH = 128
DVH = 256
DK = H * DKH
DV = H * DVH
RANK_PAD = 128
GATE_NORMALIZER = 16.0
GLA_CHUNK = 64
E = 64
TOPK = 8
NGROUPS = 8
TOPK_GROUPS = 4
DE = 256
ROUTED_SCALE = 2.5
EPS = 1e-6
NEG = float("-inf")

V7X_VMEM_BYTES = 64 * 1024 * 1024
VMEM_LIMIT = V7X_VMEM_BYTES - 6 * 1024 * 1024
MOE_BM = 512
EXPERT_IN_SLOTS = 4
EXPERT_OUT_SLOTS = 3
RING = 2
TILE = 256
MAX_TILES = 128
ROUTE_TILES = 2
RUN_ALIGN = 16
SLOTS = TILE * TOPK + 1024
SLOTS_MAIN = TILE * TOPK + 512
RUN_PIECES = (256, 128, 64, 32, 16)
TAIL_PIECES = (256, 128, 64, 32, 16)
WAIT_PIECES = (2048, 1024, 512, 256, 128, 64, 32, 16)

def _dot(a, b):
    return jnp.dot(a, b, preferred_element_type=f32)


def _split3(x):
    hi = x.astype(bf16)
    r1 = x - hi.astype(f32)
    mid = r1.astype(bf16)
    lo = (r1 - mid.astype(f32)).astype(bf16)
    return hi, mid, lo


def _dot01(m01, x):
    hi, mid, lo = _split3(x)
    return _dot(m01, hi) + _dot(m01, mid) + _dot(m01, lo)


def _dotx01(x, m01):
    hi, mid, lo = _split3(x)
    return _dot(hi, m01) + _dot(mid, m01) + _dot(lo, m01)


def _rms(v, w):
    ms = jnp.mean(v * v, axis=-1, keepdims=True)
    return v * lax.rsqrt(ms + EPS) * w


def _const_spec(shape):
    n = len(shape)
    return pl.BlockSpec(shape, lambda *_: (0,) * n, pipeline_mode=pl.Buffered(1))


def _adaln_body(c_ref, w_ref, b_ref, o_ref):
    c = c_ref[...]
    a = (c * jax.nn.sigmoid(c)).astype(bf16)
    o_ref[...] = _dot(a, w_ref[...].astype(bf16)) + b_ref[...]


def _adaln(c_all, w_ada, b_ada):
    nrow = c_all.shape[0]
    tn = 1536
    return pl.pallas_call(
        _adaln_body,
        grid=(6 * D // tn,),
        in_specs=[pl.BlockSpec((nrow, D), lambda j: (0, 0)),
                  pl.BlockSpec((D, tn), lambda j: (0, j)),
                  pl.BlockSpec((1, tn), lambda j: (0, j))],
        out_specs=pl.BlockSpec((nrow, tn), lambda j: (0, j)),
        out_shape=jax.ShapeDtypeStruct((nrow, 6 * D), f32),
        compiler_params=pltpu.CompilerParams(dimension_semantics=("arbitrary",), vmem_limit_bytes=VMEM_LIMIT),
        name="adaln",
    )(c_all, w_ada, b_ada)


def _mixer_body(nb, lt, chunk, has_state, *refs):
    if has_state:
        (x_ref, mod_ref, cs_ref, gs_ref, *rest) = refs
    else:
        (x_ref, mod_ref, *rest) = refs
        cs_ref = gs_ref = None
    (n1_ref, wmain_ref, wz_ref, wu_ref, wgk_ref, bgk_ref, cw_ref, gnw_ref, woc_ref, wog_ref, wo_ref,
     n2_ref, rwt_ref, wsg_ref, wsu_ref, wsd_ref,
     x2_ref, h2_ref, lg_ref, nc_ref, ng_ref, carry_ref, st_ref, og_ref, sin_ref) = rest
    m = nb * lt
    i = pl.program_id(1)
    last = pl.num_programs(1) - 1

    def rows(v):
        w = v.shape[-1]
        if nb == 1:
            return v.reshape(1, w)
        return jnp.broadcast_to(v, (nb, lt, w)).reshape(m, w)

    @pl.when(i == 0)
    def _():
        if has_state:
            cs = cs_ref[...]
            carry_ref[0] = cs[:, 0:1, :]
            carry_ref[1] = cs[:, 1:2, :]
        else:
            carry_ref[...] = jnp.zeros_like(carry_ref)
            st_ref[...] = jnp.zeros_like(st_ref)

    x = x_ref[...].reshape(m, D)
    sh1, sc1, g1, sh2, sc2, g2 = [rows(mod_ref[j]) for j in range(6)]
    hb = (_rms(x, n1_ref[...]) * (1.0 + sc1) + sh1).astype(bf16)

    def proj(lo, hi):
        return _dot(hb, wmain_ref[:, lo:hi])

    pre = proj(D, 2 * D) * proj(2 * D, 3 * D)
    c0 = rows(carry_ref[0])
    c1 = rows(carry_ref[1])
    l_idx = lax.broadcasted_iota(i32, (m, D), 0) & (lt - 1)
    r1 = pltpu.roll(pre, 1, 0)
    r2 = pltpu.roll(pre, 2, 0)
    prev1 = jnp.where(l_idx == 0, c1, r1)
    prev2 = jnp.where(l_idx == 0, c0, jnp.where(l_idx == 1, c1, r2))
    cw = cw_ref[...]
    conv = cw[0:1] * prev2 + cw[1:2] * prev1 + cw[2:3] * pre
    y_a = _dot((proj(0, D) * conv).astype(bf16), woc_ref[...])
    pre3 = pre.reshape(nb, lt, D)
    tail = pre3[:, lt - 2:lt, :]
    carry_ref[0] = tail[:, 0:1, :]
    carry_ref[1] = tail[:, 1:2, :]
    nc_ref[0] = tail

    q = proj(3 * D, 3 * D + DK)
    k = proj(3 * D + DK, 3 * D + 2 * DK)
    v = proj(4 * D, 5 * D)
    g = proj(5 * D, 6 * D)
    z = _dot(hb, wz_ref[...]).astype(bf16)
    pa = _dot(z, wgk_ref[...]) + bgk_ref[...]
    la = (jnp.minimum(pa, 0.0) - jnp.log1p(jnp.exp(-jnp.abs(pa)))) / GATE_NORMALIZER
    pos = lax.broadcasted_iota(i32, (m, DK), 0) & (chunk - 1)
    bcum = la
    shift = 1
    while shift < chunk:
        bcum = bcum + jnp.where(pos >= shift, pltpu.roll(bcum, shift, 0), 0.0)
        shift *= 2
    ngrp = m // chunk
    bend = bcum.reshape(ngrp, chunk, DK)[:, chunk - 1:chunk, :]
    bend_rows = jnp.broadcast_to(bend, (ngrp, chunk, DK)).reshape(m, DK)
    nch = lt // chunk
    assert nb == 1 or nch == 1
    hs = H * nb
    nbat = nch * hs

    def per_head(xm, w):
        x3 = xm.reshape(nb * nch, chunk, H * w)
        parts = [x3[:, :, hh * w:(hh + 1) * w] for hh in range(H)]
        return jnp.stack(parts, axis=1 if nb == 1 else 0).reshape(nbat, chunk, w)

    qd = per_head((q * (DKH ** -0.5) * jnp.exp(bcum)).astype(bf16), DKH)
    kd = per_head((k * jnp.exp(-bcum)).astype(bf16), DKH)
    ke = per_head((k * jnp.exp(bend_rows - bcum)).astype(bf16), DKH)
    vv = per_head(v.astype(bf16), DVH)
    assert nbat <= 128
    a_end = per_head(jnp.exp(bend_rows), DKH)[:, 0, :]
    a_col = jnp.concatenate([a_end, jnp.zeros((128 - nbat, DKH), f32)], axis=0).T
    causal = lax.broadcasted_iota(i32, (chunk, chunk), 1) <= lax.broadcasted_iota(i32, (chunk, chunk), 0)
    sc = jnp.einsum('bcd,bsd->bcs', qd, kd, preferred_element_type=f32)
    sc = jnp.where(causal, sc, 0.0).astype(bf16)
    o = jnp.einsum('bcs,bse->bce', sc, vv, preferred_element_type=f32)
    upd = jnp.einsum('bsd,bse->bde', ke, vv, preferred_element_type=f32)
    st = [gs_ref[s % nb, s // nb] if has_state else st_ref[s // nb, s % nb] for s in range(hs)]
    for j in range(nch):
        for s in range(hs):
            b = j * hs + s
            sin_ref[b] = st[s].astype(bf16)
            st[s] = st[s] * a_col[:, b:b + 1] + upd[b]
    for s in range(hs):
        if has_state:
            ng_ref[0, s % nb, s // nb] = st[s]
        else:
            st_ref[s // nb, s % nb] = st[s]
    o = o + jnp.einsum('bcd,bde->bce', qd, sin_ref[...], preferred_element_type=f32)
    on = _rms(o, gnw_ref[...])
    g3 = g.reshape(nb, lt, DV)
    for j in range(nch):
        js = slice(j * chunk, (j + 1) * chunk)
        for hh in range(H):
            vs = slice(hh * DVH, (hh + 1) * DVH)
            gh = g3[:, js, vs]
            b0 = (j * H + hh) * nb
            og_ref[:, js, vs] = (on[b0:b0 + nb] * (gh * jax.nn.sigmoid(gh))).astype(bf16)

    if not has_state:
        @pl.when(i == last)
        def _():
            for hh in range(H):
                ng_ref[0, :, hh] = st_ref[hh]

    y_b = _dot(og_ref[...].reshape(m, DV), wog_ref[...])
    u_a = _dot(hb, wu_ref[:, 0:D])
    u_b = _dot(hb, wu_ref[:, D:2 * D])
    merged = (jax.nn.sigmoid(u_a) * y_a + jax.nn.sigmoid(u_b) * y_b).astype(bf16)
    x1 = x + g1 * _dot(merged, wo_ref[...])

    h2 = (_rms(x1, n2_ref[...]) * (1.0 + sc2) + sh2).astype(bf16)
    sg = _dot(h2, wsg_ref[...])
    act = (sg * jax.nn.sigmoid(sg) * _dot(h2, wsu_ref[...])).astype(bf16)
    x2_ref[...] = x1 + g2 * _dot(act, wsd_ref[...])
    h2_ref[...] = h2
    lg_ref[0] = lax.dot_general(rwt_ref[...], h2, (((1,), (1,)), ((), ())), preferred_element_type=f32)


def _mixer(x, mod4, states, weights, nb, lt, chunk):
    bsz, seqlen, _ = x.shape
    has_state = states is not None
    m = nb * lt
    n_lt = seqlen // lt
    assert not has_state or n_lt == 1
    nsteps = (bsz // nb) * n_lt
    t_all = bsz * seqlen

    def tok_blk(b, i):
        return b * n_lt + i

    in_specs = [pl.BlockSpec((nb, lt, D), lambda b, i: (b, i, 0)),
                pl.BlockSpec((6, nb, 1, D), lambda b, i: (0, b, 0, 0))]
    args = [x, mod4]
    if has_state:
        in_specs += [pl.BlockSpec((nb, 2, D), lambda b, i: (b, 0, 0)),
                     pl.BlockSpec((nb, H, DKH, DVH), lambda b, i: (b, 0, 0, 0))]
        args += list(states)
    in_specs += [_const_spec(w.shape) for w in weights]
    args += list(weights)
    out_specs = [pl.BlockSpec((m, D), lambda b, i: (tok_blk(b, i), 0)),
                 pl.BlockSpec((m, D), lambda b, i: (tok_blk(b, i), 0)),
                 pl.BlockSpec((1, E, m), lambda b, i: (b * n_lt + i, 0, 0)),
                 pl.BlockSpec((1, nb, 2, D), lambda b, i: (0, b, 0, 0)),
                 pl.BlockSpec((1, nb, H, DKH, DVH), lambda b, i: (0, b, 0, 0, 0))]
    out_shape = [jax.ShapeDtypeStruct((t_all, D), f32),
                 jax.ShapeDtypeStruct((t_all, D), bf16),
                 jax.ShapeDtypeStruct((nsteps, E, m), f32),
                 jax.ShapeDtypeStruct((1, bsz, 2, D), f32),
                 jax.ShapeDtypeStruct((1, bsz, H, DKH, DVH), f32)]
    return pl.pallas_call(
        functools.partial(_mixer_body, nb, lt, chunk, has_state),
        grid=(bsz // nb, n_lt),
        in_specs=in_specs,
        out_specs=out_specs,
        out_shape=out_shape,
        scratch_shapes=[pltpu.VMEM((2, nb, 1, D), f32),
                        pltpu.VMEM((H, nb, DKH, DVH), f32),
                        pltpu.VMEM((nb, lt, DV), bf16),
                        pltpu.VMEM(((lt // chunk) * H * nb, DKH, DVH), bf16)],
        compiler_params=pltpu.CompilerParams(dimension_semantics=("arbitrary", "arbitrary"),
                                             vmem_limit_bytes=VMEM_LIMIT),
        name="mixer_state" if has_state else "mixer_prompt",
    )(*args)


def _route_body(lg_ref, bias_ref, eidx_ref, w_ref, lrank_ref, cnt_ref):
    step = pl.program_id(0)
    tl = lg_ref.shape[1]

    scores = jax.nn.sigmoid(lg_ref[...])
    choice = scores + bias_ref[...]
    gsz = E // NGROUPS
    grp = choice.reshape(NGROUPS, gsz, tl)
    mi = lax.broadcasted_iota(i32, (NGROUPS, gsz, tl), 1)
    m1 = jnp.max(grp, axis=1, keepdims=True)
    first = jnp.min(jnp.where(grp == m1, mi, gsz), axis=1, keepdims=True)
    m2 = jnp.max(jnp.where(mi == first, NEG, grp), axis=1, keepdims=True)
    gscore = (m1 + m2).reshape(NGROUPS, tl)
    gi = lax.broadcasted_iota(i32, (NGROUPS, tl), 0)
    gsel = jnp.zeros((NGROUPS, tl), f32)
    work = gscore
    for _ in range(TOPK_GROUPS):
        mx = jnp.max(work, axis=0, keepdims=True)
        f = jnp.min(jnp.where(work == mx, gi, NGROUPS), axis=0, keepdims=True)
        hit = gi == f
        gsel = jnp.where(hit, 1.0, gsel)
        work = jnp.where(hit, NEG, work)
    emask = jnp.broadcast_to(gsel.reshape(NGROUPS, 1, tl), (NGROUPS, gsz, tl)).reshape(E, tl)
    masked = jnp.where(emask > 0.0, choice, NEG)
    ei = lax.broadcasted_iota(i32, (E, tl), 0)
    sel = jnp.zeros((E, tl), f32)
    hits, ws = [], []
    for kk in range(TOPK):
        mx = jnp.max(masked, axis=0, keepdims=True)
        f = jnp.min(jnp.where(masked == mx, ei, E), axis=0, keepdims=True)
        hit = ei == f
        ws.append(jnp.sum(jnp.where(hit, scores, 0.0), axis=0, keepdims=True))
        masked = jnp.where(hit, NEG, masked)
        sel = jnp.where(hit, 1.0, sel)
        hits.append(hit)
        eidx_ref[kk:kk + 1, :] = f
    wsum = ws[0]
    for t in ws[1:]:
        wsum = wsum + t
    for kk in range(TOPK):
        w_ref[kk:kk + 1, :] = ws[kk] / wsum * ROUTED_SCALE

    ui = lax.broadcasted_iota(i32, (tl, tl), 0)
    uj = lax.broadcasted_iota(i32, (tl, tl), 1)
    upper = ((ui <= uj) & ((ui & -TILE) == (uj & -TILE))).astype(bf16)
    lrank = _dot(sel.astype(bf16), upper) - sel
    for kk in range(TOPK):
        lrank_ref[kk:kk + 1, :] = jnp.sum(jnp.where(hits[kk], lrank, 0.0), axis=0, keepdims=True).astype(i32)

    @pl.when(step == 0)
    def _():
        cnt_ref[...] = jnp.zeros_like(cnt_ref)
    lane = lax.broadcasted_iota(i32, cnt_ref.shape, 1)
    tiles_per_step = tl // TILE
    for sub in range(tiles_per_step):
        col = jnp.sum(sel[:, sub * TILE:(sub + 1) * TILE], axis=1, keepdims=True)
        cnt_ref[...] = jnp.where(lane == step * tiles_per_step + sub, col, cnt_ref[...])


def _route(logits_t, bias_col):
    t_all = logits_t.shape[1]
    tl = ROUTE_TILES * TILE
    assert t_all % tl == 0 and t_all // TILE <= MAX_TILES
    tok = lambda s: (0, s)
    return pl.pallas_call(
        _route_body,
        grid=(t_all // tl,),
        in_specs=[pl.BlockSpec((E, tl), tok), pl.BlockSpec((E, 1), lambda s: (0, 0))],
        out_specs=[pl.BlockSpec((TOPK, tl), tok), pl.BlockSpec((TOPK, tl), tok), pl.BlockSpec((TOPK, tl), tok),
                   pl.BlockSpec((E, MAX_TILES), lambda s: (0, 0))],
        out_shape=[jax.ShapeDtypeStruct((TOPK, t_all), i32), jax.ShapeDtypeStruct((TOPK, t_all), f32),
                   jax.ShapeDtypeStruct((TOPK, t_all), i32), jax.ShapeDtypeStruct((E, MAX_TILES), f32)],
        compiler_params=pltpu.CompilerParams(dimension_semantics=("arbitrary",), vmem_limit_bytes=VMEM_LIMIT),
        name="route",
    )(logits_t, bias_col)


def _plan_body(eidx_ref, lrank_ref, cnt_ref, slot_ref, n_ref, l_ref, g_ref, blk_ref):
    t_all = eidx_ref.shape[1]
    cnt = cnt_ref[...]
    nrun = jnp.floor((cnt + (RUN_ALIGN - 1)) * (1.0 / RUN_ALIGN)) * RUN_ALIGN
    ti = lax.broadcasted_iota(i32, (MAX_TILES, MAX_TILES), 0)
    tj = lax.broadcasted_iota(i32, (MAX_TILES, MAX_TILES), 1)
    before = _dotx01(nrun, (ti < tj).astype(bf16))
    region = jnp.sum(nrun, axis=1, keepdims=True)
    nblocks = jnp.floor((region + (MOE_BM - 1)) * (1.0 / MOE_BM))
    ri = lax.broadcasted_iota(i32, (E, E), 0)
    ci = lax.broadcasted_iota(i32, (E, E), 1)
    below = (ci < ri).astype(bf16)
    bstart = _dot01(below, jnp.broadcast_to(nblocks, (E, MAX_TILES)))
    loff = _dot01(below, nrun)
    n_ref[...] = nrun.astype(i32)
    l_ref[...] = loff.astype(i32)
    g_ref[...] = (bstart * MOE_BM + before).astype(i32)
    lane0 = lax.broadcasted_iota(i32, (E, MAX_TILES), 1) == 0
    blk_ref[...] = jnp.where(lane0, bstart, nblocks).astype(i32)

    tile_of = lax.shift_right_logical(lax.broadcasted_iota(i32, (MAX_TILES, t_all), 1), TILE.bit_length() - 1)
    spread = (lax.broadcasted_iota(i32, (MAX_TILES, t_all), 0) == tile_of).astype(bf16)
    ltok = _dotx01(loff, spread)
    ei = lax.broadcasted_iota(i32, (E, t_all), 0)
    for kk in range(TOPK):
        hit = ei == eidx_ref[kk:kk + 1, :]
        start = jnp.sum(jnp.where(hit, ltok, 0.0), axis=0, keepdims=True)
        slot_ref[kk:kk + 1, :] = start.astype(i32) + lrank_ref[kk:kk + 1, :]


def _plan(eidx, lrank, cnt):
    t_all = eidx.shape[1]
    assert TILE & (TILE - 1) == 0
    tl = t_all
    tok = lambda s: (0, s)
    table = pl.BlockSpec((E, MAX_TILES), lambda s: (0, 0))
    return pl.pallas_call(
        _plan_body,
        grid=(1,),
        in_specs=[pl.BlockSpec((TOPK, tl), tok), pl.BlockSpec((TOPK, tl), tok), table],
        out_specs=[pl.BlockSpec((TOPK, tl), tok), table, table, table, table],
        out_shape=[jax.ShapeDtypeStruct((TOPK, t_all), i32)] + [jax.ShapeDtypeStruct((E, MAX_TILES), i32)] * 4,
        compiler_params=pltpu.CompilerParams(dimension_semantics=("arbitrary",), vmem_limit_bytes=VMEM_LIMIT),
        name="plan",
    )(eidx, lrank, cnt)


def _for_each_piece(total, pieces, fn):
    for j, b in enumerate(pieces):
        @pl.when((total & b) != 0)
        def _():
            fn(pl.multiple_of(total & ~(2 * b - 1), pieces[-1]), b, j)


def _for_each_run(n_ref, l_ref, g_ref, tile, fn):
    def per_expert(e, c):
        lo = pl.multiple_of(l_ref[e, tile], RUN_ALIGN)
        go = pl.multiple_of(g_ref[e, tile], RUN_ALIGN)
        _for_each_piece(n_ref[e, tile], RUN_PIECES, lambda off, b, j: fn(lo + off, go + off, b, j))
        return c
    lax.fori_loop(0, E, per_expert, 0)


def _start(copy, j):
    copy.start(priority=j % 2)


def _wait_rows(make_copy, rows):
    _for_each_piece(rows, WAIT_PIECES, lambda off, b, j: make_copy(b).wait())


def _dispatch_body(tiles_a, n_ref, l_ref, g_ref, slot_ref, ha_ref, hb_ref, xs_ref, buf_ref, zero_ref, sem, zsem):
    tile = pl.program_id(0)
    nsteps = pl.num_programs(0)
    cur = tile % RING

    slot = slot_ref[...].astype(i16)
    h = jnp.where(tile < tiles_a, ha_ref[...], hb_ref[...])
    filled = l_ref[E - 1, tile] + n_ref[E - 1, tile]

    def order(row0, nrows):
        srow = lax.broadcasted_iota(i32, (nrows, TILE), 0).astype(i16) + jnp.int16(row0)
        onehot = jnp.zeros((nrows, TILE), bf16)
        for kk in range(TOPK):
            onehot = jnp.where(srow == slot[kk:kk + 1, :], jnp.ones((), bf16), onehot)
        buf_ref[cur, row0:row0 + nrows, :] = _dot(onehot, h).astype(bf16)

    order(0, SLOTS_MAIN)

    @pl.when(filled > SLOTS_MAIN)
    def _():
        order(SLOTS_MAIN, SLOTS - SLOTS_MAIN)

    def run_copy(sl):
        return lambda srow_, grow, b: pltpu.make_async_copy(
            buf_ref.at[sl, pl.ds(srow_, b)], xs_ref.at[pl.ds(grow, b)], sem.at[sl])

    _for_each_run(n_ref, l_ref, g_ref, tile, lambda s, g, b, j: _start(run_copy(cur)(s, g, b), j))

    def wait_tile(t, sl):
        _wait_rows(lambda b: run_copy(sl)(0, 0, b), l_ref[E - 1, t] + n_ref[E - 1, t])

    @pl.when(tile >= RING - 1)
    def _():
        wait_tile(tile - (RING - 1), (tile + 1) % RING)

    last = nsteps - 1

    def zero_fill(fn):
        def per_expert(e, c):
            end = pl.multiple_of(g_ref[e, last] + n_ref[e, last], RUN_ALIGN)
            _for_each_piece((-end) & (MOE_BM - 1), TAIL_PIECES, lambda off, b, j: fn(pltpu.make_async_copy(
                zero_ref.at[pl.ds(0, b)], xs_ref.at[pl.ds(end + off, b)], zsem)))
            return c
        lax.fori_loop(0, E, per_expert, 0)
        used = (g_ref[E - 1, last] + n_ref[E - 1, last] + (MOE_BM - 1)) // MOE_BM

        def per_block(blk, c):
            fn(pltpu.make_async_copy(zero_ref, xs_ref.at[pl.ds(pl.multiple_of(blk * MOE_BM, MOE_BM), MOE_BM)], zsem))
            return c
        lax.fori_loop(used, xs_ref.shape[0] // MOE_BM, per_block, 0)

    @pl.when(tile == 0)
    def _():
        zero_ref[...] = jnp.zeros_like(zero_ref)
        zero_fill(lambda cp: cp.start())

    @pl.when(tile == last)
    def _():
        for back in range(RING - 2, -1, -1):
            @pl.when(tile >= back)
            def _():
                wait_tile(tile - back, (tile - back) % RING)
        zero_fill(lambda cp: cp.wait())


def _dispatch(tables, slot, h2_a, h2_b, nrows):
    tiles_a, tiles_b = h2_a.shape[0] // TILE, h2_b.shape[0] // TILE
    return pl.pallas_call(
        functools.partial(_dispatch_body, tiles_a),
        grid_spec=pltpu.PrefetchScalarGridSpec(
            num_scalar_prefetch=len(tables),
            grid=(tiles_a + tiles_b,),
            in_specs=[pl.BlockSpec((TOPK, TILE), lambda i, *_: (0, i)),
                      pl.BlockSpec((TILE, D), lambda i, *_: (jnp.minimum(i, tiles_a - 1), 0)),
                      pl.BlockSpec((TILE, D), lambda i, *_: (jnp.maximum(i - tiles_a, 0), 0))],
            out_specs=pl.BlockSpec(memory_space=pl.ANY),
            scratch_shapes=[pltpu.VMEM((RING, SLOTS, D), bf16), pltpu.VMEM((MOE_BM, D), bf16),
                            pltpu.SemaphoreType.DMA((RING,)), pltpu.SemaphoreType.DMA]),
        out_shape=jax.ShapeDtypeStruct((nrows, D), bf16),
        compiler_params=pltpu.CompilerParams(dimension_semantics=("arbitrary",), vmem_limit_bytes=VMEM_LIMIT),
        name="dispatch",
    )(*tables, slot, h2_a, h2_b)


def _expert_body(blk_ref, wg_ref, wu_ref, wd_ref, xs_ref, ys_ref, xbuf_ref, ybuf_ref, wgu_ref, wdb_ref, isem, osem):
    e = pl.program_id(0)
    first, count = blk_ref[e, 0], blk_ref[e, 1]
    used = blk_ref[E - 1, 0] + blk_ref[E - 1, 1]
    wgu_ref[:, :DE] = wg_ref[0].astype(bf16)
    wgu_ref[:, DE:] = wu_ref[0].astype(bf16)
    wdb_ref[...] = wd_ref[0].astype(bf16)

    def rows(g):
        return pl.ds(pl.multiple_of(g * MOE_BM, MOE_BM), MOE_BM)

    def load(g):
        sl = g % EXPERT_IN_SLOTS
        return pltpu.make_async_copy(xs_ref.at[rows(g)], xbuf_ref.at[sl], isem.at[sl])

    def store(g):
        sl = g % EXPERT_OUT_SLOTS
        return pltpu.make_async_copy(ybuf_ref.at[sl], ys_ref.at[rows(g)], osem.at[sl])

    @pl.when(e == 0)
    def _():
        for g in range(EXPERT_IN_SLOTS - 1):
            @pl.when(g < used)
            def _():
                load(g).start()

    def block(j, c):
        g = first + j

        @pl.when(g + (EXPERT_IN_SLOTS - 1) < used)
        def _():
            load(g + (EXPERT_IN_SLOTS - 1)).start()
        load(g).wait()

        @pl.when(g >= EXPERT_OUT_SLOTS)
        def _():
            store(g - EXPERT_OUT_SLOTS).wait()
        hgu = _dot(xbuf_ref[g % EXPERT_IN_SLOTS], wgu_ref[...])
        hg = hgu[:, :DE]
        act = (hg * jax.nn.sigmoid(hg) * hgu[:, DE:]).astype(bf16)
        ybuf_ref[g % EXPERT_OUT_SLOTS] = _dot(act, wdb_ref[...]).astype(bf16)
        store(g).start()
        return c
    lax.fori_loop(0, count, block, 0)

    @pl.when(e == E - 1)
    def _():
        for back in range(EXPERT_OUT_SLOTS, 0, -1):
            @pl.when(used >= back)
            def _():
                store(used - back).wait()


def _experts(blk, xs, we_gate, we_up, we_down):
    ex = lambda e, blk_: (e, 0, 0)
    return pl.pallas_call(
        _expert_body,
        grid_spec=pltpu.PrefetchScalarGridSpec(
            num_scalar_prefetch=1,
            grid=(E,),
            in_specs=[pl.BlockSpec((1, D, DE), ex), pl.BlockSpec((1, D, DE), ex), pl.BlockSpec((1, DE, D), ex),
                      pl.BlockSpec(memory_space=pl.ANY)],
            out_specs=pl.BlockSpec(memory_space=pl.ANY),
            scratch_shapes=[pltpu.VMEM((EXPERT_IN_SLOTS, MOE_BM, D), bf16),
                            pltpu.VMEM((EXPERT_OUT_SLOTS, MOE_BM, D), bf16),
                            pltpu.VMEM((D, 2 * DE), bf16), pltpu.VMEM((DE, D), bf16),
                            pltpu.SemaphoreType.DMA((EXPERT_IN_SLOTS,)),
                            pltpu.SemaphoreType.DMA((EXPERT_OUT_SLOTS,))]),
        out_shape=jax.ShapeDtypeStruct(xs.shape, xs.dtype),
        input_output_aliases={4: 0},
        compiler_params=pltpu.CompilerParams(dimension_semantics=("arbitrary",), vmem_limit_bytes=VMEM_LIMIT),
        name="experts",
    )(blk, we_gate, we_up, we_down, xs)


def _final_body(nb, lt, tile0, n_ref, l_ref, g_ref, slot_ref, wt_ref, x2_ref, g2_ref, fw_ref, ys_ref, o_ref,
                buf_ref, routed_ref, sem):
    m = nb * lt
    n_inner = pl.num_programs(1)
    step = pl.program_id(0) * n_inner + pl.program_id(1)
    nsteps = pl.num_programs(0) * n_inner
    tile = tile0 + step
    cur = step % RING

    def run_copy(sl):
        return lambda srow, grow, b: pltpu.make_async_copy(
            ys_ref.at[pl.ds(grow, b)], buf_ref.at[sl, pl.ds(srow, b)], sem.at[sl])

    def fetch(ahead):
        sl = (step + ahead) % RING
        _for_each_run(n_ref, l_ref, g_ref, tile + ahead, lambda s, g, b, j: _start(run_copy(sl)(s, g, b), j))

    @pl.when(step == 0)
    def _():
        buf_ref[...] = jnp.zeros_like(buf_ref)
        for ahead in range(RING - 1):
            @pl.when(ahead < nsteps)
            def _():
                fetch(ahead)

    @pl.when(step + (RING - 1) < nsteps)
    def _():
        fetch(RING - 1)

    _wait_rows(lambda b: run_copy(cur)(0, 0, b), l_ref[E - 1, tile] + n_ref[E - 1, tile])

    slot = slot_ref[...].astype(i16)
    wt = wt_ref[...].astype(bf16)

    def combine(col0, ncols):
        scol = lax.broadcasted_iota(i32, (m, ncols), 1).astype(i16) + jnp.int16(col0)
        wsel = jnp.zeros((m, ncols), bf16)
        for kk in range(TOPK):
            wsel = jnp.where(scol == slot[:, kk:kk + 1], wt[:, kk:kk + 1], wsel)
        return _dot(wsel, buf_ref[cur, col0:col0 + ncols, :])

    routed_ref[...] = combine(0, SLOTS_MAIN)

    @pl.when(l_ref[E - 1, tile] + n_ref[E - 1, tile] > SLOTS_MAIN)
    def _():
        routed_ref[...] += combine(SLOTS_MAIN, SLOTS - SLOTS_MAIN)
    routed = routed_ref[...]
    g2 = g2_ref[0]
    if nb == 1:
        g2 = g2.reshape(1, D)
    else:
        g2 = jnp.broadcast_to(g2, (nb, lt, D)).reshape(m, D)
    y = x2_ref[...] + g2 * routed
    o_ref[...] = _rms(y, fw_ref[...]).reshape(nb, lt, D)


def _final(tables, slot_t, wts_t, x2, ys, mod4, fw, bsz, seqlen, row0, nb, lt):
    m = nb * lt
    assert m == TILE and row0 % TILE == 0
    n_lt = seqlen // lt
    tile0 = row0 // TILE
    tile = lambda b, i: tile0 + b * n_lt + i
    return pl.pallas_call(
        functools.partial(_final_body, nb, lt, tile0),
        grid_spec=pltpu.PrefetchScalarGridSpec(
            num_scalar_prefetch=len(tables),
            grid=(bsz // nb, n_lt),
            in_specs=[pl.BlockSpec((m, TOPK), lambda b, i, *_: (tile(b, i), 0)),
                      pl.BlockSpec((m, TOPK), lambda b, i, *_: (tile(b, i), 0)),
                      pl.BlockSpec((m, D), lambda b, i, *_: (b * n_lt + i, 0)),
                      pl.BlockSpec((1, nb, 1, D), lambda b, i, *_: (5, b, 0, 0)),
                      pl.BlockSpec((1, D), lambda b, i, *_: (0, 0)),
                      pl.BlockSpec(memory_space=pl.ANY)],
            out_specs=pl.BlockSpec((nb, lt, D), lambda b, i, *_: (b, i, 0)),
            scratch_shapes=[pltpu.VMEM((RING, SLOTS, D), bf16), pltpu.VMEM((m, D), f32),
                            pltpu.SemaphoreType.DMA((RING,))]),
        out_shape=jax.ShapeDtypeStruct((bsz, seqlen, D), f32),
        compiler_params=pltpu.CompilerParams(dimension_semantics=("arbitrary", "arbitrary"),
                                             vmem_limit_bytes=VMEM_LIMIT),
        name="final",
    )(*tables, slot_t, wts_t, x2, mod4, fw, ys)


def kernel(x_prompt, x_sample, state_conv, state_gla, c_prompt, c_sample, w_ada, b_ada, norm1_w, w_in, conv_w,
           w_gk, b_gk, gla_norm_w, w_out_conv, w_out_gla, w_o, norm2_w, router_w, router_bias, we_gate, we_up,
           we_down, ws_gate, ws_up, ws_down, final_norm_w):
    assert w_ada.shape[0] == 1, "single-layer step"
    bp, lp, _ = x_prompt.shape
    bs, ls, _ = x_sample.shape
    tp, ts = bp * lp, bs * ls
    t_all = tp + ts

    w_in0 = w_in[0]
    n_main = 3 * D + 2 * DK + 2 * DV
    rank = w_gk.shape[1]
    weights = (
        norm1_w[0].reshape(1, D),
        w_in0[:, :n_main].astype(bf16),
        jnp.pad(w_in0[:, n_main:n_main + rank], ((0, 0), (0, RANK_PAD - rank))).astype(bf16),
        w_in0[:, n_main + rank:].astype(bf16),
        jnp.pad(w_gk[0], ((0, RANK_PAD - rank), (0, 0))).astype(bf16),
        b_gk[0].reshape(1, DK),
        conv_w[0],
        gla_norm_w[0].reshape(1, DVH),
        w_out_conv[0].astype(bf16),
        w_out_gla[0].astype(bf16),
        w_o[0].astype(bf16),
        norm2_w[0].reshape(1, D),
        router_w[0].T.astype(bf16),
        ws_gate[0].astype(bf16),
        ws_up[0].astype(bf16),
        ws_down[0].astype(bf16),
    )

    mod = _adaln(jnp.concatenate([c_prompt, c_sample], axis=0), w_ada[0], b_ada[0].reshape(1, 6 * D))
    mod4 = mod.reshape(bp + bs, 6, 1, D).transpose(1, 0, 2, 3)
    mod_p, mod_s = mod4[:, :bp], mod4[:, bp:]

    lt_p = 512
    x2_p, h2_p, lg_p, conv_p, gla_p = _mixer(x_prompt, mod_p, None, weights, 1, lt_p, GLA_CHUNK)
    nb_s = 8
    x2_s, h2_s, lg_s, conv_s, gla_s = _mixer(x_sample, mod_s, (state_conv[0], state_gla[0]), weights,
                                             nb_s, ls, math.gcd(ls, GLA_CHUNK))
    lg = jnp.concatenate([lg_p.transpose(1, 0, 2).reshape(E, tp), lg_s.transpose(1, 0, 2).reshape(E, ts)], axis=1)

    eidx, wts, lrank, cnt = _route(lg, router_bias[0].reshape(E, 1))
    n_tiles = t_all // TILE
    max_rows = t_all * TOPK + n_tiles * E * (RUN_ALIGN - 1) + E * (MOE_BM - RUN_ALIGN)
    nblk = -(-max_rows // MOE_BM)
    slot, nrun, loff, goff, blk = _plan(eidx, lrank, cnt)
    tables = (nrun, loff, goff)

    xs = _dispatch(tables, slot, h2_p, h2_s, nblk * MOE_BM)
    ys = _experts(blk, xs, we_gate[0], we_up[0], we_down[0])

    fw = final_norm_w.reshape(1, D)
    slot_t, wts_t = slot.T, wts.T
    y_prompt = _final(tables, slot_t, wts_t, x2_p, ys, mod_p, fw, bp, lp, 0, 1, TILE)
    y_sample = _final(tables, slot_t, wts_t, x2_s, ys, mod_s, fw, bs, ls, tp, TILE // ls, ls)
    return (y_prompt, y_sample, conv_p, gla_p, conv_s, gla_s)
```

```python
import functools
import math

import jax
import jax.numpy as jnp
from jax import lax
from jax.experimental import pallas as pl
from jax.experimental.pallas import tpu as pltpu

f32 = jnp.float32
bf16 = jnp.bfloat16
i32 = jnp.int32
i16 = jnp.int16

D = 1024
H = 4
DKH = 128
DVH = 256
DK = H * DKH
DV = H * DVH
RANK_PAD = 128
GATE_NORMALIZER = 16.0
GLA_CHUNK = 64
E = 64
TOPK = 8
NGROUPS = 8
TOPK_GROUPS = 4
DE = 256
ROUTED_SCALE = 2.5
EPS = 1e-6
NEG = float("-inf")

V7X_VMEM_BYTES = 64 * 1024 * 1024
VMEM_LIMIT = V7X_VMEM_BYTES - 6 * 1024 * 1024
MOE_BM = 512
EXPERT_IN_SLOTS = 4
RING = 2
TILE = 256
MAX_TILES = 128
ROUTE_TILES = 4
RUN_ALIGN = 16
SLOTS = TILE * TOPK + 1024
SLOTS_MAIN = TILE * TOPK + 512
RUN_PIECES = (256, 128, 64, 32, 16)
TAIL_PIECES = (256, 128, 64, 32, 16)
WAIT_PIECES = (2048, 1024, 512, 256, 128, 64, 32, 16)

def _dot(a, b):
    return jnp.dot(a, b, preferred_element_type=f32)


def _split3(x):
    hi = x.astype(bf16)
    r1 = x - hi.astype(f32)
    mid = r1.astype(bf16)
    lo = (r1 - mid.astype(f32)).astype(bf16)
    return hi, mid, lo


def _dot01(m01, x):
    hi, mid, lo = _split3(x)
    return _dot(m01, hi) + _dot(m01, mid) + _dot(m01, lo)


def _dotx01(x, m01):
    hi, mid, lo = _split3(x)
    return _dot(hi, m01) + _dot(mid, m01) + _dot(lo, m01)


def _rms(v, w):
    ms = jnp.mean(v * v, axis=-1, keepdims=True)
    return v * lax.rsqrt(ms + EPS) * w


def _const_spec(shape):
    n = len(shape)
    return pl.BlockSpec(shape, lambda *_: (0,) * n, pipeline_mode=pl.Buffered(1))


def _adaln_body(c_ref, w_ref, b_ref, o_ref):
    c = c_ref[...]
    a = (c * jax.nn.sigmoid(c)).astype(bf16)
    o_ref[...] = _dot(a, w_ref[...].astype(bf16)) + b_ref[...]


def _adaln(c_all, w_ada, b_ada):
    nrow = c_all.shape[0]
    tn = 1536
    return pl.pallas_call(
        _adaln_body,
        grid=(6 * D // tn,),
        in_specs=[pl.BlockSpec((nrow, D), lambda j: (0, 0)),
                  pl.BlockSpec((D, tn), lambda j: (0, j)),
                  pl.BlockSpec((1, tn), lambda j: (0, j))],
        out_specs=pl.BlockSpec((nrow, tn), lambda j: (0, j)),
        out_shape=jax.ShapeDtypeStruct((nrow, 6 * D), f32),
        compiler_params=pltpu.CompilerParams(dimension_semantics=("arbitrary",), vmem_limit_bytes=VMEM_LIMIT),
        name="adaln",
    )(c_all, w_ada, b_ada)


def _mixer_body(nb, lt, chunk, has_state, *refs):
    if has_state:
        (x_ref, mod_ref, cs_ref, gs_ref, *rest) = refs
    else:
        (x_ref, mod_ref, *rest) = refs
        cs_ref = gs_ref = None
    (n1_ref, wmain_ref, wz_ref, wu_ref, wgk_ref, bgk_ref, cw_ref, gnw_ref, woc_ref, wog_ref, wo_ref,
     n2_ref, rwt_ref, wsg_ref, wsu_ref, wsd_ref,
     x2_ref, h2_ref, lg_ref, nc_ref, ng_ref, carry_ref, st_ref, og_ref, sin_ref) = rest
    m = nb * lt
    i = pl.program_id(1)
    last = pl.num_programs(1) - 1

    def rows(v):
        w = v.shape[-1]
        if nb == 1:
            return v.reshape(1, w)
        return jnp.broadcast_to(v, (nb, lt, w)).reshape(m, w)

    @pl.when(i == 0)
    def _():
        if has_state:
            cs = cs_ref[...]
            carry_ref[0] = cs[:, 0:1, :]
            carry_ref[1] = cs[:, 1:2, :]
        else:
            carry_ref[...] = jnp.zeros_like(carry_ref)
            st_ref[...] = jnp.zeros_like(st_ref)

    x = x_ref[...].reshape(m, D)
    sh1, sc1, g1, sh2, sc2, g2 = [rows(mod_ref[j]) for j in range(6)]
    hb = (_rms(x, n1_ref[...]) * (1.0 + sc1) + sh1).astype(bf16)

    def proj(lo, hi):
        return _dot(hb, wmain_ref[:, lo:hi])

    pre = proj(D, 2 * D) * proj(2 * D, 3 * D)
    c0 = rows(carry_ref[0])
    c1 = rows(carry_ref[1])
    l_idx = lax.broadcasted_iota(i32, (m, D), 0) & (lt - 1)
    r1 = pltpu.roll(pre, 1, 0)
    r2 = pltpu.roll(pre, 2, 0)
    prev1 = jnp.where(l_idx == 0, c1, r1)
    prev2 = jnp.where(l_idx == 0, c0, jnp.where(l_idx == 1, c1, r2))
    cw = cw_ref[...]
    conv = cw[0:1] * prev2 + cw[1:2] * prev1 + cw[2:3] * pre
    y_a = _dot((proj(0, D) * conv).astype(bf16), woc_ref[...])
    pre3 = pre.reshape(nb, lt, D)
    tail = pre3[:, lt - 2:lt, :]
    carry_ref[0] = tail[:, 0:1, :]
    carry_ref[1] = tail[:, 1:2, :]
    nc_ref[0] = tail

    q = proj(3 * D, 3 * D + DK)
    k = proj(3 * D + DK, 3 * D + 2 * DK)
    v = proj(4 * D, 5 * D)
    g = proj(5 * D, 6 * D)
    z = _dot(hb, wz_ref[...]).astype(bf16)
    pa = _dot(z, wgk_ref[...]) + bgk_ref[...]
    la = (jnp.minimum(pa, 0.0) - jnp.log1p(jnp.exp(-jnp.abs(pa)))) / GATE_NORMALIZER
    pos = lax.broadcasted_iota(i32, (m, DK), 0) & (chunk - 1)
    bcum = la
    shift = 1
    while shift < chunk:
        bcum = bcum + jnp.where(pos >= shift, pltpu.roll(bcum, shift, 0), 0.0)
        shift *= 2
    ngrp = m // chunk
    bend = bcum.reshape(ngrp, chunk, DK)[:, chunk - 1:chunk, :]
    bend_rows = jnp.broadcast_to(bend, (ngrp, chunk, DK)).reshape(m, DK)
    nch = lt // chunk
    assert nb == 1 or nch == 1
    hs = H * nb
    nbat = nch * hs

    def per_head(xm, w):
        x3 = xm.reshape(nb * nch, chunk, H * w)
        parts = [x3[:, :, hh * w:(hh + 1) * w] for hh in range(H)]
        return jnp.stack(parts, axis=1 if nb == 1 else 0).reshape(nbat, chunk, w)

    qd = per_head((q * (DKH ** -0.5) * jnp.exp(bcum)).astype(bf16), DKH)
    kd = per_head((k * jnp.exp(-bcum)).astype(bf16), DKH)
    ke = per_head((k * jnp.exp(bend_rows - bcum)).astype(bf16), DKH)
    vv = per_head(v.astype(bf16), DVH)
    assert nbat <= 128
    a_end = per_head(jnp.exp(bend_rows), DKH)[:, 0, :]
    a_col = jnp.concatenate([a_end, jnp.zeros((128 - nbat, DKH), f32)], axis=0).T
    causal = lax.broadcasted_iota(i32, (chunk, chunk), 1) <= lax.broadcasted_iota(i32, (chunk, chunk), 0)
    sc = jnp.einsum('bcd,bsd->bcs', qd, kd, preferred_element_type=f32)
    sc = jnp.where(causal, sc, 0.0).astype(bf16)
    o = jnp.einsum('bcs,bse->bce', sc, vv, preferred_element_type=f32)
    upd = jnp.einsum('bsd,bse->bde', ke, vv, preferred_element_type=f32)
    st = [gs_ref[s % nb, s // nb] if has_state else st_ref[s // nb, s % nb] for s in range(hs)]
    for j in range(nch):
        for s in range(hs):
            b = j * hs + s
            sin_ref[b] = st[s].astype(bf16)
            st[s] = st[s] * a_col[:, b:b + 1] + upd[b]
    for s in range(hs):
        if has_state:
            ng_ref[0, s % nb, s // nb] = st[s]
        else:
            st_ref[s // nb, s % nb] = st[s]
    o = o + jnp.einsum('bcd,bde->bce', qd, sin_ref[...], preferred_element_type=f32)
    on = _rms(o, gnw_ref[...])
    g3 = g.reshape(nb, lt, DV)
    for j in range(nch):
        js = slice(j * chunk, (j + 1) * chunk)
        for hh in range(H):
            vs = slice(hh * DVH, (hh + 1) * DVH)
            gh = g3[:, js, vs]
            b0 = (j * H + hh) * nb
            og_ref[:, js, vs] = (on[b0:b0 + nb] * (gh * jax.nn.sigmoid(gh))).astype(bf16)

    if not has_state:
        @pl.when(i == last)
        def _():
            for hh in range(H):
                ng_ref[0, :, hh] = st_ref[hh]

    y_b = _dot(og_ref[...].reshape(m, DV), wog_ref[...])
    u_a = _dot(hb, wu_ref[:, 0:D])
    u_b = _dot(hb, wu_ref[:, D:2 * D])
    merged = (jax.nn.sigmoid(u_a) * y_a + jax.nn.sigmoid(u_b) * y_b).astype(bf16)
    x1 = x + g1 * _dot(merged, wo_ref[...])

    h2 = (_rms(x1, n2_ref[...]) * (1.0 + sc2) + sh2).astype(bf16)
    sg = _dot(h2, wsg_ref[...])
    act = (sg * jax.nn.sigmoid(sg) * _dot(h2, wsu_ref[...])).astype(bf16)
    x2_ref[...] = x1 + g2 * _dot(act, wsd_ref[...])
    h2_ref[...] = h2
    lg_ref[0] = lax.dot_general(rwt_ref[...], h2, (((1,), (1,)), ((), ())), preferred_element_type=f32)


def _mixer(x, mod4, states, weights, nb, lt, chunk):
    bsz, seqlen, _ = x.shape
    has_state = states is not None
    m = nb * lt
    n_lt = seqlen // lt
    assert not has_state or n_lt == 1
    nsteps = (bsz // nb) * n_lt
    t_all = bsz * seqlen

    def tok_blk(b, i):
        return b * n_lt + i

    in_specs = [pl.BlockSpec((nb, lt, D), lambda b, i: (b, i, 0)),
                pl.BlockSpec((6, nb, 1, D), lambda b, i: (0, b, 0, 0))]
    args = [x, mod4]
    if has_state:
        in_specs += [pl.BlockSpec((nb, 2, D), lambda b, i: (b, 0, 0)),
                     pl.BlockSpec((nb, H, DKH, DVH), lambda b, i: (b, 0, 0, 0))]
        args += list(states)
    in_specs += [_const_spec(w.shape) for w in weights]
    args += list(weights)
    out_specs = [pl.BlockSpec((m, D), lambda b, i: (tok_blk(b, i), 0)),
                 pl.BlockSpec((m, D), lambda b, i: (tok_blk(b, i), 0)),
                 pl.BlockSpec((1, E, m), lambda b, i: (b * n_lt + i, 0, 0)),
                 pl.BlockSpec((1, nb, 2, D), lambda b, i: (0, b, 0, 0)),
                 pl.BlockSpec((1, nb, H, DKH, DVH), lambda b, i: (0, b, 0, 0, 0))]
    out_shape = [jax.ShapeDtypeStruct((t_all, D), f32),
                 jax.ShapeDtypeStruct((t_all, D), bf16),
                 jax.ShapeDtypeStruct((nsteps, E, m), f32),
                 jax.ShapeDtypeStruct((1, bsz, 2, D), f32),
                 jax.ShapeDtypeStruct((1, bsz, H, DKH, DVH), f32)]
    return pl.pallas_call(
        functools.partial(_mixer_body, nb, lt, chunk, has_state),
        grid=(bsz // nb, n_lt),
        in_specs=in_specs,
        out_specs=out_specs,
        out_shape=out_shape,
        scratch_shapes=[pltpu.VMEM((2, nb, 1, D), f32),
                        pltpu.VMEM((H, nb, DKH, DVH), f32),
                        pltpu.VMEM((nb, lt, DV), bf16),
                        pltpu.VMEM(((lt // chunk) * H * nb, DKH, DVH), bf16)],
        compiler_params=pltpu.CompilerParams(dimension_semantics=("arbitrary", "arbitrary"),
                                             vmem_limit_bytes=VMEM_LIMIT),
        name="mixer_state" if has_state else "mixer_prompt",
    )(*args)


def _route_body(lg_ref, bias_ref, eidx_ref, w_ref, lrank_ref, cnt_ref):
    step = pl.program_id(0)
    tl = lg_ref.shape[1]

    scores = jax.nn.sigmoid(lg_ref[...])
    choice = scores + bias_ref[...]
    gsz = E // NGROUPS
    grp = choice.reshape(NGROUPS, gsz, tl)
    mi = lax.broadcasted_iota(i32, (NGROUPS, gsz, tl), 1)
    m1 = jnp.max(grp, axis=1, keepdims=True)
    first = jnp.min(jnp.where(grp == m1, mi, gsz), axis=1, keepdims=True)
    m2 = jnp.max(jnp.where(mi == first, NEG, grp), axis=1, keepdims=True)
    gscore = (m1 + m2).reshape(NGROUPS, tl)
    gi = lax.broadcasted_iota(i32, (NGROUPS, tl), 0)
    gsel = jnp.zeros((NGROUPS, tl), f32)
    work = gscore
    for _ in range(TOPK_GROUPS):
        mx = jnp.max(work, axis=0, keepdims=True)
        f = jnp.min(jnp.where(work == mx, gi, NGROUPS), axis=0, keepdims=True)
        hit = gi == f
        gsel = jnp.where(hit, 1.0, gsel)
        work = jnp.where(hit, NEG, work)
    emask = jnp.broadcast_to(gsel.reshape(NGROUPS, 1, tl), (NGROUPS, gsz, tl)).reshape(E, tl)
    masked = jnp.where(emask > 0.0, choice, NEG)
    ei = lax.broadcasted_iota(i32, (E, tl), 0)
    sel = jnp.zeros((E, tl), f32)
    hits, ws = [], []
    for kk in range(TOPK):
        mx = jnp.max(masked, axis=0, keepdims=True)
        f = jnp.min(jnp.where(masked == mx, ei, E), axis=0, keepdims=True)
        hit = ei == f
        ws.append(jnp.sum(jnp.where(hit, scores, 0.0), axis=0, keepdims=True))
        masked = jnp.where(hit, NEG, masked)
        sel = jnp.where(hit, 1.0, sel)
        hits.append(hit)
        eidx_ref[kk:kk + 1, :] = f
    wsum = ws[0]
    for t in ws[1:]:
        wsum = wsum + t
    for kk in range(TOPK):
        w_ref[kk:kk + 1, :] = ws[kk] / wsum * ROUTED_SCALE

    ui = lax.broadcasted_iota(i32, (tl, tl), 0)
    uj = lax.broadcasted_iota(i32, (tl, tl), 1)
    upper = ((ui <= uj) & ((ui & -TILE) == (uj & -TILE))).astype(bf16)
    lrank = _dot(sel.astype(bf16), upper) - sel
    for kk in range(TOPK):
        lrank_ref[kk:kk + 1, :] = jnp.sum(jnp.where(hits[kk], lrank, 0.0), axis=0, keepdims=True).astype(i32)

    @pl.when(step == 0)
    def _():
        cnt_ref[...] = jnp.zeros_like(cnt_ref)
    lane = lax.broadcasted_iota(i32, cnt_ref.shape, 1)
    tiles_per_step = tl // TILE
    for sub in range(tiles_per_step):
        col = jnp.sum(sel[:, sub * TILE:(sub + 1) * TILE], axis=1, keepdims=True)
        cnt_ref[...] = jnp.where(lane == step * tiles_per_step + sub, col, cnt_ref[...])


def _route(logits_t, bias_col):
    t_all = logits_t.shape[1]
    tl = ROUTE_TILES * TILE
    assert t_all % tl == 0 and t_all // TILE <= MAX_TILES
    tok = lambda s: (0, s)
    return pl.pallas_call(
        _route_body,
        grid=(t_all // tl,),
        in_specs=[pl.BlockSpec((E, tl), tok), pl.BlockSpec((E, 1), lambda s: (0, 0))],
        out_specs=[pl.BlockSpec((TOPK, tl), tok), pl.BlockSpec((TOPK, tl), tok), pl.BlockSpec((TOPK, tl), tok),
                   pl.BlockSpec((E, MAX_TILES), lambda s: (0, 0))],
        out_shape=[jax.ShapeDtypeStruct((TOPK, t_all), i32), jax.ShapeDtypeStruct((TOPK, t_all), f32),
                   jax.ShapeDtypeStruct((TOPK, t_all), i32), jax.ShapeDtypeStruct((E, MAX_TILES), f32)],
        compiler_params=pltpu.CompilerParams(dimension_semantics=("arbitrary",), vmem_limit_bytes=VMEM_LIMIT),
        name="route",
    )(logits_t, bias_col)


def _plan_body(eidx_ref, lrank_ref, cnt_ref, slot_ref, n_ref, l_ref, g_ref, blk_ref):
    t_all = eidx_ref.shape[1]
    cnt = cnt_ref[...]
    nrun = jnp.floor((cnt + (RUN_ALIGN - 1)) * (1.0 / RUN_ALIGN)) * RUN_ALIGN
    ti = lax.broadcasted_iota(i32, (MAX_TILES, MAX_TILES), 0)
    tj = lax.broadcasted_iota(i32, (MAX_TILES, MAX_TILES), 1)
    before = _dotx01(nrun, (ti < tj).astype(bf16))
    region = jnp.sum(nrun, axis=1, keepdims=True)
    nblocks = jnp.floor((region + (MOE_BM - 1)) * (1.0 / MOE_BM))
    ri = lax.broadcasted_iota(i32, (E, E), 0)
    ci = lax.broadcasted_iota(i32, (E, E), 1)
    below = (ci < ri).astype(bf16)
    bstart = _dot01(below, jnp.broadcast_to(nblocks, (E, MAX_TILES)))
    loff = _dot01(below, nrun)
    n_ref[...] = nrun.astype(i32)
    l_ref[...] = loff.astype(i32)
    g_ref[...] = (bstart * MOE_BM + before).astype(i32)
    lane0 = lax.broadcasted_iota(i32, (E, MAX_TILES), 1) == 0
    blk_ref[...] = jnp.where(lane0, bstart, nblocks).astype(i32)

    tile_of = lax.shift_right_logical(lax.broadcasted_iota(i32, (MAX_TILES, t_all), 1), TILE.bit_length() - 1)
    spread = (lax.broadcasted_iota(i32, (MAX_TILES, t_all), 0) == tile_of).astype(bf16)
    ltok = _dotx01(loff, spread)
    ei = lax.broadcasted_iota(i32, (E, t_all), 0)
    for kk in range(TOPK):
        hit = ei == eidx_ref[kk:kk + 1, :]
        start = jnp.sum(jnp.where(hit, ltok, 0.0), axis=0, keepdims=True)
        slot_ref[kk:kk + 1, :] = start.astype(i32) + lrank_ref[kk:kk + 1, :]


def _plan(eidx, lrank, cnt):
    t_all = eidx.shape[1]
    assert TILE & (TILE - 1) == 0
    tl = t_all
    tok = lambda s: (0, s)
    table = pl.BlockSpec((E, MAX_TILES), lambda s: (0, 0))
    return pl.pallas_call(
        _plan_body,
        grid=(1,),
        in_specs=[pl.BlockSpec((TOPK, tl), tok), pl.BlockSpec((TOPK, tl), tok), table],
        out_specs=[pl.BlockSpec((TOPK, tl), tok), table, table, table, table],
        out_shape=[jax.ShapeDtypeStruct((TOPK, t_all), i32)] + [jax.ShapeDtypeStruct((E, MAX_TILES), i32)] * 4,
        compiler_params=pltpu.CompilerParams(dimension_semantics=("arbitrary",), vmem_limit_bytes=VMEM_LIMIT),
        name="plan",
    )(eidx, lrank, cnt)


def _for_each_piece(total, pieces, fn):
    for j, b in enumerate(pieces):
        @pl.when((total & b) != 0)
        def _():
            fn(pl.multiple_of(total & ~(2 * b - 1), pieces[-1]), b, j)


def _for_each_run(n_ref, l_ref, g_ref, tile, fn):
    def per_expert(e, c):
        lo = pl.multiple_of(l_ref[e, tile], RUN_ALIGN)
        go = pl.multiple_of(g_ref[e, tile], RUN_ALIGN)
        _for_each_piece(n_ref[e, tile], RUN_PIECES, lambda off, b, j: fn(lo + off, go + off, b, j))
        return c
    lax.fori_loop(0, E, per_expert, 0)


def _start(copy, j):
    copy.start(priority=j % 2)


def _wait_rows(make_copy, rows):
    _for_each_piece(rows, WAIT_PIECES, lambda off, b, j: make_copy(b).wait())


def _dispatch_body(tiles_a, n_ref, l_ref, g_ref, slot_ref, ha_ref, hb_ref, xs_ref, buf_ref, zero_ref, sem, zsem):
    tile = pl.program_id(0)
    nsteps = pl.num_programs(0)
    cur = tile % RING

    slot = slot_ref[...].astype(i16)
    h = jnp.where(tile < tiles_a, ha_ref[...], hb_ref[...])
    filled = l_ref[E - 1, tile] + n_ref[E - 1, tile]

    def order(row0, nrows):
        srow = lax.broadcasted_iota(i32, (nrows, TILE), 0).astype(i16) + jnp.int16(row0)
        onehot = jnp.zeros((nrows, TILE), bf16)
        for kk in range(TOPK):
            onehot = jnp.where(srow == slot[kk:kk + 1, :], jnp.ones((), bf16), onehot)
        buf_ref[cur, row0:row0 + nrows, :] = _dot(onehot, h).astype(bf16)

    order(0, SLOTS_MAIN)

    @pl.when(filled > SLOTS_MAIN)
    def _():
        order(SLOTS_MAIN, SLOTS - SLOTS_MAIN)

    def run_copy(sl):
        return lambda srow_, grow, b: pltpu.make_async_copy(
            buf_ref.at[sl, pl.ds(srow_, b)], xs_ref.at[pl.ds(grow, b)], sem.at[sl])

    _for_each_run(n_ref, l_ref, g_ref, tile, lambda s, g, b, j: _start(run_copy(cur)(s, g, b), j))

    def wait_tile(t, sl):
        _wait_rows(lambda b: run_copy(sl)(0, 0, b), l_ref[E - 1, t] + n_ref[E - 1, t])

    @pl.when(tile >= RING - 1)
    def _():
        wait_tile(tile - (RING - 1), (tile + 1) % RING)

    last = nsteps - 1

    def zero_fill(fn):
        def per_expert(e, c):
            end = pl.multiple_of(g_ref[e, last] + n_ref[e, last], RUN_ALIGN)
            _for_each_piece((-end) & (MOE_BM - 1), TAIL_PIECES, lambda off, b, j: fn(pltpu.make_async_copy(
                zero_ref.at[pl.ds(0, b)], xs_ref.at[pl.ds(end + off, b)], zsem)))
            return c
        lax.fori_loop(0, E, per_expert, 0)
        used = (g_ref[E - 1, last] + n_ref[E - 1, last] + (MOE_BM - 1)) // MOE_BM

        def per_block(blk, c):
            fn(pltpu.make_async_copy(zero_ref, xs_ref.at[pl.ds(pl.multiple_of(blk * MOE_BM, MOE_BM), MOE_BM)], zsem))
            return c
        lax.fori_loop(used, xs_ref.shape[0] // MOE_BM, per_block, 0)

    @pl.when(tile == 0)
    def _():
        zero_ref[...] = jnp.zeros_like(zero_ref)
        zero_fill(lambda cp: cp.start())

    @pl.when(tile == last)
    def _():
        for back in range(RING - 2, -1, -1):
            @pl.when(tile >= back)
            def _():
                wait_tile(tile - back, (tile - back) % RING)
        zero_fill(lambda cp: cp.wait())


def _dispatch(tables, slot, h2_a, h2_b, nrows):
    tiles_a, tiles_b = h2_a.shape[0] // TILE, h2_b.shape[0] // TILE
    return pl.pallas_call(
        functools.partial(_dispatch_body, tiles_a),
        grid_spec=pltpu.PrefetchScalarGridSpec(
            num_scalar_prefetch=len(tables),
            grid=(tiles_a + tiles_b,),
            in_specs=[pl.BlockSpec((TOPK, TILE), lambda i, *_: (0, i)),
                      pl.BlockSpec((TILE, D), lambda i, *_: (jnp.minimum(i, tiles_a - 1), 0)),
                      pl.BlockSpec((TILE, D), lambda i, *_: (jnp.maximum(i - tiles_a, 0), 0))],
            out_specs=pl.BlockSpec(memory_space=pl.ANY),
            scratch_shapes=[pltpu.VMEM((RING, SLOTS, D), bf16), pltpu.VMEM((MOE_BM, D), bf16),
                            pltpu.SemaphoreType.DMA((RING,)), pltpu.SemaphoreType.DMA]),
        out_shape=jax.ShapeDtypeStruct((nrows, D), bf16),
        compiler_params=pltpu.CompilerParams(dimension_semantics=("arbitrary",), vmem_limit_bytes=VMEM_LIMIT),
        name="dispatch",
    )(*tables, slot, h2_a, h2_b)


def _expert_body(blk_ref, wg_ref, wu_ref, wd_ref, xs_ref, ys_ref, xbuf_ref, ybuf_ref, wgu_ref, wdb_ref, isem, osem):
    e = pl.program_id(0)
    first, count = blk_ref[e, 0], blk_ref[e, 1]
    used = blk_ref[E - 1, 0] + blk_ref[E - 1, 1]
    wgu_ref[:, :DE] = wg_ref[0].astype(bf16)
    wgu_ref[:, DE:] = wu_ref[0].astype(bf16)
    wdb_ref[...] = wd_ref[0].astype(bf16)

    def rows(g):
        return pl.ds(pl.multiple_of(g * MOE_BM, MOE_BM), MOE_BM)

    def load(g):
        sl = g % EXPERT_IN_SLOTS
        return pltpu.make_async_copy(xs_ref.at[rows(g)], xbuf_ref.at[sl], isem.at[sl])

    def store(g):
        sl = g % 2
        return pltpu.make_async_copy(ybuf_ref.at[sl], ys_ref.at[rows(g)], osem.at[sl])

    @pl.when(e == 0)
    def _():
        for g in range(EXPERT_IN_SLOTS - 1):
            @pl.when(g < used)
            def _():
                load(g).start()

    def block(j, c):
        g = first + j

        @pl.when(g + (EXPERT_IN_SLOTS - 1) < used)
        def _():
            load(g + (EXPERT_IN_SLOTS - 1)).start()
        load(g).wait()

        @pl.when(g >= 2)
        def _():
            store(g - 2).wait()
        hgu = _dot(xbuf_ref[g % EXPERT_IN_SLOTS], wgu_ref[...])
        hg = hgu[:, :DE]
        act = (hg * jax.nn.sigmoid(hg) * hgu[:, DE:]).astype(bf16)
        ybuf_ref[g % 2] = _dot(act, wdb_ref[...]).astype(bf16)
        store(g).start()
        return c
    lax.fori_loop(0, count, block, 0)

    @pl.when(e == E - 1)
    def _():
        for back in (2, 1):
            @pl.when(used >= back)
            def _():
                store(used - back).wait()


def _experts(blk, xs, we_gate, we_up, we_down):
    ex = lambda e, blk_: (e, 0, 0)
    return pl.pallas_call(
        _expert_body,
        grid_spec=pltpu.PrefetchScalarGridSpec(
            num_scalar_prefetch=1,
            grid=(E,),
            in_specs=[pl.BlockSpec((1, D, DE), ex), pl.BlockSpec((1, D, DE), ex), pl.BlockSpec((1, DE, D), ex),
                      pl.BlockSpec(memory_space=pl.ANY)],
            out_specs=pl.BlockSpec(memory_space=pl.ANY),
            scratch_shapes=[pltpu.VMEM((EXPERT_IN_SLOTS, MOE_BM, D), bf16), pltpu.VMEM((2, MOE_BM, D), bf16),
                            pltpu.VMEM((D, 2 * DE), bf16), pltpu.VMEM((DE, D), bf16),
                            pltpu.SemaphoreType.DMA((EXPERT_IN_SLOTS,)), pltpu.SemaphoreType.DMA((2,))]),
        out_shape=jax.ShapeDtypeStruct(xs.shape, xs.dtype),
        input_output_aliases={4: 0},
        compiler_params=pltpu.CompilerParams(dimension_semantics=("arbitrary",), vmem_limit_bytes=VMEM_LIMIT),
        name="experts",
    )(blk, we_gate, we_up, we_down, xs)


def _final_body(nb, lt, tile0, n_ref, l_ref, g_ref, slot_ref, wt_ref, x2_ref, g2_ref, fw_ref, ys_ref, o_ref,
                buf_ref, routed_ref, sem):
    m = nb * lt
    n_inner = pl.num_programs(1)
    step = pl.program_id(0) * n_inner + pl.program_id(1)
    nsteps = pl.num_programs(0) * n_inner
    tile = tile0 + step
    cur = step % RING

    def run_copy(sl):
        return lambda srow, grow, b: pltpu.make_async_copy(
            ys_ref.at[pl.ds(grow, b)], buf_ref.at[sl, pl.ds(srow, b)], sem.at[sl])

    def fetch(ahead):
        sl = (step + ahead) % RING
        _for_each_run(n_ref, l_ref, g_ref, tile + ahead, lambda s, g, b, j: _start(run_copy(sl)(s, g, b), j))

    @pl.when(step == 0)
    def _():
        buf_ref[...] = jnp.zeros_like(buf_ref)
        for ahead in range(RING - 1):
            @pl.when(ahead < nsteps)
            def _():
                fetch(ahead)

    @pl.when(step + (RING - 1) < nsteps)
    def _():
        fetch(RING - 1)

    _wait_rows(lambda b: run_copy(cur)(0, 0, b), l_ref[E - 1, tile] + n_ref[E - 1, tile])

    slot = slot_ref[...].astype(i16)
    wt = wt_ref[...].astype(bf16)

    def combine(col0, ncols):
        scol = lax.broadcasted_iota(i32, (m, ncols), 1).astype(i16) + jnp.int16(col0)
        wsel = jnp.zeros((m, ncols), bf16)
        for kk in range(TOPK):
            wsel = jnp.where(scol == slot[:, kk:kk + 1], wt[:, kk:kk + 1], wsel)
        return _dot(wsel, buf_ref[cur, col0:col0 + ncols, :])

    routed_ref[...] = combine(0, SLOTS_MAIN)

    @pl.when(l_ref[E - 1, tile] + n_ref[E - 1, tile] > SLOTS_MAIN)
    def _():
        routed_ref[...] += combine(SLOTS_MAIN, SLOTS - SLOTS_MAIN)
    routed = routed_ref[...]
    g2 = g2_ref[0]
    if nb == 1:
        g2 = g2.reshape(1, D)
    else:
        g2 = jnp.broadcast_to(g2, (nb, lt, D)).reshape(m, D)
    y = x2_ref[...] + g2 * routed
    o_ref[...] = _rms(y, fw_ref[...]).reshape(nb, lt, D)


def _final(tables, slot_t, wts_t, x2, ys, mod4, fw, bsz, seqlen, row0, nb, lt):
    m = nb * lt
    assert m == TILE and row0 % TILE == 0
    n_lt = seqlen // lt
    tile0 = row0 // TILE
    tile = lambda b, i: tile0 + b * n_lt + i
    return pl.pallas_call(
        functools.partial(_final_body, nb, lt, tile0),
        grid_spec=pltpu.PrefetchScalarGridSpec(
            num_scalar_prefetch=len(tables),
            grid=(bsz // nb, n_lt),
            in_specs=[pl.BlockSpec((m, TOPK), lambda b, i, *_: (tile(b, i), 0)),
                      pl.BlockSpec((m, TOPK), lambda b, i, *_: (tile(b, i), 0)),
                      pl.BlockSpec((m, D), lambda b, i, *_: (b * n_lt + i, 0)),
                      pl.BlockSpec((1, nb, 1, D), lambda b, i, *_: (5, b, 0, 0)),
                      pl.BlockSpec((1, D), lambda b, i, *_: (0, 0)),
                      pl.BlockSpec(memory_space=pl.ANY)],
            out_specs=pl.BlockSpec((nb, lt, D), lambda b, i, *_: (b, i, 0)),
            scratch_shapes=[pltpu.VMEM((RING, SLOTS, D), bf16), pltpu.VMEM((m, D), f32),
                            pltpu.SemaphoreType.DMA((RING,))]),
        out_shape=jax.ShapeDtypeStruct((bsz, seqlen, D), f32),
        compiler_params=pltpu.CompilerParams(dimension_semantics=("arbitrary", "arbitrary"),
                                             vmem_limit_bytes=VMEM_LIMIT),
        name="final",
    )(*tables, slot_t, wts_t, x2, mod4, fw, ys)


def kernel(x_prompt, x_sample, state_conv, state_gla, c_prompt, c_sample, w_ada, b_ada, norm1_w, w_in, conv_w,
           w_gk, b_gk, gla_norm_w, w_out_conv, w_out_gla, w_o, norm2_w, router_w, router_bias, we_gate, we_up,
           we_down, ws_gate, ws_up, ws_down, final_norm_w):
    assert w_ada.shape[0] == 1, "single-layer step"
    bp, lp, _ = x_prompt.shape
    bs, ls, _ = x_sample.shape
    tp, ts = bp * lp, bs * ls
    t_all = tp + ts

    w_in0 = w_in[0]
    n_main = 3 * D + 2 * DK + 2 * DV
    rank = w_gk.shape[1]
    weights = (
        norm1_w[0].reshape(1, D),
        w_in0[:, :n_main].astype(bf16),
        jnp.pad(w_in0[:, n_main:n_main + rank], ((0, 0), (0, RANK_PAD - rank))).astype(bf16),
        w_in0[:, n_main + rank:].astype(bf16),
        jnp.pad(w_gk[0], ((0, RANK_PAD - rank), (0, 0))).astype(bf16),
        b_gk[0].reshape(1, DK),
        conv_w[0],
        gla_norm_w[0].reshape(1, DVH),
        w_out_conv[0].astype(bf16),
        w_out_gla[0].astype(bf16),
        w_o[0].astype(bf16),
        norm2_w[0].reshape(1, D),
        router_w[0].T.astype(bf16),
        ws_gate[0].astype(bf16),
        ws_up[0].astype(bf16),
        ws_down[0].astype(bf16),
    )

    mod = _adaln(jnp.concatenate([c_prompt, c_sample], axis=0), w_ada[0], b_ada[0].reshape(1, 6 * D))
    mod4 = mod.reshape(bp + bs, 6, 1, D).transpose(1, 0, 2, 3)
    mod_p, mod_s = mod4[:, :bp], mod4[:, bp:]

    lt_p = 512
    x2_p, h2_p, lg_p, conv_p, gla_p = _mixer(x_prompt, mod_p, None, weights, 1, lt_p, GLA_CHUNK)
    nb_s = 8
    x2_s, h2_s, lg_s, conv_s, gla_s = _mixer(x_sample, mod_s, (state_conv[0], state_gla[0]), weights,
                                             nb_s, ls, math.gcd(ls, GLA_CHUNK))
    lg = jnp.concatenate([lg_p.transpose(1, 0, 2).reshape(E, tp), lg_s.transpose(1, 0, 2).reshape(E, ts)], axis=1)

    eidx, wts, lrank, cnt = _route(lg, router_bias[0].reshape(E, 1))
    n_tiles = t_all // TILE
    max_rows = t_all * TOPK + n_tiles * E * (RUN_ALIGN - 1) + E * (MOE_BM - RUN_ALIGN)
    nblk = -(-max_rows // MOE_BM)
    slot, nrun, loff, goff, blk = _plan(eidx, lrank, cnt)
    tables = (nrun, loff, goff)

    xs = _dispatch(tables, slot, h2_p, h2_s, nblk * MOE_BM)
    ys = _experts(blk, xs, we_gate[0], we_up[0], we_down[0])

    fw = final_norm_w.reshape(1, D)
    slot_t, wts_t = slot.T, wts.T
    y_prompt = _final(tables, slot_t, wts_t, x2_p, ys, mod_p, fw, bp, lp, 0, 1, TILE)
    y_sample = _final(tables, slot_t, wts_t, x2_s, ys, mod_s, fw, bs, ls, tp, TILE // ls, ls)
    return (y_prompt, y_sample, conv_p, gla_p, conv_s, gla_s)
```
